```python
import jax, jax.numpy as jnp
from jax import lax
import numpy as np

D_MODEL = 1024
BATCH = 8
SEQ = 8192
DEPTH = 4

D_PLE = 256
N_MIXERS = 2
GDN_HEADS = 8
GDN_DK = 128
GDN_DV = 128
GDN_CONV = 4
GDN_CHUNK = 64
GDN_QK = GDN_HEADS * GDN_DK
GDN_V = GDN_HEADS * GDN_DV
GDN_IN = 2 * GDN_QK + 2 * GDN_V + 2 * GDN_HEADS
FOX_HEADS = 16
FOX_DH = 64
FOX_BLOCK = 128
FOX_W = FOX_HEADS * FOX_DH
FOX_IN = 4 * FOX_W + FOX_HEADS
N_GDN = (DEPTH + 1) // 2
N_FOX = DEPTH // 2
DN_ALPHA = (2 * DEPTH) ** 0.25
DN_BETA = (8 * DEPTH) ** -0.25
LN_EPS = 1e-5
RMS_EPS = 1e-6

kernel_name = 'hybrid_gdn_fox_deepnorm_ple'

F32 = jnp.float32


def layer_norm(x, g, b):
    xf = x.astype(F32)
    mu = jnp.mean(xf, -1, keepdims=True)
    var = jnp.mean(jnp.square(xf - mu), -1, keepdims=True)
    return ((xf - mu) * lax.rsqrt(var + LN_EPS) * g + b).astype(x.dtype)


def rms_norm(x, g):
    xf = x.astype(F32)
    return (xf * lax.rsqrt(jnp.mean(xf * xf, -1, keepdims=True) + RMS_EPS) * g).astype(x.dtype)


def l2_normalize(x):
    xf = x.astype(F32)
    return xf * lax.rsqrt(jnp.sum(xf * xf, -1, keepdims=True) + RMS_EPS)


def causal_depthwise_conv(x, w):
    K, C = w.shape
    return lax.conv_general_dilated(x, w[:, None, :].astype(x.dtype), window_strides=(1,),
                                    padding=[(K - 1, 0)], dimension_numbers=('NWC', 'WIO', 'NWC'),
                                    feature_group_count=C)


def gated_delta_rule(q, k, v, beta, g):
    B, S, H, dk = q.shape
    dv = v.shape[-1]
    C = GDN_CHUNK
    N = S // C

    def to_chunks(t):
        return t.astype(F32).reshape(B, N, C, H, *t.shape[3:]).swapaxes(2, 3)

    q = to_chunks(q) * (dk ** -0.5)
    k = to_chunks(k)
    v = to_chunks(v)
    beta = to_chunks(beta)
    g = jnp.cumsum(to_chunks(g), axis=-1)
    causal = jnp.tril(jnp.ones((C, C), bool))
    strict = jnp.tril(jnp.ones((C, C), bool), -1)
    decay = jnp.exp(jnp.where(causal, g[..., :, None] - g[..., None, :], -jnp.inf))
    kb = k * beta[..., None]
    L = jnp.where(strict, jnp.einsum('bnhid,bnhjd->bnhij', kb, k) * decay, 0.0)
    eye = jnp.eye(C, dtype=F32)
    rhs = jnp.concatenate([v * beta[..., None], kb * jnp.exp(g)[..., None]], axis=-1)
    sol = lax.linalg.triangular_solve(eye + L, rhs, left_side=True, lower=True, unit_diagonal=True)
    u, w = sol[..., :dv], sol[..., dv:]
    a_qk = jnp.where(causal, jnp.einsum('bnhid,bnhjd->bnhij', q, k) * decay, 0.0)
    q_dec = q * jnp.exp(g)[..., None]
    k_dec = k * jnp.exp(g[..., -1:] - g)[..., None]
    g_last = jnp.exp(g[..., -1])

    def step(state, xs):
        q_c, k_c, u_c, w_c, a_c, gl = xs
        v_new = u_c - jnp.einsum('bhcd,bhde->bhce', w_c, state)
        o_c = jnp.einsum('bhcd,bhde->bhce', q_c, state) + jnp.einsum('bhij,bhje->bhie', a_c, v_new)
        state = state * gl[..., None, None] + jnp.einsum('bhcd,bhce->bhde', k_c, v_new)
        return state, o_c

    xs = tuple(jnp.moveaxis(t, 1, 0) for t in (q_dec, k_dec, u, w, a_qk, g_last))
    s0 = jnp.zeros((B, H, dk, dv), F32)
    _, o = lax.scan(step, s0, xs)
    return o.transpose(1, 0, 3, 2, 4).reshape(B, S, H, dv)


def gdn_mixer(x, w_in, conv_w, a_log, dt_bias, norm_g, w_out):
    B, S, _ = x.shape
    h = x @ w_in
    qkv, z, b_raw, a_raw = jnp.split(h, [2 * GDN_QK + GDN_V, 2 * GDN_QK + 2 * GDN_V,
                                         2 * GDN_QK + 2 * GDN_V + GDN_HEADS], axis=-1)
    qkv = jax.nn.silu(causal_depthwise_conv(qkv, conv_w))
    q, k, v = jnp.split(qkv, [GDN_QK, 2 * GDN_QK], axis=-1)
    q = l2_normalize(q.reshape(B, S, GDN_HEADS, GDN_DK))
    k = l2_normalize(k.reshape(B, S, GDN_HEADS, GDN_DK))
    v = v.reshape(B, S, GDN_HEADS, GDN_DV)
    beta = jax.nn.sigmoid(b_raw.astype(F32))
    g = -jnp.exp(a_log.astype(F32)) * jax.nn.softplus(a_raw.astype(F32) + dt_bias.astype(F32))
    o = gated_delta_rule(q, k, v, beta, g)
    o = rms_norm(o, norm_g) * jax.nn.silu(z.reshape(B, S, GDN_HEADS, GDN_DV).astype(F32))
    return o.reshape(B, S, GDN_V).astype(x.dtype) @ w_out


def forgetting_attention(q, k, v, c):
    B, S, H, Dh = q.shape
    nb = S // FOX_BLOCK
    kf = k.astype(F32)
    vf = v.astype(F32)
    c_t = c.transpose(0, 2, 1)
    q_blocks = (q.astype(F32) * (Dh ** -0.5)).reshape(B, nb, FOX_BLOCK, H, Dh).transpose(1, 0, 3, 2, 4)
    c_blocks = c_t.reshape(B, H, nb, FOX_BLOCK).transpose(2, 0, 1, 3)
    key_pos = jnp.arange(S)

    def one_block(args):
        qb, cb, blk = args
        logits = jnp.einsum('bhtd,bshd->bhts', qb, kf) + cb[..., None] - c_t[:, :, None, :]
        q_pos = blk * FOX_BLOCK + jnp.arange(FOX_BLOCK)
        logits = jnp.where(key_pos[None, :] <= q_pos[:, None], logits, -jnp.inf)
        probs = jax.nn.softmax(logits, axis=-1)
        return jnp.einsum('bhts,bshd->bthd', probs, vf)

    o = lax.map(one_block, (q_blocks, c_blocks, jnp.arange(nb)))
    return o.transpose(1, 0, 2, 3, 4).reshape(B, S, H, Dh)


def fox_mixer(x, w_in, b_f, q_norm_g, k_norm_g, w_out):
    B, S, _ = x.shape
    h = x @ w_in
    q, k, v, z, f_raw = jnp.split(h, [FOX_W, 2 * FOX_W, 3 * FOX_W, 4 * FOX_W], axis=-1)
    q = rms_norm(q.reshape(B, S, FOX_HEADS, FOX_DH), q_norm_g)
    k = rms_norm(k.reshape(B, S, FOX_HEADS, FOX_DH), k_norm_g)
    v = v.reshape(B, S, FOX_HEADS, FOX_DH)
    log_f = jax.nn.log_sigmoid(f_raw.astype(F32) + b_f.astype(F32))
    c = jnp.cumsum(log_f, axis=1)
    o = forgetting_attention(q, k, v, c)
    o = o * jax.nn.silu(z.reshape(B, S, FOX_HEADS, FOX_DH).astype(F32))
    return o.reshape(B, S, FOX_W).astype(x.dtype) @ w_out


def _fwd_setup_inputs(seed: int = 0) -> dict:
    key = jax.random.key(seed)
    ks = jax.random.split(key, 20)
    nrm = jax.random.normal
    x = nrm(ks[0], (BATCH, SEQ, D_MODEL), F32)
    p = nrm(ks[1], (DEPTH, BATCH, SEQ, D_PLE), F32)
    ln_g = 1.0 + 0.02 * nrm(ks[2], (DEPTH, D_MODEL), F32)
    ln_b = 0.02 * nrm(ks[3], (DEPTH, D_MODEL), F32)
    ple_w_gate = nrm(ks[4], (DEPTH, D_MODEL, D_MODEL), F32) * D_MODEL ** -0.5
    ple_w_proj = nrm(ks[5], (DEPTH, D_PLE, D_MODEL), F32) * D_PLE ** -0.5
    gdn_w_in = nrm(ks[6], (N_GDN, D_MODEL, GDN_IN), F32) * D_MODEL ** -0.5
    gdn_conv_w = nrm(ks[7], (N_GDN, GDN_CONV, 2 * GDN_QK + GDN_V), F32) * GDN_CONV ** -0.5
    gdn_a_log = jnp.log(jax.random.uniform(ks[8], (N_GDN, GDN_HEADS), F32, 1.0, 16.0))
    dt = jnp.exp(jax.random.uniform(ks[9], (N_GDN, GDN_HEADS), F32, np.log(1e-3), np.log(1e-1)))
    gdn_dt_bias = dt + jnp.log(-jnp.expm1(-dt))
    gdn_norm_g = 1.0 + 0.02 * nrm(ks[10], (N_GDN, GDN_DV), F32)
    gdn_w_out = nrm(ks[11], (N_GDN, GDN_V, D_MODEL), F32) * (GDN_V ** -0.5) * DN_BETA
    fox_w_in = nrm(ks[12], (N_FOX, D_MODEL, FOX_IN), F32) * D_MODEL ** -0.5
    fox_b_f = jax.random.uniform(ks[13], (N_FOX, FOX_HEADS), F32, 1.0, 5.0)
    fox_q_norm_g = 1.0 + 0.02 * nrm(ks[14], (N_FOX, FOX_DH), F32)
    fox_k_norm_g = 1.0 + 0.02 * nrm(ks[15], (N_FOX, FOX_DH), F32)
    fox_w_out = nrm(ks[16], (N_FOX, FOX_W, D_MODEL), F32) * (FOX_W ** -0.5) * DN_BETA
    return {'x': x, 'p': p, 'ln_g': ln_g, 'ln_b': ln_b, 'ple_w_gate': ple_w_gate,
            'ple_w_proj': ple_w_proj, 'gdn_w_in': gdn_w_in, 'gdn_conv_w': gdn_conv_w,
            'gdn_a_log': gdn_a_log, 'gdn_dt_bias': gdn_dt_bias, 'gdn_norm_g': gdn_norm_g,
            'gdn_w_out': gdn_w_out, 'fox_w_in': fox_w_in, 'fox_b_f': fox_b_f,
            'fox_q_norm_g': fox_q_norm_g, 'fox_k_norm_g': fox_k_norm_g, 'fox_w_out': fox_w_out}


def _fwd_reference(x, p, ln_g, ln_b, ple_w_gate, ple_w_proj, gdn_w_in, gdn_conv_w, gdn_a_log,
              gdn_dt_bias, gdn_norm_g, gdn_w_out, fox_w_in, fox_b_f, fox_q_norm_g,
              fox_k_norm_g, fox_w_out):
    for i in range(DEPTH):
        j = i // N_MIXERS
        if i % N_MIXERS == 0:
            y = gdn_mixer(x, gdn_w_in[j], gdn_conv_w[j], gdn_a_log[j], gdn_dt_bias[j],
                          gdn_norm_g[j], gdn_w_out[j])
        else:
            y = fox_mixer(x, fox_w_in[j], fox_b_f[j], fox_q_norm_g[j], fox_k_norm_g[j],
                          fox_w_out[j])
        x = layer_norm(DN_ALPHA * x + y, ln_g[i], ln_b[i])
        gate = jax.nn.sigmoid((x @ ple_w_gate[i]).astype(F32))
        x = x + (gate * (p[i] @ ple_w_proj[i]).astype(F32)).astype(x.dtype)
    return x


import jax as _jax
import jax.numpy as _jnp

TWIN_FORMAT = 'train_step'
FWD_PARAMS = ['x', 'p', 'ln_g', 'ln_b', 'ple_w_gate', 'ple_w_proj', 'gdn_w_in', 'gdn_conv_w', 'gdn_a_log', 'gdn_dt_bias', 'gdn_norm_g', 'gdn_w_out', 'fox_w_in', 'fox_b_f', 'fox_q_norm_g', 'fox_k_norm_g', 'fox_w_out']
TWIN_WEIGHTS = ['ln_g', 'ln_b', 'ple_w_gate', 'ple_w_proj', 'gdn_w_in', 'gdn_conv_w', 'gdn_a_log', 'gdn_dt_bias', 'gdn_norm_g', 'gdn_w_out', 'fox_w_in', 'fox_b_f', 'fox_q_norm_g', 'fox_k_norm_g', 'fox_w_out']
TWIN_DIFF_INPUT = 'x'
TWIN_INPUTS = ['x', 'p', 'ln_g', 'ln_b', 'ple_w_gate', 'ple_w_proj', 'gdn_w_in', 'gdn_conv_w', 'gdn_a_log', 'gdn_dt_bias', 'gdn_norm_g', 'gdn_w_out', 'fox_w_in', 'fox_b_f', 'fox_q_norm_g', 'fox_k_norm_g', 'fox_w_out', 'loss_target', 'm_ln_g', 'm_ln_b', 'm_ple_w_gate', 'm_ple_w_proj', 'm_gdn_w_in', 'm_gdn_conv_w', 'm_gdn_a_log', 'm_gdn_dt_bias', 'm_gdn_norm_g', 'm_gdn_w_out', 'm_fox_w_in', 'm_fox_b_f', 'm_fox_q_norm_g', 'm_fox_k_norm_g', 'm_fox_w_out', 'v_ln_g', 'v_ln_b', 'v_ple_w_gate', 'v_ple_w_proj', 'v_gdn_w_in', 'v_gdn_conv_w', 'v_gdn_a_log', 'v_gdn_dt_bias', 'v_gdn_norm_g', 'v_gdn_w_out', 'v_fox_w_in', 'v_fox_b_f', 'v_fox_q_norm_g', 'v_fox_k_norm_g', 'v_fox_w_out']
TWIN_OUTPUTS = ['loss', 'grad_x', 'grad_ln_g', 'grad_ln_b', 'grad_ple_w_gate', 'grad_ple_w_proj', 'grad_gdn_w_in', 'grad_gdn_conv_w', 'grad_gdn_a_log', 'grad_gdn_dt_bias', 'grad_gdn_norm_g', 'grad_gdn_w_out', 'grad_fox_w_in', 'grad_fox_b_f', 'grad_fox_q_norm_g', 'grad_fox_k_norm_g', 'grad_fox_w_out', 'delta_ln_g', 'delta_ln_b', 'delta_ple_w_gate', 'delta_ple_w_proj', 'delta_gdn_w_in', 'delta_gdn_conv_w', 'delta_gdn_a_log', 'delta_gdn_dt_bias', 'delta_gdn_norm_g', 'delta_gdn_w_out', 'delta_fox_w_in', 'delta_fox_b_f', 'delta_fox_q_norm_g', 'delta_fox_k_norm_g', 'delta_fox_w_out', 'new_m_ln_g', 'new_m_ln_b', 'new_m_ple_w_gate', 'new_m_ple_w_proj', 'new_m_gdn_w_in', 'new_m_gdn_conv_w', 'new_m_gdn_a_log', 'new_m_gdn_dt_bias', 'new_m_gdn_norm_g', 'new_m_gdn_w_out', 'new_m_fox_w_in', 'new_m_fox_b_f', 'new_m_fox_q_norm_g', 'new_m_fox_k_norm_g', 'new_m_fox_w_out', 'new_v_ln_g', 'new_v_ln_b', 'new_v_ple_w_gate', 'new_v_ple_w_proj', 'new_v_gdn_w_in', 'new_v_gdn_conv_w', 'new_v_gdn_a_log', 'new_v_gdn_dt_bias', 'new_v_gdn_norm_g', 'new_v_gdn_w_out', 'new_v_fox_w_in', 'new_v_fox_b_f', 'new_v_fox_q_norm_g', 'new_v_fox_k_norm_g', 'new_v_fox_w_out']
TWIN_LEAF_KINDS = {'loss': 'loss', 'grad_x': 'grad_x', 'grad_ln_g': 'grad_w', 'grad_ln_b': 'grad_w', 'grad_ple_w_gate': 'grad_w', 'grad_ple_w_proj': 'grad_w', 'grad_gdn_w_in': 'grad_w', 'grad_gdn_conv_w': 'grad_w', 'grad_gdn_a_log': 'grad_w', 'grad_gdn_dt_bias': 'grad_w', 'grad_gdn_norm_g': 'grad_w', 'grad_gdn_w_out': 'grad_w', 'grad_fox_w_in': 'grad_w', 'grad_fox_b_f': 'grad_w', 'grad_fox_q_norm_g': 'grad_w', 'grad_fox_k_norm_g': 'grad_w', 'grad_fox_w_out': 'grad_w', 'delta_ln_g': 'delta_w', 'delta_ln_b': 'delta_w', 'delta_ple_w_gate': 'delta_w', 'delta_ple_w_proj': 'delta_w', 'delta_gdn_w_in': 'delta_w', 'delta_gdn_conv_w': 'delta_w', 'delta_gdn_a_log': 'delta_w', 'delta_gdn_dt_bias': 'delta_w', 'delta_gdn_norm_g': 'delta_w', 'delta_gdn_w_out': 'delta_w', 'delta_fox_w_in': 'delta_w', 'delta_fox_b_f': 'delta_w', 'delta_fox_q_norm_g': 'delta_w', 'delta_fox_k_norm_g': 'delta_w', 'delta_fox_w_out': 'delta_w', 'new_m_ln_g': 'new_m', 'new_m_ln_b': 'new_m', 'new_m_ple_w_gate': 'new_m', 'new_m_ple_w_proj': 'new_m', 'new_m_gdn_w_in': 'new_m', 'new_m_gdn_conv_w': 'new_m', 'new_m_gdn_a_log': 'new_m', 'new_m_gdn_dt_bias': 'new_m', 'new_m_gdn_norm_g': 'new_m', 'new_m_gdn_w_out': 'new_m', 'new_m_fox_w_in': 'new_m', 'new_m_fox_b_f': 'new_m', 'new_m_fox_q_norm_g': 'new_m', 'new_m_fox_k_norm_g': 'new_m', 'new_m_fox_w_out': 'new_m', 'new_v_ln_g': 'new_v', 'new_v_ln_b': 'new_v', 'new_v_ple_w_gate': 'new_v', 'new_v_ple_w_proj': 'new_v', 'new_v_gdn_w_in': 'new_v', 'new_v_gdn_conv_w': 'new_v', 'new_v_gdn_a_log': 'new_v', 'new_v_gdn_dt_bias': 'new_v', 'new_v_gdn_norm_g': 'new_v', 'new_v_gdn_w_out': 'new_v', 'new_v_fox_w_in': 'new_v', 'new_v_fox_b_f': 'new_v', 'new_v_fox_q_norm_g': 'new_v', 'new_v_fox_k_norm_g': 'new_v', 'new_v_fox_w_out': 'new_v'}


def _forward(args):
    return _fwd_reference(*[args[k] for k in FWD_PARAMS])


def _output_shape():
    def fwd():
        inp = _fwd_setup_inputs(0)
        return _fwd_reference(*[inp[k] for k in FWD_PARAMS])
    out = _jax.eval_shape(fwd)
    return out.shape, out.dtype

N_MICROBATCH = 1
ADAM_LR = 0.001
ADAM_B1 = 0.9
ADAM_B2 = 0.999
ADAM_EPS = 1e-08
ADAM_WD = 0.01
ADAM_STEP = 10
PER_EXAMPLE_BATCH_AXIS = {'x': 0, 'p': 1, 'loss_target': 0}
SHARED_INPUTS = []
_WEIGHT_DTYPES = {'ln_g': _jnp.float32, 'ln_b': _jnp.float32, 'ple_w_gate': _jnp.float32, 'ple_w_proj': _jnp.float32, 'gdn_w_in': _jnp.float32, 'gdn_conv_w': _jnp.float32, 'gdn_a_log': _jnp.float32, 'gdn_dt_bias': _jnp.float32, 'gdn_norm_g': _jnp.float32, 'gdn_w_out': _jnp.float32, 'fox_w_in': _jnp.float32, 'fox_b_f': _jnp.float32, 'fox_q_norm_g': _jnp.float32, 'fox_k_norm_g': _jnp.float32, 'fox_w_out': _jnp.float32}
MOMENT_SCALE = {'ln_g': 3.309023e+01, 'ln_b': 5.630053e+00, 'ple_w_gate': 1.255210e-01, 'ple_w_proj': 4.229983e-01, 'gdn_w_in': 3.053424e-02, 'gdn_conv_w': 5.238395e-02, 'gdn_a_log': 1.646415e-01, 'gdn_dt_bias': 1.560807e-01, 'gdn_norm_g': 3.916455e-01, 'gdn_w_out': 3.162954e-01, 'fox_w_in': 1.511307e-02, 'fox_b_f': 5.526052e-02, 'fox_q_norm_g': 5.103887e-02, 'fox_k_norm_g': 5.148915e-02, 'fox_w_out': 4.044849e-02}


def _to_microbatches(a, axis):
    t = _jnp.moveaxis(a, axis, 0)
    t = t.reshape((N_MICROBATCH, t.shape[0] // N_MICROBATCH) + t.shape[1:])
    return _jnp.moveaxis(t, 1, axis + 1)


def setup_inputs(seed: int = 0) -> dict:
    inp = _fwd_setup_inputs(seed)
    key = _jax.random.fold_in(_jax.random.key(seed), 7919)
    shape, _ = _output_shape()
    out = dict(inp)
    out["loss_target"] = _jax.random.normal(_jax.random.fold_in(key, 0), shape, _jnp.float32)
    for i, name in enumerate(TWIN_WEIGHTS):
        w = inp[name].astype(_jnp.float32)
        if MOMENT_SCALE is None:
            s = _jnp.sqrt(_jnp.mean(_jnp.square(w)) + 1e-30)
        else:
            s = MOMENT_SCALE[name]
        km, kv = _jax.random.split(_jax.random.fold_in(key, i + 1))
        out[name] = w
        out["m_" + name] = s * _jax.random.normal(km, w.shape, _jnp.float32)
        out["v_" + name] = (s * s) * _jax.random.uniform(kv, w.shape, _jnp.float32, 0.5, 1.5)
    if N_MICROBATCH > 1:
        for name, axis in PER_EXAMPLE_BATCH_AXIS.items():
            out[name] = _to_microbatches(out[name], axis)
    return {'x': out['x'], 'p': out['p'], 'ln_g': out['ln_g'], 'ln_b': out['ln_b'], 'ple_w_gate': out['ple_w_gate'], 'ple_w_proj': out['ple_w_proj'], 'gdn_w_in': out['gdn_w_in'], 'gdn_conv_w': out['gdn_conv_w'], 'gdn_a_log': out['gdn_a_log'], 'gdn_dt_bias': out['gdn_dt_bias'], 'gdn_norm_g': out['gdn_norm_g'], 'gdn_w_out': out['gdn_w_out'], 'fox_w_in': out['fox_w_in'], 'fox_b_f': out['fox_b_f'], 'fox_q_norm_g': out['fox_q_norm_g'], 'fox_k_norm_g': out['fox_k_norm_g'], 'fox_w_out': out['fox_w_out'], 'loss_target': out['loss_target'], 'm_ln_g': out['m_ln_g'], 'm_ln_b': out['m_ln_b'], 'm_ple_w_gate': out['m_ple_w_gate'], 'm_ple_w_proj': out['m_ple_w_proj'], 'm_gdn_w_in': out['m_gdn_w_in'], 'm_gdn_conv_w': out['m_gdn_conv_w'], 'm_gdn_a_log': out['m_gdn_a_log'], 'm_gdn_dt_bias': out['m_gdn_dt_bias'], 'm_gdn_norm_g': out['m_gdn_norm_g'], 'm_gdn_w_out': out['m_gdn_w_out'], 'm_fox_w_in': out['m_fox_w_in'], 'm_fox_b_f': out['m_fox_b_f'], 'm_fox_q_norm_g': out['m_fox_q_norm_g'], 'm_fox_k_norm_g': out['m_fox_k_norm_g'], 'm_fox_w_out': out['m_fox_w_out'], 'v_ln_g': out['v_ln_g'], 'v_ln_b': out['v_ln_b'], 'v_ple_w_gate': out['v_ple_w_gate'], 'v_ple_w_proj': out['v_ple_w_proj'], 'v_gdn_w_in': out['v_gdn_w_in'], 'v_gdn_conv_w': out['v_gdn_conv_w'], 'v_gdn_a_log': out['v_gdn_a_log'], 'v_gdn_dt_bias': out['v_gdn_dt_bias'], 'v_gdn_norm_g': out['v_gdn_norm_g'], 'v_gdn_w_out': out['v_gdn_w_out'], 'v_fox_w_in': out['v_fox_w_in'], 'v_fox_b_f': out['v_fox_b_f'], 'v_fox_q_norm_g': out['v_fox_q_norm_g'], 'v_fox_k_norm_g': out['v_fox_k_norm_g'], 'v_fox_w_out': out['v_fox_w_out']}


def _loss(weights, diff, rest, loss_target):
    with _jax.named_scope("forward"):
        args = {**rest, TWIN_DIFF_INPUT: diff, **{k: w.astype(_WEIGHT_DTYPES[k]) for k, w in weights.items()}}
        y = _forward(args)
    with _jax.named_scope("loss_head"):
        err = _jnp.square(y.astype(_jnp.float32) - loss_target)
        return 0.5 * _jnp.sum(_jnp.mean(err, axis=-1)) if err.ndim else 0.5 * err


def _adamw(w, g, m, v):
    m = ADAM_B1 * m + (1.0 - ADAM_B1) * g
    v = ADAM_B2 * v + (1.0 - ADAM_B2) * _jnp.square(g)
    m_hat = m / (1.0 - ADAM_B1 ** ADAM_STEP)
    v_hat = v / (1.0 - ADAM_B2 ** ADAM_STEP)
    delta = -ADAM_LR * (m_hat / (_jnp.sqrt(v_hat) + ADAM_EPS) + ADAM_WD * w)
    return delta, m, v


def reference(x, p, ln_g, ln_b, ple_w_gate, ple_w_proj, gdn_w_in, gdn_conv_w, gdn_a_log, gdn_dt_bias, gdn_norm_g, gdn_w_out, fox_w_in, fox_b_f, fox_q_norm_g, fox_k_norm_g, fox_w_out, loss_target, m_ln_g, m_ln_b, m_ple_w_gate, m_ple_w_proj, m_gdn_w_in, m_gdn_conv_w, m_gdn_a_log, m_gdn_dt_bias, m_gdn_norm_g, m_gdn_w_out, m_fox_w_in, m_fox_b_f, m_fox_q_norm_g, m_fox_k_norm_g, m_fox_w_out, v_ln_g, v_ln_b, v_ple_w_gate, v_ple_w_proj, v_gdn_w_in, v_gdn_conv_w, v_gdn_a_log, v_gdn_dt_bias, v_gdn_norm_g, v_gdn_w_out, v_fox_w_in, v_fox_b_f, v_fox_q_norm_g, v_fox_k_norm_g, v_fox_w_out):
    given = dict(x=x, p=p, ln_g=ln_g, ln_b=ln_b, ple_w_gate=ple_w_gate, ple_w_proj=ple_w_proj, gdn_w_in=gdn_w_in, gdn_conv_w=gdn_conv_w, gdn_a_log=gdn_a_log, gdn_dt_bias=gdn_dt_bias, gdn_norm_g=gdn_norm_g, gdn_w_out=gdn_w_out, fox_w_in=fox_w_in, fox_b_f=fox_b_f, fox_q_norm_g=fox_q_norm_g, fox_k_norm_g=fox_k_norm_g, fox_w_out=fox_w_out, loss_target=loss_target, m_ln_g=m_ln_g, m_ln_b=m_ln_b, m_ple_w_gate=m_ple_w_gate, m_ple_w_proj=m_ple_w_proj, m_gdn_w_in=m_gdn_w_in, m_gdn_conv_w=m_gdn_conv_w, m_gdn_a_log=m_gdn_a_log, m_gdn_dt_bias=m_gdn_dt_bias, m_gdn_norm_g=m_gdn_norm_g, m_gdn_w_out=m_gdn_w_out, m_fox_w_in=m_fox_w_in, m_fox_b_f=m_fox_b_f, m_fox_q_norm_g=m_fox_q_norm_g, m_fox_k_norm_g=m_fox_k_norm_g, m_fox_w_out=m_fox_w_out, v_ln_g=v_ln_g, v_ln_b=v_ln_b, v_ple_w_gate=v_ple_w_gate, v_ple_w_proj=v_ple_w_proj, v_gdn_w_in=v_gdn_w_in, v_gdn_conv_w=v_gdn_conv_w, v_gdn_a_log=v_gdn_a_log, v_gdn_dt_bias=v_gdn_dt_bias, v_gdn_norm_g=v_gdn_norm_g, v_gdn_w_out=v_gdn_w_out, v_fox_w_in=v_fox_w_in, v_fox_b_f=v_fox_b_f, v_fox_q_norm_g=v_fox_q_norm_g, v_fox_k_norm_g=v_fox_k_norm_g, v_fox_w_out=v_fox_w_out)
    weights = {n: given[n] for n in TWIN_WEIGHTS}
    shared = {n: given[n] for n in SHARED_INPUTS}
    per_example = {n: given[n] for n in ['x', 'p']}
    grad_fn = _jax.value_and_grad(_loss, argnums=(0, 1))

    def one_microbatch(ex, loss_target):
        ex = dict(ex)
        diff = ex.pop(TWIN_DIFF_INPUT)
        return grad_fn(weights, diff, {**shared, **ex}, loss_target)

    if N_MICROBATCH == 1:
        loss, (grad_w, grad_x) = one_microbatch(per_example, given["loss_target"])
    else:
        def body(carry, xs):
            loss_sum, grad_sum = carry
            l_k, (gw_k, gx_k) = one_microbatch(xs[0], xs[1])
            with _jax.named_scope("update"):
                return (loss_sum + l_k, _jax.tree.map(_jnp.add, grad_sum, gw_k)), gx_k

        init = (_jnp.zeros((), _jnp.float32), _jax.tree.map(_jnp.zeros_like, weights))
        (loss, grad_w), grad_x = _jax.lax.scan(body, init, (per_example, given["loss_target"]))
    with _jax.named_scope("update"):
        delta_w, new_m, new_v = {}, {}, {}
        for n in TWIN_WEIGHTS:
            delta_w[n], new_m[n], new_v[n] = _adamw(weights[n], grad_w[n], given["m_" + n], given["v_" + n])
    return (loss, grad_x, *[grad_w[n] for n in TWIN_WEIGHTS], *[delta_w[n] for n in TWIN_WEIGHTS],
            *[new_m[n] for n in TWIN_WEIGHTS], *[new_v[n] for n in TWIN_WEIGHTS])
```

```python
import functools
import math

import jax
import jax.numpy as jnp
from jax import lax
from jax.experimental import pallas as pl
from jax.experimental.pallas import tpu as pltpu

F32 = jnp.float32
BF16 = jnp.bfloat16
MXU_DTYPE = BF16
HI = lax.Precision.HIGHEST

N_DEV = 8
LANES = 128
SUBLANES = 8
VMEM_BYTES = 64 * 1024 * 1024

GDN_CHUNK = 64
GDN_CONV = 4
LN_EPS = 1e-5
RMS_EPS = 1e-6
NEG = -1e30

ADAM_LR = 0.001
ADAM_B1 = 0.9
ADAM_B2 = 0.999
ADAM_EPS = 1e-08
ADAM_WD = 0.01
ADAM_STEP = 10


def _params(semantics, vmem_mb=40):
    return pltpu.CompilerParams(dimension_semantics=semantics, vmem_limit_bytes=vmem_mb * 1024 * 1024)


def _pick(dim, cap, unit=LANES):
    if dim <= cap:
        return dim
    best = None
    for t in range(unit, cap + 1, unit):
        if dim % t == 0:
            best = t
    assert best is not None, (dim, cap)
    return best


def _dot(a, b, dims):
    return lax.dot_general(a.astype(MXU_DTYPE), b.astype(MXU_DTYPE), (dims, ((), ())), preferred_element_type=F32)


_NN = ((1,), (0,))
_NT = ((1,), (1,))
_TN = ((0,), (0,))


@jax.custom_vjp
def _mm_nn(a, b):
    return _dot(a, b, _NN)


@jax.custom_vjp
def _mm_nt(a, b):
    return _dot(a, b, _NT)


@jax.custom_vjp
def _mm_tn(a, b):
    return _dot(a, b, _TN)


_mm_nn.defvjp(lambda a, b: (_dot(a, b, _NN), (a, b)), lambda r, g: (_mm_nt(g, r[1]), _mm_tn(r[0], g)))
_mm_nt.defvjp(lambda a, b: (_dot(a, b, _NT), (a, b)), lambda r, g: (_mm_nn(g, r[1]), _mm_tn(g, r[0])))
_mm_tn.defvjp(lambda a, b: (_dot(a, b, _TN), (a, b)), lambda r, g: (_mm_nt(r[1], g), _mm_nn(r[0], g)))


def _mm_hi(a, b):
    return lax.dot_general(a, b, (_NN, ((), ())), precision=HI, preferred_element_type=F32)


def _sigmoid(x):
    return 1.0 / (1.0 + jnp.exp(-x))


def _silu(x):
    return x * _sigmoid(x)


def _softplus(x):
    return jnp.maximum(x, 0.0) + jnp.log(1.0 + jnp.exp(-jnp.abs(x)))


def _iota2(shape, dim):
    return lax.broadcasted_iota(jnp.int32, shape, dim)


def _lane_pick(tile, lane):
    return jnp.sum(jnp.where(_iota2(tile.shape, 1) == lane, tile, 0.0), axis=1, keepdims=True)


def _lane_put(col, lane, width=LANES):
    return jnp.where(_iota2((col.shape[0], width), 1) == lane, col, 0.0)


def _matmul(a, b, mode, out_dtype=F32, *, name, tm=512, tn=1408, tk=1408, a_cols=None, b_cols=None):
    def cols(arr, rng):
        return (0, arr.shape[1]) if rng is None else rng

    a0, an = cols(a, a_cols)
    b0, bn = cols(b, b_cols)
    if mode == "nn":
        M, K, N = a.shape[0], an, bn
        assert b.shape[0] == K
    elif mode == "nt":
        M, K, N = a.shape[0], an, b.shape[0]
        assert bn == K
    else:
        K, M, N = a.shape[0], an, bn
        assert b.shape[0] == K
    tm, tn, tk = _pick(M, tm), _pick(N, tn), _pick(K, tk)
    nk = K // tk
    if mode == "nn":
        assert a0 % tk == 0 and b0 % tn == 0
        a_spec = pl.BlockSpec((tm, tk), lambda i, j, k: (i, a0 // tk + k))
        b_spec = pl.BlockSpec((tk, tn), lambda i, j, k: (k, b0 // tn + j))
        dims = _NN
    elif mode == "nt":
        assert a0 % tk == 0 and b0 % tk == 0
        a_spec = pl.BlockSpec((tm, tk), lambda i, j, k: (i, a0 // tk + k))
        b_spec = pl.BlockSpec((tn, tk), lambda i, j, k: (j, b0 // tk + k))
        dims = _NT
    else:
        assert a0 % tm == 0 and b0 % tn == 0
        a_spec = pl.BlockSpec((tk, tm), lambda i, j, k: (k, a0 // tm + i))
        b_spec = pl.BlockSpec((tk, tn), lambda i, j, k: (k, b0 // tn + j))
        dims = _TN

    def body(a_ref, b_ref, o_ref, acc_ref):
        k = pl.program_id(2)

        @pl.when(k == 0)
        def _():
            acc_ref[...] = jnp.zeros_like(acc_ref)

        acc_ref[...] += _dot(a_ref[...], b_ref[...], dims)

        @pl.when(k == nk - 1)
        def _():
            o_ref[...] = acc_ref[...].astype(o_ref.dtype)

    return pl.pallas_call(
        body,
        grid=(M // tm, N // tn, nk),
        in_specs=[a_spec, b_spec],
        out_specs=pl.BlockSpec((tm, tn), lambda i, j, k: (i, j)),
        out_shape=jax.ShapeDtypeStruct((M, N), out_dtype),
        scratch_shapes=[pltpu.VMEM((tm, tn), F32)],
        compiler_params=_params(("parallel", "parallel", "arbitrary"), 48),
        name=name,
    )(a, b)


def _rowwise(fn, rows, consts, out_rows, out_accs, *, tile, name, reverse=False, carries=(), vmem_mb=40):
    rows = [r if isinstance(r, tuple) else (r, r.shape[1], 0) for r in rows]
    S = rows[0][0].shape[0]
    tile = _pick(S, tile, SUBLANES)
    nt = S // tile
    nr, nc, no, na = len(rows), len(consts), len(out_rows), len(out_accs)

    def ridx(i):
        return nt - 1 - i if reverse else i

    in_specs = [pl.BlockSpec((tile, w), functools.partial(lambda i, cb: (ridx(i), cb), cb=cb)) for _, w, cb in rows]
    in_specs += [pl.BlockSpec(c.shape, functools.partial(lambda i, nd: (0,) * nd, nd=c.ndim)) for c in consts]
    out_specs = [pl.BlockSpec((tile, c), lambda i: (ridx(i), 0)) for c, _ in out_rows]
    out_specs += [pl.BlockSpec(s, functools.partial(lambda i, nd: (0,) * nd, nd=len(s))) for s, _ in out_accs]
    out_shape = [jax.ShapeDtypeStruct((S, c), d) for c, d in out_rows]
    out_shape += [jax.ShapeDtypeStruct(s, d) for s, d in out_accs]

    def body(*refs):
        rin, cin = refs[:nr], refs[nr:nr + nc]
        rout, aout = refs[nr + nc:nr + nc + no], refs[nr + nc + no:nr + nc + no + na]
        carr = refs[nr + nc + no + na:]
        step = pl.program_id(0)

        @pl.when(step == 0)
        def _():
            for r in aout + carr:
                r[...] = jnp.zeros_like(r)

        outs, accs, newc = fn([r[...] for r in rin], [c[...] for c in cin], [c[...] for c in carr])
        for r, o in zip(rout, outs, strict=True):
            r[...] = o.astype(r.dtype)
        for r, v in zip(aout, accs, strict=True):
            r[...] += v
        for r, v in zip(carr, newc, strict=True):
            r[...] = v

    res = pl.pallas_call(
        body,
        grid=(nt,),
        in_specs=in_specs,
        out_specs=out_specs,
        out_shape=out_shape,
        scratch_shapes=[pltpu.VMEM(s, F32) for s in carries],
        compiler_params=_params(("arbitrary",), vmem_mb),
        name=name,
    )(*[r[0] for r in rows], *consts)
    return res


def _ln_fn(x, y, g, b, alpha):
    r = alpha * x + y
    mu = jnp.mean(r, -1, keepdims=True)
    var = jnp.mean(jnp.square(r - mu), -1, keepdims=True)
    return (r - mu) * lax.rsqrt(var + LN_EPS) * g + b


def _ln_fwd(x, y, g, b, alpha, name):
    D = x.shape[1]

    def fn(rows, consts, _):
        return [_ln_fn(rows[0], rows[1], consts[0], consts[1], alpha)], [], []

    return _rowwise(fn, [x, y], [g, b], [(D, F32)], [], tile=256, name=name)[0]


def _ln_bwd(x, y, g, b, dx1, alpha, name):
    D = x.shape[1]

    def fn(rows, consts, _):
        xv, yv, d = rows
        _, vjp = jax.vjp(lambda yy, gg, bb: _ln_fn(xv, yy, gg, bb, alpha), yv, consts[0], consts[1])
        dy, dg, db = vjp(d)
        return [dy], [dg, db], []

    return _rowwise(fn, [x, y, dx1], [g, b], [(D, F32)], [((1, D), F32), ((1, D), F32)], tile=256, name=name)


def _ple_fwd(x1, gate_pre, pp, name):
    D = x1.shape[1]

    def fn(rows, _, __):
        return [rows[0] + _sigmoid(rows[1]) * rows[2]], [], []

    return _rowwise(fn, [x1, gate_pre, pp], [], [(D, F32)], [], tile=256, name=name)[0]


def _ple_bwd(dx2, gate_pre, pp, name):
    D = dx2.shape[1]

    def fn(rows, _, __):
        d, gp, ppv = rows
        s = _sigmoid(gp)
        return [d * ppv * s * (1.0 - s), d * s], [], []

    return _rowwise(fn, [dx2, gate_pre, pp], [], [(D, F32), (D, F32)], [], tile=256, name=name)


def _add(a, b, name):
    def fn(rows, _, __):
        return [rows[0] + rows[1]], [], []

    return _rowwise(fn, [a, b], [], [(a.shape[1], F32)], [], tile=256, name=name)[0]


def _axpy(alpha, a, b, name):
    def fn(rows, _, __):
        return [alpha * rows[0] + rows[1]], [], []

    return _rowwise(fn, [a, b], [], [(a.shape[1], F32)], [], tile=256, name=name)[0]


def _loss_head(y, target, name):
    D = y.shape[1]

    def fn(rows, _, __):
        e = rows[0] - rows[1]
        part = 0.5 * jnp.sum(jnp.sum(e * e, axis=1, keepdims=True), axis=0, keepdims=True) / D
        return [e / D], [jnp.broadcast_to(part, (SUBLANES, LANES))], []

    return _rowwise(fn, [y, target], [], [(D, F32)], [((SUBLANES, LANES), F32)], tile=256, name=name)


def _conv_fwd(h, w, n_cols, name):
    S = h.shape[0]
    T = _pick(S, 512, SUBLANES)
    CB = _pick(n_cols, 512)
    nt = S // T
    K = GDN_CONV

    def body(x_ref, halo_ref, w_ref, o_ref, buf):
        i = pl.program_id(1)
        buf[0:SUBLANES, :] = jnp.where(i > 0, halo_ref[...], 0.0)
        buf[SUBLANES:, :] = x_ref[...]
        acc = jnp.zeros((T, CB), F32)
        for k in range(K):
            acc = acc + w_ref[k:k + 1, :] * buf[pl.ds(SUBLANES - (K - 1) + k, T), :]
        o_ref[...] = acc

    return pl.pallas_call(
        body,
        grid=(n_cols // CB, nt),
        in_specs=[pl.BlockSpec((T, CB), lambda c, i: (i, c)),
                  pl.BlockSpec((SUBLANES, CB), lambda c, i: (jnp.maximum(i * (T // SUBLANES) - 1, 0), c)),
                  pl.BlockSpec((K, CB), lambda c, i: (0, c))],
        out_specs=pl.BlockSpec((T, CB), lambda c, i: (i, c)),
        out_shape=jax.ShapeDtypeStruct((S, n_cols), F32),
        scratch_shapes=[pltpu.VMEM((T + SUBLANES, CB), F32)],
        compiler_params=_params(("parallel", "parallel")),
        name=name,
    )(h, h, w)


def _conv_bwd(dc, h, w, h_col0, name):
    S, n_cols = dc.shape
    T = _pick(S, 512, SUBLANES)
    CB = _pick(n_cols, 512)
    nt = S // T
    K = GDN_CONV
    assert h_col0 % CB == 0
    hb = h_col0 // CB

    def body(d_ref, halo_ref, x_ref, w_ref, dx_ref, dw_ref, buf):
        i = pl.program_id(1)

        @pl.when(i == 0)
        def _():
            dw_ref[...] = jnp.zeros_like(dw_ref)

        buf[0:T, :] = d_ref[...]
        buf[T:, :] = jnp.where(i < nt - 1, halo_ref[...], 0.0)
        x = x_ref[...]
        acc = jnp.zeros((T, CB), F32)
        for k in range(K):
            shifted = buf[pl.ds(K - 1 - k, T), :]
            acc = acc + w_ref[k:k + 1, :] * shifted
            dw_ref[k:k + 1, :] += jnp.sum(shifted * x, axis=0, keepdims=True)
        dx_ref[...] = acc

    last = S // SUBLANES - 1
    return pl.pallas_call(
        body,
        grid=(n_cols // CB, nt),
        in_specs=[pl.BlockSpec((T, CB), lambda c, i: (i, c)),
                  pl.BlockSpec((SUBLANES, CB), lambda c, i: (jnp.minimum((i + 1) * (T // SUBLANES), last), c)),
                  pl.BlockSpec((T, CB), lambda c, i: (i, hb + c)),
                  pl.BlockSpec((K, CB), lambda c, i: (0, c))],
        out_specs=[pl.BlockSpec((T, CB), lambda c, i: (i, c)),
                   pl.BlockSpec((SUBLANES, CB), lambda c, i: (0, c))],
        out_shape=[jax.ShapeDtypeStruct((S, n_cols), F32), jax.ShapeDtypeStruct((SUBLANES, n_cols), F32)],
        scratch_shapes=[pltpu.VMEM((T + SUBLANES, CB), F32)],
        compiler_params=_params(("parallel", "arbitrary")),
        name=name,
    )(dc, dc, h, w)


def _neumann_inverse(L):
    C = L.shape[0]
    eye = (_iota2((C, C), 0) == _iota2((C, C), 1)).astype(F32)
    X = eye - L
    P = L
    for _ in range(max(0, math.ceil(math.log2(C)) - 1)):
        P = _mm_hi(P, P)
        X = _mm_hi(X, eye + P)
    return X


def _gdn_chunk(cq, ck, cv, zz, bcol, acol, alog, dtb, ng, state):
    C, dk = cq.shape
    q = _silu(cq)
    k = _silu(ck)
    v = _silu(cv)
    q = q * lax.rsqrt(jnp.sum(q * q, -1, keepdims=True) + RMS_EPS) * (dk ** -0.5)
    k = k * lax.rsqrt(jnp.sum(k * k, -1, keepdims=True) + RMS_EPS)
    beta = _sigmoid(bcol)
    g = -jnp.exp(alog) * _softplus(acol + dtb)

    row, col = _iota2((C, C), 0), _iota2((C, C), 1)
    causal, strict, eye = row >= col, row > col, row == col
    tril = causal.astype(F32)
    gcb = _mm_hi(tril, jnp.broadcast_to(g, (C, C)))
    gc = jnp.sum(jnp.where(eye, gcb, 0.0), axis=1, keepdims=True)
    g_last = jnp.sum(jnp.sum(jnp.where((row == C - 1) & (col == 0), gcb, 0.0), axis=1, keepdims=True), axis=0, keepdims=True)
    gc_rows = _mm_hi(jnp.ones((C, C), F32), jnp.where(eye, gcb, 0.0))
    decay = jnp.exp(jnp.where(causal, gcb - gc_rows, NEG))

    kb = k * beta
    L = jnp.where(strict, _mm_nt(kb, k) * decay, 0.0)
    T = _neumann_inverse(L)
    u = _mm_hi(T, v * beta)
    w = _mm_hi(T, kb * jnp.exp(gc))
    a_qk = jnp.where(causal, _mm_nt(q, k) * decay, 0.0)
    q_dec = q * jnp.exp(gc)
    k_dec = k * jnp.exp(g_last - gc)
    v_new = u - _mm_nn(w, state)
    o = _mm_nn(q_dec, state) + _mm_nn(a_qk, v_new)
    new_state = state * jnp.exp(g_last) + _mm_tn(k_dec, v_new)
    y = o * lax.rsqrt(jnp.mean(o * o, -1, keepdims=True) + RMS_EPS) * ng * _silu(zz)
    return y, new_state


def _gdn_specs(H, dk, n_qkv_blocks):
    C = GDN_CHUNK
    cq = pl.BlockSpec((C, dk), lambda n, h: (n, h))
    ck = pl.BlockSpec((C, dk), lambda n, h: (n, H + h))
    cv = pl.BlockSpec((C, dk), lambda n, h: (n, 2 * H + h))
    zz = pl.BlockSpec((C, dk), lambda n, h: (n, n_qkv_blocks + h))
    ba = pl.BlockSpec((C, LANES), lambda n, h: (n, (n_qkv_blocks + H) * dk // LANES))
    return cq, ck, cv, zz, ba


def _gdn_scan_fwd(c, h, hp, ng, H, name):
    S = c.shape[0]
    dk = c.shape[1] // (3 * H)
    assert dk == LANES
    C = GDN_CHUNK
    NC = S // C

    def body(cq_ref, ck_ref, cv_ref, z_ref, ba_ref, hp_ref, ng_ref, y_ref, s_ref, state):
        n, hd = pl.program_id(0), pl.program_id(1)

        @pl.when(n == 0)
        def _():
            state[hd] = jnp.zeros((dk, dk), F32)

        st = state[hd]
        s_ref[0, 0] = st
        ba = ba_ref[...]
        y, new_state = _gdn_chunk(
            cq_ref[...], ck_ref[...], cv_ref[...], z_ref[...], _lane_pick(ba, hd), _lane_pick(ba, H + hd),
            _lane_pick(hp_ref[0:1, :], hd), _lane_pick(hp_ref[1:2, :], hd), ng_ref[...], st)
        y_ref[...] = y.astype(y_ref.dtype)
        state[hd] = new_state

    cq, ck, cv, zz, ba = _gdn_specs(H, dk, 3 * H)
    return pl.pallas_call(
        body,
        grid=(NC, H),
        in_specs=[cq, ck, cv, zz, ba, pl.BlockSpec((SUBLANES, LANES), lambda n, h: (0, 0)),
                  pl.BlockSpec((1, dk), lambda n, h: (0, 0))],
        out_specs=[pl.BlockSpec((C, dk), lambda n, h: (n, h)),
                   pl.BlockSpec((1, 1, dk, dk), lambda n, h: (n, h, 0, 0))],
        out_shape=[jax.ShapeDtypeStruct((S, H * dk), MXU_DTYPE), jax.ShapeDtypeStruct((NC, H, dk, dk), F32)],
        scratch_shapes=[pltpu.VMEM((H, dk, dk), F32)],
        compiler_params=_params(("arbitrary", "arbitrary")),
        name=name,
    )(c, c, c, h, h, hp, ng)


def _gdn_scan_bwd(c, h, hp, ng, states, dy, H, name):
    S = c.shape[0]
    dk = c.shape[1] // (3 * H)
    C = GDN_CHUNK
    NC = S // C

    def body(cq_ref, ck_ref, cv_ref, z_ref, ba_ref, hp_ref, ng_ref, s_ref, dy_ref,
             dq_ref, dk_ref, dv_ref, dz_ref, dba_ref, dhp_ref, dng_ref, dstate):
        n, hd = pl.program_id(0), pl.program_id(1)

        @pl.when(n == 0)
        def _():
            dstate[hd] = jnp.zeros((dk, dk), F32)

        @pl.when((n == 0) & (hd == 0))
        def _():
            dhp_ref[...] = jnp.zeros_like(dhp_ref)
            dng_ref[...] = jnp.zeros_like(dng_ref)

        @pl.when(hd == 0)
        def _():
            dba_ref[...] = jnp.zeros_like(dba_ref)

        ba = ba_ref[...]
        args = (cq_ref[...], ck_ref[...], cv_ref[...], z_ref[...], _lane_pick(ba, hd), _lane_pick(ba, H + hd),
                _lane_pick(hp_ref[0:1, :], hd), _lane_pick(hp_ref[1:2, :], hd), ng_ref[...], s_ref[0, 0])
        _, vjp = jax.vjp(_gdn_chunk, *args)
        dcq, dck, dcv, dzz, dbc, dac, dal, ddt, dngv, dst = vjp((dy_ref[...], dstate[hd]))
        dq_ref[...] = dcq
        dk_ref[...] = dck
        dv_ref[...] = dcv
        dz_ref[...] = dzz
        dba_ref[...] += _lane_put(dbc, hd) + _lane_put(dac, H + hd)
        dhp_ref[0:1, :] += _lane_put(dal, hd)
        dhp_ref[1:2, :] += _lane_put(ddt, hd)
        dng_ref[...] += dngv
        dstate[hd] = dst

    def rev(spec_fn):
        return lambda n, h: spec_fn(NC - 1 - n, h)

    blk = pl.BlockSpec
    in_specs = [blk((C, dk), rev(lambda n, h: (n, h))), blk((C, dk), rev(lambda n, h: (n, H + h))),
                blk((C, dk), rev(lambda n, h: (n, 2 * H + h))), blk((C, dk), rev(lambda n, h: (n, 3 * H + h))),
                blk((C, LANES), rev(lambda n, h: (n, 4 * H * dk // LANES))),
                blk((SUBLANES, LANES), lambda n, h: (0, 0)), blk((1, dk), lambda n, h: (0, 0)),
                blk((1, 1, dk, dk), rev(lambda n, h: (n, h, 0, 0))), blk((C, dk), rev(lambda n, h: (n, h)))]
    out_specs = [blk((C, dk), rev(lambda n, h: (n, h))), blk((C, dk), rev(lambda n, h: (n, h))),
                 blk((C, dk), rev(lambda n, h: (n, h))), blk((C, dk), rev(lambda n, h: (n, h))),
                 blk((C, LANES), rev(lambda n, h: (n, 0))),
                 blk((SUBLANES, LANES), lambda n, h: (0, 0)), blk((1, dk), lambda n, h: (0, 0))]
    out_shape = [jax.ShapeDtypeStruct((S, H * dk), F32)] * 4 + [
        jax.ShapeDtypeStruct((S, LANES), F32), jax.ShapeDtypeStruct((SUBLANES, LANES), F32),
        jax.ShapeDtypeStruct((1, dk), F32)]
    return pl.pallas_call(
        body,
        grid=(NC, H),
        in_specs=in_specs,
        out_specs=out_specs,
        out_shape=out_shape,
        scratch_shapes=[pltpu.VMEM((H, dk, dk), F32)],
        compiler_params=_params(("arbitrary", "arbitrary")),
        name=name,
    )(c, c, c, h, h, hp, ng, states, dy)


def _pair_rms(x, gain, dh):
    first = _iota2(x.shape, 1) < dh
    sq = x * x
    ss_a = jnp.sum(jnp.where(first, sq, 0.0), axis=1, keepdims=True)
    ss_b = jnp.sum(jnp.where(first, 0.0, sq), axis=1, keepdims=True)
    inv = jnp.where(first, lax.rsqrt(ss_a / dh + RMS_EPS), lax.rsqrt(ss_b / dh + RMS_EPS))
    return x * inv * gain


def _log_sigmoid(x):
    return jnp.minimum(x, 0.0) - jnp.log(1.0 + jnp.exp(-jnp.abs(x)))


def _cum_fn(fr, bf, carry):
    T = fr.shape[0]
    tril = (_iota2((T, T), 0) >= _iota2((T, T), 1)).astype(F32)
    c = _mm_hi(tril, _log_sigmoid(fr + bf)) + carry
    last = jnp.sum(jnp.where(_iota2(c.shape, 0) == T - 1, c, 0.0), axis=0, keepdims=True)
    return c, last


def _fox_prep_fwd(h, bf, qg, kg, H, dh, name):
    S = h.shape[0]
    W = H * dh
    assert 2 * dh == LANES and H <= LANES
    T = _pick(S, 256, LANES)
    nt = S // T
    npair = H // 2

    def body(hq_ref, hk_ref, f_ref, bf_ref, qg_ref, kg_ref, q_ref, k_ref, ccol_ref, crow_ref, carry):
        i = pl.program_id(0)

        @pl.when(i == 0)
        def _():
            carry[...] = jnp.zeros_like(carry)

        for p in range(npair):
            sl = slice(p * LANES, (p + 1) * LANES)
            q_ref[:, sl] = (_pair_rms(hq_ref[:, sl], qg_ref[...], dh) * (dh ** -0.5)).astype(q_ref.dtype)
            k_ref[:, sl] = _pair_rms(hk_ref[:, sl], kg_ref[...], dh).astype(k_ref.dtype)
        c, last = _cum_fn(f_ref[...], bf_ref[...], carry[...])
        carry[...] = last
        ct = c.T
        for hh in range(H):
            ccol_ref[hh] = c[:, hh:hh + 1]
            crow_ref[hh] = ct[hh:hh + 1, :]

    return pl.pallas_call(
        body,
        grid=(nt,),
        in_specs=[pl.BlockSpec((T, W), lambda i: (i, 0)), pl.BlockSpec((T, W), lambda i: (i, 1)),
                  pl.BlockSpec((T, LANES), lambda i: (i, 4 * W // LANES)),
                  pl.BlockSpec((1, LANES), lambda i: (0, 0)), pl.BlockSpec((1, LANES), lambda i: (0, 0)),
                  pl.BlockSpec((1, LANES), lambda i: (0, 0))],
        out_specs=[pl.BlockSpec((T, W), lambda i: (i, 0)), pl.BlockSpec((T, W), lambda i: (i, 0)),
                   pl.BlockSpec((H, T, 1), lambda i: (0, i, 0)), pl.BlockSpec((H, 1, T), lambda i: (0, 0, i))],
        out_shape=[jax.ShapeDtypeStruct((S, W), MXU_DTYPE), jax.ShapeDtypeStruct((S, W), MXU_DTYPE),
                   jax.ShapeDtypeStruct((H, S, 1), F32), jax.ShapeDtypeStruct((H, 1, S), F32)],
        scratch_shapes=[pltpu.VMEM((1, LANES), F32)],
        compiler_params=_params(("arbitrary",)),
        name=name,
    )(h, h, h, bf, qg, kg)


def _fox_prep_bwd(h, bf, qg, kg, dq, dk, dcrow, H, dh, name):
    S = h.shape[0]
    W = H * dh
    T = _pick(S, 256, LANES)
    nt = S // T
    npair = H // 2

    def body(hq_ref, hk_ref, f_ref, bf_ref, qg_ref, kg_ref, dq_ref, dk_ref, dcrow_ref,
             dhq_ref, dhk_ref, df_ref, dbf_ref, dqg_ref, dkg_ref, dcarry, dct):
        i = pl.program_id(0)

        @pl.when(i == 0)
        def _():
            dcarry[...] = jnp.zeros_like(dcarry)
            dbf_ref[...] = jnp.zeros_like(dbf_ref)
            dqg_ref[...] = jnp.zeros_like(dqg_ref)
            dkg_ref[...] = jnp.zeros_like(dkg_ref)

        for p in range(npair):
            sl = slice(p * LANES, (p + 1) * LANES)
            _, vjp = jax.vjp(lambda x, g: _pair_rms(x, g, dh) * (dh ** -0.5), hq_ref[:, sl], qg_ref[...])
            dx, dg = vjp(dq_ref[:, sl])
            dhq_ref[:, sl] = dx
            dqg_ref[...] += dg
            _, vjp = jax.vjp(lambda x, g: _pair_rms(x, g, dh), hk_ref[:, sl], kg_ref[...])
            dx, dg = vjp(dk_ref[:, sl])
            dhk_ref[:, sl] = dx
            dkg_ref[...] += dg
        dct[...] = jnp.zeros_like(dct)
        for hh in range(H):
            dct[hh:hh + 1, :] = dcrow_ref[hh]
        _, vjp = jax.vjp(lambda f, b: _cum_fn(f, b, jnp.zeros((1, LANES), F32)), f_ref[...], bf_ref[...])
        dc = dct[...].T
        df, dbf = vjp((dc, dcarry[...]))
        df_ref[...] = df
        dbf_ref[...] += dbf
        dcarry[...] = dcarry[...] + jnp.sum(dc, axis=0, keepdims=True)

    rv = lambda i: nt - 1 - i
    return pl.pallas_call(
        body,
        grid=(nt,),
        in_specs=[pl.BlockSpec((T, W), lambda i: (rv(i), 0)), pl.BlockSpec((T, W), lambda i: (rv(i), 1)),
                  pl.BlockSpec((T, LANES), lambda i: (rv(i), 4 * W // LANES)),
                  pl.BlockSpec((1, LANES), lambda i: (0, 0)), pl.BlockSpec((1, LANES), lambda i: (0, 0)),
                  pl.BlockSpec((1, LANES), lambda i: (0, 0)),
                  pl.BlockSpec((T, W), lambda i: (rv(i), 0)), pl.BlockSpec((T, W), lambda i: (rv(i), 0)),
                  pl.BlockSpec((H, 1, T), lambda i: (0, 0, rv(i)))],
        out_specs=[pl.BlockSpec((T, W), lambda i: (rv(i), 0)), pl.BlockSpec((T, W), lambda i: (rv(i), 0)),
                   pl.BlockSpec((T, LANES), lambda i: (rv(i), 0)),
                   pl.BlockSpec((1, LANES), lambda i: (0, 0)), pl.BlockSpec((1, LANES), lambda i: (0, 0)),
                   pl.BlockSpec((1, LANES), lambda i: (0, 0))],
        out_shape=[jax.ShapeDtypeStruct((S, W), F32), jax.ShapeDtypeStruct((S, W), F32),
                   jax.ShapeDtypeStruct((S, LANES), F32)] + [jax.ShapeDtypeStruct((1, LANES), F32)] * 3,
        scratch_shapes=[pltpu.VMEM((1, LANES), F32), pltpu.VMEM((LANES, T), F32)],
        compiler_params=_params(("arbitrary",)),
        name=name,
    )(h, h, h, bf, qg, kg, dq, dk, dcrow)


def _head_masks(dh):
    first = _iota2((1, LANES), 1) < dh
    return first, jnp.logical_not(first)


def _flash_scores(q_m, k, ccol, crow, i, j, tq, tk):
    s = _dot(q_m, k, _NT) + ccol - crow
    qpos = i * tq + _iota2((tq, tk), 0)
    kpos = j * tk + _iota2((tq, tk), 1)
    return jnp.where(kpos <= qpos, s, NEG)


def _flash_fwd(q, k, h, ccol, crow, H, dh, name):
    S, W = q.shape
    tq = tk = _pick(S, 512, LANES)
    nq = S // tq
    npair = H // 2
    vblk = 2 * W // LANES

    def body(q_ref, k_ref, v_ref, ccol_ref, crow_ref, o_ref, lse_ref, m_s, l_s, acc_s):
        i, j = pl.program_id(1), pl.program_id(2)

        @pl.when(j == 0)
        def _():
            m_s[...] = jnp.full_like(m_s, NEG)
            l_s[...] = jnp.zeros_like(l_s)
            acc_s[...] = jnp.zeros_like(acc_s)

        @pl.when(j <= i)
        def _():
            qv, kv, vv = q_ref[...], k_ref[...], v_ref[...].astype(MXU_DTYPE)
            for a, mask in enumerate(_head_masks(dh)):
                s = _flash_scores(jnp.where(mask, qv, 0), kv, ccol_ref[a], crow_ref[a], i, j, tq, tk)
                m_old = m_s[a]
                m_new = jnp.maximum(m_old, jnp.max(s, axis=1, keepdims=True))
                alpha = jnp.exp(m_old - m_new)
                p = jnp.exp(s - m_new)
                l_s[a] = alpha * l_s[a] + jnp.sum(p, axis=1, keepdims=True)
                acc_s[a] = alpha * acc_s[a] + _dot(p, vv, _NN)
                m_s[a] = m_new

        @pl.when(j == nq - 1)
        def _():
            first, _ = _head_masks(dh)
            o_ref[...] = jnp.where(first, acc_s[0] / l_s[0], acc_s[1] / l_s[1])
            for a in range(2):
                lse_ref[a] = m_s[a] + jnp.log(l_s[a])

    kj = lambda i, j: jnp.minimum(i, j)
    return pl.pallas_call(
        body,
        grid=(npair, nq, nq),
        in_specs=[pl.BlockSpec((tq, LANES), lambda p, i, j: (i, p)),
                  pl.BlockSpec((tk, LANES), lambda p, i, j: (kj(i, j), p)),
                  pl.BlockSpec((tk, LANES), lambda p, i, j: (kj(i, j), vblk + p)),
                  pl.BlockSpec((2, tq, 1), lambda p, i, j: (p, i, 0)),
                  pl.BlockSpec((2, 1, tk), lambda p, i, j: (p, 0, kj(i, j)))],
        out_specs=[pl.BlockSpec((tq, LANES), lambda p, i, j: (i, p)),
                   pl.BlockSpec((2, tq, 1), lambda p, i, j: (p, i, 0))],
        out_shape=[jax.ShapeDtypeStruct((S, W), F32), jax.ShapeDtypeStruct((H, S, 1), F32)],
        scratch_shapes=[pltpu.VMEM((2, tq, 1), F32), pltpu.VMEM((2, tq, 1), F32), pltpu.VMEM((2, tq, LANES), F32)],
        compiler_params=_params(("parallel", "parallel", "arbitrary")),
        name=name,
    )(q, k, h, ccol, crow)


def _flash_bwd_kv(q, k, h, ccol, crow, lse, delta, resid, do, H, dh, name):
    S, W = q.shape
    tq = tk = _pick(S, 512, LANES)
    nq = S // tq
    npair = H // 2
    vblk = 2 * W // LANES

    def body(q_ref, k_ref, v_ref, ccol_ref, crow_ref, lse_ref, dl_ref, rs_ref, do_ref, dk_ref, dv_ref, dcr_ref):
        j, i = pl.program_id(1), pl.program_id(2)

        @pl.when(i == 0)
        def _():
            dk_ref[...] = jnp.zeros_like(dk_ref)
            dv_ref[...] = jnp.zeros_like(dv_ref)
            dcr_ref[...] = jnp.zeros_like(dcr_ref)

        @pl.when(i >= j)
        def _():
            qv, kv, vv = q_ref[...], k_ref[...], v_ref[...].astype(MXU_DTYPE)
            dov = do_ref[...].astype(MXU_DTYPE)
            for a, mask in enumerate(_head_masks(dh)):
                q_m = jnp.where(mask, qv, 0)
                do_m = jnp.where(mask, dov, 0)
                s = _flash_scores(q_m, kv, ccol_ref[a], crow_ref[a], i, j, tq, tk)
                p = jnp.exp(s - lse_ref[a])
                dv_ref[...] += _dot(p, do_m, _TN)
                ds = p * (_dot(do_m, vv, _NT) - (dl_ref[a] + rs_ref[a]))
                dk_ref[...] += _dot(ds, q_m, _TN)
                dcr_ref[a] -= jnp.sum(ds, axis=0, keepdims=True)

    qi = lambda j, i: jnp.maximum(i, j)
    return pl.pallas_call(
        body,
        grid=(npair, nq, nq),
        in_specs=[pl.BlockSpec((tq, LANES), lambda p, j, i: (qi(j, i), p)),
                  pl.BlockSpec((tk, LANES), lambda p, j, i: (j, p)),
                  pl.BlockSpec((tk, LANES), lambda p, j, i: (j, vblk + p)),
                  pl.BlockSpec((2, tq, 1), lambda p, j, i: (p, qi(j, i), 0)),
                  pl.BlockSpec((2, 1, tk), lambda p, j, i: (p, 0, j)),
                  pl.BlockSpec((2, tq, 1), lambda p, j, i: (p, qi(j, i), 0)),
                  pl.BlockSpec((2, tq, 1), lambda p, j, i: (p, qi(j, i), 0)),
                  pl.BlockSpec((2, tq, 1), lambda p, j, i: (p, qi(j, i), 0)),
                  pl.BlockSpec((tq, LANES), lambda p, j, i: (qi(j, i), p))],
        out_specs=[pl.BlockSpec((tk, LANES), lambda p, j, i: (j, p)),
                   pl.BlockSpec((tk, LANES), lambda p, j, i: (j, p)),
                   pl.BlockSpec((2, 1, tk), lambda p, j, i: (p, 0, j))],
        out_shape=[jax.ShapeDtypeStruct((S, W), F32), jax.ShapeDtypeStruct((S, W), F32),
                   jax.ShapeDtypeStruct((H, 1, S), F32)],
        compiler_params=_params(("parallel", "parallel", "arbitrary")),
        name=name,
    )(q, k, h, ccol, crow, lse, delta, resid, do)


def _flash_bwd_q(q, k, h, ccol, crow, lse, delta, do, H, dh, name):
    S, W = q.shape
    tq = tk = _pick(S, 512, LANES)
    nq = S // tq
    npair = H // 2
    vblk = 2 * W // LANES

    def body(q_ref, k_ref, v_ref, ccol_ref, crow_ref, lse_ref, dl_ref, do_ref, dq_ref, rs_ref):
        i, j = pl.program_id(1), pl.program_id(2)

        @pl.when(j == 0)
        def _():
            dq_ref[...] = jnp.zeros_like(dq_ref)
            rs_ref[...] = jnp.zeros_like(rs_ref)

        @pl.when(j <= i)
        def _():
            qv, kv, vv = q_ref[...], k_ref[...], v_ref[...].astype(MXU_DTYPE)
            dov = do_ref[...].astype(MXU_DTYPE)
            for a, mask in enumerate(_head_masks(dh)):
                s = _flash_scores(jnp.where(mask, qv, 0), kv, ccol_ref[a], crow_ref[a], i, j, tq, tk)
                p = jnp.exp(s - lse_ref[a])
                ds = p * (_dot(jnp.where(mask, dov, 0), vv, _NT) - dl_ref[a])
                dq_ref[...] += _dot(ds, jnp.where(mask, kv, 0), _NN)
                rs_ref[a] += jnp.sum(ds, axis=1, keepdims=True)

    kj = lambda i, j: jnp.minimum(i, j)
    return pl.pallas_call(
        body,
        grid=(npair, nq, nq),
        in_specs=[pl.BlockSpec((tq, LANES), lambda p, i, j: (i, p)),
                  pl.BlockSpec((tk, LANES), lambda p, i, j: (kj(i, j), p)),
                  pl.BlockSpec((tk, LANES), lambda p, i, j: (kj(i, j), vblk + p)),
                  pl.BlockSpec((2, tq, 1), lambda p, i, j: (p, i, 0)),
                  pl.BlockSpec((2, 1, tk), lambda p, i, j: (p, 0, kj(i, j))),
                  pl.BlockSpec((2, tq, 1), lambda p, i, j: (p, i, 0)),
                  pl.BlockSpec((2, tq, 1), lambda p, i, j: (p, i, 0)),
                  pl.BlockSpec((tq, LANES), lambda p, i, j: (i, p))],
        out_specs=[pl.BlockSpec((tq, LANES), lambda p, i, j: (i, p)),
                   pl.BlockSpec((2, tq, 1), lambda p, i, j: (p, i, 0))],
        out_shape=[jax.ShapeDtypeStruct((S, W), F32), jax.ShapeDtypeStruct((H, S, 1), F32)],
        compiler_params=_params(("parallel", "parallel", "arbitrary")),
        name=name,
    )(q, k, h, ccol, crow, lse, delta, do)


def _fox_gate_fwd(o, h, W, name):
    def fn(rows, _, __):
        return [rows[0] * _silu(rows[1])], [], []

    return _rowwise(fn, [o, (h, W, 3)], [], [(W, MXU_DTYPE)], [], tile=256, name=name)[0]


def _fox_gate_bwd(o, h, dog, H, dh, name):
    S, W = o.shape
    T = _pick(S, 256, SUBLANES)

    def body(o_ref, z_ref, d_ref, do_ref, dz_ref, dl_ref):
        ov, zv, dv = o_ref[...], z_ref[...], d_ref[...]
        sg = _sigmoid(zv)
        do = dv * zv * sg
        do_ref[...] = do
        dz_ref[...] = dv * ov * sg * (1.0 + zv * (1.0 - sg))
        prod = do * ov
        for p in range(H // 2):
            blk = prod[:, p * LANES:(p + 1) * LANES]
            first = _iota2(blk.shape, 1) < dh
            dl_ref[2 * p] = jnp.sum(jnp.where(first, blk, 0.0), axis=1, keepdims=True)
            dl_ref[2 * p + 1] = jnp.sum(jnp.where(first, 0.0, blk), axis=1, keepdims=True)

    return pl.pallas_call(
        body,
        grid=(S // T,),
        in_specs=[pl.BlockSpec((T, W), lambda i: (i, 0)), pl.BlockSpec((T, W), lambda i: (i, 3)),
                  pl.BlockSpec((T, W), lambda i: (i, 0))],
        out_specs=[pl.BlockSpec((T, W), lambda i: (i, 0)), pl.BlockSpec((T, W), lambda i: (i, 0)),
                   pl.BlockSpec((H, T, 1), lambda i: (0, i, 0))],
        out_shape=[jax.ShapeDtypeStruct((S, W), F32), jax.ShapeDtypeStruct((S, W), F32),
                   jax.ShapeDtypeStruct((H, S, 1), F32)],
        compiler_params=_params(("parallel",)),
        name=name,
    )(o, h, dog)


def _gdn_layer_fwd(x, w, tag):
    H = w["H"]
    qk = H * LANES
    h = _matmul(x, w["w_in"], "nn", name=f"{tag}_in")
    c = _conv_fwd(h, w["conv"], 3 * qk, name=f"{tag}_conv")
    og, states = _gdn_scan_fwd(c, h, w["hp"], w["ng"], H, name=f"{tag}_scan")
    y = _matmul(og, w["w_out"], "nn", name=f"{tag}_out")
    return y, (x, h, c, states, og)


def _gdn_layer_bwd(dy, res, w, tag):
    x, h, c, states, og = res
    H = w["H"]
    qk = H * LANES
    dog = _matmul(dy, w["w_out"], "nt", name=f"{tag}_out_dx")
    dw_out = _matmul(og, dy, "tn", name=f"{tag}_out_dw")
    dq, dk, dv, dz, dba, dhp, dng = _gdn_scan_bwd(c, h, w["hp"], w["ng"], states, dog, H, name=f"{tag}_scan_bwd")
    dh_parts, dconv = [], []
    for part, d in enumerate((dq, dk, dv)):
        dh_p, dw_p = _conv_bwd(d, h, w["conv"][:, part * qk:(part + 1) * qk], part * qk, name=f"{tag}_conv_bwd{part}")
        dh_parts.append(dh_p)
        dconv.append(dw_p[:GDN_CONV])
    dh = jnp.concatenate(dh_parts + [dz, dba], axis=1)
    dx = _matmul(dh, w["w_in"], "nt", name=f"{tag}_in_dx")
    dw_in = _matmul(x, dh, "tn", name=f"{tag}_in_dw")
    grads = {"w_in": dw_in, "w_out": dw_out, "conv": jnp.concatenate(dconv, axis=1),
             "a_log": dhp[0, :H], "dt_bias": dhp[1, :H], "norm_g": dng[0]}
    return dx, grads


def _fox_layer_fwd(x, w, tag):
    H, dh = w["H"], w["dh"]
    W = H * dh
    h = _matmul(x, w["w_in"], "nn", name=f"{tag}_in")
    q, k, ccol, crow = _fox_prep_fwd(h, w["bf"], w["qg"], w["kg"], H, dh, name=f"{tag}_prep")
    o, lse = _flash_fwd(q, k, h, ccol, crow, H, dh, name=f"{tag}_flash")
    og = _fox_gate_fwd(o, h, W, name=f"{tag}_gate")
    y = _matmul(og, w["w_out"], "nn", name=f"{tag}_out")
    return y, (x, h, q, k, ccol, crow, o, lse, og)


def _fox_layer_bwd(dy, res, w, tag):
    x, h, q, k, ccol, crow, o, lse, og = res
    H, dh = w["H"], w["dh"]
    dog = _matmul(dy, w["w_out"], "nt", name=f"{tag}_out_dx")
    dw_out = _matmul(og, dy, "tn", name=f"{tag}_out_dw")
    do, dz, delta = _fox_gate_bwd(o, h, dog, H, dh, name=f"{tag}_gate_bwd")
    dqq, resid = _flash_bwd_q(q, k, h, ccol, crow, lse, delta, do, H, dh, name=f"{tag}_flash_bwd_q")
    dkk, dvv, dcrow = _flash_bwd_kv(q, k, h, ccol, crow, lse, delta, resid, do, H, dh, name=f"{tag}_flash_bwd_kv")
    dhq, dhk, df, dbf, dqg, dkg = _fox_prep_bwd(h, w["bf"], w["qg"], w["kg"], dqq, dkk, dcrow, H, dh, name=f"{tag}_prep_bwd")
    dhh = jnp.concatenate([dhq, dhk, dvv, dz, df], axis=1)
    dx = _matmul(dhh, w["w_in"], "nt", name=f"{tag}_in_dx")
    dw_in = _matmul(x, dhh, "tn", name=f"{tag}_in_dw")
    grads = {"w_in": dw_in, "w_out": dw_out, "b_f": dbf[0, :H],
             "q_norm_g": dqg[0, :dh] + dqg[0, dh:], "k_norm_g": dkg[0, :dh] + dkg[0, dh:]}
    return dx, grads


def _pad_cols(w, n):
    return jnp.pad(w, ((0, 0), (0, n - w.shape[1])))


def _build_layers(full, small):
    depth = small["ln_g"].shape[0]
    gh = small["gdn_a_log"].shape[1]
    fh, dh = small["fox_b_f"].shape[1], small["fox_q_norm_g"].shape[1]
    layers = []
    for i in range(depth):
        j = i // 2
        w = {"ln_g": small["ln_g"][i][None], "ln_b": small["ln_b"][i][None],
             "w_gate": full["ple_w_gate"][i], "w_proj": full["ple_w_proj"][i]}
        if i % 2 == 0:
            hp = jnp.zeros((SUBLANES, LANES), F32).at[0, :gh].set(small["gdn_a_log"][j]).at[1, :gh].set(small["gdn_dt_bias"][j])
            w.update(kind="gdn", H=gh, w_in=_pad_cols(full["gdn_w_in"][j], 4 * gh * LANES + LANES),
                     conv=full["gdn_conv_w"][j], hp=hp, ng=small["gdn_norm_g"][j][None], w_out=full["gdn_w_out"][j])
        else:
            bf = jnp.zeros((1, LANES), F32).at[0, :fh].set(small["fox_b_f"][j])
            w.update(kind="fox", H=fh, dh=dh, w_in=_pad_cols(full["fox_w_in"][j], 4 * fh * dh + LANES), bf=bf,
                     qg=jnp.tile(small["fox_q_norm_g"][j], 2)[None], kg=jnp.tile(small["fox_k_norm_g"][j], 2)[None],
                     w_out=full["fox_w_out"][j])
        layers.append(w)
    return layers


def _local_step(x, p, target, layers):
    depth = len(layers)
    alpha = (2 * depth) ** 0.25
    saved = []
    for i, w in enumerate(layers):
        tag = f"l{i}"
        if w["kind"] == "gdn":
            y, res = _gdn_layer_fwd(x, w, tag)
        else:
            y, res = _fox_layer_fwd(x, w, tag)
        x1 = _ln_fwd(x, y, w["ln_g"], w["ln_b"], alpha, name=f"{tag}_ln")
        gate_pre = _matmul(x1, w["w_gate"], "nn", name=f"{tag}_gate_mm")
        pp = _matmul(p[i], w["w_proj"], "nn", name=f"{tag}_proj_mm")
        x2 = _ple_fwd(x1, gate_pre, pp, name=f"{tag}_ple")
        saved.append((res, x, y, x1, gate_pre, pp))
        x = x2
    dx, loss_tile = _loss_head(x, target, name="loss_head")
    grads = [None] * depth
    for i in reversed(range(depth)):
        w = layers[i]
        tag = f"l{i}"
        res, xin, y, x1, gate_pre, pp = saved[i]
        dgp, dpp = _ple_bwd(dx, gate_pre, pp, name=f"{tag}_ple_bwd")
        dx1 = _add(dx, _matmul(dgp, w["w_gate"], "nt", name=f"{tag}_gate_dx"), name=f"{tag}_dx1")
        dw_gate = _matmul(x1, dgp, "tn", name=f"{tag}_gate_dw")
        dw_proj = _matmul(p[i], dpp, "tn", name=f"{tag}_proj_dw")
        dy, dg, db = _ln_bwd(xin, y, w["ln_g"], w["ln_b"], dx1, alpha, name=f"{tag}_ln_bwd")
        if w["kind"] == "gdn":
            dxm, g = _gdn_layer_bwd(dy, res, w, tag)
        else:
            dxm, g = _fox_layer_bwd(dy, res, w, tag)
        dx = _axpy(alpha, dy, dxm, name=f"{tag}_dx")
        g.update({"w_gate": dw_gate, "w_proj": dw_proj, "ln_g": dg[0], "ln_b": db[0]})
        grads[i] = g
    return loss_tile, dx, grads


MESH_ID = pl.DeviceIdType.MESH
HBM_SPEC = pl.BlockSpec(memory_space=pl.ANY)
PACK_COLS = 1024
PACK_ROWS = 256


def _all_gather(shard, name):
    R, C = shard.shape

    def body(x_ref, out_ref, send_sems, recv_sems, local_sem):
        x, y, c = lax.axis_index("x"), lax.axis_index("y"), lax.axis_index("c")
        me, sibling = (x, y, c), (x, y, 1 - c)
        chips = [(1 - x, y), (x, 1 - y), (1 - x, 1 - y)]

        def slot(px, py, pc):
            return out_ref.at[4 * px + 2 * py + pc]

        def copy(k, block, to, src=None):
            return pltpu.make_async_remote_copy(
                src_ref=slot(*block) if src is None else src, dst_ref=slot(*block),
                send_sem=send_sems.at[k], recv_sem=recv_sems.at[k], device_id=to, device_id_type=MESH_ID)

        mine = pltpu.make_async_copy(x_ref, slot(*me), local_sem)
        mine.start()
        first = [copy(0, me, sibling, src=x_ref)]
        first += [copy(1 + j, me, (*chip, c), src=x_ref) for j, chip in enumerate(chips)]
        for cp in first:
            cp.start()
        passed = [copy(4 + j, (*chip, c), sibling) for j, chip in enumerate(chips)]
        for j, chip in enumerate(chips):
            copy(1 + j, (*chip, c), me).wait_recv()
            passed[j].start()
        copy(0, sibling, me).wait_recv()
        for j, chip in enumerate(chips):
            copy(4 + j, (*chip, 1 - c), me).wait_recv()
        for cp in first + passed:
            cp.wait_send()
        mine.wait()

    return pl.pallas_call(
        body,
        out_shape=jax.ShapeDtypeStruct((N_DEV, R, C), shard.dtype),
        in_specs=[HBM_SPEC],
        out_specs=HBM_SPEC,
        scratch_shapes=[pltpu.SemaphoreType.DMA((7,)), pltpu.SemaphoreType.DMA((7,)), pltpu.SemaphoreType.DMA],
        name=name,
    )(shard)


def _all_to_all(slabs, name):
    _, R, C = slabs.shape

    def body(g_ref, out_ref, send_sems, recv_sems, local_sem):
        x, y, c = lax.axis_index("x"), lax.axis_index("y"), lax.axis_index("c")
        me = 4 * x + 2 * y + c
        mine = pltpu.make_async_copy(g_ref.at[me], out_ref.at[me], local_sem)
        mine.start()
        copies = []
        for k in range(1, N_DEV):
            px = 1 - x if k & 4 else x
            py = 1 - y if k & 2 else y
            pc = 1 - c if k & 1 else c
            peer = 4 * px + 2 * py + pc
            copies.append((
                pltpu.make_async_remote_copy(src_ref=g_ref.at[peer], dst_ref=out_ref.at[me], send_sem=send_sems.at[k - 1],
                                             recv_sem=recv_sems.at[k - 1], device_id=(px, py, pc), device_id_type=MESH_ID),
                pltpu.make_async_remote_copy(src_ref=g_ref.at[peer], dst_ref=out_ref.at[peer], send_sem=send_sems.at[k - 1],
                                             recv_sem=recv_sems.at[k - 1], device_id=(px, py, pc), device_id_type=MESH_ID)))
        for send, _ in copies:
            send.start()
        for send, arrive in copies:
            arrive.wait_recv()
            send.wait_send()
        mine.wait()

    return pl.pallas_call(
        body,
        out_shape=jax.ShapeDtypeStruct(slabs.shape, slabs.dtype),
        in_specs=[HBM_SPEC],
        out_specs=HBM_SPEC,
        scratch_shapes=[pltpu.SemaphoreType.DMA((7,)), pltpu.SemaphoreType.DMA((7,)), pltpu.SemaphoreType.DMA],
        name=name,
    )(slabs)


def _sum_slots(parts, name):
    _, R, C = parts.shape
    tr = _pick(R, PACK_ROWS, SUBLANES)

    def body(p_ref, o_ref):
        acc = p_ref[0]
        for s in range(1, N_DEV):
            acc = acc + p_ref[s]
        o_ref[...] = acc

    return pl.pallas_call(
        body,
        grid=(R // tr,),
        in_specs=[pl.BlockSpec((N_DEV, tr, C), lambda i: (0, i, 0))],
        out_specs=pl.BlockSpec((tr, C), lambda i: (i, 0)),
        out_shape=jax.ShapeDtypeStruct((R, C), parts.dtype),
        compiler_params=_params(("parallel",)),
        name=name,
    )(parts)


def _pack(flats, dtype):
    flat = jnp.concatenate([f.astype(dtype).reshape(-1) for f in flats])
    unit = PACK_ROWS * PACK_COLS
    n = -(-flat.shape[0] // unit) * unit
    return jnp.pad(flat, (0, n - flat.shape[0])).reshape(n // PACK_COLS, PACK_COLS)


def _pack_slabs(per_dest, dtype=F32):
    flat = jnp.concatenate([a.astype(dtype).reshape(N_DEV, -1) for a in per_dest], axis=1)
    unit = PACK_ROWS * PACK_COLS
    n = -(-flat.shape[1] // unit) * unit
    return jnp.pad(flat, ((0, 0), (0, n - flat.shape[1]))).reshape(N_DEV, n // PACK_COLS, PACK_COLS)


def _unpack(buf, shapes):
    lead = buf.shape[:-2]
    flat = buf.reshape(*lead, -1)
    out, off = [], 0
    for s in shapes:
        n = math.prod(s)
        out.append(flat[..., off:off + n].reshape(*lead, *s))
        off += n
    return out


_ROW_SPLIT = ("ple_w_gate", "gdn_w_out", "fox_w_out")
_COL_SPLIT = ("ple_w_proj", "gdn_w_in", "gdn_conv_w", "fox_w_in")
_SHARDED = ("ple_w_gate", "ple_w_proj", "gdn_w_in", "gdn_conv_w", "gdn_w_out", "fox_w_in", "fox_w_out")
_REPLICATED = ("ln_g", "ln_b", "gdn_a_log", "gdn_dt_bias", "gdn_norm_g", "fox_b_f", "fox_q_norm_g", "fox_k_norm_g")
_WEIGHTS = ("ln_g", "ln_b", "ple_w_gate", "ple_w_proj", "gdn_w_in", "gdn_conv_w", "gdn_a_log", "gdn_dt_bias",
            "gdn_norm_g", "gdn_w_out", "fox_w_in", "fox_b_f", "fox_q_norm_g", "fox_k_norm_g", "fox_w_out")


def _join(name, gathered):
    n, l, a, b = gathered.shape
    if name in _ROW_SPLIT:
        return gathered.transpose(1, 0, 2, 3).reshape(l, n * a, b)
    return gathered.transpose(1, 2, 0, 3).reshape(l, a, n * b)


def _split(name, full):
    l, a, b = full.shape
    if name in _ROW_SPLIT:
        return full.reshape(l, N_DEV, a // N_DEV, b).transpose(1, 0, 2, 3)
    return full.reshape(l, a, N_DEV, b // N_DEV).transpose(2, 0, 1, 3)


def _adamw(w, g, m, v, name):
    shape = w.shape
    two_d = (math.prod(shape[:-1]), shape[-1])
    c1 = 1.0 - ADAM_B1 ** ADAM_STEP
    c2 = 1.0 - ADAM_B2 ** ADAM_STEP

    def fn(rows, _, __):
        wv, gv, mv, vv = rows
        mn = ADAM_B1 * mv + (1.0 - ADAM_B1) * gv
        vn = ADAM_B2 * vv + (1.0 - ADAM_B2) * jnp.square(gv)
        delta = -ADAM_LR * ((mn / c1) / (jnp.sqrt(vn / c2) + ADAM_EPS) + ADAM_WD * wv)
        return [delta, mn, vn], [], []

    outs = _rowwise(fn, [a.reshape(two_d) for a in (w, g, m, v)], [], [(two_d[1], F32)] * 3, [], tile=256, name=name)
    return [o.reshape(shape) for o in outs]


def _train_step(x, p, target, w, m, v):
    matmul_names = [n for n in _SHARDED if n != "gdn_conv_w"]
    gathered = _all_gather(_pack([w[n] for n in matmul_names], MXU_DTYPE), name="gather_weights")
    parts = _unpack(gathered, [w[n].shape for n in matmul_names])
    full = {n: _join(n, part) for n, part in zip(matmul_names, parts)}
    conv = _all_gather(_pack([w["gdn_conv_w"]], F32), name="gather_conv")
    full["gdn_conv_w"] = _join("gdn_conv_w", _unpack(conv, [w["gdn_conv_w"].shape])[0])
    layers = _build_layers(full, {n: w[n] for n in _REPLICATED})

    loss_tile, dx, grads = _local_step(x[0], p[:, 0], target[0], layers)
    loss = lax.psum(loss_tile[0, 0], ("x", "y", "c"))

    depth = len(layers)
    gdn_l = [i for i in range(depth) if i % 2 == 0]
    fox_l = [i for i in range(depth) if i % 2 == 1]

    def stack(key, idx):
        return jnp.stack([grads[i][key] for i in idx])

    full_g = {
        "ple_w_gate": stack("w_gate", range(depth)), "ple_w_proj": stack("w_proj", range(depth)),
        "gdn_w_in": stack("w_in", gdn_l)[..., :w["gdn_w_in"].shape[-1] * N_DEV], "gdn_conv_w": stack("conv", gdn_l),
        "gdn_w_out": stack("w_out", gdn_l),
        "fox_w_in": stack("w_in", fox_l)[..., :w["fox_w_in"].shape[-1] * N_DEV], "fox_w_out": stack("w_out", fox_l)}
    small_g = {
        "ln_g": stack("ln_g", range(depth)), "ln_b": stack("ln_b", range(depth)),
        "gdn_a_log": stack("a_log", gdn_l), "gdn_dt_bias": stack("dt_bias", gdn_l), "gdn_norm_g": stack("norm_g", gdn_l),
        "fox_b_f": stack("b_f", fox_l), "fox_q_norm_g": stack("q_norm_g", fox_l), "fox_k_norm_g": stack("k_norm_g", fox_l)}

    exchanged = _all_to_all(_pack_slabs([_split(n, full_g[n]) for n in _SHARDED]), name="scatter_grads")
    summed = _sum_slots(exchanged, name="sum_grads")
    g = dict(zip(_SHARDED, _unpack(summed, [w[n].shape for n in _SHARDED])))
    small_all = _all_gather(_pack([small_g[n] for n in _REPLICATED], F32), name="gather_small_grads")
    small_sum = _sum_slots(small_all, name="sum_small_grads")
    g.update(zip(_REPLICATED, _unpack(small_sum, [w[n].shape for n in _REPLICATED])))

    delta, new_m, new_v = {}, {}, {}
    for n in _WEIGHTS:
        delta[n], new_m[n], new_v[n] = _adamw(w[n], g[n], m[n], v[n], name=f"adamw_{n}")
    return (loss, dx[None], *[g[n] for n in _WEIGHTS], *[delta[n] for n in _WEIGHTS],
            *[new_m[n] for n in _WEIGHTS], *[new_v[n] for n in _WEIGHTS])


def kernel(x, p, ln_g, ln_b, ple_w_gate, ple_w_proj, gdn_w_in, gdn_conv_w, gdn_a_log, gdn_dt_bias, gdn_norm_g, gdn_w_out, fox_w_in, fox_b_f, fox_q_norm_g, fox_k_norm_g, fox_w_out, loss_target, m_ln_g, m_ln_b, m_ple_w_gate, m_ple_w_proj, m_gdn_w_in, m_gdn_conv_w, m_gdn_a_log, m_gdn_dt_bias, m_gdn_norm_g, m_gdn_w_out, m_fox_w_in, m_fox_b_f, m_fox_q_norm_g, m_fox_k_norm_g, m_fox_w_out, v_ln_g, v_ln_b, v_ple_w_gate, v_ple_w_proj, v_gdn_w_in, v_gdn_conv_w, v_gdn_a_log, v_gdn_dt_bias, v_gdn_norm_g, v_gdn_w_out, v_fox_w_in, v_fox_b_f, v_fox_q_norm_g, v_fox_k_norm_g, v_fox_w_out):
    given = dict(locals())
    w = {n: given[n] for n in _WEIGHTS}
    m = {n: given["m_" + n] for n in _WEIGHTS}
    v = {n: given["v_" + n] for n in _WEIGHTS}
    return _train_step(x, p, loss_target, w, m, v)
```

```python
import functools
import math

import jax
import jax.numpy as jnp
from jax import lax
from jax.experimental import pallas as pl
from jax.experimental.pallas import tpu as pltpu

F32 = jnp.float32
BF16 = jnp.bfloat16
MXU_DTYPE = BF16
HI = lax.Precision.HIGHEST

N_DEV = 8
LANES = 128
SUBLANES = 8
VMEM_BYTES = 64 * 1024 * 1024

GDN_CHUNK = 64
GDN_CONV = 4
LN_EPS = 1e-5
RMS_EPS = 1e-6
NEG = -1e30

ADAM_LR = 0.001
ADAM_B1 = 0.9
ADAM_B2 = 0.999
ADAM_EPS = 1e-08
ADAM_WD = 0.01
ADAM_STEP = 10


def _params(semantics, vmem_mb=40):
    return pltpu.CompilerParams(dimension_semantics=semantics, vmem_limit_bytes=vmem_mb * 1024 * 1024)


def _pick(dim, cap, unit=LANES):
    if dim <= cap:
        return dim
    best = None
    for t in range(unit, cap + 1, unit):
        if dim % t == 0:
            best = t
    assert best is not None, (dim, cap)
    return best


def _dims(dims, ndim):
    if ndim == 2:
        return (dims, ((), ()))
    return (((dims[0][0] + 1,), (dims[1][0] + 1,)), ((0,), (0,)))


def _dot(a, b, dims):
    return lax.dot_general(a.astype(MXU_DTYPE), b.astype(MXU_DTYPE), _dims(dims, a.ndim), preferred_element_type=F32)


_NN = ((1,), (0,))
_NT = ((1,), (1,))
_TN = ((0,), (0,))


@jax.custom_vjp
def _mm_nn(a, b):
    return _dot(a, b, _NN)


@jax.custom_vjp
def _mm_nt(a, b):
    return _dot(a, b, _NT)


@jax.custom_vjp
def _mm_tn(a, b):
    return _dot(a, b, _TN)


_mm_nn.defvjp(lambda a, b: (_dot(a, b, _NN), (a, b)), lambda r, g: (_mm_nt(g, r[1]), _mm_tn(r[0], g)))
_mm_nt.defvjp(lambda a, b: (_dot(a, b, _NT), (a, b)), lambda r, g: (_mm_nn(g, r[1]), _mm_tn(g, r[0])))
_mm_tn.defvjp(lambda a, b: (_dot(a, b, _TN), (a, b)), lambda r, g: (_mm_nt(r[1], g), _mm_nn(r[0], g)))


def _mm_hi(a, b):
    return lax.dot_general(a, b, _dims(_NN, a.ndim), precision=HI, preferred_element_type=F32)


def _sigmoid(x):
    return 1.0 / (1.0 + jnp.exp(-x))


def _silu(x):
    return x * _sigmoid(x)


def _softplus(x):
    return jnp.maximum(x, 0.0) + jnp.log(1.0 + jnp.exp(-jnp.abs(x)))


def _iota2(shape, dim):
    return lax.broadcasted_iota(jnp.int32, shape, dim)


def _lane_pick(tile, lane):
    return jnp.sum(jnp.where(_iota2(tile.shape, 1) == lane, tile, 0.0), axis=1, keepdims=True)


def _lane_put(col, lane, width=LANES):
    return jnp.where(_iota2((col.shape[0], width), 1) == lane, col, 0.0)


def _matmul(a, b, mode, out_dtype=F32, *, name, tm=512, tn=1408, tk=1408, a_cols=None, b_cols=None):
    def cols(arr, rng):
        return (0, arr.shape[1]) if rng is None else rng

    a0, an = cols(a, a_cols)
    b0, bn = cols(b, b_cols)
    if mode == "nn":
        M, K, N = a.shape[0], an, bn
        assert b.shape[0] == K
    elif mode == "nt":
        M, K, N = a.shape[0], an, b.shape[0]
        assert bn == K
    else:
        K, M, N = a.shape[0], an, bn
        assert b.shape[0] == K
    tm, tn, tk = _pick(M, tm), _pick(N, tn), _pick(K, tk)
    nk = K // tk
    if mode == "nn":
        assert a0 % tk == 0 and b0 % tn == 0
        a_spec = pl.BlockSpec((tm, tk), lambda i, j, k: (i, a0 // tk + k))
        b_spec = pl.BlockSpec((tk, tn), lambda i, j, k: (k, b0 // tn + j))
        dims = _NN
    elif mode == "nt":
        assert a0 % tk == 0 and b0 % tk == 0
        a_spec = pl.BlockSpec((tm, tk), lambda i, j, k: (i, a0 // tk + k))
        b_spec = pl.BlockSpec((tn, tk), lambda i, j, k: (j, b0 // tk + k))
        dims = _NT
    else:
        assert a0 % tm == 0 and b0 % tn == 0
        a_spec = pl.BlockSpec((tk, tm), lambda i, j, k: (k, a0 // tm + i))
        b_spec = pl.BlockSpec((tk, tn), lambda i, j, k: (k, b0 // tn + j))
        dims = _TN

    def body(a_ref, b_ref, o_ref, acc_ref):
        k = pl.program_id(2)

        @pl.when(k == 0)
        def _():
            acc_ref[...] = jnp.zeros_like(acc_ref)

        acc_ref[...] += _dot(a_ref[...], b_ref[...], dims)

        @pl.when(k == nk - 1)
        def _():
            o_ref[...] = acc_ref[...].astype(o_ref.dtype)

    return pl.pallas_call(
        body,
        grid=(M // tm, N // tn, nk),
        in_specs=[a_spec, b_spec],
        out_specs=pl.BlockSpec((tm, tn), lambda i, j, k: (i, j)),
        out_shape=jax.ShapeDtypeStruct((M, N), out_dtype),
        scratch_shapes=[pltpu.VMEM((tm, tn), F32)],
        compiler_params=_params(("parallel", "parallel", "arbitrary"), 48),
        name=name,
    )(a, b)


def _rowwise(fn, rows, consts, out_rows, out_accs, *, tile, name, reverse=False, carries=(), vmem_mb=40):
    rows = [r if isinstance(r, tuple) else (r, r.shape[1], 0) for r in rows]
    S = rows[0][0].shape[0]
    tile = _pick(S, tile, SUBLANES)
    nt = S // tile
    nr, nc, no, na = len(rows), len(consts), len(out_rows), len(out_accs)

    def ridx(i):
        return nt - 1 - i if reverse else i

    in_specs = [pl.BlockSpec((tile, w), functools.partial(lambda i, cb: (ridx(i), cb), cb=cb)) for _, w, cb in rows]
    in_specs += [pl.BlockSpec(c.shape, functools.partial(lambda i, nd: (0,) * nd, nd=c.ndim)) for c in consts]
    out_specs = [pl.BlockSpec((tile, c), lambda i: (ridx(i), 0)) for c, _ in out_rows]
    out_specs += [pl.BlockSpec(s, functools.partial(lambda i, nd: (0,) * nd, nd=len(s))) for s, _ in out_accs]
    out_shape = [jax.ShapeDtypeStruct((S, c), d) for c, d in out_rows]
    out_shape += [jax.ShapeDtypeStruct(s, d) for s, d in out_accs]

    def body(*refs):
        rin, cin = refs[:nr], refs[nr:nr + nc]
        rout, aout = refs[nr + nc:nr + nc + no], refs[nr + nc + no:nr + nc + no + na]
        carr = refs[nr + nc + no + na:]
        step = pl.program_id(0)

        @pl.when(step == 0)
        def _():
            for r in aout + carr:
                r[...] = jnp.zeros_like(r)

        outs, accs, newc = fn([r[...] for r in rin], [c[...] for c in cin], [c[...] for c in carr])
        for r, o in zip(rout, outs, strict=True):
            r[...] = o.astype(r.dtype)
        for r, v in zip(aout, accs, strict=True):
            r[...] += v
        for r, v in zip(carr, newc, strict=True):
            r[...] = v

    res = pl.pallas_call(
        body,
        grid=(nt,),
        in_specs=in_specs,
        out_specs=out_specs,
        out_shape=out_shape,
        scratch_shapes=[pltpu.VMEM(s, F32) for s in carries],
        compiler_params=_params(("arbitrary",), vmem_mb),
        name=name,
    )(*[r[0] for r in rows], *consts)
    return res


def _ln_fn(x, y, g, b, alpha):
    r = alpha * x + y
    mu = jnp.mean(r, -1, keepdims=True)
    var = jnp.mean(jnp.square(r - mu), -1, keepdims=True)
    return (r - mu) * lax.rsqrt(var + LN_EPS) * g + b


def _ln_fwd(x, y, g, b, alpha, name):
    D = x.shape[1]

    def fn(rows, consts, _):
        return [_ln_fn(rows[0], rows[1], consts[0], consts[1], alpha)], [], []

    return _rowwise(fn, [x, y], [g, b], [(D, F32)], [], tile=256, name=name)[0]


def _ln_bwd(x, y, g, b, dx1, alpha, name):
    D = x.shape[1]

    def fn(rows, consts, _):
        xv, yv, d = rows
        _, vjp = jax.vjp(lambda yy, gg, bb: _ln_fn(xv, yy, gg, bb, alpha), yv, consts[0], consts[1])
        dy, dg, db = vjp(d)
        return [dy], [dg, db], []

    return _rowwise(fn, [x, y, dx1], [g, b], [(D, F32)], [((1, D), F32), ((1, D), F32)], tile=256, name=name)


def _ple_fwd(x1, gate_pre, pp, name):
    D = x1.shape[1]

    def fn(rows, _, __):
        return [rows[0] + _sigmoid(rows[1]) * rows[2]], [], []

    return _rowwise(fn, [x1, gate_pre, pp], [], [(D, F32)], [], tile=256, name=name)[0]


def _ple_bwd(dx2, gate_pre, pp, name):
    D = dx2.shape[1]

    def fn(rows, _, __):
        d, gp, ppv = rows
        s = _sigmoid(gp)
        return [d * ppv * s * (1.0 - s), d * s], [], []

    return _rowwise(fn, [dx2, gate_pre, pp], [], [(D, F32), (D, F32)], [], tile=256, name=name)


def _add(a, b, name):
    def fn(rows, _, __):
        return [rows[0] + rows[1]], [], []

    return _rowwise(fn, [a, b], [], [(a.shape[1], F32)], [], tile=256, name=name)[0]


def _axpy(alpha, a, b, name):
    def fn(rows, _, __):
        return [alpha * rows[0] + rows[1]], [], []

    return _rowwise(fn, [a, b], [], [(a.shape[1], F32)], [], tile=256, name=name)[0]


def _loss_head(y, target, name):
    D = y.shape[1]

    def fn(rows, _, __):
        e = rows[0] - rows[1]
        part = 0.5 * jnp.sum(jnp.sum(e * e, axis=1, keepdims=True), axis=0, keepdims=True) / D
        return [e / D], [jnp.broadcast_to(part, (SUBLANES, LANES))], []

    return _rowwise(fn, [y, target], [], [(D, F32)], [((SUBLANES, LANES), F32)], tile=256, name=name)


def _conv_fwd(h, w, n_cols, name):
    S = h.shape[0]
    T = _pick(S, 512, SUBLANES)
    CB = _pick(n_cols, 512)
    nt = S // T
    K = GDN_CONV

    def body(x_ref, halo_ref, w_ref, o_ref, buf):
        i = pl.program_id(1)
        buf[0:SUBLANES, :] = jnp.where(i > 0, halo_ref[...], 0.0)
        buf[SUBLANES:, :] = x_ref[...]
        acc = jnp.zeros((T, CB), F32)
        for k in range(K):
            acc = acc + w_ref[k:k + 1, :] * buf[pl.ds(SUBLANES - (K - 1) + k, T), :]
        o_ref[...] = acc

    return pl.pallas_call(
        body,
        grid=(n_cols // CB, nt),
        in_specs=[pl.BlockSpec((T, CB), lambda c, i: (i, c)),
                  pl.BlockSpec((SUBLANES, CB), lambda c, i: (jnp.maximum(i * (T // SUBLANES) - 1, 0), c)),
                  pl.BlockSpec((K, CB), lambda c, i: (0, c))],
        out_specs=pl.BlockSpec((T, CB), lambda c, i: (i, c)),
        out_shape=jax.ShapeDtypeStruct((S, n_cols), F32),
        scratch_shapes=[pltpu.VMEM((T + SUBLANES, CB), F32)],
        compiler_params=_params(("parallel", "parallel")),
        name=name,
    )(h, h, w)


def _conv_bwd(dc, h, w, h_col0, name):
    S, n_cols = dc.shape
    T = _pick(S, 512, SUBLANES)
    CB = _pick(n_cols, 512)
    nt = S // T
    K = GDN_CONV
    assert h_col0 % CB == 0
    hb = h_col0 // CB

    def body(d_ref, halo_ref, x_ref, w_ref, dx_ref, dw_ref, buf):
        i = pl.program_id(1)

        @pl.when(i == 0)
        def _():
            dw_ref[...] = jnp.zeros_like(dw_ref)

        buf[0:T, :] = d_ref[...]
        buf[T:, :] = jnp.where(i < nt - 1, halo_ref[...], 0.0)
        x = x_ref[...]
        acc = jnp.zeros((T, CB), F32)
        for k in range(K):
            shifted = buf[pl.ds(K - 1 - k, T), :]
            acc = acc + w_ref[k:k + 1, :] * shifted
            dw_ref[k:k + 1, :] += jnp.sum(shifted * x, axis=0, keepdims=True)
        dx_ref[...] = acc

    last = S // SUBLANES - 1
    return pl.pallas_call(
        body,
        grid=(n_cols // CB, nt),
        in_specs=[pl.BlockSpec((T, CB), lambda c, i: (i, c)),
                  pl.BlockSpec((SUBLANES, CB), lambda c, i: (jnp.minimum((i + 1) * (T // SUBLANES), last), c)),
                  pl.BlockSpec((T, CB), lambda c, i: (i, hb + c)),
                  pl.BlockSpec((K, CB), lambda c, i: (0, c))],
        out_specs=[pl.BlockSpec((T, CB), lambda c, i: (i, c)),
                   pl.BlockSpec((SUBLANES, CB), lambda c, i: (0, c))],
        out_shape=[jax.ShapeDtypeStruct((S, n_cols), F32), jax.ShapeDtypeStruct((SUBLANES, n_cols), F32)],
        scratch_shapes=[pltpu.VMEM((T + SUBLANES, CB), F32)],
        compiler_params=_params(("parallel", "arbitrary")),
        name=name,
    )(dc, dc, h, w)


def _neumann_inverse(L):
    C = L.shape[-1]
    eye = (_iota2((C, C), 0) == _iota2((C, C), 1)).astype(F32)
    X = eye - L
    P = L
    for _ in range(max(0, math.ceil(math.log2(C)) - 1)):
        P = _mm_hi(P, P)
        X = _mm_hi(X, eye + P)
    return X


def _gdn_chunk(cq, ck, cv, zz, bcol, acol, alog, dtb, ng, state):
    G, C, dk = cq.shape
    q = _silu(cq)
    k = _silu(ck)
    v = _silu(cv)
    q = q * lax.rsqrt(jnp.sum(q * q, -1, keepdims=True) + RMS_EPS) * (dk ** -0.5)
    k = k * lax.rsqrt(jnp.sum(k * k, -1, keepdims=True) + RMS_EPS)
    beta = _sigmoid(bcol)
    g = -jnp.exp(alog) * _softplus(acol + dtb)

    row, col = _iota2((C, C), 0), _iota2((C, C), 1)
    causal, strict, eye = row >= col, row > col, row == col
    tril = jnp.broadcast_to(causal.astype(F32), (G, C, C))
    gcb = _mm_hi(tril, jnp.broadcast_to(g, (G, C, C)))
    gc = jnp.sum(jnp.where(eye, gcb, 0.0), axis=-1, keepdims=True)
    g_last = jnp.sum(jnp.sum(jnp.where((row == C - 1) & (col == 0), gcb, 0.0), axis=-1, keepdims=True), axis=-2, keepdims=True)
    gc_rows = _mm_hi(jnp.ones((G, C, C), F32), jnp.where(eye, gcb, 0.0))
    decay = jnp.exp(jnp.where(causal, gcb - gc_rows, NEG))

    kb = k * beta
    L = jnp.where(strict, _mm_nt(kb, k) * decay, 0.0)
    T = _neumann_inverse(L)
    u = _mm_hi(T, v * beta)
    w = _mm_hi(T, kb * jnp.exp(gc))
    a_qk = jnp.where(causal, _mm_nt(q, k) * decay, 0.0)
    q_dec = q * jnp.exp(gc)
    k_dec = k * jnp.exp(g_last - gc)
    v_new = u - _mm_nn(w, state)
    o = _mm_nn(q_dec, state) + _mm_nn(a_qk, v_new)
    new_state = state * jnp.exp(g_last) + _mm_tn(k_dec, v_new)
    y = o * lax.rsqrt(jnp.mean(o * o, -1, keepdims=True) + RMS_EPS) * ng * _silu(zz)
    return y, new_state


GDN_HEADS_PER_STEP = 8


def _gdn_specs(H, dk, G, chunk_of=lambda n: n):
    C = GDN_CHUNK
    NG = H // G
    cq = pl.BlockSpec((C, G * dk), lambda n, h: (chunk_of(n), h))
    ck = pl.BlockSpec((C, G * dk), lambda n, h: (chunk_of(n), NG + h))
    cv = pl.BlockSpec((C, G * dk), lambda n, h: (chunk_of(n), 2 * NG + h))
    zz = pl.BlockSpec((C, G * dk), lambda n, h: (chunk_of(n), 3 * NG + h))
    ba = pl.BlockSpec((C, LANES), lambda n, h: (chunk_of(n), 4 * H * dk // LANES))
    return cq, ck, cv, zz, ba


def _gdn_step_args(cq_ref, ck_ref, cv_ref, z_ref, ba_ref, hp_ref, hg, G, H, dk):
    ba = ba_ref[...]

    def heads(ref):
        return jnp.stack([ref[:, g * dk:(g + 1) * dk] for g in range(G)])

    def picks(tile, offset):
        return jnp.stack([_lane_pick(tile, offset + hg * G + g) for g in range(G)])

    return (heads(cq_ref), heads(ck_ref), heads(cv_ref), heads(z_ref), picks(ba, 0), picks(ba, H),
            picks(hp_ref[0:1, :], 0), picks(hp_ref[1:2, :], 0))


def _gdn_scan_fwd(c, h, hp, ng, H, name):
    S = c.shape[0]
    dk = c.shape[1] // (3 * H)
    assert dk == LANES
    C = GDN_CHUNK
    NC = S // C

    G = min(GDN_HEADS_PER_STEP, H)
    assert H % G == 0
    NG = H // G

    def body(cq_ref, ck_ref, cv_ref, z_ref, ba_ref, hp_ref, ng_ref, y_ref, s_ref, state):
        n, hg = pl.program_id(0), pl.program_id(1)
        heads = pl.ds(hg * G, G)

        @pl.when(n == 0)
        def _():
            state[heads] = jnp.zeros((G, dk, dk), F32)

        st = state[heads]
        s_ref[0] = st
        y, new_state = _gdn_chunk(*_gdn_step_args(cq_ref, ck_ref, cv_ref, z_ref, ba_ref, hp_ref, hg, G, H, dk), ng_ref[...], st)
        for g in range(G):
            y_ref[:, g * dk:(g + 1) * dk] = y[g].astype(y_ref.dtype)
        state[heads] = new_state

    cq, ck, cv, zz, ba = _gdn_specs(H, dk, G)
    return pl.pallas_call(
        body,
        grid=(NC, NG),
        in_specs=[cq, ck, cv, zz, ba, pl.BlockSpec((SUBLANES, LANES), lambda n, h: (0, 0)),
                  pl.BlockSpec((1, dk), lambda n, h: (0, 0))],
        out_specs=[pl.BlockSpec((C, G * dk), lambda n, h: (n, h)),
                   pl.BlockSpec((1, G, dk, dk), lambda n, h: (n, h, 0, 0))],
        out_shape=[jax.ShapeDtypeStruct((S, H * dk), MXU_DTYPE), jax.ShapeDtypeStruct((NC, H, dk, dk), F32)],
        scratch_shapes=[pltpu.VMEM((H, dk, dk), F32)],
        compiler_params=_params(("arbitrary", "arbitrary")),
        name=name,
    )(c, c, c, h, h, hp, ng)


def _gdn_scan_bwd(c, h, hp, ng, states, dy, H, name):
    S = c.shape[0]
    dk = c.shape[1] // (3 * H)
    C = GDN_CHUNK
    NC = S // C

    G = min(GDN_HEADS_PER_STEP, H)
    NG = H // G

    def body(cq_ref, ck_ref, cv_ref, z_ref, ba_ref, hp_ref, ng_ref, s_ref, dy_ref,
             dq_ref, dk_ref, dv_ref, dz_ref, dba_ref, dhp_ref, dng_ref, dstate):
        n, hg = pl.program_id(0), pl.program_id(1)

        @pl.when((n == 0) & (hg == 0))
        def _():
            dhp_ref[...] = jnp.zeros_like(dhp_ref)
            dng_ref[...] = jnp.zeros_like(dng_ref)

        heads = pl.ds(hg * G, G)

        @pl.when(n == 0)
        def _():
            dstate[heads] = jnp.zeros((G, dk, dk), F32)

        args = (*_gdn_step_args(cq_ref, ck_ref, cv_ref, z_ref, ba_ref, hp_ref, hg, G, H, dk), ng_ref[...], s_ref[0])
        _, vjp = jax.vjp(_gdn_chunk, *args)
        dy = jnp.stack([dy_ref[:, g * dk:(g + 1) * dk] for g in range(G)])
        dcq, dck, dcv, dzz, dbc, dac, dal, ddt, dng, dst = vjp((dy, dstate[heads]))
        dstate[heads] = dst
        dba = jnp.zeros((C, LANES), F32)
        dhp0 = jnp.zeros((1, LANES), F32)
        dhp1 = jnp.zeros((1, LANES), F32)
        for g in range(G):
            hd = hg * G + g
            sl = slice(g * dk, (g + 1) * dk)
            dq_ref[:, sl] = dcq[g]
            dk_ref[:, sl] = dck[g]
            dv_ref[:, sl] = dcv[g]
            dz_ref[:, sl] = dzz[g]
            dba = dba + _lane_put(dbc[g], hd) + _lane_put(dac[g], H + hd)
            dhp0 = dhp0 + _lane_put(dal[g], hd)
            dhp1 = dhp1 + _lane_put(ddt[g], hd)

        @pl.when(hg == 0)
        def _():
            dba_ref[...] = dba

        @pl.when(hg > 0)
        def _():
            dba_ref[...] += dba

        dhp_ref[0:1, :] += dhp0
        dhp_ref[1:2, :] += dhp1
        dng_ref[...] += dng

    rev = lambda n: NC - 1 - n
    blk = pl.BlockSpec
    in_specs = [*_gdn_specs(H, dk, G, rev),
                blk((SUBLANES, LANES), lambda n, h: (0, 0)), blk((1, dk), lambda n, h: (0, 0)),
                blk((1, G, dk, dk), lambda n, h: (rev(n), h, 0, 0)), blk((C, G * dk), lambda n, h: (rev(n), h))]
    out_specs = [blk((C, G * dk), lambda n, h: (rev(n), h))] * 4 + [
        blk((C, LANES), lambda n, h: (rev(n), 0)),
        blk((SUBLANES, LANES), lambda n, h: (0, 0)), blk((1, dk), lambda n, h: (0, 0))]
    out_shape = [jax.ShapeDtypeStruct((S, H * dk), F32)] * 4 + [
        jax.ShapeDtypeStruct((S, LANES), F32), jax.ShapeDtypeStruct((SUBLANES, LANES), F32),
        jax.ShapeDtypeStruct((1, dk), F32)]
    return pl.pallas_call(
        body,
        grid=(NC, NG),
        in_specs=in_specs,
        out_specs=out_specs,
        out_shape=out_shape,
        scratch_shapes=[pltpu.VMEM((H, dk, dk), F32)],
        compiler_params=_params(("arbitrary", "arbitrary")),
        name=name,
    )(c, c, c, h, h, hp, ng, states, dy)


def _pair_rms(x, gain, dh):
    first = _iota2(x.shape, 1) < dh
    sq = x * x
    ss_a = jnp.sum(jnp.where(first, sq, 0.0), axis=1, keepdims=True)
    ss_b = jnp.sum(jnp.where(first, 0.0, sq), axis=1, keepdims=True)
    inv = jnp.where(first, lax.rsqrt(ss_a / dh + RMS_EPS), lax.rsqrt(ss_b / dh + RMS_EPS))
    return x * inv * gain


def _log_sigmoid(x):
    return jnp.minimum(x, 0.0) - jnp.log(1.0 + jnp.exp(-jnp.abs(x)))


def _cum_fn(fr, bf, carry):
    T = fr.shape[0]
    tril = (_iota2((T, T), 0) >= _iota2((T, T), 1)).astype(F32)
    c = _mm_hi(tril, _log_sigmoid(fr + bf)) + carry
    last = jnp.sum(jnp.where(_iota2(c.shape, 0) == T - 1, c, 0.0), axis=0, keepdims=True)
    return c, last


def _fox_prep_fwd(h, bf, qg, kg, H, dh, name):
    S = h.shape[0]
    W = H * dh
    assert 2 * dh == LANES and H <= LANES
    T = _pick(S, 256, LANES)
    nt = S // T
    npair = H // 2

    def body(hq_ref, hk_ref, f_ref, bf_ref, qg_ref, kg_ref, q_ref, k_ref, ccol_ref, crow_ref, carry):
        i = pl.program_id(0)

        @pl.when(i == 0)
        def _():
            carry[...] = jnp.zeros_like(carry)

        for p in range(npair):
            sl = slice(p * LANES, (p + 1) * LANES)
            q_ref[:, sl] = (_pair_rms(hq_ref[:, sl], qg_ref[...], dh) * (dh ** -0.5)).astype(q_ref.dtype)
            k_ref[:, sl] = _pair_rms(hk_ref[:, sl], kg_ref[...], dh).astype(k_ref.dtype)
        c, last = _cum_fn(f_ref[...], bf_ref[...], carry[...])
        carry[...] = last
        ct = c.T
        for hh in range(H):
            ccol_ref[hh] = c[:, hh:hh + 1]
            crow_ref[hh] = ct[hh:hh + 1, :]

    return pl.pallas_call(
        body,
        grid=(nt,),
        in_specs=[pl.BlockSpec((T, W), lambda i: (i, 0)), pl.BlockSpec((T, W), lambda i: (i, 1)),
                  pl.BlockSpec((T, LANES), lambda i: (i, 4 * W // LANES)),
                  pl.BlockSpec((1, LANES), lambda i: (0, 0)), pl.BlockSpec((1, LANES), lambda i: (0, 0)),
                  pl.BlockSpec((1, LANES), lambda i: (0, 0))],
        out_specs=[pl.BlockSpec((T, W), lambda i: (i, 0)), pl.BlockSpec((T, W), lambda i: (i, 0)),
                   pl.BlockSpec((H, T, 1), lambda i: (0, i, 0)), pl.BlockSpec((H, 1, T), lambda i: (0, 0, i))],
        out_shape=[jax.ShapeDtypeStruct((S, W), MXU_DTYPE), jax.ShapeDtypeStruct((S, W), MXU_DTYPE),
                   jax.ShapeDtypeStruct((H, S, 1), F32), jax.ShapeDtypeStruct((H, 1, S), F32)],
        scratch_shapes=[pltpu.VMEM((1, LANES), F32)],
        compiler_params=_params(("arbitrary",)),
        name=name,
    )(h, h, h, bf, qg, kg)


def _fox_prep_bwd(h, bf, qg, kg, dq, dk, dcrow, H, dh, name):
    S = h.shape[0]
    W = H * dh
    T = _pick(S, 256, LANES)
    nt = S // T
    npair = H // 2

    def body(hq_ref, hk_ref, f_ref, bf_ref, qg_ref, kg_ref, dq_ref, dk_ref, dcrow_ref,
             dhq_ref, dhk_ref, df_ref, dbf_ref, dqg_ref, dkg_ref, dcarry, dct):
        i = pl.program_id(0)

        @pl.when(i == 0)
        def _():
            dcarry[...] = jnp.zeros_like(dcarry)
            dbf_ref[...] = jnp.zeros_like(dbf_ref)
            dqg_ref[...] = jnp.zeros_like(dqg_ref)
            dkg_ref[...] = jnp.zeros_like(dkg_ref)

        for p in range(npair):
            sl = slice(p * LANES, (p + 1) * LANES)
            _, vjp = jax.vjp(lambda x, g: _pair_rms(x, g, dh) * (dh ** -0.5), hq_ref[:, sl], qg_ref[...])
            dx, dg = vjp(dq_ref[:, sl])
            dhq_ref[:, sl] = dx
            dqg_ref[...] += dg
            _, vjp = jax.vjp(lambda x, g: _pair_rms(x, g, dh), hk_ref[:, sl], kg_ref[...])
            dx, dg = vjp(dk_ref[:, sl])
            dhk_ref[:, sl] = dx
            dkg_ref[...] += dg
        dct[...] = jnp.zeros_like(dct)
        for hh in range(H):
            dct[hh:hh + 1, :] = dcrow_ref[hh]
        _, vjp = jax.vjp(lambda f, b: _cum_fn(f, b, jnp.zeros((1, LANES), F32)), f_ref[...], bf_ref[...])
        dc = dct[...].T
        df, dbf = vjp((dc, dcarry[...]))
        df_ref[...] = df
        dbf_ref[...] += dbf
        dcarry[...] = dcarry[...] + jnp.sum(dc, axis=0, keepdims=True)

    rv = lambda i: nt - 1 - i
    return pl.pallas_call(
        body,
        grid=(nt,),
        in_specs=[pl.BlockSpec((T, W), lambda i: (rv(i), 0)), pl.BlockSpec((T, W), lambda i: (rv(i), 1)),
                  pl.BlockSpec((T, LANES), lambda i: (rv(i), 4 * W // LANES)),
                  pl.BlockSpec((1, LANES), lambda i: (0, 0)), pl.BlockSpec((1, LANES), lambda i: (0, 0)),
                  pl.BlockSpec((1, LANES), lambda i: (0, 0)),
                  pl.BlockSpec((T, W), lambda i: (rv(i), 0)), pl.BlockSpec((T, W), lambda i: (rv(i), 0)),
                  pl.BlockSpec((H, 1, T), lambda i: (0, 0, rv(i)))],
        out_specs=[pl.BlockSpec((T, W), lambda i: (rv(i), 0)), pl.BlockSpec((T, W), lambda i: (rv(i), 0)),
                   pl.BlockSpec((T, LANES), lambda i: (rv(i), 0)),
                   pl.BlockSpec((1, LANES), lambda i: (0, 0)), pl.BlockSpec((1, LANES), lambda i: (0, 0)),
                   pl.BlockSpec((1, LANES), lambda i: (0, 0))],
        out_shape=[jax.ShapeDtypeStruct((S, W), F32), jax.ShapeDtypeStruct((S, W), F32),
                   jax.ShapeDtypeStruct((S, LANES), F32)] + [jax.ShapeDtypeStruct((1, LANES), F32)] * 3,
        scratch_shapes=[pltpu.VMEM((1, LANES), F32), pltpu.VMEM((LANES, T), F32)],
        compiler_params=_params(("arbitrary",)),
        name=name,
    )(h, h, h, bf, qg, kg, dq, dk, dcrow)


def _head_masks(dh):
    first = _iota2((1, LANES), 1) < dh
    return first, jnp.logical_not(first)


def _per_head(tile, dh):
    return jnp.stack([jnp.where(mask, tile, 0) for mask in _head_masks(dh)])


def _both(tile):
    return jnp.stack([tile, tile])


FLASH_SUB = 128


def _flash_scores(q_m, k, ccol, crow, row0, diagonal):
    s = _dot(q_m, k, _NT) + ccol - crow
    if diagonal:
        s = jnp.where(_iota2(s.shape, s.ndim - 1) <= row0 + _iota2(s.shape, s.ndim - 2), s, NEG)
    return s


def _flash_blocks(i, j, tq, fn, sub=FLASH_SUB):
    sub = min(sub, tq)
    for diagonal in (False, True):
        @pl.when((j == i) if diagonal else (j < i))
        def _():
            for r in range(tq // sub):
                fn(slice(r * sub, (r + 1) * sub), r * sub, diagonal)


def _flash_fwd(q, k, h, ccol, crow, H, dh, name):
    S, W = q.shape
    tq = tk = _pick(S, 512, LANES)
    nq = S // tq
    npair = H // 2
    vblk = 2 * W // LANES

    def body(q_ref, k_ref, v_ref, ccol_ref, crow_ref, o_ref, lse_ref, m_s, l_s, acc_s):
        i, j = pl.program_id(1), pl.program_id(2)

        @pl.when(j == 0)
        def _():
            m_s[...] = jnp.full_like(m_s, NEG)
            l_s[...] = jnp.zeros_like(l_s)
            acc_s[...] = jnp.zeros_like(acc_s)

        def tile(rows, row0, diagonal):
            kv, vv = _both(k_ref[...]), _both(v_ref[...].astype(MXU_DTYPE))
            s = _flash_scores(_per_head(q_ref[rows, :], dh), kv, ccol_ref[:, rows, :], crow_ref[...], row0, diagonal)
            m_old = m_s[:, rows, :]
            m_new = jnp.maximum(m_old, jnp.max(s, axis=-1, keepdims=True))
            alpha = jnp.exp(m_old - m_new)
            p = jnp.exp(s - m_new)
            l_s[:, rows, :] = alpha * l_s[:, rows, :] + jnp.sum(p, axis=-1, keepdims=True)
            acc_s[:, rows, :] = alpha * acc_s[:, rows, :] + _dot(p, vv, _NN)
            m_s[:, rows, :] = m_new

        _flash_blocks(i, j, tq, tile)

        @pl.when(j == nq - 1)
        def _():
            first, _ = _head_masks(dh)
            o_ref[...] = jnp.where(first, acc_s[0] / l_s[0], acc_s[1] / l_s[1])
            for a in range(2):
                lse_ref[a] = m_s[a] + jnp.log(l_s[a])

    kj = lambda i, j: jnp.minimum(i, j)
    return pl.pallas_call(
        body,
        grid=(npair, nq, nq),
        in_specs=[pl.BlockSpec((tq, LANES), lambda p, i, j: (i, p)),
                  pl.BlockSpec((tk, LANES), lambda p, i, j: (kj(i, j), p)),
                  pl.BlockSpec((tk, LANES), lambda p, i, j: (kj(i, j), vblk + p)),
                  pl.BlockSpec((2, tq, 1), lambda p, i, j: (p, i, 0)),
                  pl.BlockSpec((2, 1, tk), lambda p, i, j: (p, 0, kj(i, j)))],
        out_specs=[pl.BlockSpec((tq, LANES), lambda p, i, j: (i, p)),
                   pl.BlockSpec((2, tq, 1), lambda p, i, j: (p, i, 0))],
        out_shape=[jax.ShapeDtypeStruct((S, W), F32), jax.ShapeDtypeStruct((H, S, 1), F32)],
        scratch_shapes=[pltpu.VMEM((2, tq, 1), F32), pltpu.VMEM((2, tq, 1), F32), pltpu.VMEM((2, tq, LANES), F32)],
        compiler_params=_params(("parallel", "parallel", "arbitrary")),
        name=name,
    )(q, k, h, ccol, crow)


def _flash_bwd_kv(q, k, h, ccol, crow, lse, delta, resid, do, H, dh, name):
    S, W = q.shape
    tq = tk = _pick(S, 512, LANES)
    nq = S // tq
    npair = H // 2
    vblk = 2 * W // LANES

    def body(q_ref, k_ref, v_ref, ccol_ref, crow_ref, lse_ref, dl_ref, rs_ref, do_ref, dk_ref, dv_ref, dcr_ref):
        j, i = pl.program_id(1), pl.program_id(2)

        @pl.when(i == 0)
        def _():
            dk_ref[...] = jnp.zeros_like(dk_ref)
            dv_ref[...] = jnp.zeros_like(dv_ref)
            dcr_ref[...] = jnp.zeros_like(dcr_ref)

        def tile(rows, row0, diagonal):
            kv, vv = _both(k_ref[...]), _both(v_ref[...].astype(MXU_DTYPE))
            q_m = _per_head(q_ref[rows, :], dh)
            do_m = _per_head(do_ref[rows, :].astype(MXU_DTYPE), dh)
            s = _flash_scores(q_m, kv, ccol_ref[:, rows, :], crow_ref[...], row0, diagonal)
            p = jnp.exp(s - lse_ref[:, rows, :])
            dv = _dot(p, do_m, _TN)
            ds = p * (_dot(do_m, vv, _NT) - (dl_ref[:, rows, :] + rs_ref[:, rows, :]))
            dk = _dot(ds, q_m, _TN)
            dv_ref[...] += dv[0] + dv[1]
            dk_ref[...] += dk[0] + dk[1]
            dcr_ref[...] -= jnp.sum(ds, axis=1, keepdims=True)

        _flash_blocks(i, j, tq, tile)

    qi = lambda j, i: jnp.maximum(i, j)
    return pl.pallas_call(
        body,
        grid=(npair, nq, nq),
        in_specs=[pl.BlockSpec((tq, LANES), lambda p, j, i: (qi(j, i), p)),
                  pl.BlockSpec((tk, LANES), lambda p, j, i: (j, p)),
                  pl.BlockSpec((tk, LANES), lambda p, j, i: (j, vblk + p)),
                  pl.BlockSpec((2, tq, 1), lambda p, j, i: (p, qi(j, i), 0)),
                  pl.BlockSpec((2, 1, tk), lambda p, j, i: (p, 0, j)),
                  pl.BlockSpec((2, tq, 1), lambda p, j, i: (p, qi(j, i), 0)),
                  pl.BlockSpec((2, tq, 1), lambda p, j, i: (p, qi(j, i), 0)),
                  pl.BlockSpec((2, tq, 1), lambda p, j, i: (p, qi(j, i), 0)),
                  pl.BlockSpec((tq, LANES), lambda p, j, i: (qi(j, i), p))],
        out_specs=[pl.BlockSpec((tk, LANES), lambda p, j, i: (j, p)),
                   pl.BlockSpec((tk, LANES), lambda p, j, i: (j, p)),
                   pl.BlockSpec((2, 1, tk), lambda p, j, i: (p, 0, j))],
        out_shape=[jax.ShapeDtypeStruct((S, W), F32), jax.ShapeDtypeStruct((S, W), F32),
                   jax.ShapeDtypeStruct((H, 1, S), F32)],
        compiler_params=_params(("parallel", "parallel", "arbitrary")),
        name=name,
    )(q, k, h, ccol, crow, lse, delta, resid, do)


def _flash_bwd_q(q, k, h, ccol, crow, lse, delta, do, H, dh, name):
    S, W = q.shape
    tq = tk = _pick(S, 512, LANES)
    nq = S // tq
    npair = H // 2
    vblk = 2 * W // LANES

    def body(q_ref, k_ref, v_ref, ccol_ref, crow_ref, lse_ref, dl_ref, do_ref, dq_ref, rs_ref):
        i, j = pl.program_id(1), pl.program_id(2)

        @pl.when(j == 0)
        def _():
            dq_ref[...] = jnp.zeros_like(dq_ref)
            rs_ref[...] = jnp.zeros_like(rs_ref)

        def tile(rows, row0, diagonal):
            kv, vv = k_ref[...], _both(v_ref[...].astype(MXU_DTYPE))
            s = _flash_scores(_per_head(q_ref[rows, :], dh), _both(kv), ccol_ref[:, rows, :], crow_ref[...], row0, diagonal)
            p = jnp.exp(s - lse_ref[:, rows, :])
            ds = p * (_dot(_per_head(do_ref[rows, :].astype(MXU_DTYPE), dh), vv, _NT) - dl_ref[:, rows, :])
            dq = _dot(ds, _per_head(kv, dh), _NN)
            dq_ref[rows, :] += dq[0] + dq[1]
            rs_ref[:, rows, :] += jnp.sum(ds, axis=-1, keepdims=True)

        _flash_blocks(i, j, tq, tile, sub=tq)

    kj = lambda i, j: jnp.minimum(i, j)
    return pl.pallas_call(
        body,
        grid=(npair, nq, nq),
        in_specs=[pl.BlockSpec((tq, LANES), lambda p, i, j: (i, p)),
                  pl.BlockSpec((tk, LANES), lambda p, i, j: (kj(i, j), p)),
                  pl.BlockSpec((tk, LANES), lambda p, i, j: (kj(i, j), vblk + p)),
                  pl.BlockSpec((2, tq, 1), lambda p, i, j: (p, i, 0)),
                  pl.BlockSpec((2, 1, tk), lambda p, i, j: (p, 0, kj(i, j))),
                  pl.BlockSpec((2, tq, 1), lambda p, i, j: (p, i, 0)),
                  pl.BlockSpec((2, tq, 1), lambda p, i, j: (p, i, 0)),
                  pl.BlockSpec((tq, LANES), lambda p, i, j: (i, p))],
        out_specs=[pl.BlockSpec((tq, LANES), lambda p, i, j: (i, p)),
                   pl.BlockSpec((2, tq, 1), lambda p, i, j: (p, i, 0))],
        out_shape=[jax.ShapeDtypeStruct((S, W), F32), jax.ShapeDtypeStruct((H, S, 1), F32)],
        compiler_params=_params(("parallel", "parallel", "arbitrary")),
        name=name,
    )(q, k, h, ccol, crow, lse, delta, do)


def _fox_gate_fwd(o, h, W, name):
    def fn(rows, _, __):
        return [rows[0] * _silu(rows[1])], [], []

    return _rowwise(fn, [o, (h, W, 3)], [], [(W, MXU_DTYPE)], [], tile=256, name=name)[0]


def _fox_gate_bwd(o, h, dog, H, dh, name):
    S, W = o.shape
    T = _pick(S, 256, SUBLANES)

    def body(o_ref, z_ref, d_ref, do_ref, dz_ref, dl_ref):
        ov, zv, dv = o_ref[...], z_ref[...], d_ref[...]
        sg = _sigmoid(zv)
        do = dv * zv * sg
        do_ref[...] = do
        dz_ref[...] = dv * ov * sg * (1.0 + zv * (1.0 - sg))
        prod = do * ov
        for p in range(H // 2):
            blk = prod[:, p * LANES:(p + 1) * LANES]
            first = _iota2(blk.shape, 1) < dh
            dl_ref[2 * p] = jnp.sum(jnp.where(first, blk, 0.0), axis=1, keepdims=True)
            dl_ref[2 * p + 1] = jnp.sum(jnp.where(first, 0.0, blk), axis=1, keepdims=True)

    return pl.pallas_call(
        body,
        grid=(S // T,),
        in_specs=[pl.BlockSpec((T, W), lambda i: (i, 0)), pl.BlockSpec((T, W), lambda i: (i, 3)),
                  pl.BlockSpec((T, W), lambda i: (i, 0))],
        out_specs=[pl.BlockSpec((T, W), lambda i: (i, 0)), pl.BlockSpec((T, W), lambda i: (i, 0)),
                   pl.BlockSpec((H, T, 1), lambda i: (0, i, 0))],
        out_shape=[jax.ShapeDtypeStruct((S, W), F32), jax.ShapeDtypeStruct((S, W), F32),
                   jax.ShapeDtypeStruct((H, S, 1), F32)],
        compiler_params=_params(("parallel",)),
        name=name,
    )(o, h, dog)


def _gdn_layer_fwd(x, w, tag):
    H = w["H"]
    qk = H * LANES
    h = _matmul(x, w["w_in"], "nn", name=f"{tag}_in")
    c = _conv_fwd(h, w["conv"], 3 * qk, name=f"{tag}_conv")
    og, states = _gdn_scan_fwd(c, h, w["hp"], w["ng"], H, name=f"{tag}_scan")
    y = _matmul(og, w["w_out"], "nn", name=f"{tag}_out")
    return y, (x, h, c, states, og)


def _gdn_layer_bwd(dy, res, w, tag):
    x, h, c, states, og = res
    H = w["H"]
    qk = H * LANES
    dog = _matmul(dy, w["w_out"], "nt", name=f"{tag}_out_dx")
    dw_out = _matmul(og, dy, "tn", name=f"{tag}_out_dw")
    dq, dk, dv, dz, dba, dhp, dng = _gdn_scan_bwd(c, h, w["hp"], w["ng"], states, dog, H, name=f"{tag}_scan_bwd")
    dh_parts, dconv = [], []
    for part, d in enumerate((dq, dk, dv)):
        dh_p, dw_p = _conv_bwd(d, h, w["conv"][:, part * qk:(part + 1) * qk], part * qk, name=f"{tag}_conv_bwd{part}")
        dh_parts.append(dh_p)
        dconv.append(dw_p[:GDN_CONV])
    dh = jnp.concatenate(dh_parts + [dz, dba], axis=1)
    dx = _matmul(dh, w["w_in"], "nt", name=f"{tag}_in_dx")
    dw_in = _matmul(x, dh, "tn", name=f"{tag}_in_dw")
    grads = {"w_in": dw_in, "w_out": dw_out, "conv": jnp.concatenate(dconv, axis=1),
             "a_log": dhp[0, :H], "dt_bias": dhp[1, :H], "norm_g": dng[0]}
    return dx, grads


def _fox_layer_fwd(x, w, tag):
    H, dh = w["H"], w["dh"]
    W = H * dh
    h = _matmul(x, w["w_in"], "nn", name=f"{tag}_in")
    q, k, ccol, crow = _fox_prep_fwd(h, w["bf"], w["qg"], w["kg"], H, dh, name=f"{tag}_prep")
    o, lse = _flash_fwd(q, k, h, ccol, crow, H, dh, name=f"{tag}_flash")
    og = _fox_gate_fwd(o, h, W, name=f"{tag}_gate")
    y = _matmul(og, w["w_out"], "nn", name=f"{tag}_out")
    return y, (x, h, q, k, ccol, crow, o, lse, og)


def _fox_layer_bwd(dy, res, w, tag):
    x, h, q, k, ccol, crow, o, lse, og = res
    H, dh = w["H"], w["dh"]
    dog = _matmul(dy, w["w_out"], "nt", name=f"{tag}_out_dx")
    dw_out = _matmul(og, dy, "tn", name=f"{tag}_out_dw")
    do, dz, delta = _fox_gate_bwd(o, h, dog, H, dh, name=f"{tag}_gate_bwd")
    dqq, resid = _flash_bwd_q(q, k, h, ccol, crow, lse, delta, do, H, dh, name=f"{tag}_flash_bwd_q")
    dkk, dvv, dcrow = _flash_bwd_kv(q, k, h, ccol, crow, lse, delta, resid, do, H, dh, name=f"{tag}_flash_bwd_kv")
    dhq, dhk, df, dbf, dqg, dkg = _fox_prep_bwd(h, w["bf"], w["qg"], w["kg"], dqq, dkk, dcrow, H, dh, name=f"{tag}_prep_bwd")
    dhh = jnp.concatenate([dhq, dhk, dvv, dz, df], axis=1)
    dx = _matmul(dhh, w["w_in"], "nt", name=f"{tag}_in_dx")
    dw_in = _matmul(x, dhh, "tn", name=f"{tag}_in_dw")
    grads = {"w_in": dw_in, "w_out": dw_out, "b_f": dbf[0, :H],
             "q_norm_g": dqg[0, :dh] + dqg[0, dh:], "k_norm_g": dkg[0, :dh] + dkg[0, dh:]}
    return dx, grads


def _pad_cols(w, n):
    return jnp.pad(w, ((0, 0), (0, n - w.shape[1])))


def _build_layers(full, small):
    depth = small["ln_g"].shape[0]
    gh = small["gdn_a_log"].shape[1]
    fh, dh = small["fox_b_f"].shape[1], small["fox_q_norm_g"].shape[1]
    layers = []
    for i in range(depth):
        j = i // 2
        w = {"ln_g": small["ln_g"][i][None], "ln_b": small["ln_b"][i][None],
             "w_gate": full["ple_w_gate"][i], "w_proj": full["ple_w_proj"][i]}
        if i % 2 == 0:
            hp = jnp.zeros((SUBLANES, LANES), F32).at[0, :gh].set(small["gdn_a_log"][j]).at[1, :gh].set(small["gdn_dt_bias"][j])
            w.update(kind="gdn", H=gh, w_in=_pad_cols(full["gdn_w_in"][j], 4 * gh * LANES + LANES),
                     conv=full["gdn_conv_w"][j], hp=hp, ng=small["gdn_norm_g"][j][None], w_out=full["gdn_w_out"][j])
        else:
            bf = jnp.zeros((1, LANES), F32).at[0, :fh].set(small["fox_b_f"][j])
            w.update(kind="fox", H=fh, dh=dh, w_in=_pad_cols(full["fox_w_in"][j], 4 * fh * dh + LANES), bf=bf,
                     qg=jnp.tile(small["fox_q_norm_g"][j], 2)[None], kg=jnp.tile(small["fox_k_norm_g"][j], 2)[None],
                     w_out=full["fox_w_out"][j])
        layers.append(w)
    return layers


def _local_step(x, p, target, layers):
    depth = len(layers)
    alpha = (2 * depth) ** 0.25
    saved = []
    for i, w in enumerate(layers):
        tag = f"l{i}"
        if w["kind"] == "gdn":
            y, res = _gdn_layer_fwd(x, w, tag)
        else:
            y, res = _fox_layer_fwd(x, w, tag)
        x1 = _ln_fwd(x, y, w["ln_g"], w["ln_b"], alpha, name=f"{tag}_ln")
        gate_pre = _matmul(x1, w["w_gate"], "nn", name=f"{tag}_gate_mm")
        pp = _matmul(p[i], w["w_proj"], "nn", name=f"{tag}_proj_mm")
        x2 = _ple_fwd(x1, gate_pre, pp, name=f"{tag}_ple")
        saved.append((res, x, y, x1, gate_pre, pp))
        x = x2
    dx, loss_tile = _loss_head(x, target, name="loss_head")
    grads = [None] * depth
    for i in reversed(range(depth)):
        w = layers[i]
        tag = f"l{i}"
        res, xin, y, x1, gate_pre, pp = saved[i]
        dgp, dpp = _ple_bwd(dx, gate_pre, pp, name=f"{tag}_ple_bwd")
        dx1 = _add(dx, _matmul(dgp, w["w_gate"], "nt", name=f"{tag}_gate_dx"), name=f"{tag}_dx1")
        dw_gate = _matmul(x1, dgp, "tn", name=f"{tag}_gate_dw")
        dw_proj = _matmul(p[i], dpp, "tn", name=f"{tag}_proj_dw")
        dy, dg, db = _ln_bwd(xin, y, w["ln_g"], w["ln_b"], dx1, alpha, name=f"{tag}_ln_bwd")
        if w["kind"] == "gdn":
            dxm, g = _gdn_layer_bwd(dy, res, w, tag)
        else:
            dxm, g = _fox_layer_bwd(dy, res, w, tag)
        dx = _axpy(alpha, dy, dxm, name=f"{tag}_dx")
        g.update({"w_gate": dw_gate, "w_proj": dw_proj, "ln_g": dg[0], "ln_b": db[0]})
        grads[i] = g
    return loss_tile, dx, grads


MESH_ID = pl.DeviceIdType.MESH
HBM_SPEC = pl.BlockSpec(memory_space=pl.ANY)
PACK_COLS = 1024
PACK_ROWS = 256


def _all_gather(shards, name):
    nt = len(shards)

    def body(*refs):
        x_refs, out_refs = refs[:nt], refs[nt:2 * nt]
        send_sems, recv_sems, local_sems = refs[2 * nt:]
        x, y, c = lax.axis_index("x"), lax.axis_index("y"), lax.axis_index("c")
        me, sibling = (x, y, c), (x, y, 1 - c)
        chips = [(1 - x, y), (x, 1 - y), (1 - x, 1 - y)]

        def slot(t, px, py, pc):
            return out_refs[t].at[4 * px + 2 * py + pc]

        def copy(k, t, block, to, src=None):
            return pltpu.make_async_remote_copy(
                src_ref=slot(t, *block) if src is None else src, dst_ref=slot(t, *block),
                send_sem=send_sems.at[k, t], recv_sem=recv_sems.at[k, t], device_id=to, device_id_type=MESH_ID)

        every = range(nt)
        mine = [pltpu.make_async_copy(x_refs[t], slot(t, *me), local_sems.at[t]) for t in every]
        for cp in mine:
            cp.start()
        first = [copy(0, t, me, sibling, src=x_refs[t]) for t in every]
        first += [copy(1 + j, t, me, (*chip, c), src=x_refs[t]) for j, chip in enumerate(chips) for t in every]
        for cp in first:
            cp.start()
        passed = []
        for j, chip in enumerate(chips):
            for t in every:
                copy(1 + j, t, (*chip, c), me).wait_recv()
                passed.append(copy(4 + j, t, (*chip, c), sibling))
                passed[-1].start()
        for t in every:
            copy(0, t, sibling, me).wait_recv()
        for j, chip in enumerate(chips):
            for t in every:
                copy(4 + j, t, (*chip, 1 - c), me).wait_recv()
        for cp in first + passed:
            cp.wait_send()
        for cp in mine:
            cp.wait()

    return pl.pallas_call(
        body,
        out_shape=[jax.ShapeDtypeStruct((N_DEV, *s.shape), s.dtype) for s in shards],
        in_specs=[HBM_SPEC] * nt,
        out_specs=[HBM_SPEC] * nt,
        scratch_shapes=[pltpu.SemaphoreType.DMA((7, nt)), pltpu.SemaphoreType.DMA((7, nt)), pltpu.SemaphoreType.DMA((nt,))],
        name=name,
    )(*shards)


def _all_to_all(slabs, name):
    nt = len(slabs)

    def body(*refs):
        g_refs, out_refs = refs[:nt], refs[nt:2 * nt]
        send_sems, recv_sems, local_sems = refs[2 * nt:]
        x, y, c = lax.axis_index("x"), lax.axis_index("y"), lax.axis_index("c")
        me = 4 * x + 2 * y + c
        mine = [pltpu.make_async_copy(g_refs[t].at[me], out_refs[t].at[me], local_sems.at[t]) for t in range(nt)]
        for cp in mine:
            cp.start()
        copies = []
        for k in range(1, N_DEV):
            px = 1 - x if k & 4 else x
            py = 1 - y if k & 2 else y
            pc = 1 - c if k & 1 else c
            peer = 4 * px + 2 * py + pc
            for t in range(nt):
                copies.append(tuple(
                    pltpu.make_async_remote_copy(src_ref=g_refs[t].at[peer], dst_ref=out_refs[t].at[dst],
                                                 send_sem=send_sems.at[k - 1, t], recv_sem=recv_sems.at[k - 1, t],
                                                 device_id=(px, py, pc), device_id_type=MESH_ID)
                    for dst in (me, peer)))
        for send, _ in copies:
            send.start()
        for send, arrive in copies:
            arrive.wait_recv()
            send.wait_send()
        for cp in mine:
            cp.wait()

    return pl.pallas_call(
        body,
        out_shape=[jax.ShapeDtypeStruct(s.shape, s.dtype) for s in slabs],
        in_specs=[HBM_SPEC] * nt,
        out_specs=[HBM_SPEC] * nt,
        scratch_shapes=[pltpu.SemaphoreType.DMA((7, nt)), pltpu.SemaphoreType.DMA((7, nt)), pltpu.SemaphoreType.DMA((nt,))],
        name=name,
    )(*slabs)


def _pack(flats, dtype):
    flat = jnp.concatenate([f.astype(dtype).reshape(-1) for f in flats])
    unit = PACK_ROWS * PACK_COLS
    n = -(-flat.shape[0] // unit) * unit
    return jnp.pad(flat, (0, n - flat.shape[0])).reshape(n // PACK_COLS, PACK_COLS)


def _unpack(buf, shapes):
    lead = buf.shape[:-2]
    flat = buf.reshape(*lead, -1)
    out, off = [], 0
    for s in shapes:
        n = math.prod(s)
        out.append(flat[..., off:off + n].reshape(*lead, *s))
        off += n
    return out


_ROW_SPLIT = ("ple_w_gate", "gdn_w_out", "fox_w_out")
_COL_SPLIT = ("ple_w_proj", "gdn_w_in", "gdn_conv_w", "fox_w_in")
_SHARDED = ("ple_w_gate", "ple_w_proj", "gdn_w_in", "gdn_conv_w", "gdn_w_out", "fox_w_in", "fox_w_out")
_REPLICATED = ("ln_g", "ln_b", "gdn_a_log", "gdn_dt_bias", "gdn_norm_g", "fox_b_f", "fox_q_norm_g", "fox_k_norm_g")
_WEIGHTS = ("ln_g", "ln_b", "ple_w_gate", "ple_w_proj", "gdn_w_in", "gdn_conv_w", "gdn_a_log", "gdn_dt_bias",
            "gdn_norm_g", "gdn_w_out", "fox_w_in", "fox_b_f", "fox_q_norm_g", "fox_k_norm_g", "fox_w_out")


def _join(name, gathered):
    n, l, a, b = gathered.shape
    if name in _ROW_SPLIT:
        return gathered.transpose(1, 0, 2, 3).reshape(l, n * a, b)
    return gathered.transpose(1, 2, 0, 3).reshape(l, a, n * b)


def _split(name, full):
    l, a, b = full.shape
    if name in _ROW_SPLIT:
        return full.reshape(l, N_DEV, a // N_DEV, b).transpose(1, 0, 2, 3)
    return full.reshape(l, a, N_DEV, b // N_DEV).transpose(2, 0, 1, 3)


def _adamw(w, g_parts, m, v, name):
    shape = w.shape
    R, C = math.prod(shape[:-1]), shape[-1]
    tr = _pick(R, 256, SUBLANES)
    c1 = 1.0 - ADAM_B1 ** ADAM_STEP
    c2 = 1.0 - ADAM_B2 ** ADAM_STEP

    def body(w_ref, g_ref, m_ref, v_ref, go_ref, d_ref, mo_ref, vo_ref):
        gv = g_ref[0]
        for s in range(1, N_DEV):
            gv = gv + g_ref[s]
        mn = ADAM_B1 * m_ref[...] + (1.0 - ADAM_B1) * gv
        vn = ADAM_B2 * v_ref[...] + (1.0 - ADAM_B2) * jnp.square(gv)
        go_ref[...] = gv
        d_ref[...] = -ADAM_LR * ((mn / c1) / (jnp.sqrt(vn / c2) + ADAM_EPS) + ADAM_WD * w_ref[...])
        mo_ref[...] = mn
        vo_ref[...] = vn

    row = pl.BlockSpec((tr, C), lambda i: (i, 0))
    outs = pl.pallas_call(
        body,
        grid=(R // tr,),
        in_specs=[row, pl.BlockSpec((N_DEV, tr, C), lambda i: (0, i, 0)), row, row],
        out_specs=[row] * 4,
        out_shape=[jax.ShapeDtypeStruct((R, C), F32)] * 4,
        compiler_params=_params(("parallel",)),
        name=name,
    )(w.reshape(R, C), g_parts.reshape(N_DEV, R, C), m.reshape(R, C), v.reshape(R, C))
    return [o.reshape(shape) for o in outs]


def _train_step(x, p, target, w, m, v):
    shards = [w[n] if n == "gdn_conv_w" else w[n].astype(MXU_DTYPE) for n in _SHARDED]
    gathered = _all_gather(shards, name="gather_weights")
    full = {n: _join(n, part) for n, part in zip(_SHARDED, gathered)}
    layers = _build_layers(full, {n: w[n] for n in _REPLICATED})

    loss_tile, dx, grads = _local_step(x[0], p[:, 0], target[0], layers)
    loss = lax.psum(loss_tile[0, 0], ("x", "y", "c"))

    depth = len(layers)
    gdn_l = [i for i in range(depth) if i % 2 == 0]
    fox_l = [i for i in range(depth) if i % 2 == 1]

    def stack(key, idx):
        return jnp.stack([grads[i][key] for i in idx])

    full_g = {
        "ple_w_gate": stack("w_gate", range(depth)), "ple_w_proj": stack("w_proj", range(depth)),
        "gdn_w_in": stack("w_in", gdn_l)[..., :w["gdn_w_in"].shape[-1] * N_DEV], "gdn_conv_w": stack("conv", gdn_l),
        "gdn_w_out": stack("w_out", gdn_l),
        "fox_w_in": stack("w_in", fox_l)[..., :w["fox_w_in"].shape[-1] * N_DEV], "fox_w_out": stack("w_out", fox_l)}
    small_g = {
        "ln_g": stack("ln_g", range(depth)), "ln_b": stack("ln_b", range(depth)),
        "gdn_a_log": stack("a_log", gdn_l), "gdn_dt_bias": stack("dt_bias", gdn_l), "gdn_norm_g": stack("norm_g", gdn_l),
        "fox_b_f": stack("b_f", fox_l), "fox_q_norm_g": stack("q_norm_g", fox_l), "fox_k_norm_g": stack("k_norm_g", fox_l)}

    g_parts = dict(zip(_SHARDED, _all_to_all([_split(n, full_g[n]) for n in _SHARDED], name="scatter_grads")))
    small_all = _all_gather([_pack([small_g[n] for n in _REPLICATED], F32)], name="gather_small_grads")[0]
    g_parts.update(zip(_REPLICATED, _unpack(small_all, [w[n].shape for n in _REPLICATED])))

    g, delta, new_m, new_v = {}, {}, {}, {}
    for n in _WEIGHTS:
        g[n], delta[n], new_m[n], new_v[n] = _adamw(w[n], g_parts[n], m[n], v[n], name=f"adamw_{n}")
    return (loss, dx[None], *[g[n] for n in _WEIGHTS], *[delta[n] for n in _WEIGHTS],
            *[new_m[n] for n in _WEIGHTS], *[new_v[n] for n in _WEIGHTS])


def kernel(x, p, ln_g, ln_b, ple_w_gate, ple_w_proj, gdn_w_in, gdn_conv_w, gdn_a_log, gdn_dt_bias, gdn_norm_g, gdn_w_out, fox_w_in, fox_b_f, fox_q_norm_g, fox_k_norm_g, fox_w_out, loss_target, m_ln_g, m_ln_b, m_ple_w_gate, m_ple_w_proj, m_gdn_w_in, m_gdn_conv_w, m_gdn_a_log, m_gdn_dt_bias, m_gdn_norm_g, m_gdn_w_out, m_fox_w_in, m_fox_b_f, m_fox_q_norm_g, m_fox_k_norm_g, m_fox_w_out, v_ln_g, v_ln_b, v_ple_w_gate, v_ple_w_proj, v_gdn_w_in, v_gdn_conv_w, v_gdn_a_log, v_gdn_dt_bias, v_gdn_norm_g, v_gdn_w_out, v_fox_w_in, v_fox_b_f, v_fox_q_norm_g, v_fox_k_norm_g, v_fox_w_out):
    given = dict(locals())
    w = {n: given[n] for n in _WEIGHTS}
    m = {n: given["m_" + n] for n in _WEIGHTS}
    v = {n: given["v_" + n] for n in _WEIGHTS}
    return _train_step(x, p, loss_target, w, m, v)
```

```python
import functools
import math

import jax
import jax.numpy as jnp
from jax import lax
from jax.experimental import pallas as pl
from jax.experimental.pallas import tpu as pltpu

F32 = jnp.float32
BF16 = jnp.bfloat16
MXU_DTYPE = BF16
HI = lax.Precision.HIGHEST

N_DEV = 8
LANES = 128
SUBLANES = 8
VMEM_BYTES = 64 * 1024 * 1024

GDN_CHUNK = 64
GDN_CONV = 4
LN_EPS = 1e-5
RMS_EPS = 1e-6
NEG = -1e30

ADAM_LR = 0.001
ADAM_B1 = 0.9
ADAM_B2 = 0.999
ADAM_EPS = 1e-08
ADAM_WD = 0.01
ADAM_STEP = 10


def _params(semantics, vmem_mb=40):
    return pltpu.CompilerParams(dimension_semantics=semantics, vmem_limit_bytes=vmem_mb * 1024 * 1024)


def _pick(dim, cap, unit=LANES):
    if dim <= cap:
        return dim
    best = None
    for t in range(unit, cap + 1, unit):
        if dim % t == 0:
            best = t
    assert best is not None, (dim, cap)
    return best


def _dims(dims, ndim):
    if ndim == 2:
        return (dims, ((), ()))
    return (((dims[0][0] + 1,), (dims[1][0] + 1,)), ((0,), (0,)))


def _dot(a, b, dims):
    return lax.dot_general(a.astype(MXU_DTYPE), b.astype(MXU_DTYPE), _dims(dims, a.ndim), preferred_element_type=F32)


_NN = ((1,), (0,))
_NT = ((1,), (1,))
_TN = ((0,), (0,))


@jax.custom_vjp
def _mm_nn(a, b):
    return _dot(a, b, _NN)


@jax.custom_vjp
def _mm_nt(a, b):
    return _dot(a, b, _NT)


@jax.custom_vjp
def _mm_tn(a, b):
    return _dot(a, b, _TN)


_mm_nn.defvjp(lambda a, b: (_dot(a, b, _NN), (a, b)), lambda r, g: (_mm_nt(g, r[1]), _mm_tn(r[0], g)))
_mm_nt.defvjp(lambda a, b: (_dot(a, b, _NT), (a, b)), lambda r, g: (_mm_nn(g, r[1]), _mm_tn(g, r[0])))
_mm_tn.defvjp(lambda a, b: (_dot(a, b, _TN), (a, b)), lambda r, g: (_mm_nt(r[1], g), _mm_nn(r[0], g)))


def _mm_hi(a, b, precision=HI):
    return lax.dot_general(a, b, _dims(_NN, a.ndim), precision=precision, preferred_element_type=F32)


def _mm_3x(a, b):
    return _mm_hi(a, b, lax.Precision.HIGH)


def _sigmoid(x):
    return 1.0 / (1.0 + jnp.exp(-x))


def _silu(x):
    return x * _sigmoid(x)


def _softplus(x):
    return jnp.maximum(x, 0.0) + jnp.log(1.0 + jnp.exp(-jnp.abs(x)))


def _iota2(shape, dim):
    return lax.broadcasted_iota(jnp.int32, shape, dim)


def _lane_pick(tile, lane):
    return jnp.sum(jnp.where(_iota2(tile.shape, 1) == lane, tile, 0.0), axis=1, keepdims=True)


def _lane_put(col, lane, width=LANES):
    return jnp.where(_iota2((col.shape[0], width), 1) == lane, col, 0.0)


def _matmul(a, b, mode, out_dtype=F32, *, name, tm=512, tn=1408, tk=1408, a_cols=None, b_cols=None):
    def cols(arr, rng):
        return (0, arr.shape[1]) if rng is None else rng

    a0, an = cols(a, a_cols)
    b0, bn = cols(b, b_cols)
    if mode == "nn":
        M, K, N = a.shape[0], an, bn
        assert b.shape[0] == K
    elif mode == "nt":
        M, K, N = a.shape[0], an, b.shape[0]
        assert bn == K
    else:
        K, M, N = a.shape[0], an, bn
        assert b.shape[0] == K
    tm, tn, tk = _pick(M, tm), _pick(N, tn), _pick(K, tk)
    nk = K // tk
    if mode == "nn":
        assert a0 % tk == 0 and b0 % tn == 0
        a_spec = pl.BlockSpec((tm, tk), lambda i, j, k: (i, a0 // tk + k))
        b_spec = pl.BlockSpec((tk, tn), lambda i, j, k: (k, b0 // tn + j))
        dims = _NN
    elif mode == "nt":
        assert a0 % tk == 0 and b0 % tk == 0
        a_spec = pl.BlockSpec((tm, tk), lambda i, j, k: (i, a0 // tk + k))
        b_spec = pl.BlockSpec((tn, tk), lambda i, j, k: (j, b0 // tk + k))
        dims = _NT
    else:
        assert a0 % tm == 0 and b0 % tn == 0
        a_spec = pl.BlockSpec((tk, tm), lambda i, j, k: (k, a0 // tm + i))
        b_spec = pl.BlockSpec((tk, tn), lambda i, j, k: (k, b0 // tn + j))
        dims = _TN

    def body(a_ref, b_ref, o_ref, acc_ref):
        k = pl.program_id(2)

        @pl.when(k == 0)
        def _():
            acc_ref[...] = jnp.zeros_like(acc_ref)

        acc_ref[...] += _dot(a_ref[...], b_ref[...], dims)

        @pl.when(k == nk - 1)
        def _():
            o_ref[...] = acc_ref[...].astype(o_ref.dtype)

    return pl.pallas_call(
        body,
        grid=(M // tm, N // tn, nk),
        in_specs=[a_spec, b_spec],
        out_specs=pl.BlockSpec((tm, tn), lambda i, j, k: (i, j)),
        out_shape=jax.ShapeDtypeStruct((M, N), out_dtype),
        scratch_shapes=[pltpu.VMEM((tm, tn), F32)],
        compiler_params=_params(("parallel", "parallel", "arbitrary"), 48),
        name=name,
    )(a, b)


def _rowwise(fn, rows, consts, out_rows, out_accs, *, tile, name, reverse=False, carries=(), vmem_mb=40):
    rows = [r if isinstance(r, tuple) else (r, r.shape[1], 0) for r in rows]
    S = rows[0][0].shape[0]
    tile = _pick(S, tile, SUBLANES)
    nt = S // tile
    nr, nc, no, na = len(rows), len(consts), len(out_rows), len(out_accs)

    def ridx(i):
        return nt - 1 - i if reverse else i

    in_specs = [pl.BlockSpec((tile, w), functools.partial(lambda i, cb: (ridx(i), cb), cb=cb)) for _, w, cb in rows]
    in_specs += [pl.BlockSpec(c.shape, functools.partial(lambda i, nd: (0,) * nd, nd=c.ndim)) for c in consts]
    out_specs = [pl.BlockSpec((tile, c), lambda i: (ridx(i), 0)) for c, _ in out_rows]
    out_specs += [pl.BlockSpec(s, functools.partial(lambda i, nd: (0,) * nd, nd=len(s))) for s, _ in out_accs]
    out_shape = [jax.ShapeDtypeStruct((S, c), d) for c, d in out_rows]
    out_shape += [jax.ShapeDtypeStruct(s, d) for s, d in out_accs]

    def body(*refs):
        rin, cin = refs[:nr], refs[nr:nr + nc]
        rout, aout = refs[nr + nc:nr + nc + no], refs[nr + nc + no:nr + nc + no + na]
        carr = refs[nr + nc + no + na:]
        step = pl.program_id(0)

        @pl.when(step == 0)
        def _():
            for r in aout + carr:
                r[...] = jnp.zeros_like(r)

        outs, accs, newc = fn([r[...] for r in rin], [c[...] for c in cin], [c[...] for c in carr])
        for r, o in zip(rout, outs, strict=True):
            r[...] = o.astype(r.dtype)
        for r, v in zip(aout, accs, strict=True):
            r[...] += v
        for r, v in zip(carr, newc, strict=True):
            r[...] = v

    res = pl.pallas_call(
        body,
        grid=(nt,),
        in_specs=in_specs,
        out_specs=out_specs,
        out_shape=out_shape,
        scratch_shapes=[pltpu.VMEM(s, F32) for s in carries],
        compiler_params=_params(("arbitrary",), vmem_mb),
        name=name,
    )(*[r[0] for r in rows], *consts)
    return res


def _ln_fn(x, y, g, b, alpha):
    r = alpha * x + y
    mu = jnp.mean(r, -1, keepdims=True)
    var = jnp.mean(jnp.square(r - mu), -1, keepdims=True)
    return (r - mu) * lax.rsqrt(var + LN_EPS) * g + b


def _ln_fwd(x, y, g, b, alpha, name):
    D = x.shape[1]

    def fn(rows, consts, _):
        return [_ln_fn(rows[0], rows[1], consts[0], consts[1], alpha)], [], []

    return _rowwise(fn, [x, y], [g, b], [(D, F32)], [], tile=256, name=name)[0]


def _ln_bwd(x, y, g, b, dx1, alpha, name):
    D = x.shape[1]

    def fn(rows, consts, _):
        xv, yv, d = rows
        _, vjp = jax.vjp(lambda yy, gg, bb: _ln_fn(xv, yy, gg, bb, alpha), yv, consts[0], consts[1])
        dy, dg, db = vjp(d)
        return [dy], [dg, db], []

    return _rowwise(fn, [x, y, dx1], [g, b], [(D, F32)], [((1, D), F32), ((1, D), F32)], tile=256, name=name)


def _ple_fwd(x1, gate_pre, pp, name):
    D = x1.shape[1]

    def fn(rows, _, __):
        return [rows[0] + _sigmoid(rows[1]) * rows[2]], [], []

    return _rowwise(fn, [x1, gate_pre, pp], [], [(D, F32)], [], tile=256, name=name)[0]


def _ple_bwd(dx2, gate_pre, pp, name):
    D = dx2.shape[1]

    def fn(rows, _, __):
        d, gp, ppv = rows
        s = _sigmoid(gp)
        return [d * ppv * s * (1.0 - s), d * s], [], []

    return _rowwise(fn, [dx2, gate_pre, pp], [], [(D, F32), (D, F32)], [], tile=256, name=name)


def _add(a, b, name):
    def fn(rows, _, __):
        return [rows[0] + rows[1]], [], []

    return _rowwise(fn, [a, b], [], [(a.shape[1], F32)], [], tile=256, name=name)[0]


def _axpy(alpha, a, b, name):
    def fn(rows, _, __):
        return [alpha * rows[0] + rows[1]], [], []

    return _rowwise(fn, [a, b], [], [(a.shape[1], F32)], [], tile=256, name=name)[0]


def _loss_head(y, target, name):
    D = y.shape[1]

    def fn(rows, _, __):
        e = rows[0] - rows[1]
        part = 0.5 * jnp.sum(jnp.sum(e * e, axis=1, keepdims=True), axis=0, keepdims=True) / D
        return [e / D], [jnp.broadcast_to(part, (SUBLANES, LANES))], []

    return _rowwise(fn, [y, target], [], [(D, F32)], [((SUBLANES, LANES), F32)], tile=256, name=name)


def _conv_fwd(h, w, n_cols, name):
    S = h.shape[0]
    T = _pick(S, 512, SUBLANES)
    CB = _pick(n_cols, 512)
    nt = S // T
    K = GDN_CONV

    def body(x_ref, halo_ref, w_ref, o_ref, buf):
        i = pl.program_id(1)
        buf[0:SUBLANES, :] = jnp.where(i > 0, halo_ref[...], 0.0)
        buf[SUBLANES:, :] = x_ref[...]
        acc = jnp.zeros((T, CB), F32)
        for k in range(K):
            acc = acc + w_ref[k:k + 1, :] * buf[pl.ds(SUBLANES - (K - 1) + k, T), :]
        o_ref[...] = acc

    return pl.pallas_call(
        body,
        grid=(n_cols // CB, nt),
        in_specs=[pl.BlockSpec((T, CB), lambda c, i: (i, c)),
                  pl.BlockSpec((SUBLANES, CB), lambda c, i: (jnp.maximum(i * (T // SUBLANES) - 1, 0), c)),
                  pl.BlockSpec((K, CB), lambda c, i: (0, c))],
        out_specs=pl.BlockSpec((T, CB), lambda c, i: (i, c)),
        out_shape=jax.ShapeDtypeStruct((S, n_cols), F32),
        scratch_shapes=[pltpu.VMEM((T + SUBLANES, CB), F32)],
        compiler_params=_params(("parallel", "parallel")),
        name=name,
    )(h, h, w)


def _conv_bwd(dc, h, w, h_col0, name):
    S, n_cols = dc.shape
    T = _pick(S, 512, SUBLANES)
    CB = _pick(n_cols, 512)
    nt = S // T
    K = GDN_CONV
    assert h_col0 % CB == 0
    hb = h_col0 // CB

    def body(d_ref, halo_ref, x_ref, w_ref, dx_ref, dw_ref, buf):
        i = pl.program_id(1)

        @pl.when(i == 0)
        def _():
            dw_ref[...] = jnp.zeros_like(dw_ref)

        buf[0:T, :] = d_ref[...]
        buf[T:, :] = jnp.where(i < nt - 1, halo_ref[...], 0.0)
        x = x_ref[...]
        acc = jnp.zeros((T, CB), F32)
        for k in range(K):
            shifted = buf[pl.ds(K - 1 - k, T), :]
            acc = acc + w_ref[k:k + 1, :] * shifted
            dw_ref[k:k + 1, :] += jnp.sum(shifted * x, axis=0, keepdims=True)
        dx_ref[...] = acc

    last = S // SUBLANES - 1
    return pl.pallas_call(
        body,
        grid=(n_cols // CB, nt),
        in_specs=[pl.BlockSpec((T, CB), lambda c, i: (i, c)),
                  pl.BlockSpec((SUBLANES, CB), lambda c, i: (jnp.minimum((i + 1) * (T // SUBLANES), last), c)),
                  pl.BlockSpec((T, CB), lambda c, i: (i, hb + c)),
                  pl.BlockSpec((K, CB), lambda c, i: (0, c))],
        out_specs=[pl.BlockSpec((T, CB), lambda c, i: (i, c)),
                   pl.BlockSpec((SUBLANES, CB), lambda c, i: (0, c))],
        out_shape=[jax.ShapeDtypeStruct((S, n_cols), F32), jax.ShapeDtypeStruct((SUBLANES, n_cols), F32)],
        scratch_shapes=[pltpu.VMEM((T + SUBLANES, CB), F32)],
        compiler_params=_params(("parallel", "arbitrary")),
        name=name,
    )(dc, dc, h, w)


def _neumann_inverse(L):
    C = L.shape[-1]
    eye = (_iota2((C, C), 0) == _iota2((C, C), 1)).astype(F32)
    X = eye - L
    P = L
    for _ in range(max(0, math.ceil(math.log2(C)) - 1)):
        P = _mm_3x(P, P)
        X = _mm_3x(X, eye + P)
    return X


def _gdn_chunk(cq, ck, cv, zz, bcol, acol, alog, dtb, ng, state):
    G, C, dk = cq.shape
    q = _silu(cq)
    k = _silu(ck)
    v = _silu(cv)
    q = q * lax.rsqrt(jnp.sum(q * q, -1, keepdims=True) + RMS_EPS) * (dk ** -0.5)
    k = k * lax.rsqrt(jnp.sum(k * k, -1, keepdims=True) + RMS_EPS)
    beta = _sigmoid(bcol)
    g = -jnp.exp(alog) * _softplus(acol + dtb)

    row, col = _iota2((C, C), 0), _iota2((C, C), 1)
    causal, strict, eye = row >= col, row > col, row == col
    g_rows = jnp.swapaxes(jnp.broadcast_to(g, (G, C, C)), -1, -2)
    gc = jnp.sum(jnp.where(causal, g_rows, 0.0), axis=-1, keepdims=True)
    gcb = jnp.broadcast_to(gc, (G, C, C))
    g_last = jnp.sum(jnp.sum(jnp.where((row == C - 1) & (col == 0), gcb, 0.0), axis=-1, keepdims=True), axis=-2, keepdims=True)
    gc_rows = jnp.swapaxes(gcb, -1, -2)
    decay = jnp.exp(jnp.where(causal, gcb - gc_rows, NEG))

    kb = k * beta
    L = jnp.where(strict, _mm_nt(kb, k) * decay, 0.0)
    T = _neumann_inverse(L)
    u = _mm_3x(T, v * beta)
    w = _mm_3x(T, kb * jnp.exp(gc))
    a_qk = jnp.where(causal, _mm_nt(q, k) * decay, 0.0)
    q_dec = q * jnp.exp(gc)
    k_dec = k * jnp.exp(g_last - gc)
    v_new = u - _mm_nn(w, state)
    o = _mm_nn(q_dec, state) + _mm_nn(a_qk, v_new)
    new_state = state * jnp.exp(g_last) + _mm_tn(k_dec, v_new)
    y = o * lax.rsqrt(jnp.mean(o * o, -1, keepdims=True) + RMS_EPS) * ng * _silu(zz)
    return y, new_state


GDN_HEADS_PER_STEP = 8


def _gdn_specs(H, dk, G, chunk_of=lambda n: n):
    C = GDN_CHUNK
    NG = H // G
    cq = pl.BlockSpec((C, G * dk), lambda n, h: (chunk_of(n), h))
    ck = pl.BlockSpec((C, G * dk), lambda n, h: (chunk_of(n), NG + h))
    cv = pl.BlockSpec((C, G * dk), lambda n, h: (chunk_of(n), 2 * NG + h))
    zz = pl.BlockSpec((C, G * dk), lambda n, h: (chunk_of(n), 3 * NG + h))
    ba = pl.BlockSpec((C, LANES), lambda n, h: (chunk_of(n), 4 * H * dk // LANES))
    return cq, ck, cv, zz, ba


def _gdn_step_args(cq_ref, ck_ref, cv_ref, z_ref, ba_ref, hp_ref, hg, G, H, dk):
    ba = ba_ref[...]

    def heads(ref):
        return jnp.stack([ref[:, g * dk:(g + 1) * dk] for g in range(G)])

    def picks(tile, offset):
        return jnp.stack([_lane_pick(tile, offset + hg * G + g) for g in range(G)])

    return (heads(cq_ref), heads(ck_ref), heads(cv_ref), heads(z_ref), picks(ba, 0), picks(ba, H),
            picks(hp_ref[0:1, :], 0), picks(hp_ref[1:2, :], 0))


def _gdn_scan_fwd(c, h, hp, ng, H, name):
    S = c.shape[0]
    dk = c.shape[1] // (3 * H)
    assert dk == LANES
    C = GDN_CHUNK
    NC = S // C

    G = min(GDN_HEADS_PER_STEP, H)
    assert H % G == 0
    NG = H // G

    def body(cq_ref, ck_ref, cv_ref, z_ref, ba_ref, hp_ref, ng_ref, y_ref, s_ref, state):
        n, hg = pl.program_id(0), pl.program_id(1)
        heads = pl.ds(hg * G, G)

        @pl.when(n == 0)
        def _():
            state[heads] = jnp.zeros((G, dk, dk), F32)

        st = state[heads]
        s_ref[0] = st
        y, new_state = _gdn_chunk(*_gdn_step_args(cq_ref, ck_ref, cv_ref, z_ref, ba_ref, hp_ref, hg, G, H, dk), ng_ref[...], st)
        for g in range(G):
            y_ref[:, g * dk:(g + 1) * dk] = y[g].astype(y_ref.dtype)
        state[heads] = new_state

    cq, ck, cv, zz, ba = _gdn_specs(H, dk, G)
    return pl.pallas_call(
        body,
        grid=(NC, NG),
        in_specs=[cq, ck, cv, zz, ba, pl.BlockSpec((SUBLANES, LANES), lambda n, h: (0, 0)),
                  pl.BlockSpec((1, dk), lambda n, h: (0, 0))],
        out_specs=[pl.BlockSpec((C, G * dk), lambda n, h: (n, h)),
                   pl.BlockSpec((1, G, dk, dk), lambda n, h: (n, h, 0, 0))],
        out_shape=[jax.ShapeDtypeStruct((S, H * dk), MXU_DTYPE), jax.ShapeDtypeStruct((NC, H, dk, dk), F32)],
        scratch_shapes=[pltpu.VMEM((H, dk, dk), F32)],
        compiler_params=_params(("arbitrary", "arbitrary")),
        name=name,
    )(c, c, c, h, h, hp, ng)


def _gdn_scan_bwd(c, h, hp, ng, states, dy, H, name):
    S = c.shape[0]
    dk = c.shape[1] // (3 * H)
    C = GDN_CHUNK
    NC = S // C

    G = min(GDN_HEADS_PER_STEP, H)
    NG = H // G

    def body(cq_ref, ck_ref, cv_ref, z_ref, ba_ref, hp_ref, ng_ref, s_ref, dy_ref,
             dq_ref, dk_ref, dv_ref, dz_ref, dba_ref, dhp_ref, dng_ref, dstate):
        n, hg = pl.program_id(0), pl.program_id(1)

        @pl.when((n == 0) & (hg == 0))
        def _():
            dhp_ref[...] = jnp.zeros_like(dhp_ref)
            dng_ref[...] = jnp.zeros_like(dng_ref)

        heads = pl.ds(hg * G, G)

        @pl.when(n == 0)
        def _():
            dstate[heads] = jnp.zeros((G, dk, dk), F32)

        args = (*_gdn_step_args(cq_ref, ck_ref, cv_ref, z_ref, ba_ref, hp_ref, hg, G, H, dk), ng_ref[...], s_ref[0])
        _, vjp = jax.vjp(_gdn_chunk, *args)
        dy = jnp.stack([dy_ref[:, g * dk:(g + 1) * dk] for g in range(G)])
        dcq, dck, dcv, dzz, dbc, dac, dal, ddt, dng, dst = vjp((dy, dstate[heads]))
        dstate[heads] = dst
        dba = jnp.zeros((C, LANES), F32)
        dhp0 = jnp.zeros((1, LANES), F32)
        dhp1 = jnp.zeros((1, LANES), F32)
        for g in range(G):
            hd = hg * G + g
            sl = slice(g * dk, (g + 1) * dk)
            dq_ref[:, sl] = dcq[g]
            dk_ref[:, sl] = dck[g]
            dv_ref[:, sl] = dcv[g]
            dz_ref[:, sl] = dzz[g]
            dba = dba + _lane_put(dbc[g], hd) + _lane_put(dac[g], H + hd)
            dhp0 = dhp0 + _lane_put(dal[g], hd)
            dhp1 = dhp1 + _lane_put(ddt[g], hd)

        @pl.when(hg == 0)
        def _():
            dba_ref[...] = dba

        @pl.when(hg > 0)
        def _():
            dba_ref[...] += dba

        dhp_ref[0:1, :] += dhp0
        dhp_ref[1:2, :] += dhp1
        dng_ref[...] += dng

    rev = lambda n: NC - 1 - n
    blk = pl.BlockSpec
    in_specs = [*_gdn_specs(H, dk, G, rev),
                blk((SUBLANES, LANES), lambda n, h: (0, 0)), blk((1, dk), lambda n, h: (0, 0)),
                blk((1, G, dk, dk), lambda n, h: (rev(n), h, 0, 0)), blk((C, G * dk), lambda n, h: (rev(n), h))]
    out_specs = [blk((C, G * dk), lambda n, h: (rev(n), h))] * 4 + [
        blk((C, LANES), lambda n, h: (rev(n), 0)),
        blk((SUBLANES, LANES), lambda n, h: (0, 0)), blk((1, dk), lambda n, h: (0, 0))]
    out_shape = [jax.ShapeDtypeStruct((S, H * dk), F32)] * 4 + [
        jax.ShapeDtypeStruct((S, LANES), F32), jax.ShapeDtypeStruct((SUBLANES, LANES), F32),
        jax.ShapeDtypeStruct((1, dk), F32)]
    return pl.pallas_call(
        body,
        grid=(NC, NG),
        in_specs=in_specs,
        out_specs=out_specs,
        out_shape=out_shape,
        scratch_shapes=[pltpu.VMEM((H, dk, dk), F32)],
        compiler_params=_params(("arbitrary", "arbitrary")),
        name=name,
    )(c, c, c, h, h, hp, ng, states, dy)


def _pair_rms(x, gain, dh):
    first = _iota2(x.shape, 1) < dh
    sq = x * x
    ss_a = jnp.sum(jnp.where(first, sq, 0.0), axis=1, keepdims=True)
    ss_b = jnp.sum(jnp.where(first, 0.0, sq), axis=1, keepdims=True)
    inv = jnp.where(first, lax.rsqrt(ss_a / dh + RMS_EPS), lax.rsqrt(ss_b / dh + RMS_EPS))
    return x * inv * gain


def _log_sigmoid(x):
    return jnp.minimum(x, 0.0) - jnp.log(1.0 + jnp.exp(-jnp.abs(x)))


def _cum_fn(fr, bf, carry):
    T = fr.shape[0]
    tril = (_iota2((T, T), 0) >= _iota2((T, T), 1)).astype(F32)
    c = _mm_hi(tril, _log_sigmoid(fr + bf)) + carry
    last = jnp.sum(jnp.where(_iota2(c.shape, 0) == T - 1, c, 0.0), axis=0, keepdims=True)
    return c, last


def _fox_prep_fwd(h, bf, qg, kg, H, dh, name):
    S = h.shape[0]
    W = H * dh
    assert 2 * dh == LANES and H <= LANES
    T = _pick(S, 256, LANES)
    nt = S // T
    npair = H // 2

    def body(hq_ref, hk_ref, f_ref, bf_ref, qg_ref, kg_ref, q_ref, k_ref, ccol_ref, crow_ref, carry):
        i = pl.program_id(0)

        @pl.when(i == 0)
        def _():
            carry[...] = jnp.zeros_like(carry)

        for p in range(npair):
            sl = slice(p * LANES, (p + 1) * LANES)
            q_ref[:, sl] = (_pair_rms(hq_ref[:, sl], qg_ref[...], dh) * (dh ** -0.5)).astype(q_ref.dtype)
            k_ref[:, sl] = _pair_rms(hk_ref[:, sl], kg_ref[...], dh).astype(k_ref.dtype)
        c, last = _cum_fn(f_ref[...], bf_ref[...], carry[...])
        carry[...] = last
        ct = c.T
        for hh in range(H):
            ccol_ref[hh] = c[:, hh:hh + 1]
            crow_ref[hh] = ct[hh:hh + 1, :]

    return pl.pallas_call(
        body,
        grid=(nt,),
        in_specs=[pl.BlockSpec((T, W), lambda i: (i, 0)), pl.BlockSpec((T, W), lambda i: (i, 1)),
                  pl.BlockSpec((T, LANES), lambda i: (i, 4 * W // LANES)),
                  pl.BlockSpec((1, LANES), lambda i: (0, 0)), pl.BlockSpec((1, LANES), lambda i: (0, 0)),
                  pl.BlockSpec((1, LANES), lambda i: (0, 0))],
        out_specs=[pl.BlockSpec((T, W), lambda i: (i, 0)), pl.BlockSpec((T, W), lambda i: (i, 0)),
                   pl.BlockSpec((H, T, 1), lambda i: (0, i, 0)), pl.BlockSpec((H, 1, T), lambda i: (0, 0, i))],
        out_shape=[jax.ShapeDtypeStruct((S, W), MXU_DTYPE), jax.ShapeDtypeStruct((S, W), MXU_DTYPE),
                   jax.ShapeDtypeStruct((H, S, 1), F32), jax.ShapeDtypeStruct((H, 1, S), F32)],
        scratch_shapes=[pltpu.VMEM((1, LANES), F32)],
        compiler_params=_params(("arbitrary",)),
        name=name,
    )(h, h, h, bf, qg, kg)


def _fox_prep_bwd(h, bf, qg, kg, dq, dk, dcrow, H, dh, name):
    S = h.shape[0]
    W = H * dh
    T = _pick(S, 256, LANES)
    nt = S // T
    npair = H // 2

    def body(hq_ref, hk_ref, f_ref, bf_ref, qg_ref, kg_ref, dq_ref, dk_ref, dcrow_ref,
             dhq_ref, dhk_ref, df_ref, dbf_ref, dqg_ref, dkg_ref, dcarry, dct):
        i = pl.program_id(0)

        @pl.when(i == 0)
        def _():
            dcarry[...] = jnp.zeros_like(dcarry)
            dbf_ref[...] = jnp.zeros_like(dbf_ref)
            dqg_ref[...] = jnp.zeros_like(dqg_ref)
            dkg_ref[...] = jnp.zeros_like(dkg_ref)

        for p in range(npair):
            sl = slice(p * LANES, (p + 1) * LANES)
            _, vjp = jax.vjp(lambda x, g: _pair_rms(x, g, dh) * (dh ** -0.5), hq_ref[:, sl], qg_ref[...])
            dx, dg = vjp(dq_ref[:, sl])
            dhq_ref[:, sl] = dx
            dqg_ref[...] += dg
            _, vjp = jax.vjp(lambda x, g: _pair_rms(x, g, dh), hk_ref[:, sl], kg_ref[...])
            dx, dg = vjp(dk_ref[:, sl])
            dhk_ref[:, sl] = dx
            dkg_ref[...] += dg
        dct[...] = jnp.zeros_like(dct)
        for hh in range(H):
            dct[hh:hh + 1, :] = dcrow_ref[hh]
        _, vjp = jax.vjp(lambda f, b: _cum_fn(f, b, jnp.zeros((1, LANES), F32)), f_ref[...], bf_ref[...])
        dc = dct[...].T
        df, dbf = vjp((dc, dcarry[...]))
        df_ref[...] = df
        dbf_ref[...] += dbf
        dcarry[...] = dcarry[...] + jnp.sum(dc, axis=0, keepdims=True)

    rv = lambda i: nt - 1 - i
    return pl.pallas_call(
        body,
        grid=(nt,),
        in_specs=[pl.BlockSpec((T, W), lambda i: (rv(i), 0)), pl.BlockSpec((T, W), lambda i: (rv(i), 1)),
                  pl.BlockSpec((T, LANES), lambda i: (rv(i), 4 * W // LANES)),
                  pl.BlockSpec((1, LANES), lambda i: (0, 0)), pl.BlockSpec((1, LANES), lambda i: (0, 0)),
                  pl.BlockSpec((1, LANES), lambda i: (0, 0)),
                  pl.BlockSpec((T, W), lambda i: (rv(i), 0)), pl.BlockSpec((T, W), lambda i: (rv(i), 0)),
                  pl.BlockSpec((H, 1, T), lambda i: (0, 0, rv(i)))],
        out_specs=[pl.BlockSpec((T, W), lambda i: (rv(i), 0)), pl.BlockSpec((T, W), lambda i: (rv(i), 0)),
                   pl.BlockSpec((T, LANES), lambda i: (rv(i), 0)),
                   pl.BlockSpec((1, LANES), lambda i: (0, 0)), pl.BlockSpec((1, LANES), lambda i: (0, 0)),
                   pl.BlockSpec((1, LANES), lambda i: (0, 0))],
        out_shape=[jax.ShapeDtypeStruct((S, W), F32), jax.ShapeDtypeStruct((S, W), F32),
                   jax.ShapeDtypeStruct((S, LANES), F32)] + [jax.ShapeDtypeStruct((1, LANES), F32)] * 3,
        scratch_shapes=[pltpu.VMEM((1, LANES), F32), pltpu.VMEM((LANES, T), F32)],
        compiler_params=_params(("arbitrary",)),
        name=name,
    )(h, h, h, bf, qg, kg, dq, dk, dcrow)


def _head_masks(dh):
    first = _iota2((1, LANES), 1) < dh
    return first, jnp.logical_not(first)


def _per_head(tile, dh):
    return jnp.stack([jnp.where(mask, tile, 0) for mask in _head_masks(dh)])


def _both(tile):
    return jnp.stack([tile, tile])


FLASH_SUB = 128


def _flash_scores(q_m, k, ccol, crow, row0, diagonal):
    s = _dot(q_m, k, _NT) + ccol - crow
    if diagonal:
        s = jnp.where(_iota2(s.shape, s.ndim - 1) <= row0 + _iota2(s.shape, s.ndim - 2), s, NEG)
    return s


def _causal_blocks(nq, by_query):
    if by_query:
        pairs = [(i, j) for i in range(nq) for j in range(i + 1)]
    else:
        pairs = [(i, j) for j in range(nq) for i in range(j, nq)]
    return (jnp.asarray([a for a, _ in pairs], jnp.int32), jnp.asarray([b for _, b in pairs], jnp.int32))


def _flash_blocks(i, j, tq, fn, sub=FLASH_SUB):
    sub = min(sub, tq)
    for diagonal in (False, True):
        @pl.when((j == i) if diagonal else (j < i))
        def _():
            for r in range(tq // sub):
                fn(slice(r * sub, (r + 1) * sub), r * sub, diagonal)


def _flash_fwd(q, k, h, ccol, crow, H, dh, name):
    S, W = q.shape
    tq = tk = _pick(S, 512, LANES)
    nq = S // tq
    npair = H // 2
    vblk = 2 * W // LANES

    def body(ii_ref, jj_ref, q_ref, k_ref, v_ref, ccol_ref, crow_ref, o_ref, lse_ref, m_s, l_s, acc_s):
        i, j = ii_ref[pl.program_id(1)], jj_ref[pl.program_id(1)]

        @pl.when(j == 0)
        def _():
            m_s[...] = jnp.full_like(m_s, NEG)
            l_s[...] = jnp.zeros_like(l_s)
            acc_s[...] = jnp.zeros_like(acc_s)

        def tile(rows, row0, diagonal):
            kv, vv = _both(k_ref[...]), _both(v_ref[...].astype(MXU_DTYPE))
            s = _flash_scores(_per_head(q_ref[rows, :], dh), kv, ccol_ref[:, rows, :], crow_ref[...], row0, diagonal)
            m_old = m_s[:, rows, :]
            m_new = jnp.maximum(m_old, jnp.max(s, axis=-1, keepdims=True))
            alpha = jnp.exp(m_old - m_new)
            p = jnp.exp(s - m_new)
            l_s[:, rows, :] = alpha * l_s[:, rows, :] + jnp.sum(p, axis=-1, keepdims=True)
            acc_s[:, rows, :] = alpha * acc_s[:, rows, :] + _dot(p, vv, _NN)
            m_s[:, rows, :] = m_new

        _flash_blocks(i, j, tq, tile)

        @pl.when(j == i)
        def _():
            first, _ = _head_masks(dh)
            o_ref[...] = jnp.where(first, acc_s[0] / l_s[0], acc_s[1] / l_s[1])
            for a in range(2):
                lse_ref[a] = m_s[a] + jnp.log(l_s[a])

    ii, jj = _causal_blocks(nq, by_query=True)
    return pl.pallas_call(
        body,
        grid_spec=pltpu.PrefetchScalarGridSpec(
            num_scalar_prefetch=2,
            grid=(npair, len(ii)),
            in_specs=[pl.BlockSpec((tq, LANES), lambda p, t, ii, jj: (ii[t], p)),
                      pl.BlockSpec((tk, LANES), lambda p, t, ii, jj: (jj[t], p)),
                      pl.BlockSpec((tk, LANES), lambda p, t, ii, jj: (jj[t], vblk + p)),
                      pl.BlockSpec((2, tq, 1), lambda p, t, ii, jj: (p, ii[t], 0)),
                      pl.BlockSpec((2, 1, tk), lambda p, t, ii, jj: (p, 0, jj[t]))],
            out_specs=[pl.BlockSpec((tq, LANES), lambda p, t, ii, jj: (ii[t], p)),
                       pl.BlockSpec((2, tq, 1), lambda p, t, ii, jj: (p, ii[t], 0))],
            scratch_shapes=[pltpu.VMEM((2, tq, 1), F32), pltpu.VMEM((2, tq, 1), F32), pltpu.VMEM((2, tq, LANES), F32)]),
        out_shape=[jax.ShapeDtypeStruct((S, W), F32), jax.ShapeDtypeStruct((H, S, 1), F32)],
        compiler_params=_params(("parallel", "arbitrary")),
        name=name,
    )(ii, jj, q, k, h, ccol, crow)


def _flash_bwd_kv(q, k, h, ccol, crow, lse, delta, resid, do, H, dh, name):
    S, W = q.shape
    tq = tk = _pick(S, 512, LANES)
    nq = S // tq
    npair = H // 2
    vblk = 2 * W // LANES

    def body(ii_ref, jj_ref, q_ref, k_ref, v_ref, ccol_ref, crow_ref, lse_ref, dl_ref, rs_ref, do_ref, dk_ref, dv_ref, dcr_ref):
        i, j = ii_ref[pl.program_id(1)], jj_ref[pl.program_id(1)]

        @pl.when(i == j)
        def _():
            dk_ref[...] = jnp.zeros_like(dk_ref)
            dv_ref[...] = jnp.zeros_like(dv_ref)
            dcr_ref[...] = jnp.zeros_like(dcr_ref)

        def tile(rows, row0, diagonal):
            kv, vv = _both(k_ref[...]), _both(v_ref[...].astype(MXU_DTYPE))
            q_m = _per_head(q_ref[rows, :], dh)
            do_m = _per_head(do_ref[rows, :].astype(MXU_DTYPE), dh)
            s = _flash_scores(q_m, kv, ccol_ref[:, rows, :], crow_ref[...], row0, diagonal)
            p = jnp.exp(s - lse_ref[:, rows, :])
            dv = _dot(p, do_m, _TN)
            ds = p * (_dot(do_m, vv, _NT) - (dl_ref[:, rows, :] + rs_ref[:, rows, :]))
            dk = _dot(ds, q_m, _TN)
            dv_ref[...] += dv[0] + dv[1]
            dk_ref[...] += dk[0] + dk[1]
            dcr_ref[...] -= jnp.sum(ds, axis=1, keepdims=True)

        _flash_blocks(i, j, tq, tile)

    ii, jj = _causal_blocks(nq, by_query=False)
    qrow = pl.BlockSpec((2, tq, 1), lambda p, t, ii, jj: (p, ii[t], 0))
    return pl.pallas_call(
        body,
        grid_spec=pltpu.PrefetchScalarGridSpec(
            num_scalar_prefetch=2,
            grid=(npair, len(ii)),
            in_specs=[pl.BlockSpec((tq, LANES), lambda p, t, ii, jj: (ii[t], p)),
                      pl.BlockSpec((tk, LANES), lambda p, t, ii, jj: (jj[t], p)),
                      pl.BlockSpec((tk, LANES), lambda p, t, ii, jj: (jj[t], vblk + p)),
                      qrow,
                      pl.BlockSpec((2, 1, tk), lambda p, t, ii, jj: (p, 0, jj[t])),
                      qrow, qrow, qrow,
                      pl.BlockSpec((tq, LANES), lambda p, t, ii, jj: (ii[t], p))],
            out_specs=[pl.BlockSpec((tk, LANES), lambda p, t, ii, jj: (jj[t], p)),
                       pl.BlockSpec((tk, LANES), lambda p, t, ii, jj: (jj[t], p)),
                       pl.BlockSpec((2, 1, tk), lambda p, t, ii, jj: (p, 0, jj[t]))]),
        out_shape=[jax.ShapeDtypeStruct((S, W), F32), jax.ShapeDtypeStruct((S, W), F32),
                   jax.ShapeDtypeStruct((H, 1, S), F32)],
        compiler_params=_params(("parallel", "arbitrary")),
        name=name,
    )(ii, jj, q, k, h, ccol, crow, lse, delta, resid, do)


def _flash_bwd_q(q, k, h, ccol, crow, lse, delta, do, H, dh, name):
    S, W = q.shape
    tq = tk = _pick(S, 512, LANES)
    nq = S // tq
    npair = H // 2
    vblk = 2 * W // LANES

    def body(ii_ref, jj_ref, q_ref, k_ref, v_ref, ccol_ref, crow_ref, lse_ref, dl_ref, do_ref, dq_ref, rs_ref):
        i, j = ii_ref[pl.program_id(1)], jj_ref[pl.program_id(1)]

        @pl.when(j == 0)
        def _():
            dq_ref[...] = jnp.zeros_like(dq_ref)
            rs_ref[...] = jnp.zeros_like(rs_ref)

        def tile(rows, row0, diagonal):
            kv, vv = k_ref[...], _both(v_ref[...].astype(MXU_DTYPE))
            s = _flash_scores(_per_head(q_ref[rows, :], dh), _both(kv), ccol_ref[:, rows, :], crow_ref[...], row0, diagonal)
            p = jnp.exp(s - lse_ref[:, rows, :])
            ds = p * (_dot(_per_head(do_ref[rows, :].astype(MXU_DTYPE), dh), vv, _NT) - dl_ref[:, rows, :])
            dq = _dot(ds, _per_head(kv, dh), _NN)
            dq_ref[rows, :] += dq[0] + dq[1]
            rs_ref[:, rows, :] += jnp.sum(ds, axis=-1, keepdims=True)

        _flash_blocks(i, j, tq, tile, sub=tq)

    ii, jj = _causal_blocks(nq, by_query=True)
    qrow = pl.BlockSpec((2, tq, 1), lambda p, t, ii, jj: (p, ii[t], 0))
    return pl.pallas_call(
        body,
        grid_spec=pltpu.PrefetchScalarGridSpec(
            num_scalar_prefetch=2,
            grid=(npair, len(ii)),
            in_specs=[pl.BlockSpec((tq, LANES), lambda p, t, ii, jj: (ii[t], p)),
                      pl.BlockSpec((tk, LANES), lambda p, t, ii, jj: (jj[t], p)),
                      pl.BlockSpec((tk, LANES), lambda p, t, ii, jj: (jj[t], vblk + p)),
                      qrow,
                      pl.BlockSpec((2, 1, tk), lambda p, t, ii, jj: (p, 0, jj[t])),
                      qrow, qrow,
                      pl.BlockSpec((tq, LANES), lambda p, t, ii, jj: (ii[t], p))],
            out_specs=[pl.BlockSpec((tq, LANES), lambda p, t, ii, jj: (ii[t], p)), qrow]),
        out_shape=[jax.ShapeDtypeStruct((S, W), F32), jax.ShapeDtypeStruct((H, S, 1), F32)],
        compiler_params=_params(("parallel", "arbitrary")),
        name=name,
    )(ii, jj, q, k, h, ccol, crow, lse, delta, do)


def _fox_gate_fwd(o, h, W, name):
    def fn(rows, _, __):
        return [rows[0] * _silu(rows[1])], [], []

    return _rowwise(fn, [o, (h, W, 3)], [], [(W, MXU_DTYPE)], [], tile=256, name=name)[0]


def _fox_gate_bwd(o, h, dog, H, dh, name):
    S, W = o.shape
    T = _pick(S, 256, SUBLANES)

    def body(o_ref, z_ref, d_ref, do_ref, dz_ref, dl_ref):
        ov, zv, dv = o_ref[...], z_ref[...], d_ref[...]
        sg = _sigmoid(zv)
        do = dv * zv * sg
        do_ref[...] = do
        dz_ref[...] = dv * ov * sg * (1.0 + zv * (1.0 - sg))
        prod = do * ov
        for p in range(H // 2):
            blk = prod[:, p * LANES:(p + 1) * LANES]
            first = _iota2(blk.shape, 1) < dh
            dl_ref[2 * p] = jnp.sum(jnp.where(first, blk, 0.0), axis=1, keepdims=True)
            dl_ref[2 * p + 1] = jnp.sum(jnp.where(first, 0.0, blk), axis=1, keepdims=True)

    return pl.pallas_call(
        body,
        grid=(S // T,),
        in_specs=[pl.BlockSpec((T, W), lambda i: (i, 0)), pl.BlockSpec((T, W), lambda i: (i, 3)),
                  pl.BlockSpec((T, W), lambda i: (i, 0))],
        out_specs=[pl.BlockSpec((T, W), lambda i: (i, 0)), pl.BlockSpec((T, W), lambda i: (i, 0)),
                   pl.BlockSpec((H, T, 1), lambda i: (0, i, 0))],
        out_shape=[jax.ShapeDtypeStruct((S, W), F32), jax.ShapeDtypeStruct((S, W), F32),
                   jax.ShapeDtypeStruct((H, S, 1), F32)],
        compiler_params=_params(("parallel",)),
        name=name,
    )(o, h, dog)


def _gdn_layer_fwd(x, w, tag):
    H = w["H"]
    qk = H * LANES
    h = _matmul(x, w["w_in"], "nn", name=f"{tag}_in")
    c = _conv_fwd(h, w["conv"], 3 * qk, name=f"{tag}_conv")
    og, states = _gdn_scan_fwd(c, h, w["hp"], w["ng"], H, name=f"{tag}_scan")
    y = _matmul(og, w["w_out"], "nn", name=f"{tag}_out")
    return y, (x, h, c, states, og)


def _gdn_layer_bwd(dy, res, w, tag):
    x, h, c, states, og = res
    H = w["H"]
    qk = H * LANES
    dog = _matmul(dy, w["w_out"], "nt", name=f"{tag}_out_dx")
    dw_out = _matmul(og, dy, "tn", name=f"{tag}_out_dw")
    dq, dk, dv, dz, dba, dhp, dng = _gdn_scan_bwd(c, h, w["hp"], w["ng"], states, dog, H, name=f"{tag}_scan_bwd")
    dh_parts, dconv = [], []
    for part, d in enumerate((dq, dk, dv)):
        dh_p, dw_p = _conv_bwd(d, h, w["conv"][:, part * qk:(part + 1) * qk], part * qk, name=f"{tag}_conv_bwd{part}")
        dh_parts.append(dh_p)
        dconv.append(dw_p[:GDN_CONV])
    dh = jnp.concatenate(dh_parts + [dz, dba], axis=1)
    dx = _matmul(dh, w["w_in"], "nt", name=f"{tag}_in_dx")
    dw_in = _matmul(x, dh, "tn", name=f"{tag}_in_dw")
    grads = {"w_in": dw_in, "w_out": dw_out, "conv": jnp.concatenate(dconv, axis=1),
             "a_log": dhp[0, :H], "dt_bias": dhp[1, :H], "norm_g": dng[0]}
    return dx, grads


def _fox_layer_fwd(x, w, tag):
    H, dh = w["H"], w["dh"]
    W = H * dh
    h = _matmul(x, w["w_in"], "nn", name=f"{tag}_in")
    q, k, ccol, crow = _fox_prep_fwd(h, w["bf"], w["qg"], w["kg"], H, dh, name=f"{tag}_prep")
    o, lse = _flash_fwd(q, k, h, ccol, crow, H, dh, name=f"{tag}_flash")
    og = _fox_gate_fwd(o, h, W, name=f"{tag}_gate")
    y = _matmul(og, w["w_out"], "nn", name=f"{tag}_out")
    return y, (x, h, q, k, ccol, crow, o, lse, og)


def _fox_layer_bwd(dy, res, w, tag):
    x, h, q, k, ccol, crow, o, lse, og = res
    H, dh = w["H"], w["dh"]
    dog = _matmul(dy, w["w_out"], "nt", name=f"{tag}_out_dx")
    dw_out = _matmul(og, dy, "tn", name=f"{tag}_out_dw")
    do, dz, delta = _fox_gate_bwd(o, h, dog, H, dh, name=f"{tag}_gate_bwd")
    dqq, resid = _flash_bwd_q(q, k, h, ccol, crow, lse, delta, do, H, dh, name=f"{tag}_flash_bwd_q")
    dkk, dvv, dcrow = _flash_bwd_kv(q, k, h, ccol, crow, lse, delta, resid, do, H, dh, name=f"{tag}_flash_bwd_kv")
    dhq, dhk, df, dbf, dqg, dkg = _fox_prep_bwd(h, w["bf"], w["qg"], w["kg"], dqq, dkk, dcrow, H, dh, name=f"{tag}_prep_bwd")
    dhh = jnp.concatenate([dhq, dhk, dvv, dz, df], axis=1)
    dx = _matmul(dhh, w["w_in"], "nt", name=f"{tag}_in_dx")
    dw_in = _matmul(x, dhh, "tn", name=f"{tag}_in_dw")
    grads = {"w_in": dw_in, "w_out": dw_out, "b_f": dbf[0, :H],
             "q_norm_g": dqg[0, :dh] + dqg[0, dh:], "k_norm_g": dkg[0, :dh] + dkg[0, dh:]}
    return dx, grads


def _pad_cols(w, n):
    return jnp.pad(w, ((0, 0), (0, n - w.shape[1])))


def _build_layers(full, small):
    depth = small["ln_g"].shape[0]
    gh = small["gdn_a_log"].shape[1]
    fh, dh = small["fox_b_f"].shape[1], small["fox_q_norm_g"].shape[1]
    layers = []
    for i in range(depth):
        j = i // 2
        w = {"ln_g": small["ln_g"][i][None], "ln_b": small["ln_b"][i][None],
             "w_gate": full["ple_w_gate"][i], "w_proj": full["ple_w_proj"][i]}
        if i % 2 == 0:
            hp = jnp.zeros((SUBLANES, LANES), F32).at[0, :gh].set(small["gdn_a_log"][j]).at[1, :gh].set(small["gdn_dt_bias"][j])
            w.update(kind="gdn", H=gh, w_in=_pad_cols(full["gdn_w_in"][j], 4 * gh * LANES + LANES),
                     conv=full["gdn_conv_w"][j], hp=hp, ng=small["gdn_norm_g"][j][None], w_out=full["gdn_w_out"][j])
        else:
            bf = jnp.zeros((1, LANES), F32).at[0, :fh].set(small["fox_b_f"][j])
            w.update(kind="fox", H=fh, dh=dh, w_in=_pad_cols(full["fox_w_in"][j], 4 * fh * dh + LANES), bf=bf,
                     qg=jnp.tile(small["fox_q_norm_g"][j], 2)[None], kg=jnp.tile(small["fox_k_norm_g"][j], 2)[None],
                     w_out=full["fox_w_out"][j])
        layers.append(w)
    return layers


def _local_step(x, p, target, layers):
    depth = len(layers)
    alpha = (2 * depth) ** 0.25
    saved = []
    for i, w in enumerate(layers):
        tag = f"l{i}"
        if w["kind"] == "gdn":
            y, res = _gdn_layer_fwd(x, w, tag)
        else:
            y, res = _fox_layer_fwd(x, w, tag)
        x1 = _ln_fwd(x, y, w["ln_g"], w["ln_b"], alpha, name=f"{tag}_ln")
        gate_pre = _matmul(x1, w["w_gate"], "nn", name=f"{tag}_gate_mm")
        pp = _matmul(p[i], w["w_proj"], "nn", name=f"{tag}_proj_mm")
        x2 = _ple_fwd(x1, gate_pre, pp, name=f"{tag}_ple")
        saved.append((res, x, y, x1, gate_pre, pp))
        x = x2
    dx, loss_tile = _loss_head(x, target, name="loss_head")
    grads = [None] * depth
    for i in reversed(range(depth)):
        w = layers[i]
        tag = f"l{i}"
        res, xin, y, x1, gate_pre, pp = saved[i]
        dgp, dpp = _ple_bwd(dx, gate_pre, pp, name=f"{tag}_ple_bwd")
        dx1 = _add(dx, _matmul(dgp, w["w_gate"], "nt", name=f"{tag}_gate_dx"), name=f"{tag}_dx1")
        dw_gate = _matmul(x1, dgp, "tn", name=f"{tag}_gate_dw")
        dw_proj = _matmul(p[i], dpp, "tn", name=f"{tag}_proj_dw")
        dy, dg, db = _ln_bwd(xin, y, w["ln_g"], w["ln_b"], dx1, alpha, name=f"{tag}_ln_bwd")
        if w["kind"] == "gdn":
            dxm, g = _gdn_layer_bwd(dy, res, w, tag)
        else:
            dxm, g = _fox_layer_bwd(dy, res, w, tag)
        dx = _axpy(alpha, dy, dxm, name=f"{tag}_dx")
        g.update({"w_gate": dw_gate, "w_proj": dw_proj, "ln_g": dg[0], "ln_b": db[0]})
        grads[i] = g
    return loss_tile, dx, grads


MESH_ID = pl.DeviceIdType.MESH
HBM_SPEC = pl.BlockSpec(memory_space=pl.ANY)
PACK_COLS = 1024
PACK_ROWS = 256


def _all_gather(shards, name):
    nt = len(shards)

    def body(*refs):
        x_refs, out_refs = refs[:nt], refs[nt:2 * nt]
        send_sems, recv_sems, local_sems = refs[2 * nt:]
        x, y, c = lax.axis_index("x"), lax.axis_index("y"), lax.axis_index("c")
        me, sibling = (x, y, c), (x, y, 1 - c)
        chips = [(1 - x, y), (x, 1 - y), (1 - x, 1 - y)]

        def slot(t, px, py, pc):
            return out_refs[t].at[4 * px + 2 * py + pc]

        def copy(k, t, block, to, src=None):
            return pltpu.make_async_remote_copy(
                src_ref=slot(t, *block) if src is None else src, dst_ref=slot(t, *block),
                send_sem=send_sems.at[k, t], recv_sem=recv_sems.at[k, t], device_id=to, device_id_type=MESH_ID)

        every = range(nt)
        mine = [pltpu.make_async_copy(x_refs[t], slot(t, *me), local_sems.at[t]) for t in every]
        for cp in mine:
            cp.start()
        first = [copy(0, t, me, sibling, src=x_refs[t]) for t in every]
        first += [copy(1 + j, t, me, (*chip, c), src=x_refs[t]) for j, chip in enumerate(chips) for t in every]
        for cp in first:
            cp.start()
        passed = []
        for j, chip in enumerate(chips):
            for t in every:
                copy(1 + j, t, (*chip, c), me).wait_recv()
                passed.append(copy(4 + j, t, (*chip, c), sibling))
                passed[-1].start()
        for t in every:
            copy(0, t, sibling, me).wait_recv()
        for j, chip in enumerate(chips):
            for t in every:
                copy(4 + j, t, (*chip, 1 - c), me).wait_recv()
        for cp in first + passed:
            cp.wait_send()
        for cp in mine:
            cp.wait()

    return pl.pallas_call(
        body,
        out_shape=[jax.ShapeDtypeStruct((N_DEV, *s.shape), s.dtype) for s in shards],
        in_specs=[HBM_SPEC] * nt,
        out_specs=[HBM_SPEC] * nt,
        scratch_shapes=[pltpu.SemaphoreType.DMA((7, nt)), pltpu.SemaphoreType.DMA((7, nt)), pltpu.SemaphoreType.DMA((nt,))],
        name=name,
    )(*shards)


def _all_to_all(slabs, name):
    nt = len(slabs)

    def body(*refs):
        g_refs, out_refs = refs[:nt], refs[nt:2 * nt]
        send_sems, recv_sems, local_sems = refs[2 * nt:]
        x, y, c = lax.axis_index("x"), lax.axis_index("y"), lax.axis_index("c")
        me = 4 * x + 2 * y + c
        mine = [pltpu.make_async_copy(g_refs[t].at[me], out_refs[t].at[me], local_sems.at[t]) for t in range(nt)]
        for cp in mine:
            cp.start()
        copies = []
        for k in range(1, N_DEV):
            px = 1 - x if k & 4 else x
            py = 1 - y if k & 2 else y
            pc = 1 - c if k & 1 else c
            peer = 4 * px + 2 * py + pc
            for t in range(nt):
                copies.append(tuple(
                    pltpu.make_async_remote_copy(src_ref=g_refs[t].at[peer], dst_ref=out_refs[t].at[dst],
                                                 send_sem=send_sems.at[k - 1, t], recv_sem=recv_sems.at[k - 1, t],
                                                 device_id=(px, py, pc), device_id_type=MESH_ID)
                    for dst in (me, peer)))
        for send, _ in copies:
            send.start()
        for send, arrive in copies:
            arrive.wait_recv()
            send.wait_send()
        for cp in mine:
            cp.wait()

    return pl.pallas_call(
        body,
        out_shape=[jax.ShapeDtypeStruct(s.shape, s.dtype) for s in slabs],
        in_specs=[HBM_SPEC] * nt,
        out_specs=[HBM_SPEC] * nt,
        scratch_shapes=[pltpu.SemaphoreType.DMA((7, nt)), pltpu.SemaphoreType.DMA((7, nt)), pltpu.SemaphoreType.DMA((nt,))],
        name=name,
    )(*slabs)


def _pack(flats, dtype):
    flat = jnp.concatenate([f.astype(dtype).reshape(-1) for f in flats])
    unit = PACK_ROWS * PACK_COLS
    n = -(-flat.shape[0] // unit) * unit
    return jnp.pad(flat, (0, n - flat.shape[0])).reshape(n // PACK_COLS, PACK_COLS)


def _unpack(buf, shapes):
    lead = buf.shape[:-2]
    flat = buf.reshape(*lead, -1)
    out, off = [], 0
    for s in shapes:
        n = math.prod(s)
        out.append(flat[..., off:off + n].reshape(*lead, *s))
        off += n
    return out


_ROW_SPLIT = ("ple_w_gate", "gdn_w_out", "fox_w_out")
_COL_SPLIT = ("ple_w_proj", "gdn_w_in", "gdn_conv_w", "fox_w_in")
_SHARDED = ("ple_w_gate", "ple_w_proj", "gdn_w_in", "gdn_conv_w", "gdn_w_out", "fox_w_in", "fox_w_out")
_REPLICATED = ("ln_g", "ln_b", "gdn_a_log", "gdn_dt_bias", "gdn_norm_g", "fox_b_f", "fox_q_norm_g", "fox_k_norm_g")
_WEIGHTS = ("ln_g", "ln_b", "ple_w_gate", "ple_w_proj", "gdn_w_in", "gdn_conv_w", "gdn_a_log", "gdn_dt_bias",
            "gdn_norm_g", "gdn_w_out", "fox_w_in", "fox_b_f", "fox_q_norm_g", "fox_k_norm_g", "fox_w_out")


def _join(name, gathered):
    n, l, a, b = gathered.shape
    if name in _ROW_SPLIT:
        return gathered.transpose(1, 0, 2, 3).reshape(l, n * a, b)
    return gathered.transpose(1, 2, 0, 3).reshape(l, a, n * b)


def _split(name, full):
    l, a, b = full.shape
    if name in _ROW_SPLIT:
        return full.reshape(l, N_DEV, a // N_DEV, b).transpose(1, 0, 2, 3)
    return full.reshape(l, a, N_DEV, b // N_DEV).transpose(2, 0, 1, 3)


def _adamw(w, g_parts, m, v, name):
    shape = w.shape
    R, C = math.prod(shape[:-1]), shape[-1]
    tr = _pick(R, 256, SUBLANES)
    c1 = 1.0 - ADAM_B1 ** ADAM_STEP
    c2 = 1.0 - ADAM_B2 ** ADAM_STEP

    def body(w_ref, g_ref, m_ref, v_ref, go_ref, d_ref, mo_ref, vo_ref):
        gv = g_ref[0]
        for s in range(1, N_DEV):
            gv = gv + g_ref[s]
        mn = ADAM_B1 * m_ref[...] + (1.0 - ADAM_B1) * gv
        vn = ADAM_B2 * v_ref[...] + (1.0 - ADAM_B2) * jnp.square(gv)
        go_ref[...] = gv
        d_ref[...] = -ADAM_LR * ((mn / c1) / (jnp.sqrt(vn / c2) + ADAM_EPS) + ADAM_WD * w_ref[...])
        mo_ref[...] = mn
        vo_ref[...] = vn

    row = pl.BlockSpec((tr, C), lambda i: (i, 0))
    outs = pl.pallas_call(
        body,
        grid=(R // tr,),
        in_specs=[row, pl.BlockSpec((N_DEV, tr, C), lambda i: (0, i, 0)), row, row],
        out_specs=[row] * 4,
        out_shape=[jax.ShapeDtypeStruct((R, C), F32)] * 4,
        compiler_params=_params(("parallel",)),
        name=name,
    )(w.reshape(R, C), g_parts.reshape(N_DEV, R, C), m.reshape(R, C), v.reshape(R, C))
    return [o.reshape(shape) for o in outs]


def _train_step(x, p, target, w, m, v):
    shards = [w[n] if n == "gdn_conv_w" else w[n].astype(MXU_DTYPE) for n in _SHARDED]
    gathered = _all_gather(shards, name="gather_weights")
    full = {n: _join(n, part) for n, part in zip(_SHARDED, gathered)}
    layers = _build_layers(full, {n: w[n] for n in _REPLICATED})

    loss_tile, dx, grads = _local_step(x[0], p[:, 0], target[0], layers)
    loss = lax.psum(loss_tile[0, 0], ("x", "y", "c"))

    depth = len(layers)
    gdn_l = [i for i in range(depth) if i % 2 == 0]
    fox_l = [i for i in range(depth) if i % 2 == 1]

    def stack(key, idx):
        return jnp.stack([grads[i][key] for i in idx])

    full_g = {
        "ple_w_gate": stack("w_gate", range(depth)), "ple_w_proj": stack("w_proj", range(depth)),
        "gdn_w_in": stack("w_in", gdn_l)[..., :w["gdn_w_in"].shape[-1] * N_DEV], "gdn_conv_w": stack("conv", gdn_l),
        "gdn_w_out": stack("w_out", gdn_l),
        "fox_w_in": stack("w_in", fox_l)[..., :w["fox_w_in"].shape[-1] * N_DEV], "fox_w_out": stack("w_out", fox_l)}
    small_g = {
        "ln_g": stack("ln_g", range(depth)), "ln_b": stack("ln_b", range(depth)),
        "gdn_a_log": stack("a_log", gdn_l), "gdn_dt_bias": stack("dt_bias", gdn_l), "gdn_norm_g": stack("norm_g", gdn_l),
        "fox_b_f": stack("b_f", fox_l), "fox_q_norm_g": stack("q_norm_g", fox_l), "fox_k_norm_g": stack("k_norm_g", fox_l)}

    g_parts = dict(zip(_SHARDED, _all_to_all([_split(n, full_g[n]) for n in _SHARDED], name="scatter_grads")))
    small_all = _all_gather([_pack([small_g[n] for n in _REPLICATED], F32)], name="gather_small_grads")[0]
    g_parts.update(zip(_REPLICATED, _unpack(small_all, [w[n].shape for n in _REPLICATED])))

    g, delta, new_m, new_v = {}, {}, {}, {}
    for n in _WEIGHTS:
        g[n], delta[n], new_m[n], new_v[n] = _adamw(w[n], g_parts[n], m[n], v[n], name=f"adamw_{n}")
    return (loss, dx[None], *[g[n] for n in _WEIGHTS], *[delta[n] for n in _WEIGHTS],
            *[new_m[n] for n in _WEIGHTS], *[new_v[n] for n in _WEIGHTS])


def kernel(x, p, ln_g, ln_b, ple_w_gate, ple_w_proj, gdn_w_in, gdn_conv_w, gdn_a_log, gdn_dt_bias, gdn_norm_g, gdn_w_out, fox_w_in, fox_b_f, fox_q_norm_g, fox_k_norm_g, fox_w_out, loss_target, m_ln_g, m_ln_b, m_ple_w_gate, m_ple_w_proj, m_gdn_w_in, m_gdn_conv_w, m_gdn_a_log, m_gdn_dt_bias, m_gdn_norm_g, m_gdn_w_out, m_fox_w_in, m_fox_b_f, m_fox_q_norm_g, m_fox_k_norm_g, m_fox_w_out, v_ln_g, v_ln_b, v_ple_w_gate, v_ple_w_proj, v_gdn_w_in, v_gdn_conv_w, v_gdn_a_log, v_gdn_dt_bias, v_gdn_norm_g, v_gdn_w_out, v_fox_w_in, v_fox_b_f, v_fox_q_norm_g, v_fox_k_norm_g, v_fox_w_out):
    given = dict(locals())
    w = {n: given[n] for n in _WEIGHTS}
    m = {n: given["m_" + n] for n in _WEIGHTS}
    v = {n: given["v_" + n] for n in _WEIGHTS}
    return _train_step(x, p, loss_target, w, m, v)
```

```python
import functools
import math

import jax
import jax.numpy as jnp
from jax import lax
from jax.experimental import pallas as pl
from jax.experimental.pallas import tpu as pltpu

F32 = jnp.float32
BF16 = jnp.bfloat16
MXU_DTYPE = BF16
HI = lax.Precision.HIGHEST

N_DEV = 8
LANES = 128
SUBLANES = 8
VMEM_BYTES = 64 * 1024 * 1024

GDN_CHUNK = 64
GDN_CONV = 4
LN_EPS = 1e-5
RMS_EPS = 1e-6
NEG = -1e30

ADAM_LR = 0.001
ADAM_B1 = 0.9
ADAM_B2 = 0.999
ADAM_EPS = 1e-08
ADAM_WD = 0.01
ADAM_STEP = 10


def _params(semantics, vmem_mb=40):
    return pltpu.CompilerParams(dimension_semantics=semantics, vmem_limit_bytes=vmem_mb * 1024 * 1024)


def _pick(dim, cap, unit=LANES):
    if dim <= cap:
        return dim
    best = None
    for t in range(unit, cap + 1, unit):
        if dim % t == 0:
            best = t
    assert best is not None, (dim, cap)
    return best


def _dims(dims, ndim):
    if ndim == 2:
        return (dims, ((), ()))
    return (((dims[0][0] + 1,), (dims[1][0] + 1,)), ((0,), (0,)))


def _dot(a, b, dims):
    return lax.dot_general(a.astype(MXU_DTYPE), b.astype(MXU_DTYPE), _dims(dims, a.ndim), preferred_element_type=F32)


_NN = ((1,), (0,))
_NT = ((1,), (1,))
_TN = ((0,), (0,))


@jax.custom_vjp
def _mm_nn(a, b):
    return _dot(a, b, _NN)


@jax.custom_vjp
def _mm_nt(a, b):
    return _dot(a, b, _NT)


@jax.custom_vjp
def _mm_tn(a, b):
    return _dot(a, b, _TN)


_mm_nn.defvjp(lambda a, b: (_dot(a, b, _NN), (a, b)), lambda r, g: (_mm_nt(g, r[1]), _mm_tn(r[0], g)))
_mm_nt.defvjp(lambda a, b: (_dot(a, b, _NT), (a, b)), lambda r, g: (_mm_nn(g, r[1]), _mm_tn(g, r[0])))
_mm_tn.defvjp(lambda a, b: (_dot(a, b, _TN), (a, b)), lambda r, g: (_mm_nt(r[1], g), _mm_nn(r[0], g)))


def _mm_hi(a, b, precision=HI):
    return lax.dot_general(a, b, _dims(_NN, a.ndim), precision=precision, preferred_element_type=F32)


def _mm_3x(a, b):
    return _mm_hi(a, b, lax.Precision.HIGH)


def _sigmoid(x):
    return 1.0 / (1.0 + jnp.exp(-x))


def _silu(x):
    return x * _sigmoid(x)


def _softplus(x):
    return jnp.maximum(x, 0.0) + jnp.log(1.0 + jnp.exp(-jnp.abs(x)))


def _iota2(shape, dim):
    return lax.broadcasted_iota(jnp.int32, shape, dim)


def _lane_pick(tile, lane):
    return jnp.sum(jnp.where(_iota2(tile.shape, 1) == lane, tile, 0.0), axis=1, keepdims=True)


def _lane_put(col, lane, width=LANES):
    return jnp.where(_iota2((col.shape[0], width), 1) == lane, col, 0.0)


def _matmul(a, b, mode, out_dtype=F32, *, name, tm=512, tn=1408, tk=1408, a_cols=None, b_cols=None):
    def cols(arr, rng):
        return (0, arr.shape[1]) if rng is None else rng

    a0, an = cols(a, a_cols)
    b0, bn = cols(b, b_cols)
    if mode == "nn":
        M, K, N = a.shape[0], an, bn
        assert b.shape[0] == K
    elif mode == "nt":
        M, K, N = a.shape[0], an, b.shape[0]
        assert bn == K
    else:
        K, M, N = a.shape[0], an, bn
        assert b.shape[0] == K
    tm, tn, tk = _pick(M, tm), _pick(N, tn), _pick(K, tk)
    nk = K // tk
    if mode == "nn":
        assert a0 % tk == 0 and b0 % tn == 0
        a_spec = pl.BlockSpec((tm, tk), lambda i, j, k: (i, a0 // tk + k))
        b_spec = pl.BlockSpec((tk, tn), lambda i, j, k: (k, b0 // tn + j))
        dims = _NN
    elif mode == "nt":
        assert a0 % tk == 0 and b0 % tk == 0
        a_spec = pl.BlockSpec((tm, tk), lambda i, j, k: (i, a0 // tk + k))
        b_spec = pl.BlockSpec((tn, tk), lambda i, j, k: (j, b0 // tk + k))
        dims = _NT
    else:
        assert a0 % tm == 0 and b0 % tn == 0
        a_spec = pl.BlockSpec((tk, tm), lambda i, j, k: (k, a0 // tm + i))
        b_spec = pl.BlockSpec((tk, tn), lambda i, j, k: (k, b0 // tn + j))
        dims = _TN

    def body(a_ref, b_ref, o_ref, acc_ref):
        k = pl.program_id(2)

        @pl.when(k == 0)
        def _():
            acc_ref[...] = jnp.zeros_like(acc_ref)

        acc_ref[...] += _dot(a_ref[...], b_ref[...], dims)

        @pl.when(k == nk - 1)
        def _():
            o_ref[...] = acc_ref[...].astype(o_ref.dtype)

    return pl.pallas_call(
        body,
        grid=(M // tm, N // tn, nk),
        in_specs=[a_spec, b_spec],
        out_specs=pl.BlockSpec((tm, tn), lambda i, j, k: (i, j)),
        out_shape=jax.ShapeDtypeStruct((M, N), out_dtype),
        scratch_shapes=[pltpu.VMEM((tm, tn), F32)],
        compiler_params=_params(("parallel", "parallel", "arbitrary"), 48),
        name=name,
    )(a, b)


def _rowwise(fn, rows, consts, out_rows, out_accs, *, tile, name, reverse=False, carries=(), vmem_mb=40):
    rows = [r if isinstance(r, tuple) else (r, r.shape[1], 0) for r in rows]
    S = rows[0][0].shape[0]
    tile = _pick(S, tile, SUBLANES)
    nt = S // tile
    nr, nc, no, na = len(rows), len(consts), len(out_rows), len(out_accs)

    def ridx(i):
        return nt - 1 - i if reverse else i

    in_specs = [pl.BlockSpec((tile, w), functools.partial(lambda i, cb: (ridx(i), cb), cb=cb)) for _, w, cb in rows]
    in_specs += [pl.BlockSpec(c.shape, functools.partial(lambda i, nd: (0,) * nd, nd=c.ndim)) for c in consts]
    out_specs = [pl.BlockSpec((tile, c), lambda i: (ridx(i), 0)) for c, _ in out_rows]
    out_specs += [pl.BlockSpec(s, functools.partial(lambda i, nd: (0,) * nd, nd=len(s))) for s, _ in out_accs]
    out_shape = [jax.ShapeDtypeStruct((S, c), d) for c, d in out_rows]
    out_shape += [jax.ShapeDtypeStruct(s, d) for s, d in out_accs]

    def body(*refs):
        rin, cin = refs[:nr], refs[nr:nr + nc]
        rout, aout = refs[nr + nc:nr + nc + no], refs[nr + nc + no:nr + nc + no + na]
        carr = refs[nr + nc + no + na:]
        step = pl.program_id(0)

        @pl.when(step == 0)
        def _():
            for r in aout + carr:
                r[...] = jnp.zeros_like(r)

        outs, accs, newc = fn([r[...] for r in rin], [c[...] for c in cin], [c[...] for c in carr])
        for r, o in zip(rout, outs, strict=True):
            r[...] = o.astype(r.dtype)
        for r, v in zip(aout, accs, strict=True):
            r[...] += v
        for r, v in zip(carr, newc, strict=True):
            r[...] = v

    res = pl.pallas_call(
        body,
        grid=(nt,),
        in_specs=in_specs,
        out_specs=out_specs,
        out_shape=out_shape,
        scratch_shapes=[pltpu.VMEM(s, F32) for s in carries],
        compiler_params=_params(("arbitrary",), vmem_mb),
        name=name,
    )(*[r[0] for r in rows], *consts)
    return res


def _ln_fn(x, y, g, b, alpha):
    r = alpha * x + y
    mu = jnp.mean(r, -1, keepdims=True)
    var = jnp.mean(jnp.square(r - mu), -1, keepdims=True)
    return (r - mu) * lax.rsqrt(var + LN_EPS) * g + b


def _ln_fwd(x, y, g, b, alpha, name):
    D = x.shape[1]

    def fn(rows, consts, _):
        return [_ln_fn(rows[0], rows[1], consts[0], consts[1], alpha)], [], []

    return _rowwise(fn, [x, y], [g, b], [(D, F32)], [], tile=256, name=name)[0]


def _ln_bwd(x, y, g, b, dx1, alpha, name):
    D = x.shape[1]

    def fn(rows, consts, _):
        xv, yv, d = rows
        _, vjp = jax.vjp(lambda yy, gg, bb: _ln_fn(xv, yy, gg, bb, alpha), yv, consts[0], consts[1])
        dy, dg, db = vjp(d)
        return [dy], [dg, db], []

    return _rowwise(fn, [x, y, dx1], [g, b], [(D, F32)], [((1, D), F32), ((1, D), F32)], tile=256, name=name)


def _ple_fwd(x1, gate_pre, pp, name):
    D = x1.shape[1]

    def fn(rows, _, __):
        return [rows[0] + _sigmoid(rows[1]) * rows[2]], [], []

    return _rowwise(fn, [x1, gate_pre, pp], [], [(D, F32)], [], tile=256, name=name)[0]


def _ple_bwd(dx2, gate_pre, pp, name):
    D = dx2.shape[1]

    def fn(rows, _, __):
        d, gp, ppv = rows
        s = _sigmoid(gp)
        return [d * ppv * s * (1.0 - s), d * s], [], []

    return _rowwise(fn, [dx2, gate_pre, pp], [], [(D, F32), (D, F32)], [], tile=256, name=name)


def _add(a, b, name):
    def fn(rows, _, __):
        return [rows[0] + rows[1]], [], []

    return _rowwise(fn, [a, b], [], [(a.shape[1], F32)], [], tile=256, name=name)[0]


def _axpy(alpha, a, b, name):
    def fn(rows, _, __):
        return [alpha * rows[0] + rows[1]], [], []

    return _rowwise(fn, [a, b], [], [(a.shape[1], F32)], [], tile=256, name=name)[0]


def _loss_head(y, target, name):
    D = y.shape[1]

    def fn(rows, _, __):
        e = rows[0] - rows[1]
        part = 0.5 * jnp.sum(jnp.sum(e * e, axis=1, keepdims=True), axis=0, keepdims=True) / D
        return [e / D], [jnp.broadcast_to(part, (SUBLANES, LANES))], []

    return _rowwise(fn, [y, target], [], [(D, F32)], [((SUBLANES, LANES), F32)], tile=256, name=name)


def _conv_fwd(h, w, n_cols, name):
    S = h.shape[0]
    T = _pick(S, 512, SUBLANES)
    CB = _pick(n_cols, 512)
    nt = S // T
    K = GDN_CONV

    def body(x_ref, halo_ref, w_ref, o_ref, buf):
        i = pl.program_id(1)
        buf[0:SUBLANES, :] = jnp.where(i > 0, halo_ref[...], 0.0)
        buf[SUBLANES:, :] = x_ref[...]
        acc = jnp.zeros((T, CB), F32)
        for k in range(K):
            acc = acc + w_ref[k:k + 1, :] * buf[pl.ds(SUBLANES - (K - 1) + k, T), :]
        o_ref[...] = acc

    return pl.pallas_call(
        body,
        grid=(n_cols // CB, nt),
        in_specs=[pl.BlockSpec((T, CB), lambda c, i: (i, c)),
                  pl.BlockSpec((SUBLANES, CB), lambda c, i: (jnp.maximum(i * (T // SUBLANES) - 1, 0), c)),
                  pl.BlockSpec((K, CB), lambda c, i: (0, c))],
        out_specs=pl.BlockSpec((T, CB), lambda c, i: (i, c)),
        out_shape=jax.ShapeDtypeStruct((S, n_cols), F32),
        scratch_shapes=[pltpu.VMEM((T + SUBLANES, CB), F32)],
        compiler_params=_params(("parallel", "parallel")),
        name=name,
    )(h, h, w)


def _conv_bwd(dc, h, w, h_col0, name):
    S, n_cols = dc.shape
    T = _pick(S, 512, SUBLANES)
    CB = _pick(n_cols, 512)
    nt = S // T
    K = GDN_CONV
    assert h_col0 % CB == 0
    hb = h_col0 // CB

    def body(d_ref, halo_ref, x_ref, w_ref, dx_ref, dw_ref, buf):
        i = pl.program_id(1)

        @pl.when(i == 0)
        def _():
            dw_ref[...] = jnp.zeros_like(dw_ref)

        buf[0:T, :] = d_ref[...]
        buf[T:, :] = jnp.where(i < nt - 1, halo_ref[...], 0.0)
        x = x_ref[...]
        acc = jnp.zeros((T, CB), F32)
        for k in range(K):
            shifted = buf[pl.ds(K - 1 - k, T), :]
            acc = acc + w_ref[k:k + 1, :] * shifted
            dw_ref[k:k + 1, :] += jnp.sum(shifted * x, axis=0, keepdims=True)
        dx_ref[...] = acc

    last = S // SUBLANES - 1
    return pl.pallas_call(
        body,
        grid=(n_cols // CB, nt),
        in_specs=[pl.BlockSpec((T, CB), lambda c, i: (i, c)),
                  pl.BlockSpec((SUBLANES, CB), lambda c, i: (jnp.minimum((i + 1) * (T // SUBLANES), last), c)),
                  pl.BlockSpec((T, CB), lambda c, i: (i, hb + c)),
                  pl.BlockSpec((K, CB), lambda c, i: (0, c))],
        out_specs=[pl.BlockSpec((T, CB), lambda c, i: (i, c)),
                   pl.BlockSpec((SUBLANES, CB), lambda c, i: (0, c))],
        out_shape=[jax.ShapeDtypeStruct((S, n_cols), F32), jax.ShapeDtypeStruct((SUBLANES, n_cols), F32)],
        scratch_shapes=[pltpu.VMEM((T + SUBLANES, CB), F32)],
        compiler_params=_params(("parallel", "arbitrary")),
        name=name,
    )(dc, dc, h, w)


def _neumann_inverse(L):
    C = L.shape[-1]
    eye = (_iota2((C, C), 0) == _iota2((C, C), 1)).astype(F32)
    X = eye - L
    P = L
    for _ in range(max(0, math.ceil(math.log2(C)) - 1)):
        P = _mm_3x(P, P)
        X = _mm_3x(X, eye + P)
    return X


def _gdn_chunk(cq, ck, cv, zz, bcol, acol, alog, dtb, ng, state):
    G, C, dk = cq.shape
    q = _silu(cq)
    k = _silu(ck)
    v = _silu(cv)
    q = q * lax.rsqrt(jnp.sum(q * q, -1, keepdims=True) + RMS_EPS) * (dk ** -0.5)
    k = k * lax.rsqrt(jnp.sum(k * k, -1, keepdims=True) + RMS_EPS)
    beta = _sigmoid(bcol)
    g = -jnp.exp(alog) * _softplus(acol + dtb)

    row, col = _iota2((C, C), 0), _iota2((C, C), 1)
    causal, strict, eye = row >= col, row > col, row == col
    g_rows = jnp.swapaxes(jnp.broadcast_to(g, (G, C, C)), -1, -2)
    gc = jnp.sum(jnp.where(causal, g_rows, 0.0), axis=-1, keepdims=True)
    gcb = jnp.broadcast_to(gc, (G, C, C))
    g_last = jnp.sum(jnp.sum(jnp.where((row == C - 1) & (col == 0), gcb, 0.0), axis=-1, keepdims=True), axis=-2, keepdims=True)
    gc_rows = jnp.swapaxes(gcb, -1, -2)
    decay = jnp.exp(jnp.where(causal, gcb - gc_rows, NEG))

    kb = k * beta
    L = jnp.where(strict, _mm_nt(kb, k) * decay, 0.0)
    T = _neumann_inverse(L)
    u = _mm_3x(T, v * beta)
    w = _mm_3x(T, kb * jnp.exp(gc))
    a_qk = jnp.where(causal, _mm_nt(q, k) * decay, 0.0)
    q_dec = q * jnp.exp(gc)
    k_dec = k * jnp.exp(g_last - gc)
    v_new = u - _mm_nn(w, state)
    o = _mm_nn(q_dec, state) + _mm_nn(a_qk, v_new)
    new_state = state * jnp.exp(g_last) + _mm_tn(k_dec, v_new)
    y = o * lax.rsqrt(jnp.mean(o * o, -1, keepdims=True) + RMS_EPS) * ng * _silu(zz)
    return y, new_state


GDN_HEADS_PER_STEP = 8


def _gdn_specs(H, dk, G, chunk_of=lambda n: n):
    C = GDN_CHUNK
    NG = H // G
    cq = pl.BlockSpec((C, G * dk), lambda n, h: (chunk_of(n), h))
    ck = pl.BlockSpec((C, G * dk), lambda n, h: (chunk_of(n), NG + h))
    cv = pl.BlockSpec((C, G * dk), lambda n, h: (chunk_of(n), 2 * NG + h))
    zz = pl.BlockSpec((C, G * dk), lambda n, h: (chunk_of(n), 3 * NG + h))
    ba = pl.BlockSpec((C, LANES), lambda n, h: (chunk_of(n), 4 * H * dk // LANES))
    return cq, ck, cv, zz, ba


def _gdn_step_args(cq_ref, ck_ref, cv_ref, z_ref, ba_ref, hp_ref, hg, G, H, dk):
    ba = ba_ref[...]

    def heads(ref):
        return jnp.stack([ref[:, g * dk:(g + 1) * dk] for g in range(G)])

    def picks(tile, offset):
        return jnp.stack([_lane_pick(tile, offset + hg * G + g) for g in range(G)])

    return (heads(cq_ref), heads(ck_ref), heads(cv_ref), heads(z_ref), picks(ba, 0), picks(ba, H),
            picks(hp_ref[0:1, :], 0), picks(hp_ref[1:2, :], 0))


def _gdn_scan_fwd(c, h, hp, ng, H, name):
    S = c.shape[0]
    dk = c.shape[1] // (3 * H)
    assert dk == LANES
    C = GDN_CHUNK
    NC = S // C

    G = min(GDN_HEADS_PER_STEP, H)
    assert H % G == 0
    NG = H // G

    def body(cq_ref, ck_ref, cv_ref, z_ref, ba_ref, hp_ref, ng_ref, y_ref, s_ref, state):
        n, hg = pl.program_id(0), pl.program_id(1)
        heads = pl.ds(hg * G, G)

        @pl.when(n == 0)
        def _():
            state[heads] = jnp.zeros((G, dk, dk), F32)

        st = state[heads]
        s_ref[0] = st
        y, new_state = _gdn_chunk(*_gdn_step_args(cq_ref, ck_ref, cv_ref, z_ref, ba_ref, hp_ref, hg, G, H, dk), ng_ref[...], st)
        for g in range(G):
            y_ref[:, g * dk:(g + 1) * dk] = y[g].astype(y_ref.dtype)
        state[heads] = new_state

    cq, ck, cv, zz, ba = _gdn_specs(H, dk, G)
    return pl.pallas_call(
        body,
        grid=(NC, NG),
        in_specs=[cq, ck, cv, zz, ba, pl.BlockSpec((SUBLANES, LANES), lambda n, h: (0, 0)),
                  pl.BlockSpec((1, dk), lambda n, h: (0, 0))],
        out_specs=[pl.BlockSpec((C, G * dk), lambda n, h: (n, h)),
                   pl.BlockSpec((1, G, dk, dk), lambda n, h: (n, h, 0, 0))],
        out_shape=[jax.ShapeDtypeStruct((S, H * dk), MXU_DTYPE), jax.ShapeDtypeStruct((NC, H, dk, dk), F32)],
        scratch_shapes=[pltpu.VMEM((H, dk, dk), F32)],
        compiler_params=_params(("arbitrary", "arbitrary")),
        name=name,
    )(c, c, c, h, h, hp, ng)


def _gdn_scan_bwd(c, h, hp, ng, states, dy, H, name):
    S = c.shape[0]
    dk = c.shape[1] // (3 * H)
    C = GDN_CHUNK
    NC = S // C

    G = min(GDN_HEADS_PER_STEP, H)
    NG = H // G

    def body(cq_ref, ck_ref, cv_ref, z_ref, ba_ref, hp_ref, ng_ref, s_ref, dy_ref,
             dq_ref, dk_ref, dv_ref, dz_ref, dba_ref, dhp_ref, dng_ref, dstate):
        n, hg = pl.program_id(0), pl.program_id(1)

        @pl.when((n == 0) & (hg == 0))
        def _():
            dhp_ref[...] = jnp.zeros_like(dhp_ref)
            dng_ref[...] = jnp.zeros_like(dng_ref)

        heads = pl.ds(hg * G, G)

        @pl.when(n == 0)
        def _():
            dstate[heads] = jnp.zeros((G, dk, dk), F32)

        args = (*_gdn_step_args(cq_ref, ck_ref, cv_ref, z_ref, ba_ref, hp_ref, hg, G, H, dk), ng_ref[...], s_ref[0])
        _, vjp = jax.vjp(_gdn_chunk, *args)
        dy = jnp.stack([dy_ref[:, g * dk:(g + 1) * dk] for g in range(G)])
        dcq, dck, dcv, dzz, dbc, dac, dal, ddt, dng, dst = vjp((dy, dstate[heads]))
        dstate[heads] = dst
        dba = jnp.zeros((C, LANES), F32)
        dhp0 = jnp.zeros((1, LANES), F32)
        dhp1 = jnp.zeros((1, LANES), F32)
        for g in range(G):
            hd = hg * G + g
            sl = slice(g * dk, (g + 1) * dk)
            dq_ref[:, sl] = dcq[g]
            dk_ref[:, sl] = dck[g]
            dv_ref[:, sl] = dcv[g]
            dz_ref[:, sl] = dzz[g]
            dba = dba + _lane_put(dbc[g], hd) + _lane_put(dac[g], H + hd)
            dhp0 = dhp0 + _lane_put(dal[g], hd)
            dhp1 = dhp1 + _lane_put(ddt[g], hd)

        @pl.when(hg == 0)
        def _():
            dba_ref[...] = dba

        @pl.when(hg > 0)
        def _():
            dba_ref[...] += dba

        dhp_ref[0:1, :] += dhp0
        dhp_ref[1:2, :] += dhp1
        dng_ref[...] += dng

    rev = lambda n: NC - 1 - n
    blk = pl.BlockSpec
    in_specs = [*_gdn_specs(H, dk, G, rev),
                blk((SUBLANES, LANES), lambda n, h: (0, 0)), blk((1, dk), lambda n, h: (0, 0)),
                blk((1, G, dk, dk), lambda n, h: (rev(n), h, 0, 0)), blk((C, G * dk), lambda n, h: (rev(n), h))]
    out_specs = [blk((C, G * dk), lambda n, h: (rev(n), h))] * 4 + [
        blk((C, LANES), lambda n, h: (rev(n), 0)),
        blk((SUBLANES, LANES), lambda n, h: (0, 0)), blk((1, dk), lambda n, h: (0, 0))]
    out_shape = [jax.ShapeDtypeStruct((S, H * dk), F32)] * 4 + [
        jax.ShapeDtypeStruct((S, LANES), F32), jax.ShapeDtypeStruct((SUBLANES, LANES), F32),
        jax.ShapeDtypeStruct((1, dk), F32)]
    return pl.pallas_call(
        body,
        grid=(NC, NG),
        in_specs=in_specs,
        out_specs=out_specs,
        out_shape=out_shape,
        scratch_shapes=[pltpu.VMEM((H, dk, dk), F32)],
        compiler_params=_params(("arbitrary", "arbitrary")),
        name=name,
    )(c, c, c, h, h, hp, ng, states, dy)


def _pair_rms(x, gain, dh):
    first = _iota2(x.shape, 1) < dh
    sq = x * x
    ss_a = jnp.sum(jnp.where(first, sq, 0.0), axis=1, keepdims=True)
    ss_b = jnp.sum(jnp.where(first, 0.0, sq), axis=1, keepdims=True)
    inv = jnp.where(first, lax.rsqrt(ss_a / dh + RMS_EPS), lax.rsqrt(ss_b / dh + RMS_EPS))
    return x * inv * gain


def _log_sigmoid(x):
    return jnp.minimum(x, 0.0) - jnp.log(1.0 + jnp.exp(-jnp.abs(x)))


def _cum_fn(fr, bf, carry):
    T = fr.shape[0]
    tril = (_iota2((T, T), 0) >= _iota2((T, T), 1)).astype(F32)
    c = _mm_hi(tril, _log_sigmoid(fr + bf)) + carry
    last = jnp.sum(jnp.where(_iota2(c.shape, 0) == T - 1, c, 0.0), axis=0, keepdims=True)
    return c, last


def _fox_prep_fwd(h, bf, qg, kg, H, dh, name):
    S = h.shape[0]
    W = H * dh
    assert 2 * dh == LANES and H <= LANES
    T = _pick(S, 256, LANES)
    nt = S // T
    npair = H // 2

    def body(hq_ref, hk_ref, f_ref, bf_ref, qg_ref, kg_ref, q_ref, k_ref, ccol_ref, crow_ref, carry):
        i = pl.program_id(0)

        @pl.when(i == 0)
        def _():
            carry[...] = jnp.zeros_like(carry)

        for p in range(npair):
            sl = slice(p * LANES, (p + 1) * LANES)
            q_ref[:, sl] = (_pair_rms(hq_ref[:, sl], qg_ref[...], dh) * (dh ** -0.5)).astype(q_ref.dtype)
            k_ref[:, sl] = _pair_rms(hk_ref[:, sl], kg_ref[...], dh).astype(k_ref.dtype)
        c, last = _cum_fn(f_ref[...], bf_ref[...], carry[...])
        carry[...] = last
        ct = c.T
        for hh in range(H):
            ccol_ref[hh] = c[:, hh:hh + 1]
            crow_ref[hh] = ct[hh:hh + 1, :]

    return pl.pallas_call(
        body,
        grid=(nt,),
        in_specs=[pl.BlockSpec((T, W), lambda i: (i, 0)), pl.BlockSpec((T, W), lambda i: (i, 1)),
                  pl.BlockSpec((T, LANES), lambda i: (i, 4 * W // LANES)),
                  pl.BlockSpec((1, LANES), lambda i: (0, 0)), pl.BlockSpec((1, LANES), lambda i: (0, 0)),
                  pl.BlockSpec((1, LANES), lambda i: (0, 0))],
        out_specs=[pl.BlockSpec((T, W), lambda i: (i, 0)), pl.BlockSpec((T, W), lambda i: (i, 0)),
                   pl.BlockSpec((H, T, 1), lambda i: (0, i, 0)), pl.BlockSpec((H, 1, T), lambda i: (0, 0, i))],
        out_shape=[jax.ShapeDtypeStruct((S, W), MXU_DTYPE), jax.ShapeDtypeStruct((S, W), MXU_DTYPE),
                   jax.ShapeDtypeStruct((H, S, 1), F32), jax.ShapeDtypeStruct((H, 1, S), F32)],
        scratch_shapes=[pltpu.VMEM((1, LANES), F32)],
        compiler_params=_params(("arbitrary",)),
        name=name,
    )(h, h, h, bf, qg, kg)


def _fox_prep_bwd(h, bf, qg, kg, dq, dk, dcrow, H, dh, name):
    S = h.shape[0]
    W = H * dh
    T = _pick(S, 256, LANES)
    nt = S // T
    npair = H // 2

    def body(hq_ref, hk_ref, f_ref, bf_ref, qg_ref, kg_ref, dq_ref, dk_ref, dcrow_ref,
             dhq_ref, dhk_ref, df_ref, dbf_ref, dqg_ref, dkg_ref, dcarry, dct):
        i = pl.program_id(0)

        @pl.when(i == 0)
        def _():
            dcarry[...] = jnp.zeros_like(dcarry)
            dbf_ref[...] = jnp.zeros_like(dbf_ref)
            dqg_ref[...] = jnp.zeros_like(dqg_ref)
            dkg_ref[...] = jnp.zeros_like(dkg_ref)

        for p in range(npair):
            sl = slice(p * LANES, (p + 1) * LANES)
            _, vjp = jax.vjp(lambda x, g: _pair_rms(x, g, dh) * (dh ** -0.5), hq_ref[:, sl], qg_ref[...])
            dx, dg = vjp(dq_ref[:, sl])
            dhq_ref[:, sl] = dx
            dqg_ref[...] += dg
            _, vjp = jax.vjp(lambda x, g: _pair_rms(x, g, dh), hk_ref[:, sl], kg_ref[...])
            dx, dg = vjp(dk_ref[:, sl])
            dhk_ref[:, sl] = dx
            dkg_ref[...] += dg
        dct[...] = jnp.zeros_like(dct)
        for hh in range(H):
            dct[hh:hh + 1, :] = dcrow_ref[hh]
        _, vjp = jax.vjp(lambda f, b: _cum_fn(f, b, jnp.zeros((1, LANES), F32)), f_ref[...], bf_ref[...])
        dc = dct[...].T
        df, dbf = vjp((dc, dcarry[...]))
        df_ref[...] = df
        dbf_ref[...] += dbf
        dcarry[...] = dcarry[...] + jnp.sum(dc, axis=0, keepdims=True)

    rv = lambda i: nt - 1 - i
    return pl.pallas_call(
        body,
        grid=(nt,),
        in_specs=[pl.BlockSpec((T, W), lambda i: (rv(i), 0)), pl.BlockSpec((T, W), lambda i: (rv(i), 1)),
                  pl.BlockSpec((T, LANES), lambda i: (rv(i), 4 * W // LANES)),
                  pl.BlockSpec((1, LANES), lambda i: (0, 0)), pl.BlockSpec((1, LANES), lambda i: (0, 0)),
                  pl.BlockSpec((1, LANES), lambda i: (0, 0)),
                  pl.BlockSpec((T, W), lambda i: (rv(i), 0)), pl.BlockSpec((T, W), lambda i: (rv(i), 0)),
                  pl.BlockSpec((H, 1, T), lambda i: (0, 0, rv(i)))],
        out_specs=[pl.BlockSpec((T, W), lambda i: (rv(i), 0)), pl.BlockSpec((T, W), lambda i: (rv(i), 0)),
                   pl.BlockSpec((T, LANES), lambda i: (rv(i), 0)),
                   pl.BlockSpec((1, LANES), lambda i: (0, 0)), pl.BlockSpec((1, LANES), lambda i: (0, 0)),
                   pl.BlockSpec((1, LANES), lambda i: (0, 0))],
        out_shape=[jax.ShapeDtypeStruct((S, W), F32), jax.ShapeDtypeStruct((S, W), F32),
                   jax.ShapeDtypeStruct((S, LANES), F32)] + [jax.ShapeDtypeStruct((1, LANES), F32)] * 3,
        scratch_shapes=[pltpu.VMEM((1, LANES), F32), pltpu.VMEM((LANES, T), F32)],
        compiler_params=_params(("arbitrary",)),
        name=name,
    )(h, h, h, bf, qg, kg, dq, dk, dcrow)


def _head_masks(dh):
    first = _iota2((1, LANES), 1) < dh
    return first, jnp.logical_not(first)


def _per_head(tile, dh):
    return jnp.stack([jnp.where(mask, tile, 0) for mask in _head_masks(dh)])


def _both(tile):
    return jnp.stack([tile, tile])


def _rows2(x):
    return x.reshape(2 * x.shape[1], x.shape[2])


def _stacked(tile, dh):
    return _rows2(_per_head(tile, dh))


FLASH_SUB = 128


def _flash_scores(q_m, k, ccol, crow, row0, diagonal):
    s = _dot(q_m, k, _NT) + ccol - crow
    if diagonal:
        s = jnp.where(_iota2(s.shape, s.ndim - 1) <= row0 + _iota2(s.shape, s.ndim - 2), s, NEG)
    return s


def _causal_blocks(nq, by_query):
    if by_query:
        pairs = [(i, j) for i in range(nq) for j in range(i + 1)]
    else:
        pairs = [(i, j) for j in range(nq) for i in range(j, nq)]
    return (jnp.asarray([a for a, _ in pairs], jnp.int32), jnp.asarray([b for _, b in pairs], jnp.int32))


def _flash_blocks(i, j, tq, fn, sub=FLASH_SUB):
    sub = min(sub, tq)
    for diagonal in (False, True):
        @pl.when((j == i) if diagonal else (j < i))
        def _():
            for r in range(tq // sub):
                fn(slice(r * sub, (r + 1) * sub), r * sub, diagonal)


def _flash_fwd(q, k, h, ccol, crow, H, dh, name):
    S, W = q.shape
    tq = tk = _pick(S, 512, LANES)
    nq = S // tq
    npair = H // 2
    vblk = 2 * W // LANES

    def body(ii_ref, jj_ref, q_ref, k_ref, v_ref, ccol_ref, crow_ref, o_ref, lse_ref, m_s, l_s, acc_s):
        i, j = ii_ref[pl.program_id(1)], jj_ref[pl.program_id(1)]

        @pl.when(j == 0)
        def _():
            m_s[...] = jnp.full_like(m_s, NEG)
            l_s[...] = jnp.zeros_like(l_s)
            acc_s[...] = jnp.zeros_like(acc_s)

        def tile(rows, row0, diagonal):
            s = _flash_scores(_per_head(q_ref[rows, :], dh), _both(k_ref[...]), ccol_ref[:, rows, :], crow_ref[...], row0, diagonal)
            m_old = m_s[:, rows, :]
            m_new = jnp.maximum(m_old, jnp.max(s, axis=-1, keepdims=True))
            alpha = jnp.exp(m_old - m_new)
            p = jnp.exp(s - m_new)
            l_s[:, rows, :] = alpha * l_s[:, rows, :] + jnp.sum(p, axis=-1, keepdims=True)
            first, _ = _head_masks(dh)
            acc_s[rows, :] = jnp.where(first, alpha[0], alpha[1]) * acc_s[rows, :] + _dot(
                jnp.concatenate([p[0], p[1]], axis=1), _stacked(v_ref[...].astype(MXU_DTYPE), dh), _NN)
            m_s[:, rows, :] = m_new

        _flash_blocks(i, j, tq, tile)

        @pl.when(j == i)
        def _():
            first, _ = _head_masks(dh)
            o_ref[...] = acc_s[...] / jnp.where(first, l_s[0], l_s[1])
            for a in range(2):
                lse_ref[a] = m_s[a] + jnp.log(l_s[a])

    ii, jj = _causal_blocks(nq, by_query=True)
    return pl.pallas_call(
        body,
        grid_spec=pltpu.PrefetchScalarGridSpec(
            num_scalar_prefetch=2,
            grid=(npair, len(ii)),
            in_specs=[pl.BlockSpec((tq, LANES), lambda p, t, ii, jj: (ii[t], p)),
                      pl.BlockSpec((tk, LANES), lambda p, t, ii, jj: (jj[t], p)),
                      pl.BlockSpec((tk, LANES), lambda p, t, ii, jj: (jj[t], vblk + p)),
                      pl.BlockSpec((2, tq, 1), lambda p, t, ii, jj: (p, ii[t], 0)),
                      pl.BlockSpec((2, 1, tk), lambda p, t, ii, jj: (p, 0, jj[t]))],
            out_specs=[pl.BlockSpec((tq, LANES), lambda p, t, ii, jj: (ii[t], p)),
                       pl.BlockSpec((2, tq, 1), lambda p, t, ii, jj: (p, ii[t], 0))],
            scratch_shapes=[pltpu.VMEM((2, tq, 1), F32), pltpu.VMEM((2, tq, 1), F32), pltpu.VMEM((tq, LANES), F32)]),
        out_shape=[jax.ShapeDtypeStruct((S, W), F32), jax.ShapeDtypeStruct((H, S, 1), F32)],
        compiler_params=_params(("parallel", "arbitrary")),
        name=name,
    )(ii, jj, q, k, h, ccol, crow)


def _flash_bwd_kv(q, k, h, ccol, crow, lse, delta, resid, do, H, dh, name):
    S, W = q.shape
    tq = tk = _pick(S, 512, LANES)
    nq = S // tq
    npair = H // 2
    vblk = 2 * W // LANES

    def body(ii_ref, jj_ref, q_ref, k_ref, v_ref, ccol_ref, crow_ref, lse_ref, dl_ref, rs_ref, do_ref, dk_ref, dv_ref, dcr_ref,
             dkt_s, dvt_s):
        i, j = ii_ref[pl.program_id(1)], jj_ref[pl.program_id(1)]

        @pl.when(i == j)
        def _():
            dkt_s[...] = jnp.zeros_like(dkt_s)
            dvt_s[...] = jnp.zeros_like(dvt_s)
            dcr_ref[...] = jnp.zeros_like(dcr_ref)

        def tile(rows, row0, diagonal):
            kv, vv = _both(k_ref[...]), _both(v_ref[...].astype(MXU_DTYPE))
            q_m = _per_head(q_ref[rows, :], dh)
            do_m = _per_head(do_ref[rows, :].astype(MXU_DTYPE), dh)
            s = _flash_scores(q_m, kv, ccol_ref[:, rows, :], crow_ref[...], row0, diagonal)
            p = jnp.exp(s - lse_ref[:, rows, :])
            ds = p * (_dot(do_m, vv, _NT) - (dl_ref[:, rows, :] + rs_ref[:, rows, :]))
            dvt_s[...] += _dot(_rows2(do_m), _rows2(p), _TN)
            dkt_s[...] += _dot(_rows2(q_m), _rows2(ds), _TN)
            dcr_ref[...] -= jnp.sum(ds, axis=1, keepdims=True)

        _flash_blocks(i, j, tq, tile, sub=2 * FLASH_SUB)

        @pl.when(i == nq - 1)
        def _():
            dk_ref[...] = dkt_s[...].T
            dv_ref[...] = dvt_s[...].T

    ii, jj = _causal_blocks(nq, by_query=False)
    qrow = pl.BlockSpec((2, tq, 1), lambda p, t, ii, jj: (p, ii[t], 0))
    return pl.pallas_call(
        body,
        grid_spec=pltpu.PrefetchScalarGridSpec(
            num_scalar_prefetch=2,
            grid=(npair, len(ii)),
            in_specs=[pl.BlockSpec((tq, LANES), lambda p, t, ii, jj: (ii[t], p)),
                      pl.BlockSpec((tk, LANES), lambda p, t, ii, jj: (jj[t], p)),
                      pl.BlockSpec((tk, LANES), lambda p, t, ii, jj: (jj[t], vblk + p)),
                      qrow,
                      pl.BlockSpec((2, 1, tk), lambda p, t, ii, jj: (p, 0, jj[t])),
                      qrow, qrow, qrow,
                      pl.BlockSpec((tq, LANES), lambda p, t, ii, jj: (ii[t], p))],
            out_specs=[pl.BlockSpec((tk, LANES), lambda p, t, ii, jj: (jj[t], p)),
                       pl.BlockSpec((tk, LANES), lambda p, t, ii, jj: (jj[t], p)),
                       pl.BlockSpec((2, 1, tk), lambda p, t, ii, jj: (p, 0, jj[t]))],
            scratch_shapes=[pltpu.VMEM((LANES, tk), F32), pltpu.VMEM((LANES, tk), F32)]),
        out_shape=[jax.ShapeDtypeStruct((S, W), F32), jax.ShapeDtypeStruct((S, W), F32),
                   jax.ShapeDtypeStruct((H, 1, S), F32)],
        compiler_params=_params(("parallel", "arbitrary")),
        name=name,
    )(ii, jj, q, k, h, ccol, crow, lse, delta, resid, do)


def _flash_bwd_q(q, k, h, ccol, crow, lse, delta, do, H, dh, name):
    S, W = q.shape
    tq = tk = _pick(S, 512, LANES)
    nq = S // tq
    npair = H // 2
    vblk = 2 * W // LANES

    def body(ii_ref, jj_ref, q_ref, k_ref, v_ref, ccol_ref, crow_ref, lse_ref, dl_ref, do_ref, dq_ref, rs_ref):
        i, j = ii_ref[pl.program_id(1)], jj_ref[pl.program_id(1)]

        @pl.when(j == 0)
        def _():
            dq_ref[...] = jnp.zeros_like(dq_ref)
            rs_ref[...] = jnp.zeros_like(rs_ref)

        def tile(rows, row0, diagonal):
            kv, vv = k_ref[...], _both(v_ref[...].astype(MXU_DTYPE))
            s = _flash_scores(_per_head(q_ref[rows, :], dh), _both(kv), ccol_ref[:, rows, :], crow_ref[...], row0, diagonal)
            p = jnp.exp(s - lse_ref[:, rows, :])
            ds = p * (_dot(_per_head(do_ref[rows, :].astype(MXU_DTYPE), dh), vv, _NT) - dl_ref[:, rows, :])
            dq = _dot(ds, _per_head(kv, dh), _NN)
            dq_ref[rows, :] += dq[0] + dq[1]
            rs_ref[:, rows, :] += jnp.sum(ds, axis=-1, keepdims=True)

        _flash_blocks(i, j, tq, tile, sub=tq)

    ii, jj = _causal_blocks(nq, by_query=True)
    qrow = pl.BlockSpec((2, tq, 1), lambda p, t, ii, jj: (p, ii[t], 0))
    return pl.pallas_call(
        body,
        grid_spec=pltpu.PrefetchScalarGridSpec(
            num_scalar_prefetch=2,
            grid=(npair, len(ii)),
            in_specs=[pl.BlockSpec((tq, LANES), lambda p, t, ii, jj: (ii[t], p)),
                      pl.BlockSpec((tk, LANES), lambda p, t, ii, jj: (jj[t], p)),
                      pl.BlockSpec((tk, LANES), lambda p, t, ii, jj: (jj[t], vblk + p)),
                      qrow,
                      pl.BlockSpec((2, 1, tk), lambda p, t, ii, jj: (p, 0, jj[t])),
                      qrow, qrow,
                      pl.BlockSpec((tq, LANES), lambda p, t, ii, jj: (ii[t], p))],
            out_specs=[pl.BlockSpec((tq, LANES), lambda p, t, ii, jj: (ii[t], p)), qrow]),
        out_shape=[jax.ShapeDtypeStruct((S, W), F32), jax.ShapeDtypeStruct((H, S, 1), F32)],
        compiler_params=_params(("parallel", "arbitrary")),
        name=name,
    )(ii, jj, q, k, h, ccol, crow, lse, delta, do)


def _fox_gate_fwd(o, h, W, name):
    def fn(rows, _, __):
        return [rows[0] * _silu(rows[1])], [], []

    return _rowwise(fn, [o, (h, W, 3)], [], [(W, MXU_DTYPE)], [], tile=256, name=name)[0]


def _fox_gate_bwd(o, h, dog, H, dh, name):
    S, W = o.shape
    T = _pick(S, 256, SUBLANES)

    def body(o_ref, z_ref, d_ref, do_ref, dz_ref, dl_ref):
        ov, zv, dv = o_ref[...], z_ref[...], d_ref[...]
        sg = _sigmoid(zv)
        do = dv * zv * sg
        do_ref[...] = do
        dz_ref[...] = dv * ov * sg * (1.0 + zv * (1.0 - sg))
        prod = do * ov
        for p in range(H // 2):
            blk = prod[:, p * LANES:(p + 1) * LANES]
            first = _iota2(blk.shape, 1) < dh
            dl_ref[2 * p] = jnp.sum(jnp.where(first, blk, 0.0), axis=1, keepdims=True)
            dl_ref[2 * p + 1] = jnp.sum(jnp.where(first, 0.0, blk), axis=1, keepdims=True)

    return pl.pallas_call(
        body,
        grid=(S // T,),
        in_specs=[pl.BlockSpec((T, W), lambda i: (i, 0)), pl.BlockSpec((T, W), lambda i: (i, 3)),
                  pl.BlockSpec((T, W), lambda i: (i, 0))],
        out_specs=[pl.BlockSpec((T, W), lambda i: (i, 0)), pl.BlockSpec((T, W), lambda i: (i, 0)),
                   pl.BlockSpec((H, T, 1), lambda i: (0, i, 0))],
        out_shape=[jax.ShapeDtypeStruct((S, W), F32), jax.ShapeDtypeStruct((S, W), F32),
                   jax.ShapeDtypeStruct((H, S, 1), F32)],
        compiler_params=_params(("parallel",)),
        name=name,
    )(o, h, dog)


def _gdn_layer_fwd(x, w, tag):
    H = w["H"]
    qk = H * LANES
    h = _matmul(x, w["w_in"], "nn", name=f"{tag}_in")
    c = _conv_fwd(h, w["conv"], 3 * qk, name=f"{tag}_conv")
    og, states = _gdn_scan_fwd(c, h, w["hp"], w["ng"], H, name=f"{tag}_scan")
    y = _matmul(og, w["w_out"], "nn", name=f"{tag}_out")
    return y, (x, h, c, states, og)


def _gdn_layer_bwd(dy, res, w, tag):
    x, h, c, states, og = res
    H = w["H"]
    qk = H * LANES
    dog = _matmul(dy, w["w_out"], "nt", name=f"{tag}_out_dx")
    dw_out = _matmul(og, dy, "tn", name=f"{tag}_out_dw")
    dq, dk, dv, dz, dba, dhp, dng = _gdn_scan_bwd(c, h, w["hp"], w["ng"], states, dog, H, name=f"{tag}_scan_bwd")
    dh_parts, dconv = [], []
    for part, d in enumerate((dq, dk, dv)):
        dh_p, dw_p = _conv_bwd(d, h, w["conv"][:, part * qk:(part + 1) * qk], part * qk, name=f"{tag}_conv_bwd{part}")
        dh_parts.append(dh_p)
        dconv.append(dw_p[:GDN_CONV])
    dh = jnp.concatenate(dh_parts + [dz, dba], axis=1)
    dx = _matmul(dh, w["w_in"], "nt", name=f"{tag}_in_dx")
    dw_in = _matmul(x, dh, "tn", name=f"{tag}_in_dw")
    grads = {"w_in": dw_in, "w_out": dw_out, "conv": jnp.concatenate(dconv, axis=1),
             "a_log": dhp[0, :H], "dt_bias": dhp[1, :H], "norm_g": dng[0]}
    return dx, grads


def _fox_layer_fwd(x, w, tag):
    H, dh = w["H"], w["dh"]
    W = H * dh
    h = _matmul(x, w["w_in"], "nn", name=f"{tag}_in")
    q, k, ccol, crow = _fox_prep_fwd(h, w["bf"], w["qg"], w["kg"], H, dh, name=f"{tag}_prep")
    o, lse = _flash_fwd(q, k, h, ccol, crow, H, dh, name=f"{tag}_flash")
    og = _fox_gate_fwd(o, h, W, name=f"{tag}_gate")
    y = _matmul(og, w["w_out"], "nn", name=f"{tag}_out")
    return y, (x, h, q, k, ccol, crow, o, lse, og)


def _fox_layer_bwd(dy, res, w, tag):
    x, h, q, k, ccol, crow, o, lse, og = res
    H, dh = w["H"], w["dh"]
    dog = _matmul(dy, w["w_out"], "nt", name=f"{tag}_out_dx")
    dw_out = _matmul(og, dy, "tn", name=f"{tag}_out_dw")
    do, dz, delta = _fox_gate_bwd(o, h, dog, H, dh, name=f"{tag}_gate_bwd")
    dqq, resid = _flash_bwd_q(q, k, h, ccol, crow, lse, delta, do, H, dh, name=f"{tag}_flash_bwd_q")
    dkk, dvv, dcrow = _flash_bwd_kv(q, k, h, ccol, crow, lse, delta, resid, do, H, dh, name=f"{tag}_flash_bwd_kv")
    dhq, dhk, df, dbf, dqg, dkg = _fox_prep_bwd(h, w["bf"], w["qg"], w["kg"], dqq, dkk, dcrow, H, dh, name=f"{tag}_prep_bwd")
    dhh = jnp.concatenate([dhq, dhk, dvv, dz, df], axis=1)
    dx = _matmul(dhh, w["w_in"], "nt", name=f"{tag}_in_dx")
    dw_in = _matmul(x, dhh, "tn", name=f"{tag}_in_dw")
    grads = {"w_in": dw_in, "w_out": dw_out, "b_f": dbf[0, :H],
             "q_norm_g": dqg[0, :dh] + dqg[0, dh:], "k_norm_g": dkg[0, :dh] + dkg[0, dh:]}
    return dx, grads


def _pad_cols(w, n):
    return jnp.pad(w, ((0, 0), (0, n - w.shape[1])))


def _build_layers(full, small):
    depth = small["ln_g"].shape[0]
    gh = small["gdn_a_log"].shape[1]
    fh, dh = small["fox_b_f"].shape[1], small["fox_q_norm_g"].shape[1]
    layers = []
    for i in range(depth):
        j = i // 2
        w = {"ln_g": small["ln_g"][i][None], "ln_b": small["ln_b"][i][None],
             "w_gate": full["ple_w_gate"][i], "w_proj": full["ple_w_proj"][i]}
        if i % 2 == 0:
            hp = jnp.zeros((SUBLANES, LANES), F32).at[0, :gh].set(small["gdn_a_log"][j]).at[1, :gh].set(small["gdn_dt_bias"][j])
            w.update(kind="gdn", H=gh, w_in=_pad_cols(full["gdn_w_in"][j], 4 * gh * LANES + LANES),
                     conv=full["gdn_conv_w"][j], hp=hp, ng=small["gdn_norm_g"][j][None], w_out=full["gdn_w_out"][j])
        else:
            bf = jnp.zeros((1, LANES), F32).at[0, :fh].set(small["fox_b_f"][j])
            w.update(kind="fox", H=fh, dh=dh, w_in=_pad_cols(full["fox_w_in"][j], 4 * fh * dh + LANES), bf=bf,
                     qg=jnp.tile(small["fox_q_norm_g"][j], 2)[None], kg=jnp.tile(small["fox_k_norm_g"][j], 2)[None],
                     w_out=full["fox_w_out"][j])
        layers.append(w)
    return layers


def _local_step(x, p, target, layers):
    depth = len(layers)
    alpha = (2 * depth) ** 0.25
    saved = []
    for i, w in enumerate(layers):
        tag = f"l{i}"
        if w["kind"] == "gdn":
            y, res = _gdn_layer_fwd(x, w, tag)
        else:
            y, res = _fox_layer_fwd(x, w, tag)
        x1 = _ln_fwd(x, y, w["ln_g"], w["ln_b"], alpha, name=f"{tag}_ln")
        gate_pre = _matmul(x1, w["w_gate"], "nn", name=f"{tag}_gate_mm")
        pp = _matmul(p[i], w["w_proj"], "nn", name=f"{tag}_proj_mm")
        x2 = _ple_fwd(x1, gate_pre, pp, name=f"{tag}_ple")
        saved.append((res, x, y, x1, gate_pre, pp))
        x = x2
    dx, loss_tile = _loss_head(x, target, name="loss_head")
    grads = [None] * depth
    for i in reversed(range(depth)):
        w = layers[i]
        tag = f"l{i}"
        res, xin, y, x1, gate_pre, pp = saved[i]
        dgp, dpp = _ple_bwd(dx, gate_pre, pp, name=f"{tag}_ple_bwd")
        dx1 = _add(dx, _matmul(dgp, w["w_gate"], "nt", name=f"{tag}_gate_dx"), name=f"{tag}_dx1")
        dw_gate = _matmul(x1, dgp, "tn", name=f"{tag}_gate_dw")
        dw_proj = _matmul(p[i], dpp, "tn", name=f"{tag}_proj_dw")
        dy, dg, db = _ln_bwd(xin, y, w["ln_g"], w["ln_b"], dx1, alpha, name=f"{tag}_ln_bwd")
        if w["kind"] == "gdn":
            dxm, g = _gdn_layer_bwd(dy, res, w, tag)
        else:
            dxm, g = _fox_layer_bwd(dy, res, w, tag)
        dx = _axpy(alpha, dy, dxm, name=f"{tag}_dx")
        g.update({"w_gate": dw_gate, "w_proj": dw_proj, "ln_g": dg[0], "ln_b": db[0]})
        grads[i] = g
    return loss_tile, dx, grads


MESH_ID = pl.DeviceIdType.MESH
HBM_SPEC = pl.BlockSpec(memory_space=pl.ANY)
PACK_COLS = 1024
PACK_ROWS = 256


def _all_gather(shards, name):
    nt = len(shards)

    def body(*refs):
        x_refs, out_refs = refs[:nt], refs[nt:2 * nt]
        send_sems, recv_sems, local_sems = refs[2 * nt:]
        x, y, c = lax.axis_index("x"), lax.axis_index("y"), lax.axis_index("c")
        me, sibling = (x, y, c), (x, y, 1 - c)
        chips = [(1 - x, y), (x, 1 - y), (1 - x, 1 - y)]

        def slot(t, px, py, pc):
            return out_refs[t].at[4 * px + 2 * py + pc]

        def copy(k, t, block, to, src=None):
            return pltpu.make_async_remote_copy(
                src_ref=slot(t, *block) if src is None else src, dst_ref=slot(t, *block),
                send_sem=send_sems.at[k, t], recv_sem=recv_sems.at[k, t], device_id=to, device_id_type=MESH_ID)

        every = range(nt)
        mine = [pltpu.make_async_copy(x_refs[t], slot(t, *me), local_sems.at[t]) for t in every]
        for cp in mine:
            cp.start()
        first = [copy(0, t, me, sibling, src=x_refs[t]) for t in every]
        first += [copy(1 + j, t, me, (*chip, c), src=x_refs[t]) for j, chip in enumerate(chips) for t in every]
        for cp in first:
            cp.start()
        passed = []
        for j, chip in enumerate(chips):
            for t in every:
                copy(1 + j, t, (*chip, c), me).wait_recv()
                passed.append(copy(4 + j, t, (*chip, c), sibling))
                passed[-1].start()
        for t in every:
            copy(0, t, sibling, me).wait_recv()
        for j, chip in enumerate(chips):
            for t in every:
                copy(4 + j, t, (*chip, 1 - c), me).wait_recv()
        for cp in first + passed:
            cp.wait_send()
        for cp in mine:
            cp.wait()

    return pl.pallas_call(
        body,
        out_shape=[jax.ShapeDtypeStruct((N_DEV, *s.shape), s.dtype) for s in shards],
        in_specs=[HBM_SPEC] * nt,
        out_specs=[HBM_SPEC] * nt,
        scratch_shapes=[pltpu.SemaphoreType.DMA((7, nt)), pltpu.SemaphoreType.DMA((7, nt)), pltpu.SemaphoreType.DMA((nt,))],
        name=name,
    )(*shards)


def _all_to_all(slabs, name):
    nt = len(slabs)

    def body(*refs):
        g_refs, out_refs = refs[:nt], refs[nt:2 * nt]
        send_sems, recv_sems, local_sems = refs[2 * nt:]
        x, y, c = lax.axis_index("x"), lax.axis_index("y"), lax.axis_index("c")
        me = 4 * x + 2 * y + c
        mine = [pltpu.make_async_copy(g_refs[t].at[me], out_refs[t].at[me], local_sems.at[t]) for t in range(nt)]
        for cp in mine:
            cp.start()
        copies = []
        for k in range(1, N_DEV):
            px = 1 - x if k & 4 else x
            py = 1 - y if k & 2 else y
            pc = 1 - c if k & 1 else c
            peer = 4 * px + 2 * py + pc
            for t in range(nt):
                copies.append(tuple(
                    pltpu.make_async_remote_copy(src_ref=g_refs[t].at[peer], dst_ref=out_refs[t].at[dst],
                                                 send_sem=send_sems.at[k - 1, t], recv_sem=recv_sems.at[k - 1, t],
                                                 device_id=(px, py, pc), device_id_type=MESH_ID)
                    for dst in (me, peer)))
        for send, _ in copies:
            send.start()
        for send, arrive in copies:
            arrive.wait_recv()
            send.wait_send()
        for cp in mine:
            cp.wait()

    return pl.pallas_call(
        body,
        out_shape=[jax.ShapeDtypeStruct(s.shape, s.dtype) for s in slabs],
        in_specs=[HBM_SPEC] * nt,
        out_specs=[HBM_SPEC] * nt,
        scratch_shapes=[pltpu.SemaphoreType.DMA((7, nt)), pltpu.SemaphoreType.DMA((7, nt)), pltpu.SemaphoreType.DMA((nt,))],
        name=name,
    )(*slabs)


def _pack(flats, dtype):
    flat = jnp.concatenate([f.astype(dtype).reshape(-1) for f in flats])
    unit = PACK_ROWS * PACK_COLS
    n = -(-flat.shape[0] // unit) * unit
    return jnp.pad(flat, (0, n - flat.shape[0])).reshape(n // PACK_COLS, PACK_COLS)


def _unpack(buf, shapes):
    lead = buf.shape[:-2]
    flat = buf.reshape(*lead, -1)
    out, off = [], 0
    for s in shapes:
        n = math.prod(s)
        out.append(flat[..., off:off + n].reshape(*lead, *s))
        off += n
    return out


_ROW_SPLIT = ("ple_w_gate", "gdn_w_out", "fox_w_out")
_COL_SPLIT = ("ple_w_proj", "gdn_w_in", "gdn_conv_w", "fox_w_in")
_SHARDED = ("ple_w_gate", "ple_w_proj", "gdn_w_in", "gdn_conv_w", "gdn_w_out", "fox_w_in", "fox_w_out")
_REPLICATED = ("ln_g", "ln_b", "gdn_a_log", "gdn_dt_bias", "gdn_norm_g", "fox_b_f", "fox_q_norm_g", "fox_k_norm_g")
_WEIGHTS = ("ln_g", "ln_b", "ple_w_gate", "ple_w_proj", "gdn_w_in", "gdn_conv_w", "gdn_a_log", "gdn_dt_bias",
            "gdn_norm_g", "gdn_w_out", "fox_w_in", "fox_b_f", "fox_q_norm_g", "fox_k_norm_g", "fox_w_out")


def _join(name, gathered):
    n, l, a, b = gathered.shape
    if name in _ROW_SPLIT:
        return gathered.transpose(1, 0, 2, 3).reshape(l, n * a, b)
    return gathered.transpose(1, 2, 0, 3).reshape(l, a, n * b)


def _split(name, full):
    l, a, b = full.shape
    if name in _ROW_SPLIT:
        return full.reshape(l, N_DEV, a // N_DEV, b).transpose(1, 0, 2, 3)
    return full.reshape(l, a, N_DEV, b // N_DEV).transpose(2, 0, 1, 3)


def _adamw(w, g_parts, m, v, name):
    shape = w.shape
    R, C = math.prod(shape[:-1]), shape[-1]
    tr = _pick(R, 256, SUBLANES)
    c1 = 1.0 - ADAM_B1 ** ADAM_STEP
    c2 = 1.0 - ADAM_B2 ** ADAM_STEP

    def body(w_ref, g_ref, m_ref, v_ref, go_ref, d_ref, mo_ref, vo_ref):
        gv = g_ref[0]
        for s in range(1, N_DEV):
            gv = gv + g_ref[s]
        mn = ADAM_B1 * m_ref[...] + (1.0 - ADAM_B1) * gv
        vn = ADAM_B2 * v_ref[...] + (1.0 - ADAM_B2) * jnp.square(gv)
        go_ref[...] = gv
        d_ref[...] = -ADAM_LR * ((mn / c1) / (jnp.sqrt(vn / c2) + ADAM_EPS) + ADAM_WD * w_ref[...])
        mo_ref[...] = mn
        vo_ref[...] = vn

    row = pl.BlockSpec((tr, C), lambda i: (i, 0))
    outs = pl.pallas_call(
        body,
        grid=(R // tr,),
        in_specs=[row, pl.BlockSpec((N_DEV, tr, C), lambda i: (0, i, 0)), row, row],
        out_specs=[row] * 4,
        out_shape=[jax.ShapeDtypeStruct((R, C), F32)] * 4,
        compiler_params=_params(("parallel",)),
        name=name,
    )(w.reshape(R, C), g_parts.reshape(N_DEV, R, C), m.reshape(R, C), v.reshape(R, C))
    return [o.reshape(shape) for o in outs]


def _train_step(x, p, target, w, m, v):
    shards = [w[n] if n == "gdn_conv_w" else w[n].astype(MXU_DTYPE) for n in _SHARDED]
    gathered = _all_gather(shards, name="gather_weights")
    full = {n: _join(n, part) for n, part in zip(_SHARDED, gathered)}
    layers = _build_layers(full, {n: w[n] for n in _REPLICATED})

    loss_tile, dx, grads = _local_step(x[0], p[:, 0], target[0], layers)
    loss = lax.psum(loss_tile[0, 0], ("x", "y", "c"))

    depth = len(layers)
    gdn_l = [i for i in range(depth) if i % 2 == 0]
    fox_l = [i for i in range(depth) if i % 2 == 1]

    def stack(key, idx):
        return jnp.stack([grads[i][key] for i in idx])

    full_g = {
        "ple_w_gate": stack("w_gate", range(depth)), "ple_w_proj": stack("w_proj", range(depth)),
        "gdn_w_in": stack("w_in", gdn_l)[..., :w["gdn_w_in"].shape[-1] * N_DEV], "gdn_conv_w": stack("conv", gdn_l),
        "gdn_w_out": stack("w_out", gdn_l),
        "fox_w_in": stack("w_in", fox_l)[..., :w["fox_w_in"].shape[-1] * N_DEV], "fox_w_out": stack("w_out", fox_l)}
    small_g = {
        "ln_g": stack("ln_g", range(depth)), "ln_b": stack("ln_b", range(depth)),
        "gdn_a_log": stack("a_log", gdn_l), "gdn_dt_bias": stack("dt_bias", gdn_l), "gdn_norm_g": stack("norm_g", gdn_l),
        "fox_b_f": stack("b_f", fox_l), "fox_q_norm_g": stack("q_norm_g", fox_l), "fox_k_norm_g": stack("k_norm_g", fox_l)}

    g_parts = dict(zip(_SHARDED, _all_to_all([_split(n, full_g[n]) for n in _SHARDED], name="scatter_grads")))
    small_all = _all_gather([_pack([small_g[n] for n in _REPLICATED], F32)], name="gather_small_grads")[0]
    g_parts.update(zip(_REPLICATED, _unpack(small_all, [w[n].shape for n in _REPLICATED])))

    g, delta, new_m, new_v = {}, {}, {}, {}
    for n in _WEIGHTS:
        g[n], delta[n], new_m[n], new_v[n] = _adamw(w[n], g_parts[n], m[n], v[n], name=f"adamw_{n}")
    return (loss, dx[None], *[g[n] for n in _WEIGHTS], *[delta[n] for n in _WEIGHTS],
            *[new_m[n] for n in _WEIGHTS], *[new_v[n] for n in _WEIGHTS])


def kernel(x, p, ln_g, ln_b, ple_w_gate, ple_w_proj, gdn_w_in, gdn_conv_w, gdn_a_log, gdn_dt_bias, gdn_norm_g, gdn_w_out, fox_w_in, fox_b_f, fox_q_norm_g, fox_k_norm_g, fox_w_out, loss_target, m_ln_g, m_ln_b, m_ple_w_gate, m_ple_w_proj, m_gdn_w_in, m_gdn_conv_w, m_gdn_a_log, m_gdn_dt_bias, m_gdn_norm_g, m_gdn_w_out, m_fox_w_in, m_fox_b_f, m_fox_q_norm_g, m_fox_k_norm_g, m_fox_w_out, v_ln_g, v_ln_b, v_ple_w_gate, v_ple_w_proj, v_gdn_w_in, v_gdn_conv_w, v_gdn_a_log, v_gdn_dt_bias, v_gdn_norm_g, v_gdn_w_out, v_fox_w_in, v_fox_b_f, v_fox_q_norm_g, v_fox_k_norm_g, v_fox_w_out):
    given = dict(locals())
    w = {n: given[n] for n in _WEIGHTS}
    m = {n: given["m_" + n] for n in _WEIGHTS}
    v = {n: given["v_" + n] for n in _WEIGHTS}
    return _train_step(x, p, loss_target, w, m, v)
```

```python
import functools
import math

import jax
import jax.numpy as jnp
from jax import lax
from jax.experimental import pallas as pl
from jax.experimental.pallas import tpu as pltpu

F32 = jnp.float32
BF16 = jnp.bfloat16
MXU_DTYPE = BF16
HI = lax.Precision.HIGHEST

N_DEV = 8
LANES = 128
SUBLANES = 8
VMEM_BYTES = 64 * 1024 * 1024

GDN_CHUNK = 64
GDN_CONV = 4
LN_EPS = 1e-5
RMS_EPS = 1e-6
NEG = -1e30

ADAM_LR = 0.001
ADAM_B1 = 0.9
ADAM_B2 = 0.999
ADAM_EPS = 1e-08
ADAM_WD = 0.01
ADAM_STEP = 10


def _params(semantics, vmem_mb=40):
    return pltpu.CompilerParams(dimension_semantics=semantics, vmem_limit_bytes=vmem_mb * 1024 * 1024)


def _pick(dim, cap, unit=LANES):
    if dim <= cap:
        return dim
    best = None
    for t in range(unit, cap + 1, unit):
        if dim % t == 0:
            best = t
    assert best is not None, (dim, cap)
    return best


def _dims(dims, ndim):
    if ndim == 2:
        return (dims, ((), ()))
    return (((dims[0][0] + 1,), (dims[1][0] + 1,)), ((0,), (0,)))


def _dot(a, b, dims):
    return lax.dot_general(a.astype(MXU_DTYPE), b.astype(MXU_DTYPE), _dims(dims, a.ndim), preferred_element_type=F32)


_NN = ((1,), (0,))
_NT = ((1,), (1,))
_TN = ((0,), (0,))


@jax.custom_vjp
def _mm_nn(a, b):
    return _dot(a, b, _NN)


@jax.custom_vjp
def _mm_nt(a, b):
    return _dot(a, b, _NT)


@jax.custom_vjp
def _mm_tn(a, b):
    return _dot(a, b, _TN)


_mm_nn.defvjp(lambda a, b: (_dot(a, b, _NN), (a, b)), lambda r, g: (_mm_nt(g, r[1]), _mm_tn(r[0], g)))
_mm_nt.defvjp(lambda a, b: (_dot(a, b, _NT), (a, b)), lambda r, g: (_mm_nn(g, r[1]), _mm_tn(g, r[0])))
_mm_tn.defvjp(lambda a, b: (_dot(a, b, _TN), (a, b)), lambda r, g: (_mm_nt(r[1], g), _mm_nn(r[0], g)))


def _mm_hi(a, b, precision=HI):
    return lax.dot_general(a, b, _dims(_NN, a.ndim), precision=precision, preferred_element_type=F32)


def _mm_3x(a, b):
    return _mm_hi(a, b, lax.Precision.HIGH)


def _sigmoid(x):
    return 1.0 / (1.0 + jnp.exp(-x))


def _silu(x):
    return x * _sigmoid(x)


def _softplus(x):
    return jnp.maximum(x, 0.0) + jnp.log(1.0 + jnp.exp(-jnp.abs(x)))


def _iota2(shape, dim):
    return lax.broadcasted_iota(jnp.int32, shape, dim)


def _lane_pick(tile, lane):
    return jnp.sum(jnp.where(_iota2(tile.shape, 1) == lane, tile, 0.0), axis=1, keepdims=True)


def _lane_put(col, lane, width=LANES):
    return jnp.where(_iota2((col.shape[0], width), 1) == lane, col, 0.0)


def _matmul(a, b, mode, out_dtype=F32, *, name, tm=1024, tn=1408, tk=1408, a_cols=None, b_cols=None):
    def cols(arr, rng):
        return (0, arr.shape[1]) if rng is None else rng

    a0, an = cols(a, a_cols)
    b0, bn = cols(b, b_cols)
    if mode == "nn":
        M, K, N = a.shape[0], an, bn
        assert b.shape[0] == K
    elif mode == "nt":
        M, K, N = a.shape[0], an, b.shape[0]
        assert bn == K
    else:
        K, M, N = a.shape[0], an, bn
        assert b.shape[0] == K
    tm, tn, tk = _pick(M, tm), _pick(N, tn), _pick(K, tk)
    nk = K // tk
    if mode == "nn":
        assert a0 % tk == 0 and b0 % tn == 0
        a_spec = pl.BlockSpec((tm, tk), lambda i, j, k: (i, a0 // tk + k))
        b_spec = pl.BlockSpec((tk, tn), lambda i, j, k: (k, b0 // tn + j))
        dims = _NN
    elif mode == "nt":
        assert a0 % tk == 0 and b0 % tk == 0
        a_spec = pl.BlockSpec((tm, tk), lambda i, j, k: (i, a0 // tk + k))
        b_spec = pl.BlockSpec((tn, tk), lambda i, j, k: (j, b0 // tk + k))
        dims = _NT
    else:
        assert a0 % tm == 0 and b0 % tn == 0
        a_spec = pl.BlockSpec((tk, tm), lambda i, j, k: (k, a0 // tm + i))
        b_spec = pl.BlockSpec((tk, tn), lambda i, j, k: (k, b0 // tn + j))
        dims = _TN

    def body(a_ref, b_ref, o_ref, acc_ref):
        k = pl.program_id(2)

        @pl.when(k == 0)
        def _():
            acc_ref[...] = jnp.zeros_like(acc_ref)

        acc_ref[...] += _dot(a_ref[...], b_ref[...], dims)

        @pl.when(k == nk - 1)
        def _():
            o_ref[...] = acc_ref[...].astype(o_ref.dtype)

    return pl.pallas_call(
        body,
        grid=(M // tm, N // tn, nk),
        in_specs=[a_spec, b_spec],
        out_specs=pl.BlockSpec((tm, tn), lambda i, j, k: (i, j)),
        out_shape=jax.ShapeDtypeStruct((M, N), out_dtype),
        scratch_shapes=[pltpu.VMEM((tm, tn), F32)],
        compiler_params=_params(("parallel", "parallel", "arbitrary"), 48),
        name=name,
    )(a, b)


def _rowwise(fn, rows, consts, out_rows, out_accs, *, tile, name, reverse=False, carries=(), vmem_mb=40):
    rows = [r if isinstance(r, tuple) else (r, r.shape[1], 0) for r in rows]
    S = rows[0][0].shape[0]
    tile = _pick(S, tile, SUBLANES)
    nt = S // tile
    nr, nc, no, na = len(rows), len(consts), len(out_rows), len(out_accs)

    def ridx(i):
        return nt - 1 - i if reverse else i

    in_specs = [pl.BlockSpec((tile, w), functools.partial(lambda i, cb: (ridx(i), cb), cb=cb)) for _, w, cb in rows]
    in_specs += [pl.BlockSpec(c.shape, functools.partial(lambda i, nd: (0,) * nd, nd=c.ndim)) for c in consts]
    out_specs = [pl.BlockSpec((tile, c), lambda i: (ridx(i), 0)) for c, _ in out_rows]
    out_specs += [pl.BlockSpec(s, functools.partial(lambda i, nd: (0,) * nd, nd=len(s))) for s, _ in out_accs]
    out_shape = [jax.ShapeDtypeStruct((S, c), d) for c, d in out_rows]
    out_shape += [jax.ShapeDtypeStruct(s, d) for s, d in out_accs]

    def body(*refs):
        rin, cin = refs[:nr], refs[nr:nr + nc]
        rout, aout = refs[nr + nc:nr + nc + no], refs[nr + nc + no:nr + nc + no + na]
        carr = refs[nr + nc + no + na:]
        step = pl.program_id(0)

        @pl.when(step == 0)
        def _():
            for r in aout + carr:
                r[...] = jnp.zeros_like(r)

        outs, accs, newc = fn([r[...] for r in rin], [c[...] for c in cin], [c[...] for c in carr])
        for r, o in zip(rout, outs, strict=True):
            r[...] = o.astype(r.dtype)
        for r, v in zip(aout, accs, strict=True):
            r[...] += v
        for r, v in zip(carr, newc, strict=True):
            r[...] = v

    res = pl.pallas_call(
        body,
        grid=(nt,),
        in_specs=in_specs,
        out_specs=out_specs,
        out_shape=out_shape,
        scratch_shapes=[pltpu.VMEM(s, F32) for s in carries],
        compiler_params=_params(("arbitrary",), vmem_mb),
        name=name,
    )(*[r[0] for r in rows], *consts)
    return res


def _ln_fn(x, y, g, b, alpha):
    r = alpha * x + y
    mu = jnp.mean(r, -1, keepdims=True)
    var = jnp.mean(jnp.square(r - mu), -1, keepdims=True)
    return (r - mu) * lax.rsqrt(var + LN_EPS) * g + b


def _ln_fwd(x, y, g, b, alpha, name):
    D = x.shape[1]

    def fn(rows, consts, _):
        return [_ln_fn(rows[0], rows[1], consts[0], consts[1], alpha)], [], []

    return _rowwise(fn, [x, y], [g, b], [(D, F32)], [], tile=256, name=name)[0]


def _ln_bwd(x, y, g, b, dx1, alpha, name):
    D = x.shape[1]

    def fn(rows, consts, _):
        xv, yv, d = rows
        _, vjp = jax.vjp(lambda yy, gg, bb: _ln_fn(xv, yy, gg, bb, alpha), yv, consts[0], consts[1])
        dy, dg, db = vjp(d)
        return [dy], [dg, db], []

    return _rowwise(fn, [x, y, dx1], [g, b], [(D, F32)], [((1, D), F32), ((1, D), F32)], tile=256, name=name)


def _ple_fwd(x1, gate_pre, pp, name):
    D = x1.shape[1]

    def fn(rows, _, __):
        return [rows[0] + _sigmoid(rows[1]) * rows[2]], [], []

    return _rowwise(fn, [x1, gate_pre, pp], [], [(D, F32)], [], tile=256, name=name)[0]


def _ple_bwd(dx2, gate_pre, pp, name):
    D = dx2.shape[1]

    def fn(rows, _, __):
        d, gp, ppv = rows
        s = _sigmoid(gp)
        return [d * ppv * s * (1.0 - s), d * s], [], []

    return _rowwise(fn, [dx2, gate_pre, pp], [], [(D, F32), (D, F32)], [], tile=256, name=name)


def _add(a, b, name):
    def fn(rows, _, __):
        return [rows[0] + rows[1]], [], []

    return _rowwise(fn, [a, b], [], [(a.shape[1], F32)], [], tile=256, name=name)[0]


def _axpy(alpha, a, b, name):
    def fn(rows, _, __):
        return [alpha * rows[0] + rows[1]], [], []

    return _rowwise(fn, [a, b], [], [(a.shape[1], F32)], [], tile=256, name=name)[0]


def _loss_head(y, target, name):
    D = y.shape[1]

    def fn(rows, _, __):
        e = rows[0] - rows[1]
        part = 0.5 * jnp.sum(jnp.sum(e * e, axis=1, keepdims=True), axis=0, keepdims=True) / D
        return [e / D], [jnp.broadcast_to(part, (SUBLANES, LANES))], []

    return _rowwise(fn, [y, target], [], [(D, F32)], [((SUBLANES, LANES), F32)], tile=256, name=name)


def _conv_fwd(h, w, n_cols, name):
    S = h.shape[0]
    T = _pick(S, 512, SUBLANES)
    CB = _pick(n_cols, 512)
    nt = S // T
    K = GDN_CONV

    def body(x_ref, halo_ref, w_ref, o_ref, buf):
        i = pl.program_id(1)
        buf[0:SUBLANES, :] = jnp.where(i > 0, halo_ref[...], 0.0)
        buf[SUBLANES:, :] = x_ref[...]
        acc = jnp.zeros((T, CB), F32)
        for k in range(K):
            acc = acc + w_ref[k:k + 1, :] * buf[pl.ds(SUBLANES - (K - 1) + k, T), :]
        o_ref[...] = acc

    return pl.pallas_call(
        body,
        grid=(n_cols // CB, nt),
        in_specs=[pl.BlockSpec((T, CB), lambda c, i: (i, c)),
                  pl.BlockSpec((SUBLANES, CB), lambda c, i: (jnp.maximum(i * (T // SUBLANES) - 1, 0), c)),
                  pl.BlockSpec((K, CB), lambda c, i: (0, c))],
        out_specs=pl.BlockSpec((T, CB), lambda c, i: (i, c)),
        out_shape=jax.ShapeDtypeStruct((S, n_cols), F32),
        scratch_shapes=[pltpu.VMEM((T + SUBLANES, CB), F32)],
        compiler_params=_params(("parallel", "parallel")),
        name=name,
    )(h, h, w)


def _conv_bwd(dc, h, w, h_col0, name):
    S, n_cols = dc.shape
    T = _pick(S, 512, SUBLANES)
    CB = _pick(n_cols, 512)
    nt = S // T
    K = GDN_CONV
    assert h_col0 % CB == 0
    hb = h_col0 // CB

    def body(d_ref, halo_ref, x_ref, w_ref, dx_ref, dw_ref, buf):
        i = pl.program_id(1)

        @pl.when(i == 0)
        def _():
            dw_ref[...] = jnp.zeros_like(dw_ref)

        buf[0:T, :] = d_ref[...]
        buf[T:, :] = jnp.where(i < nt - 1, halo_ref[...], 0.0)
        x = x_ref[...]
        acc = jnp.zeros((T, CB), F32)
        for k in range(K):
            shifted = buf[pl.ds(K - 1 - k, T), :]
            acc = acc + w_ref[k:k + 1, :] * shifted
            dw_ref[k:k + 1, :] += jnp.sum(shifted * x, axis=0, keepdims=True)
        dx_ref[...] = acc

    last = S // SUBLANES - 1
    return pl.pallas_call(
        body,
        grid=(n_cols // CB, nt),
        in_specs=[pl.BlockSpec((T, CB), lambda c, i: (i, c)),
                  pl.BlockSpec((SUBLANES, CB), lambda c, i: (jnp.minimum((i + 1) * (T // SUBLANES), last), c)),
                  pl.BlockSpec((T, CB), lambda c, i: (i, hb + c)),
                  pl.BlockSpec((K, CB), lambda c, i: (0, c))],
        out_specs=[pl.BlockSpec((T, CB), lambda c, i: (i, c)),
                   pl.BlockSpec((SUBLANES, CB), lambda c, i: (0, c))],
        out_shape=[jax.ShapeDtypeStruct((S, n_cols), F32), jax.ShapeDtypeStruct((SUBLANES, n_cols), F32)],
        scratch_shapes=[pltpu.VMEM((T + SUBLANES, CB), F32)],
        compiler_params=_params(("parallel", "arbitrary")),
        name=name,
    )(dc, dc, h, w)


def _neumann_inverse(L):
    C = L.shape[-1]
    eye = (_iota2((C, C), 0) == _iota2((C, C), 1)).astype(F32)
    X = eye - L
    P = L
    for _ in range(max(0, math.ceil(math.log2(C)) - 1)):
        P = _mm_3x(P, P)
        X = _mm_3x(X, eye + P)
    return X


@jax.custom_vjp
def _unit_lower_inverse(L):
    return _neumann_inverse(L)


def _unit_lower_inverse_bwd(T, dT):
    Tt = jnp.swapaxes(T, -1, -2)
    return (-_mm_3x(_mm_3x(Tt, dT), Tt),)


_unit_lower_inverse.defvjp(lambda L: (_neumann_inverse(L),) * 2, _unit_lower_inverse_bwd)


@jax.custom_vjp
def _known_inverse(L, T):
    return T


_known_inverse.defvjp(lambda L, T: (T, T), lambda T, dT: (*_unit_lower_inverse_bwd(T, dT), jnp.zeros_like(T)))


def _gdn_chunk(cq, ck, cv, zz, bcol, acol, alog, dtb, ng, state, inverse=None):
    G, C, dk = cq.shape
    q = _silu(cq)
    k = _silu(ck)
    v = _silu(cv)
    q = q * lax.rsqrt(jnp.sum(q * q, -1, keepdims=True) + RMS_EPS) * (dk ** -0.5)
    k = k * lax.rsqrt(jnp.sum(k * k, -1, keepdims=True) + RMS_EPS)
    beta = _sigmoid(bcol)
    g = -jnp.exp(alog) * _softplus(acol + dtb)

    row, col = _iota2((C, C), 0), _iota2((C, C), 1)
    causal, strict, eye = row >= col, row > col, row == col
    g_rows = jnp.swapaxes(jnp.broadcast_to(g, (G, C, C)), -1, -2)
    gc = jnp.sum(jnp.where(causal, g_rows, 0.0), axis=-1, keepdims=True)
    gcb = jnp.broadcast_to(gc, (G, C, C))
    g_last = jnp.sum(jnp.sum(jnp.where((row == C - 1) & (col == 0), gcb, 0.0), axis=-1, keepdims=True), axis=-2, keepdims=True)
    gc_rows = jnp.swapaxes(gcb, -1, -2)
    decay = jnp.exp(jnp.where(causal, gcb - gc_rows, NEG))

    kb = k * beta
    L = jnp.where(strict, _mm_nt(kb, k) * decay, 0.0)
    T = _unit_lower_inverse(L) if inverse is None else _known_inverse(L, inverse)
    u = _mm_3x(T, v * beta)
    w = _mm_3x(T, kb * jnp.exp(gc))
    a_qk = jnp.where(causal, _mm_nt(q, k) * decay, 0.0)
    q_dec = q * jnp.exp(gc)
    k_dec = k * jnp.exp(g_last - gc)
    v_new = u - _mm_nn(w, state)
    o = _mm_nn(q_dec, state) + _mm_nn(a_qk, v_new)
    new_state = state * jnp.exp(g_last) + _mm_tn(k_dec, v_new)
    y = o * lax.rsqrt(jnp.mean(o * o, -1, keepdims=True) + RMS_EPS) * ng * _silu(zz)
    return (y, new_state), T


GDN_HEADS_PER_STEP = 8


def _gdn_specs(H, dk, G, chunk_of=lambda n: n):
    C = GDN_CHUNK
    NG = H // G
    cq = pl.BlockSpec((C, G * dk), lambda n, h: (chunk_of(n), h))
    ck = pl.BlockSpec((C, G * dk), lambda n, h: (chunk_of(n), NG + h))
    cv = pl.BlockSpec((C, G * dk), lambda n, h: (chunk_of(n), 2 * NG + h))
    zz = pl.BlockSpec((C, G * dk), lambda n, h: (chunk_of(n), 3 * NG + h))
    ba = pl.BlockSpec((C, LANES), lambda n, h: (chunk_of(n), 4 * H * dk // LANES))
    return cq, ck, cv, zz, ba


def _gdn_step_args(cq_ref, ck_ref, cv_ref, z_ref, ba_ref, hp_ref, hg, G, H, dk):
    ba = ba_ref[...]

    def heads(ref):
        return jnp.stack([ref[:, g * dk:(g + 1) * dk] for g in range(G)])

    def picks(tile, offset):
        return jnp.stack([_lane_pick(tile, offset + hg * G + g) for g in range(G)])

    return (heads(cq_ref), heads(ck_ref), heads(cv_ref), heads(z_ref), picks(ba, 0), picks(ba, H),
            picks(hp_ref[0:1, :], 0), picks(hp_ref[1:2, :], 0))


def _gdn_scan_fwd(c, h, hp, ng, H, name):
    S = c.shape[0]
    dk = c.shape[1] // (3 * H)
    assert dk == LANES
    C = GDN_CHUNK
    NC = S // C

    G = min(GDN_HEADS_PER_STEP, H)
    assert H % G == 0
    NG = H // G

    def body(cq_ref, ck_ref, cv_ref, z_ref, ba_ref, hp_ref, ng_ref, y_ref, s_ref, t_ref, state):
        n, hg = pl.program_id(0), pl.program_id(1)
        heads = pl.ds(hg * G, G)

        @pl.when(n == 0)
        def _():
            state[heads] = jnp.zeros((G, dk, dk), F32)

        st = state[heads]
        s_ref[0] = st
        (y, new_state), inverse = _gdn_chunk(
            *_gdn_step_args(cq_ref, ck_ref, cv_ref, z_ref, ba_ref, hp_ref, hg, G, H, dk), ng_ref[...], st)
        t_ref[0] = inverse
        for g in range(G):
            y_ref[:, g * dk:(g + 1) * dk] = y[g].astype(y_ref.dtype)
        state[heads] = new_state

    cq, ck, cv, zz, ba = _gdn_specs(H, dk, G)
    return pl.pallas_call(
        body,
        grid=(NC, NG),
        in_specs=[cq, ck, cv, zz, ba, pl.BlockSpec((SUBLANES, LANES), lambda n, h: (0, 0)),
                  pl.BlockSpec((1, dk), lambda n, h: (0, 0))],
        out_specs=[pl.BlockSpec((C, G * dk), lambda n, h: (n, h)),
                   pl.BlockSpec((1, G, dk, dk), lambda n, h: (n, h, 0, 0)),
                   pl.BlockSpec((1, G, C, C), lambda n, h: (n, h, 0, 0))],
        out_shape=[jax.ShapeDtypeStruct((S, H * dk), MXU_DTYPE), jax.ShapeDtypeStruct((NC, H, dk, dk), F32),
                   jax.ShapeDtypeStruct((NC, H, C, C), F32)],
        scratch_shapes=[pltpu.VMEM((H, dk, dk), F32)],
        compiler_params=_params(("arbitrary", "arbitrary")),
        name=name,
    )(c, c, c, h, h, hp, ng)


def _gdn_scan_bwd(c, h, hp, ng, states, inverses, dy, H, name):
    S = c.shape[0]
    dk = c.shape[1] // (3 * H)
    C = GDN_CHUNK
    NC = S // C

    G = min(GDN_HEADS_PER_STEP, H)
    NG = H // G

    def body(cq_ref, ck_ref, cv_ref, z_ref, ba_ref, hp_ref, ng_ref, s_ref, t_ref, dy_ref,
             dq_ref, dk_ref, dv_ref, dz_ref, dba_ref, dhp_ref, dng_ref, dstate):
        n, hg = pl.program_id(0), pl.program_id(1)

        @pl.when((n == 0) & (hg == 0))
        def _():
            dhp_ref[...] = jnp.zeros_like(dhp_ref)
            dng_ref[...] = jnp.zeros_like(dng_ref)

        heads = pl.ds(hg * G, G)

        @pl.when(n == 0)
        def _():
            dstate[heads] = jnp.zeros((G, dk, dk), F32)

        args = (*_gdn_step_args(cq_ref, ck_ref, cv_ref, z_ref, ba_ref, hp_ref, hg, G, H, dk), ng_ref[...], s_ref[0])
        inverse = t_ref[0]
        _, vjp, _ = jax.vjp(lambda *a: _gdn_chunk(*a, inverse=inverse), *args, has_aux=True)
        dy = jnp.stack([dy_ref[:, g * dk:(g + 1) * dk] for g in range(G)])
        dcq, dck, dcv, dzz, dbc, dac, dal, ddt, dng, dst = vjp((dy, dstate[heads]))
        dstate[heads] = dst
        dba = jnp.zeros((C, LANES), F32)
        dhp0 = jnp.zeros((1, LANES), F32)
        dhp1 = jnp.zeros((1, LANES), F32)
        for g in range(G):
            hd = hg * G + g
            sl = slice(g * dk, (g + 1) * dk)
            dq_ref[:, sl] = dcq[g]
            dk_ref[:, sl] = dck[g]
            dv_ref[:, sl] = dcv[g]
            dz_ref[:, sl] = dzz[g]
            dba = dba + _lane_put(dbc[g], hd) + _lane_put(dac[g], H + hd)
            dhp0 = dhp0 + _lane_put(dal[g], hd)
            dhp1 = dhp1 + _lane_put(ddt[g], hd)

        @pl.when(hg == 0)
        def _():
            dba_ref[...] = dba

        @pl.when(hg > 0)
        def _():
            dba_ref[...] += dba

        dhp_ref[0:1, :] += dhp0
        dhp_ref[1:2, :] += dhp1
        dng_ref[...] += dng

    rev = lambda n: NC - 1 - n
    blk = pl.BlockSpec
    in_specs = [*_gdn_specs(H, dk, G, rev),
                blk((SUBLANES, LANES), lambda n, h: (0, 0)), blk((1, dk), lambda n, h: (0, 0)),
                blk((1, G, dk, dk), lambda n, h: (rev(n), h, 0, 0)), blk((1, G, C, C), lambda n, h: (rev(n), h, 0, 0)),
                blk((C, G * dk), lambda n, h: (rev(n), h))]
    out_specs = [blk((C, G * dk), lambda n, h: (rev(n), h))] * 4 + [
        blk((C, LANES), lambda n, h: (rev(n), 0)),
        blk((SUBLANES, LANES), lambda n, h: (0, 0)), blk((1, dk), lambda n, h: (0, 0))]
    out_shape = [jax.ShapeDtypeStruct((S, H * dk), F32)] * 4 + [
        jax.ShapeDtypeStruct((S, LANES), F32), jax.ShapeDtypeStruct((SUBLANES, LANES), F32),
        jax.ShapeDtypeStruct((1, dk), F32)]
    return pl.pallas_call(
        body,
        grid=(NC, NG),
        in_specs=in_specs,
        out_specs=out_specs,
        out_shape=out_shape,
        scratch_shapes=[pltpu.VMEM((H, dk, dk), F32)],
        compiler_params=_params(("arbitrary", "arbitrary")),
        name=name,
    )(c, c, c, h, h, hp, ng, states, inverses, dy)


def _pair_rms(x, gain, dh):
    first = _iota2(x.shape, 1) < dh
    sq = x * x
    ss_a = jnp.sum(jnp.where(first, sq, 0.0), axis=1, keepdims=True)
    ss_b = jnp.sum(jnp.where(first, 0.0, sq), axis=1, keepdims=True)
    inv = jnp.where(first, lax.rsqrt(ss_a / dh + RMS_EPS), lax.rsqrt(ss_b / dh + RMS_EPS))
    return x * inv * gain


def _log_sigmoid(x):
    return jnp.minimum(x, 0.0) - jnp.log(1.0 + jnp.exp(-jnp.abs(x)))


def _cum_fn(fr, bf, carry):
    T = fr.shape[0]
    tril = (_iota2((T, T), 0) >= _iota2((T, T), 1)).astype(F32)
    c = _mm_hi(tril, _log_sigmoid(fr + bf)) + carry
    last = jnp.sum(jnp.where(_iota2(c.shape, 0) == T - 1, c, 0.0), axis=0, keepdims=True)
    return c, last


def _fox_prep_fwd(h, bf, qg, kg, H, dh, name):
    S = h.shape[0]
    W = H * dh
    assert 2 * dh == LANES and H <= LANES
    T = _pick(S, 256, LANES)
    nt = S // T
    npair = H // 2

    def body(hq_ref, hk_ref, f_ref, bf_ref, qg_ref, kg_ref, q_ref, k_ref, ccol_ref, crow_ref, carry):
        i = pl.program_id(0)

        @pl.when(i == 0)
        def _():
            carry[...] = jnp.zeros_like(carry)

        for p in range(npair):
            sl = slice(p * LANES, (p + 1) * LANES)
            q_ref[:, sl] = (_pair_rms(hq_ref[:, sl], qg_ref[...], dh) * (dh ** -0.5)).astype(q_ref.dtype)
            k_ref[:, sl] = _pair_rms(hk_ref[:, sl], kg_ref[...], dh).astype(k_ref.dtype)
        c, last = _cum_fn(f_ref[...], bf_ref[...], carry[...])
        carry[...] = last
        ct = c.T
        for hh in range(H):
            ccol_ref[hh] = c[:, hh:hh + 1]
            crow_ref[hh] = ct[hh:hh + 1, :]

    return pl.pallas_call(
        body,
        grid=(nt,),
        in_specs=[pl.BlockSpec((T, W), lambda i: (i, 0)), pl.BlockSpec((T, W), lambda i: (i, 1)),
                  pl.BlockSpec((T, LANES), lambda i: (i, 4 * W // LANES)),
                  pl.BlockSpec((1, LANES), lambda i: (0, 0)), pl.BlockSpec((1, LANES), lambda i: (0, 0)),
                  pl.BlockSpec((1, LANES), lambda i: (0, 0))],
        out_specs=[pl.BlockSpec((T, W), lambda i: (i, 0)), pl.BlockSpec((T, W), lambda i: (i, 0)),
                   pl.BlockSpec((H, T, 1), lambda i: (0, i, 0)), pl.BlockSpec((H, 1, T), lambda i: (0, 0, i))],
        out_shape=[jax.ShapeDtypeStruct((S, W), MXU_DTYPE), jax.ShapeDtypeStruct((S, W), MXU_DTYPE),
                   jax.ShapeDtypeStruct((H, S, 1), F32), jax.ShapeDtypeStruct((H, 1, S), F32)],
        scratch_shapes=[pltpu.VMEM((1, LANES), F32)],
        compiler_params=_params(("arbitrary",)),
        name=name,
    )(h, h, h, bf, qg, kg)


def _fox_prep_bwd(h, bf, qg, kg, dq, dk, dcrow, H, dh, name):
    S = h.shape[0]
    W = H * dh
    T = _pick(S, 256, LANES)
    nt = S // T
    npair = H // 2

    def body(hq_ref, hk_ref, f_ref, bf_ref, qg_ref, kg_ref, dq_ref, dk_ref, dcrow_ref,
             dhq_ref, dhk_ref, df_ref, dbf_ref, dqg_ref, dkg_ref, dcarry, dct):
        i = pl.program_id(0)

        @pl.when(i == 0)
        def _():
            dcarry[...] = jnp.zeros_like(dcarry)
            dbf_ref[...] = jnp.zeros_like(dbf_ref)
            dqg_ref[...] = jnp.zeros_like(dqg_ref)
            dkg_ref[...] = jnp.zeros_like(dkg_ref)

        for p in range(npair):
            sl = slice(p * LANES, (p + 1) * LANES)
            _, vjp = jax.vjp(lambda x, g: _pair_rms(x, g, dh) * (dh ** -0.5), hq_ref[:, sl], qg_ref[...])
            dx, dg = vjp(dq_ref[:, sl])
            dhq_ref[:, sl] = dx
            dqg_ref[...] += dg
            _, vjp = jax.vjp(lambda x, g: _pair_rms(x, g, dh), hk_ref[:, sl], kg_ref[...])
            dx, dg = vjp(dk_ref[:, sl])
            dhk_ref[:, sl] = dx
            dkg_ref[...] += dg
        dct[...] = jnp.zeros_like(dct)
        for hh in range(H):
            dct[hh:hh + 1, :] = dcrow_ref[hh]
        _, vjp = jax.vjp(lambda f, b: _cum_fn(f, b, jnp.zeros((1, LANES), F32)), f_ref[...], bf_ref[...])
        dc = dct[...].T
        df, dbf = vjp((dc, dcarry[...]))
        df_ref[...] = df
        dbf_ref[...] += dbf
        dcarry[...] = dcarry[...] + jnp.sum(dc, axis=0, keepdims=True)

    rv = lambda i: nt - 1 - i
    return pl.pallas_call(
        body,
        grid=(nt,),
        in_specs=[pl.BlockSpec((T, W), lambda i: (rv(i), 0)), pl.BlockSpec((T, W), lambda i: (rv(i), 1)),
                  pl.BlockSpec((T, LANES), lambda i: (rv(i), 4 * W // LANES)),
                  pl.BlockSpec((1, LANES), lambda i: (0, 0)), pl.BlockSpec((1, LANES), lambda i: (0, 0)),
                  pl.BlockSpec((1, LANES), lambda i: (0, 0)),
                  pl.BlockSpec((T, W), lambda i: (rv(i), 0)), pl.BlockSpec((T, W), lambda i: (rv(i), 0)),
                  pl.BlockSpec((H, 1, T), lambda i: (0, 0, rv(i)))],
        out_specs=[pl.BlockSpec((T, W), lambda i: (rv(i), 0)), pl.BlockSpec((T, W), lambda i: (rv(i), 0)),
                   pl.BlockSpec((T, LANES), lambda i: (rv(i), 0)),
                   pl.BlockSpec((1, LANES), lambda i: (0, 0)), pl.BlockSpec((1, LANES), lambda i: (0, 0)),
                   pl.BlockSpec((1, LANES), lambda i: (0, 0))],
        out_shape=[jax.ShapeDtypeStruct((S, W), F32), jax.ShapeDtypeStruct((S, W), F32),
                   jax.ShapeDtypeStruct((S, LANES), F32)] + [jax.ShapeDtypeStruct((1, LANES), F32)] * 3,
        scratch_shapes=[pltpu.VMEM((1, LANES), F32), pltpu.VMEM((LANES, T), F32)],
        compiler_params=_params(("arbitrary",)),
        name=name,
    )(h, h, h, bf, qg, kg, dq, dk, dcrow)


def _head_masks(dh):
    first = _iota2((1, LANES), 1) < dh
    return first, jnp.logical_not(first)


def _per_head(tile, dh):
    return jnp.stack([jnp.where(mask, tile, 0) for mask in _head_masks(dh)])


def _both(tile):
    return jnp.stack([tile, tile])


def _rows2(x):
    return x.reshape(2 * x.shape[1], x.shape[2])


def _stacked(tile, dh):
    return _rows2(_per_head(tile, dh))


FLASH_SUB = 128


def _flash_scores(q_m, k, ccol, crow, row0, diagonal):
    s = _dot(q_m, k, _NT) + ccol - crow
    if diagonal:
        s = jnp.where(_iota2(s.shape, s.ndim - 1) <= row0 + _iota2(s.shape, s.ndim - 2), s, NEG)
    return s


def _causal_blocks(nq, by_query):
    if by_query:
        pairs = [(i, j) for i in range(nq) for j in range(i + 1)]
    else:
        pairs = [(i, j) for j in range(nq) for i in range(j, nq)]
    return (jnp.asarray([a for a, _ in pairs], jnp.int32), jnp.asarray([b for _, b in pairs], jnp.int32))


def _flash_blocks(i, j, tq, fn, sub=FLASH_SUB):
    sub = min(sub, tq)
    for diagonal in (False, True):
        @pl.when((j == i) if diagonal else (j < i))
        def _():
            for r in range(tq // sub):
                fn(slice(r * sub, (r + 1) * sub), r * sub, diagonal)


def _flash_fwd(q, k, h, ccol, crow, H, dh, name):
    S, W = q.shape
    tq = tk = _pick(S, 512, LANES)
    nq = S // tq
    npair = H // 2
    vblk = 2 * W // LANES

    def body(ii_ref, jj_ref, q_ref, k_ref, v_ref, ccol_ref, crow_ref, o_ref, lse_ref, m_s, l_s, acc_s):
        i, j = ii_ref[pl.program_id(1)], jj_ref[pl.program_id(1)]

        @pl.when(j == 0)
        def _():
            m_s[...] = jnp.full_like(m_s, NEG)
            l_s[...] = jnp.zeros_like(l_s)
            acc_s[...] = jnp.zeros_like(acc_s)

        def tile(rows, row0, diagonal):
            s = _flash_scores(_per_head(q_ref[rows, :], dh), _both(k_ref[...]), ccol_ref[:, rows, :], crow_ref[...], row0, diagonal)
            m_old = m_s[:, rows, :]
            m_new = jnp.maximum(m_old, jnp.max(s, axis=-1, keepdims=True))
            alpha = jnp.exp(m_old - m_new)
            p = jnp.exp(s - m_new)
            l_s[:, rows, :] = alpha * l_s[:, rows, :] + jnp.sum(p, axis=-1, keepdims=True)
            first, _ = _head_masks(dh)
            acc_s[rows, :] = jnp.where(first, alpha[0], alpha[1]) * acc_s[rows, :] + _dot(
                jnp.concatenate([p[0], p[1]], axis=1), _stacked(v_ref[...].astype(MXU_DTYPE), dh), _NN)
            m_s[:, rows, :] = m_new

        _flash_blocks(i, j, tq, tile)

        @pl.when(j == i)
        def _():
            first, _ = _head_masks(dh)
            o_ref[...] = acc_s[...] / jnp.where(first, l_s[0], l_s[1])
            for a in range(2):
                lse_ref[a] = m_s[a] + jnp.log(l_s[a])

    ii, jj = _causal_blocks(nq, by_query=True)
    return pl.pallas_call(
        body,
        grid_spec=pltpu.PrefetchScalarGridSpec(
            num_scalar_prefetch=2,
            grid=(npair, len(ii)),
            in_specs=[pl.BlockSpec((tq, LANES), lambda p, t, ii, jj: (ii[t], p)),
                      pl.BlockSpec((tk, LANES), lambda p, t, ii, jj: (jj[t], p)),
                      pl.BlockSpec((tk, LANES), lambda p, t, ii, jj: (jj[t], vblk + p)),
                      pl.BlockSpec((2, tq, 1), lambda p, t, ii, jj: (p, ii[t], 0)),
                      pl.BlockSpec((2, 1, tk), lambda p, t, ii, jj: (p, 0, jj[t]))],
            out_specs=[pl.BlockSpec((tq, LANES), lambda p, t, ii, jj: (ii[t], p)),
                       pl.BlockSpec((2, tq, 1), lambda p, t, ii, jj: (p, ii[t], 0))],
            scratch_shapes=[pltpu.VMEM((2, tq, 1), F32), pltpu.VMEM((2, tq, 1), F32), pltpu.VMEM((tq, LANES), F32)]),
        out_shape=[jax.ShapeDtypeStruct((S, W), F32), jax.ShapeDtypeStruct((H, S, 1), F32)],
        compiler_params=_params(("parallel", "arbitrary")),
        name=name,
    )(ii, jj, q, k, h, ccol, crow)


def _flash_bwd_kv(q, k, h, ccol, crow, lse, delta, resid, do, H, dh, name):
    S, W = q.shape
    tq = tk = _pick(S, 512, LANES)
    nq = S // tq
    npair = H // 2
    vblk = 2 * W // LANES

    def body(ii_ref, jj_ref, q_ref, k_ref, v_ref, ccol_ref, crow_ref, lse_ref, dl_ref, rs_ref, do_ref, dk_ref, dv_ref, dcr_ref,
             dkt_s, dvt_s):
        i, j = ii_ref[pl.program_id(1)], jj_ref[pl.program_id(1)]

        @pl.when(i == j)
        def _():
            dkt_s[...] = jnp.zeros_like(dkt_s)
            dvt_s[...] = jnp.zeros_like(dvt_s)
            dcr_ref[...] = jnp.zeros_like(dcr_ref)

        def tile(rows, row0, diagonal):
            kv, vv = _both(k_ref[...]), _both(v_ref[...].astype(MXU_DTYPE))
            q_m = _per_head(q_ref[rows, :], dh)
            do_m = _per_head(do_ref[rows, :].astype(MXU_DTYPE), dh)
            s = _flash_scores(q_m, kv, ccol_ref[:, rows, :], crow_ref[...], row0, diagonal)
            p = jnp.exp(s - lse_ref[:, rows, :])
            ds = p * (_dot(do_m, vv, _NT) - (dl_ref[:, rows, :] + rs_ref[:, rows, :]))
            dvt_s[...] += _dot(_rows2(do_m), _rows2(p), _TN)
            dkt_s[...] += _dot(_rows2(q_m), _rows2(ds), _TN)
            dcr_ref[...] -= jnp.sum(ds, axis=1, keepdims=True)

        _flash_blocks(i, j, tq, tile, sub=2 * FLASH_SUB)

        @pl.when(i == nq - 1)
        def _():
            dk_ref[...] = dkt_s[...].T
            dv_ref[...] = dvt_s[...].T

    ii, jj = _causal_blocks(nq, by_query=False)
    qrow = pl.BlockSpec((2, tq, 1), lambda p, t, ii, jj: (p, ii[t], 0))
    return pl.pallas_call(
        body,
        grid_spec=pltpu.PrefetchScalarGridSpec(
            num_scalar_prefetch=2,
            grid=(npair, len(ii)),
            in_specs=[pl.BlockSpec((tq, LANES), lambda p, t, ii, jj: (ii[t], p)),
                      pl.BlockSpec((tk, LANES), lambda p, t, ii, jj: (jj[t], p)),
                      pl.BlockSpec((tk, LANES), lambda p, t, ii, jj: (jj[t], vblk + p)),
                      qrow,
                      pl.BlockSpec((2, 1, tk), lambda p, t, ii, jj: (p, 0, jj[t])),
                      qrow, qrow, qrow,
                      pl.BlockSpec((tq, LANES), lambda p, t, ii, jj: (ii[t], p))],
            out_specs=[pl.BlockSpec((tk, LANES), lambda p, t, ii, jj: (jj[t], p)),
                       pl.BlockSpec((tk, LANES), lambda p, t, ii, jj: (jj[t], p)),
                       pl.BlockSpec((2, 1, tk), lambda p, t, ii, jj: (p, 0, jj[t]))],
            scratch_shapes=[pltpu.VMEM((LANES, tk), F32), pltpu.VMEM((LANES, tk), F32)]),
        out_shape=[jax.ShapeDtypeStruct((S, W), F32), jax.ShapeDtypeStruct((S, W), F32),
                   jax.ShapeDtypeStruct((H, 1, S), F32)],
        compiler_params=_params(("parallel", "arbitrary")),
        name=name,
    )(ii, jj, q, k, h, ccol, crow, lse, delta, resid, do)


def _flash_bwd_q(q, k, h, ccol, crow, lse, delta, do, H, dh, name):
    S, W = q.shape
    tq = tk = _pick(S, 512, LANES)
    nq = S // tq
    npair = H // 2
    vblk = 2 * W // LANES

    def body(ii_ref, jj_ref, q_ref, k_ref, v_ref, ccol_ref, crow_ref, lse_ref, dl_ref, do_ref, dq_ref, rs_ref):
        i, j = ii_ref[pl.program_id(1)], jj_ref[pl.program_id(1)]

        @pl.when(j == 0)
        def _():
            dq_ref[...] = jnp.zeros_like(dq_ref)
            rs_ref[...] = jnp.zeros_like(rs_ref)

        def tile(rows, row0, diagonal):
            kv, vv = k_ref[...], _both(v_ref[...].astype(MXU_DTYPE))
            s = _flash_scores(_per_head(q_ref[rows, :], dh), _both(kv), ccol_ref[:, rows, :], crow_ref[...], row0, diagonal)
            p = jnp.exp(s - lse_ref[:, rows, :])
            ds = p * (_dot(_per_head(do_ref[rows, :].astype(MXU_DTYPE), dh), vv, _NT) - dl_ref[:, rows, :])
            dq = _dot(ds, _per_head(kv, dh), _NN)
            dq_ref[rows, :] += dq[0] + dq[1]
            rs_ref[:, rows, :] += jnp.sum(ds, axis=-1, keepdims=True)

        _flash_blocks(i, j, tq, tile, sub=tq)

    ii, jj = _causal_blocks(nq, by_query=True)
    qrow = pl.BlockSpec((2, tq, 1), lambda p, t, ii, jj: (p, ii[t], 0))
    return pl.pallas_call(
        body,
        grid_spec=pltpu.PrefetchScalarGridSpec(
            num_scalar_prefetch=2,
            grid=(npair, len(ii)),
            in_specs=[pl.BlockSpec((tq, LANES), lambda p, t, ii, jj: (ii[t], p)),
                      pl.BlockSpec((tk, LANES), lambda p, t, ii, jj: (jj[t], p)),
                      pl.BlockSpec((tk, LANES), lambda p, t, ii, jj: (jj[t], vblk + p)),
                      qrow,
                      pl.BlockSpec((2, 1, tk), lambda p, t, ii, jj: (p, 0, jj[t])),
                      qrow, qrow,
                      pl.BlockSpec((tq, LANES), lambda p, t, ii, jj: (ii[t], p))],
            out_specs=[pl.BlockSpec((tq, LANES), lambda p, t, ii, jj: (ii[t], p)), qrow]),
        out_shape=[jax.ShapeDtypeStruct((S, W), F32), jax.ShapeDtypeStruct((H, S, 1), F32)],
        compiler_params=_params(("parallel", "arbitrary")),
        name=name,
    )(ii, jj, q, k, h, ccol, crow, lse, delta, do)


def _fox_gate_fwd(o, h, W, name):
    def fn(rows, _, __):
        return [rows[0] * _silu(rows[1])], [], []

    return _rowwise(fn, [o, (h, W, 3)], [], [(W, MXU_DTYPE)], [], tile=256, name=name)[0]


def _fox_gate_bwd(o, h, dog, H, dh, name):
    S, W = o.shape
    T = _pick(S, 256, SUBLANES)

    def body(o_ref, z_ref, d_ref, do_ref, dz_ref, dl_ref):
        ov, zv, dv = o_ref[...], z_ref[...], d_ref[...]
        sg = _sigmoid(zv)
        do = dv * zv * sg
        do_ref[...] = do
        dz_ref[...] = dv * ov * sg * (1.0 + zv * (1.0 - sg))
        prod = do * ov
        for p in range(H // 2):
            blk = prod[:, p * LANES:(p + 1) * LANES]
            first = _iota2(blk.shape, 1) < dh
            dl_ref[2 * p] = jnp.sum(jnp.where(first, blk, 0.0), axis=1, keepdims=True)
            dl_ref[2 * p + 1] = jnp.sum(jnp.where(first, 0.0, blk), axis=1, keepdims=True)

    return pl.pallas_call(
        body,
        grid=(S // T,),
        in_specs=[pl.BlockSpec((T, W), lambda i: (i, 0)), pl.BlockSpec((T, W), lambda i: (i, 3)),
                  pl.BlockSpec((T, W), lambda i: (i, 0))],
        out_specs=[pl.BlockSpec((T, W), lambda i: (i, 0)), pl.BlockSpec((T, W), lambda i: (i, 0)),
                   pl.BlockSpec((H, T, 1), lambda i: (0, i, 0))],
        out_shape=[jax.ShapeDtypeStruct((S, W), F32), jax.ShapeDtypeStruct((S, W), F32),
                   jax.ShapeDtypeStruct((H, S, 1), F32)],
        compiler_params=_params(("parallel",)),
        name=name,
    )(o, h, dog)


def _gdn_layer_fwd(x, w, tag):
    H = w["H"]
    qk = H * LANES
    h = _matmul(x, w["w_in"], "nn", name=f"{tag}_in")
    c = _conv_fwd(h, w["conv"], 3 * qk, name=f"{tag}_conv")
    og, states, inverses = _gdn_scan_fwd(c, h, w["hp"], w["ng"], H, name=f"{tag}_scan")
    y = _matmul(og, w["w_out"], "nn", name=f"{tag}_out")
    return y, (x, h, c, states, inverses, og)


def _gdn_layer_bwd(dy, res, w, tag):
    x, h, c, states, inverses, og = res
    H = w["H"]
    qk = H * LANES
    dog = _matmul(dy, w["w_out"], "nt", name=f"{tag}_out_dx")
    dw_out = _matmul(og, dy, "tn", name=f"{tag}_out_dw")
    dq, dk, dv, dz, dba, dhp, dng = _gdn_scan_bwd(c, h, w["hp"], w["ng"], states, inverses, dog, H, name=f"{tag}_scan_bwd")
    dh_parts, dconv = [], []
    for part, d in enumerate((dq, dk, dv)):
        dh_p, dw_p = _conv_bwd(d, h, w["conv"][:, part * qk:(part + 1) * qk], part * qk, name=f"{tag}_conv_bwd{part}")
        dh_parts.append(dh_p)
        dconv.append(dw_p[:GDN_CONV])
    dh = jnp.concatenate(dh_parts + [dz, dba], axis=1)
    dx = _matmul(dh, w["w_in"], "nt", name=f"{tag}_in_dx")
    dw_in = _matmul(x, dh, "tn", name=f"{tag}_in_dw")
    grads = {"w_in": dw_in, "w_out": dw_out, "conv": jnp.concatenate(dconv, axis=1),
             "a_log": dhp[0, :H], "dt_bias": dhp[1, :H], "norm_g": dng[0]}
    return dx, grads


def _fox_layer_fwd(x, w, tag):
    H, dh = w["H"], w["dh"]
    W = H * dh
    h = _matmul(x, w["w_in"], "nn", name=f"{tag}_in")
    q, k, ccol, crow = _fox_prep_fwd(h, w["bf"], w["qg"], w["kg"], H, dh, name=f"{tag}_prep")
    o, lse = _flash_fwd(q, k, h, ccol, crow, H, dh, name=f"{tag}_flash")
    og = _fox_gate_fwd(o, h, W, name=f"{tag}_gate")
    y = _matmul(og, w["w_out"], "nn", name=f"{tag}_out")
    return y, (x, h, q, k, ccol, crow, o, lse, og)


def _fox_layer_bwd(dy, res, w, tag):
    x, h, q, k, ccol, crow, o, lse, og = res
    H, dh = w["H"], w["dh"]
    dog = _matmul(dy, w["w_out"], "nt", name=f"{tag}_out_dx")
    dw_out = _matmul(og, dy, "tn", name=f"{tag}_out_dw")
    do, dz, delta = _fox_gate_bwd(o, h, dog, H, dh, name=f"{tag}_gate_bwd")
    dqq, resid = _flash_bwd_q(q, k, h, ccol, crow, lse, delta, do, H, dh, name=f"{tag}_flash_bwd_q")
    dkk, dvv, dcrow = _flash_bwd_kv(q, k, h, ccol, crow, lse, delta, resid, do, H, dh, name=f"{tag}_flash_bwd_kv")
    dhq, dhk, df, dbf, dqg, dkg = _fox_prep_bwd(h, w["bf"], w["qg"], w["kg"], dqq, dkk, dcrow, H, dh, name=f"{tag}_prep_bwd")
    dhh = jnp.concatenate([dhq, dhk, dvv, dz, df], axis=1)
    dx = _matmul(dhh, w["w_in"], "nt", name=f"{tag}_in_dx")
    dw_in = _matmul(x, dhh, "tn", name=f"{tag}_in_dw")
    grads = {"w_in": dw_in, "w_out": dw_out, "b_f": dbf[0, :H],
             "q_norm_g": dqg[0, :dh] + dqg[0, dh:], "k_norm_g": dkg[0, :dh] + dkg[0, dh:]}
    return dx, grads


def _pad_cols(w, n):
    return jnp.pad(w, ((0, 0), (0, n - w.shape[1])))


def _build_layers(full, small):
    depth = small["ln_g"].shape[0]
    gh = small["gdn_a_log"].shape[1]
    fh, dh = small["fox_b_f"].shape[1], small["fox_q_norm_g"].shape[1]
    layers = []
    for i in range(depth):
        j = i // 2
        w = {"ln_g": small["ln_g"][i][None], "ln_b": small["ln_b"][i][None],
             "w_gate": full["ple_w_gate"][i], "w_proj": full["ple_w_proj"][i]}
        if i % 2 == 0:
            hp = jnp.zeros((SUBLANES, LANES), F32).at[0, :gh].set(small["gdn_a_log"][j]).at[1, :gh].set(small["gdn_dt_bias"][j])
            w.update(kind="gdn", H=gh, w_in=_pad_cols(full["gdn_w_in"][j], 4 * gh * LANES + LANES),
                     conv=full["gdn_conv_w"][j], hp=hp, ng=small["gdn_norm_g"][j][None], w_out=full["gdn_w_out"][j])
        else:
            bf = jnp.zeros((1, LANES), F32).at[0, :fh].set(small["fox_b_f"][j])
            w.update(kind="fox", H=fh, dh=dh, w_in=_pad_cols(full["fox_w_in"][j], 4 * fh * dh + LANES), bf=bf,
                     qg=jnp.tile(small["fox_q_norm_g"][j], 2)[None], kg=jnp.tile(small["fox_k_norm_g"][j], 2)[None],
                     w_out=full["fox_w_out"][j])
        layers.append(w)
    return layers


def _local_step(x, p, target, layers):
    depth = len(layers)
    alpha = (2 * depth) ** 0.25
    saved = []
    for i, w in enumerate(layers):
        tag = f"l{i}"
        if w["kind"] == "gdn":
            y, res = _gdn_layer_fwd(x, w, tag)
        else:
            y, res = _fox_layer_fwd(x, w, tag)
        x1 = _ln_fwd(x, y, w["ln_g"], w["ln_b"], alpha, name=f"{tag}_ln")
        gate_pre = _matmul(x1, w["w_gate"], "nn", name=f"{tag}_gate_mm")
        pp = _matmul(p[i], w["w_proj"], "nn", name=f"{tag}_proj_mm")
        x2 = _ple_fwd(x1, gate_pre, pp, name=f"{tag}_ple")
        saved.append((res, x, y, x1, gate_pre, pp))
        x = x2
    dx, loss_tile = _loss_head(x, target, name="loss_head")
    grads = [None] * depth
    for i in reversed(range(depth)):
        w = layers[i]
        tag = f"l{i}"
        res, xin, y, x1, gate_pre, pp = saved[i]
        dgp, dpp = _ple_bwd(dx, gate_pre, pp, name=f"{tag}_ple_bwd")
        dx1 = _add(dx, _matmul(dgp, w["w_gate"], "nt", name=f"{tag}_gate_dx"), name=f"{tag}_dx1")
        dw_gate = _matmul(x1, dgp, "tn", name=f"{tag}_gate_dw")
        dw_proj = _matmul(p[i], dpp, "tn", name=f"{tag}_proj_dw")
        dy, dg, db = _ln_bwd(xin, y, w["ln_g"], w["ln_b"], dx1, alpha, name=f"{tag}_ln_bwd")
        if w["kind"] == "gdn":
            dxm, g = _gdn_layer_bwd(dy, res, w, tag)
        else:
            dxm, g = _fox_layer_bwd(dy, res, w, tag)
        dx = _axpy(alpha, dy, dxm, name=f"{tag}_dx")
        g.update({"w_gate": dw_gate, "w_proj": dw_proj, "ln_g": dg[0], "ln_b": db[0]})
        grads[i] = g
    return loss_tile, dx, grads


MESH_ID = pl.DeviceIdType.MESH
HBM_SPEC = pl.BlockSpec(memory_space=pl.ANY)
PACK_COLS = 1024
PACK_ROWS = 256


def _all_gather(shards, name):
    nt = len(shards)

    def body(*refs):
        x_refs, out_refs = refs[:nt], refs[nt:2 * nt]
        send_sems, recv_sems, local_sems = refs[2 * nt:]
        x, y, c = lax.axis_index("x"), lax.axis_index("y"), lax.axis_index("c")
        me, sibling = (x, y, c), (x, y, 1 - c)
        chips = [(1 - x, y), (x, 1 - y), (1 - x, 1 - y)]

        def slot(t, px, py, pc):
            return out_refs[t].at[4 * px + 2 * py + pc]

        def copy(k, t, block, to, src=None):
            return pltpu.make_async_remote_copy(
                src_ref=slot(t, *block) if src is None else src, dst_ref=slot(t, *block),
                send_sem=send_sems.at[k, t], recv_sem=recv_sems.at[k, t], device_id=to, device_id_type=MESH_ID)

        every = range(nt)
        mine = [pltpu.make_async_copy(x_refs[t], slot(t, *me), local_sems.at[t]) for t in every]
        for cp in mine:
            cp.start()
        first = [copy(0, t, me, sibling, src=x_refs[t]) for t in every]
        first += [copy(1 + j, t, me, (*chip, c), src=x_refs[t]) for j, chip in enumerate(chips) for t in every]
        for cp in first:
            cp.start()
        passed = []
        for j, chip in enumerate(chips):
            for t in every:
                copy(1 + j, t, (*chip, c), me).wait_recv()
                passed.append(copy(4 + j, t, (*chip, c), sibling))
                passed[-1].start()
        for t in every:
            copy(0, t, sibling, me).wait_recv()
        for j, chip in enumerate(chips):
            for t in every:
                copy(4 + j, t, (*chip, 1 - c), me).wait_recv()
        for cp in first + passed:
            cp.wait_send()
        for cp in mine:
            cp.wait()

    return pl.pallas_call(
        body,
        out_shape=[jax.ShapeDtypeStruct((N_DEV, *s.shape), s.dtype) for s in shards],
        in_specs=[HBM_SPEC] * nt,
        out_specs=[HBM_SPEC] * nt,
        scratch_shapes=[pltpu.SemaphoreType.DMA((7, nt)), pltpu.SemaphoreType.DMA((7, nt)), pltpu.SemaphoreType.DMA((nt,))],
        name=name,
    )(*shards)


def _all_to_all(slabs, name):
    nt = len(slabs)

    def body(*refs):
        g_refs, out_refs = refs[:nt], refs[nt:2 * nt]
        send_sems, recv_sems, local_sems = refs[2 * nt:]
        x, y, c = lax.axis_index("x"), lax.axis_index("y"), lax.axis_index("c")
        me = 4 * x + 2 * y + c
        mine = [pltpu.make_async_copy(g_refs[t].at[me], out_refs[t].at[me], local_sems.at[t]) for t in range(nt)]
        for cp in mine:
            cp.start()
        copies = []
        for k in range(1, N_DEV):
            px = 1 - x if k & 4 else x
            py = 1 - y if k & 2 else y
            pc = 1 - c if k & 1 else c
            peer = 4 * px + 2 * py + pc
            for t in range(nt):
                copies.append(tuple(
                    pltpu.make_async_remote_copy(src_ref=g_refs[t].at[peer], dst_ref=out_refs[t].at[dst],
                                                 send_sem=send_sems.at[k - 1, t], recv_sem=recv_sems.at[k - 1, t],
                                                 device_id=(px, py, pc), device_id_type=MESH_ID)
                    for dst in (me, peer)))
        for send, _ in copies:
            send.start()
        for send, arrive in copies:
            arrive.wait_recv()
            send.wait_send()
        for cp in mine:
            cp.wait()

    return pl.pallas_call(
        body,
        out_shape=[jax.ShapeDtypeStruct(s.shape, s.dtype) for s in slabs],
        in_specs=[HBM_SPEC] * nt,
        out_specs=[HBM_SPEC] * nt,
        scratch_shapes=[pltpu.SemaphoreType.DMA((7, nt)), pltpu.SemaphoreType.DMA((7, nt)), pltpu.SemaphoreType.DMA((nt,))],
        name=name,
    )(*slabs)


def _pack(flats, dtype):
    flat = jnp.concatenate([f.astype(dtype).reshape(-1) for f in flats])
    unit = PACK_ROWS * PACK_COLS
    n = -(-flat.shape[0] // unit) * unit
    return jnp.pad(flat, (0, n - flat.shape[0])).reshape(n // PACK_COLS, PACK_COLS)


def _unpack(buf, shapes):
    lead = buf.shape[:-2]
    flat = buf.reshape(*lead, -1)
    out, off = [], 0
    for s in shapes:
        n = math.prod(s)
        out.append(flat[..., off:off + n].reshape(*lead, *s))
        off += n
    return out


_ROW_SPLIT = ("ple_w_gate", "gdn_w_out", "fox_w_out")
_COL_SPLIT = ("ple_w_proj", "gdn_w_in", "gdn_conv_w", "fox_w_in")
_SHARDED = ("ple_w_gate", "ple_w_proj", "gdn_w_in", "gdn_conv_w", "gdn_w_out", "fox_w_in", "fox_w_out")
_REPLICATED = ("ln_g", "ln_b", "gdn_a_log", "gdn_dt_bias", "gdn_norm_g", "fox_b_f", "fox_q_norm_g", "fox_k_norm_g")
_WEIGHTS = ("ln_g", "ln_b", "ple_w_gate", "ple_w_proj", "gdn_w_in", "gdn_conv_w", "gdn_a_log", "gdn_dt_bias",
            "gdn_norm_g", "gdn_w_out", "fox_w_in", "fox_b_f", "fox_q_norm_g", "fox_k_norm_g", "fox_w_out")


def _join(name, gathered):
    n, l, a, b = gathered.shape
    if name in _ROW_SPLIT:
        return gathered.transpose(1, 0, 2, 3).reshape(l, n * a, b)
    return gathered.transpose(1, 2, 0, 3).reshape(l, a, n * b)


def _split(name, full):
    l, a, b = full.shape
    if name in _ROW_SPLIT:
        return full.reshape(l, N_DEV, a // N_DEV, b).transpose(1, 0, 2, 3)
    return full.reshape(l, a, N_DEV, b // N_DEV).transpose(2, 0, 1, 3)


def _adamw(w, g_parts, m, v, name):
    shape = w.shape
    R, C = math.prod(shape[:-1]), shape[-1]
    tr = _pick(R, 256, SUBLANES)
    c1 = 1.0 - ADAM_B1 ** ADAM_STEP
    c2 = 1.0 - ADAM_B2 ** ADAM_STEP

    def body(w_ref, g_ref, m_ref, v_ref, go_ref, d_ref, mo_ref, vo_ref):
        gv = g_ref[0]
        for s in range(1, N_DEV):
            gv = gv + g_ref[s]
        mn = ADAM_B1 * m_ref[...] + (1.0 - ADAM_B1) * gv
        vn = ADAM_B2 * v_ref[...] + (1.0 - ADAM_B2) * jnp.square(gv)
        go_ref[...] = gv
        d_ref[...] = -ADAM_LR * ((mn / c1) / (jnp.sqrt(vn / c2) + ADAM_EPS) + ADAM_WD * w_ref[...])
        mo_ref[...] = mn
        vo_ref[...] = vn

    row = pl.BlockSpec((tr, C), lambda i: (i, 0))
    outs = pl.pallas_call(
        body,
        grid=(R // tr,),
        in_specs=[row, pl.BlockSpec((N_DEV, tr, C), lambda i: (0, i, 0)), row, row],
        out_specs=[row] * 4,
        out_shape=[jax.ShapeDtypeStruct((R, C), F32)] * 4,
        compiler_params=_params(("parallel",)),
        name=name,
    )(w.reshape(R, C), g_parts.reshape(N_DEV, R, C), m.reshape(R, C), v.reshape(R, C))
    return [o.reshape(shape) for o in outs]


def _train_step(x, p, target, w, m, v):
    shards = [w[n] if n == "gdn_conv_w" else w[n].astype(MXU_DTYPE) for n in _SHARDED]
    gathered = _all_gather(shards, name="gather_weights")
    full = {n: _join(n, part) for n, part in zip(_SHARDED, gathered)}
    layers = _build_layers(full, {n: w[n] for n in _REPLICATED})

    loss_tile, dx, grads = _local_step(x[0], p[:, 0], target[0], layers)
    loss = lax.psum(loss_tile[0, 0], ("x", "y", "c"))

    depth = len(layers)
    gdn_l = [i for i in range(depth) if i % 2 == 0]
    fox_l = [i for i in range(depth) if i % 2 == 1]

    def stack(key, idx):
        return jnp.stack([grads[i][key] for i in idx])

    full_g = {
        "ple_w_gate": stack("w_gate", range(depth)), "ple_w_proj": stack("w_proj", range(depth)),
        "gdn_w_in": stack("w_in", gdn_l)[..., :w["gdn_w_in"].shape[-1] * N_DEV], "gdn_conv_w": stack("conv", gdn_l),
        "gdn_w_out": stack("w_out", gdn_l),
        "fox_w_in": stack("w_in", fox_l)[..., :w["fox_w_in"].shape[-1] * N_DEV], "fox_w_out": stack("w_out", fox_l)}
    small_g = {
        "ln_g": stack("ln_g", range(depth)), "ln_b": stack("ln_b", range(depth)),
        "gdn_a_log": stack("a_log", gdn_l), "gdn_dt_bias": stack("dt_bias", gdn_l), "gdn_norm_g": stack("norm_g", gdn_l),
        "fox_b_f": stack("b_f", fox_l), "fox_q_norm_g": stack("q_norm_g", fox_l), "fox_k_norm_g": stack("k_norm_g", fox_l)}

    g_parts = dict(zip(_SHARDED, _all_to_all([_split(n, full_g[n]) for n in _SHARDED], name="scatter_grads")))
    small_all = _all_gather([_pack([small_g[n] for n in _REPLICATED], F32)], name="gather_small_grads")[0]
    g_parts.update(zip(_REPLICATED, _unpack(small_all, [w[n].shape for n in _REPLICATED])))

    g, delta, new_m, new_v = {}, {}, {}, {}
    for n in _WEIGHTS:
        g[n], delta[n], new_m[n], new_v[n] = _adamw(w[n], g_parts[n], m[n], v[n], name=f"adamw_{n}")
    return (loss, dx[None], *[g[n] for n in _WEIGHTS], *[delta[n] for n in _WEIGHTS],
            *[new_m[n] for n in _WEIGHTS], *[new_v[n] for n in _WEIGHTS])


def kernel(x, p, ln_g, ln_b, ple_w_gate, ple_w_proj, gdn_w_in, gdn_conv_w, gdn_a_log, gdn_dt_bias, gdn_norm_g, gdn_w_out, fox_w_in, fox_b_f, fox_q_norm_g, fox_k_norm_g, fox_w_out, loss_target, m_ln_g, m_ln_b, m_ple_w_gate, m_ple_w_proj, m_gdn_w_in, m_gdn_conv_w, m_gdn_a_log, m_gdn_dt_bias, m_gdn_norm_g, m_gdn_w_out, m_fox_w_in, m_fox_b_f, m_fox_q_norm_g, m_fox_k_norm_g, m_fox_w_out, v_ln_g, v_ln_b, v_ple_w_gate, v_ple_w_proj, v_gdn_w_in, v_gdn_conv_w, v_gdn_a_log, v_gdn_dt_bias, v_gdn_norm_g, v_gdn_w_out, v_fox_w_in, v_fox_b_f, v_fox_q_norm_g, v_fox_k_norm_g, v_fox_w_out):
    given = dict(locals())
    w = {n: given[n] for n in _WEIGHTS}
    m = {n: given["m_" + n] for n in _WEIGHTS}
    v = {n: given["v_" + n] for n in _WEIGHTS}
    return _train_step(x, p, loss_target, w, m, v)
```

```python
import functools
import math

import jax
import jax.numpy as jnp
from jax import lax
from jax.experimental import pallas as pl
from jax.experimental.pallas import tpu as pltpu

F32 = jnp.float32
BF16 = jnp.bfloat16
MXU_DTYPE = BF16
HI = lax.Precision.HIGHEST

N_DEV = 8
LANES = 128
SUBLANES = 8
VMEM_BYTES = 64 * 1024 * 1024

GDN_CHUNK = 64
GDN_CONV = 4
LN_EPS = 1e-5
RMS_EPS = 1e-6
NEG = -1e30

ADAM_LR = 0.001
ADAM_B1 = 0.9
ADAM_B2 = 0.999
ADAM_EPS = 1e-08
ADAM_WD = 0.01
ADAM_STEP = 10


def _params(semantics, vmem_mb=40):
    return pltpu.CompilerParams(dimension_semantics=semantics, vmem_limit_bytes=vmem_mb * 1024 * 1024)


def _pick(dim, cap, unit=LANES):
    if dim <= cap:
        return dim
    best = None
    for t in range(unit, cap + 1, unit):
        if dim % t == 0:
            best = t
    assert best is not None, (dim, cap)
    return best


def _dims(dims, ndim):
    if ndim == 2:
        return (dims, ((), ()))
    return (((dims[0][0] + 1,), (dims[1][0] + 1,)), ((0,), (0,)))


def _dot(a, b, dims):
    return lax.dot_general(a.astype(MXU_DTYPE), b.astype(MXU_DTYPE), _dims(dims, a.ndim), preferred_element_type=F32)


_NN = ((1,), (0,))
_NT = ((1,), (1,))
_TN = ((0,), (0,))


@jax.custom_vjp
def _mm_nn(a, b):
    return _dot(a, b, _NN)


@jax.custom_vjp
def _mm_nt(a, b):
    return _dot(a, b, _NT)


@jax.custom_vjp
def _mm_tn(a, b):
    return _dot(a, b, _TN)


_mm_nn.defvjp(lambda a, b: (_dot(a, b, _NN), (a, b)), lambda r, g: (_mm_nt(g, r[1]), _mm_tn(r[0], g)))
_mm_nt.defvjp(lambda a, b: (_dot(a, b, _NT), (a, b)), lambda r, g: (_mm_nn(g, r[1]), _mm_tn(g, r[0])))
_mm_tn.defvjp(lambda a, b: (_dot(a, b, _TN), (a, b)), lambda r, g: (_mm_nt(r[1], g), _mm_nn(r[0], g)))


def _mm_hi(a, b, precision=HI):
    return lax.dot_general(a, b, _dims(_NN, a.ndim), precision=precision, preferred_element_type=F32)


def _mm_3x(a, b):
    return _mm_hi(a, b, lax.Precision.HIGH)


def _sigmoid(x):
    return 1.0 / (1.0 + jnp.exp(-x))


def _silu(x):
    return x * _sigmoid(x)


def _softplus(x):
    return jnp.maximum(x, 0.0) + jnp.log(1.0 + jnp.exp(-jnp.abs(x)))


def _iota2(shape, dim):
    return lax.broadcasted_iota(jnp.int32, shape, dim)


def _lane_pick(tile, lane):
    return jnp.sum(jnp.where(_iota2(tile.shape, 1) == lane, tile, 0.0), axis=1, keepdims=True)


def _lane_put(col, lane, width=LANES):
    return jnp.where(_iota2((col.shape[0], width), 1) == lane, col, 0.0)


def _matmul(a, b, mode, out_dtype=F32, *, name, tm=1024, tn=1408, tk=1408, a_cols=None, b_cols=None):
    def cols(arr, rng):
        return (0, arr.shape[1]) if rng is None else rng

    a0, an = cols(a, a_cols)
    b0, bn = cols(b, b_cols)
    if mode == "nn":
        M, K, N = a.shape[0], an, bn
        assert b.shape[0] == K
    elif mode == "nt":
        M, K, N = a.shape[0], an, b.shape[0]
        assert bn == K
    else:
        K, M, N = a.shape[0], an, bn
        assert b.shape[0] == K
    tm, tn, tk = _pick(M, tm), _pick(N, tn), _pick(K, tk)
    nk = K // tk
    if mode == "nn":
        assert a0 % tk == 0 and b0 % tn == 0
        a_spec = pl.BlockSpec((tm, tk), lambda i, j, k: (i, a0 // tk + k))
        b_spec = pl.BlockSpec((tk, tn), lambda i, j, k: (k, b0 // tn + j))
        dims = _NN
    elif mode == "nt":
        assert a0 % tk == 0 and b0 % tk == 0
        a_spec = pl.BlockSpec((tm, tk), lambda i, j, k: (i, a0 // tk + k))
        b_spec = pl.BlockSpec((tn, tk), lambda i, j, k: (j, b0 // tk + k))
        dims = _NT
    else:
        assert a0 % tm == 0 and b0 % tn == 0
        a_spec = pl.BlockSpec((tk, tm), lambda i, j, k: (k, a0 // tm + i))
        b_spec = pl.BlockSpec((tk, tn), lambda i, j, k: (k, b0 // tn + j))
        dims = _TN

    def body(a_ref, b_ref, o_ref, acc_ref):
        k = pl.program_id(2)

        @pl.when(k == 0)
        def _():
            acc_ref[...] = jnp.zeros_like(acc_ref)

        acc_ref[...] += _dot(a_ref[...], b_ref[...], dims)

        @pl.when(k == nk - 1)
        def _():
            o_ref[...] = acc_ref[...].astype(o_ref.dtype)

    return pl.pallas_call(
        body,
        grid=(M // tm, N // tn, nk),
        in_specs=[a_spec, b_spec],
        out_specs=pl.BlockSpec((tm, tn), lambda i, j, k: (i, j)),
        out_shape=jax.ShapeDtypeStruct((M, N), out_dtype),
        scratch_shapes=[pltpu.VMEM((tm, tn), F32)],
        compiler_params=_params(("parallel", "parallel", "arbitrary"), 48),
        name=name,
    )(a, b)


def _rowwise(fn, rows, consts, out_rows, out_accs, *, tile, name, reverse=False, carries=(), vmem_mb=40):
    rows = [r if isinstance(r, tuple) else (r, r.shape[1], 0) for r in rows]
    S = rows[0][0].shape[0]
    tile = _pick(S, tile, SUBLANES)
    nt = S // tile
    nr, nc, no, na = len(rows), len(consts), len(out_rows), len(out_accs)

    def ridx(i):
        return nt - 1 - i if reverse else i

    in_specs = [pl.BlockSpec((tile, w), functools.partial(lambda i, cb: (ridx(i), cb), cb=cb)) for _, w, cb in rows]
    in_specs += [pl.BlockSpec(c.shape, functools.partial(lambda i, nd: (0,) * nd, nd=c.ndim)) for c in consts]
    out_specs = [pl.BlockSpec((tile, c), lambda i: (ridx(i), 0)) for c, _ in out_rows]
    out_specs += [pl.BlockSpec(s, functools.partial(lambda i, nd: (0,) * nd, nd=len(s))) for s, _ in out_accs]
    out_shape = [jax.ShapeDtypeStruct((S, c), d) for c, d in out_rows]
    out_shape += [jax.ShapeDtypeStruct(s, d) for s, d in out_accs]

    def body(*refs):
        rin, cin = refs[:nr], refs[nr:nr + nc]
        rout, aout = refs[nr + nc:nr + nc + no], refs[nr + nc + no:nr + nc + no + na]
        carr = refs[nr + nc + no + na:]
        step = pl.program_id(0)

        @pl.when(step == 0)
        def _():
            for r in aout + carr:
                r[...] = jnp.zeros_like(r)

        outs, accs, newc = fn([r[...] for r in rin], [c[...] for c in cin], [c[...] for c in carr])
        for r, o in zip(rout, outs, strict=True):
            r[...] = o.astype(r.dtype)
        for r, v in zip(aout, accs, strict=True):
            r[...] += v
        for r, v in zip(carr, newc, strict=True):
            r[...] = v

    res = pl.pallas_call(
        body,
        grid=(nt,),
        in_specs=in_specs,
        out_specs=out_specs,
        out_shape=out_shape,
        scratch_shapes=[pltpu.VMEM(s, F32) for s in carries],
        compiler_params=_params(("arbitrary",), vmem_mb),
        name=name,
    )(*[r[0] for r in rows], *consts)
    return res


def _ln_fn(x, y, g, b, alpha):
    r = alpha * x + y
    mu = jnp.mean(r, -1, keepdims=True)
    var = jnp.mean(jnp.square(r - mu), -1, keepdims=True)
    return (r - mu) * lax.rsqrt(var + LN_EPS) * g + b


def _ln_fwd(x, y, g, b, alpha, name):
    D = x.shape[1]

    def fn(rows, consts, _):
        return [_ln_fn(rows[0], rows[1], consts[0], consts[1], alpha)], [], []

    return _rowwise(fn, [x, y], [g, b], [(D, F32)], [], tile=256, name=name)[0]


def _ln_bwd(x, y, g, b, dx1, alpha, name):
    D = x.shape[1]

    def fn(rows, consts, _):
        xv, yv, d = rows
        _, vjp = jax.vjp(lambda yy, gg, bb: _ln_fn(xv, yy, gg, bb, alpha), yv, consts[0], consts[1])
        dy, dg, db = vjp(d)
        return [dy], [dg, db], []

    return _rowwise(fn, [x, y, dx1], [g, b], [(D, F32)], [((1, D), F32), ((1, D), F32)], tile=256, name=name)


def _ple_fwd(x1, gate_pre, pp, name):
    D = x1.shape[1]

    def fn(rows, _, __):
        return [rows[0] + _sigmoid(rows[1]) * rows[2]], [], []

    return _rowwise(fn, [x1, gate_pre, pp], [], [(D, F32)], [], tile=256, name=name)[0]


def _ple_bwd(dx2, gate_pre, pp, name):
    D = dx2.shape[1]

    def fn(rows, _, __):
        d, gp, ppv = rows
        s = _sigmoid(gp)
        return [d * ppv * s * (1.0 - s), d * s], [], []

    return _rowwise(fn, [dx2, gate_pre, pp], [], [(D, F32), (D, F32)], [], tile=256, name=name)


def _add(a, b, name):
    def fn(rows, _, __):
        return [rows[0] + rows[1]], [], []

    return _rowwise(fn, [a, b], [], [(a.shape[1], F32)], [], tile=256, name=name)[0]


def _axpy(alpha, a, b, name):
    def fn(rows, _, __):
        return [alpha * rows[0] + rows[1]], [], []

    return _rowwise(fn, [a, b], [], [(a.shape[1], F32)], [], tile=256, name=name)[0]


def _loss_head(y, target, name):
    D = y.shape[1]

    def fn(rows, _, __):
        e = rows[0] - rows[1]
        part = 0.5 * jnp.sum(jnp.sum(e * e, axis=1, keepdims=True), axis=0, keepdims=True) / D
        return [e / D], [jnp.broadcast_to(part, (SUBLANES, LANES))], []

    return _rowwise(fn, [y, target], [], [(D, F32)], [((SUBLANES, LANES), F32)], tile=256, name=name)


def _conv_fwd(h, w, n_cols, name):
    S = h.shape[0]
    T = _pick(S, 512, SUBLANES)
    CB = _pick(n_cols, 512)
    nt = S // T
    K = GDN_CONV

    def body(x_ref, halo_ref, w_ref, o_ref, buf):
        i = pl.program_id(1)
        buf[0:SUBLANES, :] = jnp.where(i > 0, halo_ref[...], 0.0)
        buf[SUBLANES:, :] = x_ref[...]
        acc = jnp.zeros((T, CB), F32)
        for k in range(K):
            acc = acc + w_ref[k:k + 1, :] * buf[pl.ds(SUBLANES - (K - 1) + k, T), :]
        o_ref[...] = acc

    return pl.pallas_call(
        body,
        grid=(n_cols // CB, nt),
        in_specs=[pl.BlockSpec((T, CB), lambda c, i: (i, c)),
                  pl.BlockSpec((SUBLANES, CB), lambda c, i: (jnp.maximum(i * (T // SUBLANES) - 1, 0), c)),
                  pl.BlockSpec((K, CB), lambda c, i: (0, c))],
        out_specs=pl.BlockSpec((T, CB), lambda c, i: (i, c)),
        out_shape=jax.ShapeDtypeStruct((S, n_cols), F32),
        scratch_shapes=[pltpu.VMEM((T + SUBLANES, CB), F32)],
        compiler_params=_params(("parallel", "parallel")),
        name=name,
    )(h, h, w)


def _conv_bwd(dc, h, w, h_col0, name):
    S, n_cols = dc.shape
    T = _pick(S, 512, SUBLANES)
    CB = _pick(n_cols, 512)
    nt = S // T
    K = GDN_CONV
    assert h_col0 % CB == 0
    hb = h_col0 // CB

    def body(d_ref, halo_ref, x_ref, w_ref, dx_ref, dw_ref, buf):
        i = pl.program_id(1)

        @pl.when(i == 0)
        def _():
            dw_ref[...] = jnp.zeros_like(dw_ref)

        buf[0:T, :] = d_ref[...]
        buf[T:, :] = jnp.where(i < nt - 1, halo_ref[...], 0.0)
        x = x_ref[...]
        acc = jnp.zeros((T, CB), F32)
        for k in range(K):
            shifted = buf[pl.ds(K - 1 - k, T), :]
            acc = acc + w_ref[k:k + 1, :] * shifted
            dw_ref[k:k + 1, :] += jnp.sum(shifted * x, axis=0, keepdims=True)
        dx_ref[...] = acc

    last = S // SUBLANES - 1
    return pl.pallas_call(
        body,
        grid=(n_cols // CB, nt),
        in_specs=[pl.BlockSpec((T, CB), lambda c, i: (i, c)),
                  pl.BlockSpec((SUBLANES, CB), lambda c, i: (jnp.minimum((i + 1) * (T // SUBLANES), last), c)),
                  pl.BlockSpec((T, CB), lambda c, i: (i, hb + c)),
                  pl.BlockSpec((K, CB), lambda c, i: (0, c))],
        out_specs=[pl.BlockSpec((T, CB), lambda c, i: (i, c)),
                   pl.BlockSpec((SUBLANES, CB), lambda c, i: (0, c))],
        out_shape=[jax.ShapeDtypeStruct((S, n_cols), F32), jax.ShapeDtypeStruct((SUBLANES, n_cols), F32)],
        scratch_shapes=[pltpu.VMEM((T + SUBLANES, CB), F32)],
        compiler_params=_params(("parallel", "arbitrary")),
        name=name,
    )(dc, dc, h, w)


def _neumann_inverse(L):
    C = L.shape[-1]
    eye = (_iota2((C, C), 0) == _iota2((C, C), 1)).astype(F32)
    X = eye - L
    P = L
    for _ in range(max(0, math.ceil(math.log2(C)) - 1)):
        P = _mm_3x(P, P)
        X = _mm_3x(X, eye + P)
    return X


@jax.custom_vjp
def _unit_lower_inverse(L):
    return _neumann_inverse(L)


def _unit_lower_inverse_bwd(T, dT):
    Tt = jnp.swapaxes(T, -1, -2)
    return (-_mm_3x(_mm_3x(Tt, dT), Tt),)


_unit_lower_inverse.defvjp(lambda L: (_neumann_inverse(L),) * 2, _unit_lower_inverse_bwd)


@jax.custom_vjp
def _known_inverse(L, T):
    return T


_known_inverse.defvjp(lambda L, T: (T, T), lambda T, dT: (*_unit_lower_inverse_bwd(T, dT), jnp.zeros_like(T)))


def _gdn_chunk(cq, ck, cv, zz, bcol, acol, alog, dtb, ng, state, inverse=None):
    G, C, dk = cq.shape
    q = _silu(cq)
    k = _silu(ck)
    v = _silu(cv)
    q = q * lax.rsqrt(jnp.sum(q * q, -1, keepdims=True) + RMS_EPS) * (dk ** -0.5)
    k = k * lax.rsqrt(jnp.sum(k * k, -1, keepdims=True) + RMS_EPS)
    beta = _sigmoid(bcol)
    g = -jnp.exp(alog) * _softplus(acol + dtb)

    row, col = _iota2((C, C), 0), _iota2((C, C), 1)
    causal, strict, eye = row >= col, row > col, row == col
    g_rows = jnp.swapaxes(jnp.broadcast_to(g, (G, C, C)), -1, -2)
    gc = jnp.sum(jnp.where(causal, g_rows, 0.0), axis=-1, keepdims=True)
    gcb = jnp.broadcast_to(gc, (G, C, C))
    g_last = jnp.sum(jnp.sum(jnp.where((row == C - 1) & (col == 0), gcb, 0.0), axis=-1, keepdims=True), axis=-2, keepdims=True)
    gc_rows = jnp.swapaxes(gcb, -1, -2)
    decay = jnp.exp(jnp.where(causal, gcb - gc_rows, NEG))

    kb = k * beta
    L = jnp.where(strict, _mm_nt(kb, k) * decay, 0.0)
    T = _unit_lower_inverse(L) if inverse is None else _known_inverse(L, inverse)
    u = _mm_3x(T, v * beta)
    w = _mm_3x(T, kb * jnp.exp(gc))
    a_qk = jnp.where(causal, _mm_nt(q, k) * decay, 0.0)
    q_dec = q * jnp.exp(gc)
    k_dec = k * jnp.exp(g_last - gc)
    v_new = u - _mm_nn(w, state)
    o = _mm_nn(q_dec, state) + _mm_nn(a_qk, v_new)
    new_state = state * jnp.exp(g_last) + _mm_tn(k_dec, v_new)
    y = o * lax.rsqrt(jnp.mean(o * o, -1, keepdims=True) + RMS_EPS) * ng * _silu(zz)
    return (y, new_state), T


GDN_HEADS_PER_STEP = 8


def _gdn_specs(H, dk, G, chunk_of=lambda n: n):
    C = GDN_CHUNK
    NG = H // G
    cq = pl.BlockSpec((C, G * dk), lambda n, h: (chunk_of(n), h))
    ck = pl.BlockSpec((C, G * dk), lambda n, h: (chunk_of(n), NG + h))
    cv = pl.BlockSpec((C, G * dk), lambda n, h: (chunk_of(n), 2 * NG + h))
    zz = pl.BlockSpec((C, G * dk), lambda n, h: (chunk_of(n), 3 * NG + h))
    ba = pl.BlockSpec((C, LANES), lambda n, h: (chunk_of(n), 4 * H * dk // LANES))
    return cq, ck, cv, zz, ba


def _gdn_step_args(cq_ref, ck_ref, cv_ref, z_ref, ba_ref, hp_ref, hg, G, H, dk):
    ba = ba_ref[...]

    def heads(ref):
        return jnp.stack([ref[:, g * dk:(g + 1) * dk] for g in range(G)])

    def picks(tile, offset):
        return jnp.stack([_lane_pick(tile, offset + hg * G + g) for g in range(G)])

    return (heads(cq_ref), heads(ck_ref), heads(cv_ref), heads(z_ref), picks(ba, 0), picks(ba, H),
            picks(hp_ref[0:1, :], 0), picks(hp_ref[1:2, :], 0))


def _gdn_scan_fwd(c, h, hp, ng, H, name):
    S = c.shape[0]
    dk = c.shape[1] // (3 * H)
    assert dk == LANES
    C = GDN_CHUNK
    NC = S // C

    G = min(GDN_HEADS_PER_STEP, H)
    assert H % G == 0
    NG = H // G

    def body(cq_ref, ck_ref, cv_ref, z_ref, ba_ref, hp_ref, ng_ref, y_ref, s_ref, t_ref, state):
        n, hg = pl.program_id(0), pl.program_id(1)
        heads = pl.ds(hg * G, G)

        @pl.when(n == 0)
        def _():
            state[heads] = jnp.zeros((G, dk, dk), F32)

        st = state[heads]
        s_ref[0] = st
        (y, new_state), inverse = _gdn_chunk(
            *_gdn_step_args(cq_ref, ck_ref, cv_ref, z_ref, ba_ref, hp_ref, hg, G, H, dk), ng_ref[...], st)
        t_ref[0] = inverse
        for g in range(G):
            y_ref[:, g * dk:(g + 1) * dk] = y[g].astype(y_ref.dtype)
        state[heads] = new_state

    cq, ck, cv, zz, ba = _gdn_specs(H, dk, G)
    return pl.pallas_call(
        body,
        grid=(NC, NG),
        in_specs=[cq, ck, cv, zz, ba, pl.BlockSpec((SUBLANES, LANES), lambda n, h: (0, 0)),
                  pl.BlockSpec((1, dk), lambda n, h: (0, 0))],
        out_specs=[pl.BlockSpec((C, G * dk), lambda n, h: (n, h)),
                   pl.BlockSpec((1, G, dk, dk), lambda n, h: (n, h, 0, 0)),
                   pl.BlockSpec((1, G, C, C), lambda n, h: (n, h, 0, 0))],
        out_shape=[jax.ShapeDtypeStruct((S, H * dk), MXU_DTYPE), jax.ShapeDtypeStruct((NC, H, dk, dk), F32),
                   jax.ShapeDtypeStruct((NC, H, C, C), F32)],
        scratch_shapes=[pltpu.VMEM((H, dk, dk), F32)],
        compiler_params=_params(("arbitrary", "arbitrary")),
        name=name,
    )(c, c, c, h, h, hp, ng)


def _gdn_scan_bwd(c, h, hp, ng, states, inverses, dy, H, name):
    S = c.shape[0]
    dk = c.shape[1] // (3 * H)
    C = GDN_CHUNK
    NC = S // C

    G = min(GDN_HEADS_PER_STEP, H)
    NG = H // G

    def body(cq_ref, ck_ref, cv_ref, z_ref, ba_ref, hp_ref, ng_ref, s_ref, t_ref, dy_ref,
             dq_ref, dk_ref, dv_ref, dz_ref, dba_ref, dhp_ref, dng_ref, dstate):
        n, hg = pl.program_id(0), pl.program_id(1)

        @pl.when((n == 0) & (hg == 0))
        def _():
            dhp_ref[...] = jnp.zeros_like(dhp_ref)
            dng_ref[...] = jnp.zeros_like(dng_ref)

        heads = pl.ds(hg * G, G)

        @pl.when(n == 0)
        def _():
            dstate[heads] = jnp.zeros((G, dk, dk), F32)

        args = (*_gdn_step_args(cq_ref, ck_ref, cv_ref, z_ref, ba_ref, hp_ref, hg, G, H, dk), ng_ref[...], s_ref[0])
        inverse = t_ref[0]
        _, vjp, _ = jax.vjp(lambda *a: _gdn_chunk(*a, inverse=inverse), *args, has_aux=True)
        dy = jnp.stack([dy_ref[:, g * dk:(g + 1) * dk] for g in range(G)])
        dcq, dck, dcv, dzz, dbc, dac, dal, ddt, dng, dst = vjp((dy, dstate[heads]))
        dstate[heads] = dst
        dba = jnp.zeros((C, LANES), F32)
        dhp0 = jnp.zeros((1, LANES), F32)
        dhp1 = jnp.zeros((1, LANES), F32)
        for g in range(G):
            hd = hg * G + g
            sl = slice(g * dk, (g + 1) * dk)
            dq_ref[:, sl] = dcq[g]
            dk_ref[:, sl] = dck[g]
            dv_ref[:, sl] = dcv[g]
            dz_ref[:, sl] = dzz[g]
            dba = dba + _lane_put(dbc[g], hd) + _lane_put(dac[g], H + hd)
            dhp0 = dhp0 + _lane_put(dal[g], hd)
            dhp1 = dhp1 + _lane_put(ddt[g], hd)

        @pl.when(hg == 0)
        def _():
            dba_ref[...] = dba

        @pl.when(hg > 0)
        def _():
            dba_ref[...] += dba

        dhp_ref[0:1, :] += dhp0
        dhp_ref[1:2, :] += dhp1
        dng_ref[...] += dng

    rev = lambda n: NC - 1 - n
    blk = pl.BlockSpec
    in_specs = [*_gdn_specs(H, dk, G, rev),
                blk((SUBLANES, LANES), lambda n, h: (0, 0)), blk((1, dk), lambda n, h: (0, 0)),
                blk((1, G, dk, dk), lambda n, h: (rev(n), h, 0, 0)), blk((1, G, C, C), lambda n, h: (rev(n), h, 0, 0)),
                blk((C, G * dk), lambda n, h: (rev(n), h))]
    out_specs = [blk((C, G * dk), lambda n, h: (rev(n), h))] * 4 + [
        blk((C, LANES), lambda n, h: (rev(n), 0)),
        blk((SUBLANES, LANES), lambda n, h: (0, 0)), blk((1, dk), lambda n, h: (0, 0))]
    out_shape = [jax.ShapeDtypeStruct((S, H * dk), F32)] * 4 + [
        jax.ShapeDtypeStruct((S, LANES), F32), jax.ShapeDtypeStruct((SUBLANES, LANES), F32),
        jax.ShapeDtypeStruct((1, dk), F32)]
    return pl.pallas_call(
        body,
        grid=(NC, NG),
        in_specs=in_specs,
        out_specs=out_specs,
        out_shape=out_shape,
        scratch_shapes=[pltpu.VMEM((H, dk, dk), F32)],
        compiler_params=_params(("arbitrary", "arbitrary")),
        name=name,
    )(c, c, c, h, h, hp, ng, states, inverses, dy)


def _pair_rms(x, gain, dh):
    first = _iota2(x.shape, 1) < dh
    sq = x * x
    ss_a = jnp.sum(jnp.where(first, sq, 0.0), axis=1, keepdims=True)
    ss_b = jnp.sum(jnp.where(first, 0.0, sq), axis=1, keepdims=True)
    inv = jnp.where(first, lax.rsqrt(ss_a / dh + RMS_EPS), lax.rsqrt(ss_b / dh + RMS_EPS))
    return x * inv * gain


def _log_sigmoid(x):
    return jnp.minimum(x, 0.0) - jnp.log(1.0 + jnp.exp(-jnp.abs(x)))


def _cum_fn(fr, bf, carry):
    T = fr.shape[0]
    tril = (_iota2((T, T), 0) >= _iota2((T, T), 1)).astype(F32)
    c = _mm_hi(tril, _log_sigmoid(fr + bf)) + carry
    last = jnp.sum(jnp.where(_iota2(c.shape, 0) == T - 1, c, 0.0), axis=0, keepdims=True)
    return c, last


def _fox_prep_fwd(h, bf, qg, kg, H, dh, name):
    S = h.shape[0]
    W = H * dh
    assert 2 * dh == LANES and H <= LANES
    T = _pick(S, 256, LANES)
    nt = S // T
    npair = H // 2

    def body(hq_ref, hk_ref, f_ref, bf_ref, qg_ref, kg_ref, q_ref, k_ref, ccol_ref, crow_ref, carry):
        i = pl.program_id(0)

        @pl.when(i == 0)
        def _():
            carry[...] = jnp.zeros_like(carry)

        for p in range(npair):
            sl = slice(p * LANES, (p + 1) * LANES)
            q_ref[:, sl] = (_pair_rms(hq_ref[:, sl], qg_ref[...], dh) * (dh ** -0.5)).astype(q_ref.dtype)
            k_ref[:, sl] = _pair_rms(hk_ref[:, sl], kg_ref[...], dh).astype(k_ref.dtype)
        c, last = _cum_fn(f_ref[...], bf_ref[...], carry[...])
        carry[...] = last
        ct = c.T
        for hh in range(H):
            ccol_ref[hh] = c[:, hh:hh + 1]
            crow_ref[hh] = ct[hh:hh + 1, :]

    return pl.pallas_call(
        body,
        grid=(nt,),
        in_specs=[pl.BlockSpec((T, W), lambda i: (i, 0)), pl.BlockSpec((T, W), lambda i: (i, 1)),
                  pl.BlockSpec((T, LANES), lambda i: (i, 4 * W // LANES)),
                  pl.BlockSpec((1, LANES), lambda i: (0, 0)), pl.BlockSpec((1, LANES), lambda i: (0, 0)),
                  pl.BlockSpec((1, LANES), lambda i: (0, 0))],
        out_specs=[pl.BlockSpec((T, W), lambda i: (i, 0)), pl.BlockSpec((T, W), lambda i: (i, 0)),
                   pl.BlockSpec((H, T, 1), lambda i: (0, i, 0)), pl.BlockSpec((H, 1, T), lambda i: (0, 0, i))],
        out_shape=[jax.ShapeDtypeStruct((S, W), MXU_DTYPE), jax.ShapeDtypeStruct((S, W), MXU_DTYPE),
                   jax.ShapeDtypeStruct((H, S, 1), F32), jax.ShapeDtypeStruct((H, 1, S), F32)],
        scratch_shapes=[pltpu.VMEM((1, LANES), F32)],
        compiler_params=_params(("arbitrary",)),
        name=name,
    )(h, h, h, bf, qg, kg)


def _fox_prep_bwd(h, bf, qg, kg, dq, dk, dcrow, H, dh, name):
    S = h.shape[0]
    W = H * dh
    T = _pick(S, 256, LANES)
    nt = S // T
    npair = H // 2

    def body(hq_ref, hk_ref, f_ref, bf_ref, qg_ref, kg_ref, dq_ref, dk_ref, dcrow_ref,
             dhq_ref, dhk_ref, df_ref, dbf_ref, dqg_ref, dkg_ref, dcarry, dct):
        i = pl.program_id(0)

        @pl.when(i == 0)
        def _():
            dcarry[...] = jnp.zeros_like(dcarry)
            dbf_ref[...] = jnp.zeros_like(dbf_ref)
            dqg_ref[...] = jnp.zeros_like(dqg_ref)
            dkg_ref[...] = jnp.zeros_like(dkg_ref)

        for p in range(npair):
            sl = slice(p * LANES, (p + 1) * LANES)
            _, vjp = jax.vjp(lambda x, g: _pair_rms(x, g, dh) * (dh ** -0.5), hq_ref[:, sl], qg_ref[...])
            dx, dg = vjp(dq_ref[:, sl])
            dhq_ref[:, sl] = dx
            dqg_ref[...] += dg
            _, vjp = jax.vjp(lambda x, g: _pair_rms(x, g, dh), hk_ref[:, sl], kg_ref[...])
            dx, dg = vjp(dk_ref[:, sl])
            dhk_ref[:, sl] = dx
            dkg_ref[...] += dg
        dct[...] = jnp.zeros_like(dct)
        for hh in range(H):
            dct[hh:hh + 1, :] = dcrow_ref[hh]
        _, vjp = jax.vjp(lambda f, b: _cum_fn(f, b, jnp.zeros((1, LANES), F32)), f_ref[...], bf_ref[...])
        dc = dct[...].T
        df, dbf = vjp((dc, dcarry[...]))
        df_ref[...] = df
        dbf_ref[...] += dbf
        dcarry[...] = dcarry[...] + jnp.sum(dc, axis=0, keepdims=True)

    rv = lambda i: nt - 1 - i
    return pl.pallas_call(
        body,
        grid=(nt,),
        in_specs=[pl.BlockSpec((T, W), lambda i: (rv(i), 0)), pl.BlockSpec((T, W), lambda i: (rv(i), 1)),
                  pl.BlockSpec((T, LANES), lambda i: (rv(i), 4 * W // LANES)),
                  pl.BlockSpec((1, LANES), lambda i: (0, 0)), pl.BlockSpec((1, LANES), lambda i: (0, 0)),
                  pl.BlockSpec((1, LANES), lambda i: (0, 0)),
                  pl.BlockSpec((T, W), lambda i: (rv(i), 0)), pl.BlockSpec((T, W), lambda i: (rv(i), 0)),
                  pl.BlockSpec((H, 1, T), lambda i: (0, 0, rv(i)))],
        out_specs=[pl.BlockSpec((T, W), lambda i: (rv(i), 0)), pl.BlockSpec((T, W), lambda i: (rv(i), 0)),
                   pl.BlockSpec((T, LANES), lambda i: (rv(i), 0)),
                   pl.BlockSpec((1, LANES), lambda i: (0, 0)), pl.BlockSpec((1, LANES), lambda i: (0, 0)),
                   pl.BlockSpec((1, LANES), lambda i: (0, 0))],
        out_shape=[jax.ShapeDtypeStruct((S, W), F32), jax.ShapeDtypeStruct((S, W), F32),
                   jax.ShapeDtypeStruct((S, LANES), F32)] + [jax.ShapeDtypeStruct((1, LANES), F32)] * 3,
        scratch_shapes=[pltpu.VMEM((1, LANES), F32), pltpu.VMEM((LANES, T), F32)],
        compiler_params=_params(("arbitrary",)),
        name=name,
    )(h, h, h, bf, qg, kg, dq, dk, dcrow)


def _head_masks(dh):
    first = _iota2((1, LANES), 1) < dh
    return first, jnp.logical_not(first)


def _per_head(tile, dh):
    return jnp.stack([jnp.where(mask, tile, 0) for mask in _head_masks(dh)])


def _both(tile):
    return jnp.stack([tile, tile])


def _rows2(x):
    return x.reshape(2 * x.shape[1], x.shape[2])


def _stacked(tile, dh):
    return _rows2(_per_head(tile, dh))


FLASH_SUB = 128


def _flash_scores(q_m, k, ccol, crow, row0, diagonal):
    s = _dot(q_m, k, _NT) + ccol - crow
    if diagonal:
        s = jnp.where(_iota2(s.shape, s.ndim - 1) <= row0 + _iota2(s.shape, s.ndim - 2), s, NEG)
    return s


def _flash_tiles(S):
    tk = _pick(S, 512, LANES)
    tq = 2 * tk if S % (2 * tk) == 0 else tk
    return tq, tk


def _causal_blocks(S, tq, tk, by_query):
    ratio = tq // tk
    if by_query:
        pairs = [(i, j) for i in range(S // tq) for j in range(ratio * (i + 1))]
    else:
        pairs = [(i, j) for j in range(S // tk) for i in range(j // ratio, S // tq)]
    return (jnp.asarray([a for a, _ in pairs], jnp.int32), jnp.asarray([b for _, b in pairs], jnp.int32))


def _flash_blocks(i, j, tq, tk, fn, sub=FLASH_SUB):
    sub = min(sub, tq)
    ratio = tq // tk
    offset = j - ratio * i

    @pl.when(offset < 0)
    def _():
        for r in range(tq // sub):
            fn(slice(r * sub, (r + 1) * sub), 0, False)

    for d in range(ratio):
        @pl.when(offset == d)
        def _():
            for r in range(tq // sub):
                first_row, first_key = r * sub, d * tk
                if first_key > first_row + sub - 1:
                    continue
                fn(slice(first_row, first_row + sub), first_row - first_key, first_key + tk - 1 > first_row)


def _flash_fwd(q, k, h, ccol, crow, H, dh, name):
    S, W = q.shape
    tq, tk = _flash_tiles(S)
    nq, ratio = S // tq, tq // tk
    npair = H // 2
    vblk = 2 * W // LANES

    def body(ii_ref, jj_ref, q_ref, k_ref, v_ref, ccol_ref, crow_ref, o_ref, lse_ref, m_s, l_s, acc_s):
        i, j = ii_ref[pl.program_id(1)], jj_ref[pl.program_id(1)]

        @pl.when(j == 0)
        def _():
            m_s[...] = jnp.full_like(m_s, NEG)
            l_s[...] = jnp.zeros_like(l_s)
            acc_s[...] = jnp.zeros_like(acc_s)

        def tile(rows, row0, diagonal):
            s = _flash_scores(_per_head(q_ref[rows, :], dh), _both(k_ref[...]), ccol_ref[:, rows, :], crow_ref[...], row0, diagonal)
            m_old = m_s[:, rows, :]
            m_new = jnp.maximum(m_old, jnp.max(s, axis=-1, keepdims=True))
            alpha = jnp.exp(m_old - m_new)
            p = jnp.exp(s - m_new)
            l_s[:, rows, :] = alpha * l_s[:, rows, :] + jnp.sum(p, axis=-1, keepdims=True)
            first, _ = _head_masks(dh)
            acc_s[rows, :] = jnp.where(first, alpha[0], alpha[1]) * acc_s[rows, :] + _dot(
                jnp.concatenate([p[0], p[1]], axis=1), _stacked(v_ref[...].astype(MXU_DTYPE), dh), _NN)
            m_s[:, rows, :] = m_new

        _flash_blocks(i, j, tq, tk, tile)

        @pl.when(j == ratio * (i + 1) - 1)
        def _():
            first, _ = _head_masks(dh)
            o_ref[...] = acc_s[...] / jnp.where(first, l_s[0], l_s[1])
            for a in range(2):
                lse_ref[a] = m_s[a] + jnp.log(l_s[a])

    ii, jj = _causal_blocks(S, tq, tk, by_query=True)
    return pl.pallas_call(
        body,
        grid_spec=pltpu.PrefetchScalarGridSpec(
            num_scalar_prefetch=2,
            grid=(npair, len(ii)),
            in_specs=[pl.BlockSpec((tq, LANES), lambda p, t, ii, jj: (ii[t], p)),
                      pl.BlockSpec((tk, LANES), lambda p, t, ii, jj: (jj[t], p)),
                      pl.BlockSpec((tk, LANES), lambda p, t, ii, jj: (jj[t], vblk + p)),
                      pl.BlockSpec((2, tq, 1), lambda p, t, ii, jj: (p, ii[t], 0)),
                      pl.BlockSpec((2, 1, tk), lambda p, t, ii, jj: (p, 0, jj[t]))],
            out_specs=[pl.BlockSpec((tq, LANES), lambda p, t, ii, jj: (ii[t], p)),
                       pl.BlockSpec((2, tq, 1), lambda p, t, ii, jj: (p, ii[t], 0))],
            scratch_shapes=[pltpu.VMEM((2, tq, 1), F32), pltpu.VMEM((2, tq, 1), F32), pltpu.VMEM((tq, LANES), F32)]),
        out_shape=[jax.ShapeDtypeStruct((S, W), F32), jax.ShapeDtypeStruct((H, S, 1), F32)],
        compiler_params=_params(("parallel", "arbitrary")),
        name=name,
    )(ii, jj, q, k, h, ccol, crow)


def _flash_bwd_kv(q, k, h, ccol, crow, lse, delta, resid, do, H, dh, name):
    S, W = q.shape
    tq, tk = _flash_tiles(S)
    nq, ratio = S // tq, tq // tk
    npair = H // 2
    vblk = 2 * W // LANES

    def body(ii_ref, jj_ref, q_ref, k_ref, v_ref, ccol_ref, crow_ref, lse_ref, dl_ref, rs_ref, do_ref, dk_ref, dv_ref, dcr_ref,
             dkt_s, dvt_s):
        i, j = ii_ref[pl.program_id(1)], jj_ref[pl.program_id(1)]

        @pl.when(i == j // ratio)
        def _():
            dkt_s[...] = jnp.zeros_like(dkt_s)
            dvt_s[...] = jnp.zeros_like(dvt_s)
            dcr_ref[...] = jnp.zeros_like(dcr_ref)

        def tile(rows, row0, diagonal):
            kv, vv = _both(k_ref[...]), _both(v_ref[...].astype(MXU_DTYPE))
            q_m = _per_head(q_ref[rows, :], dh)
            do_m = _per_head(do_ref[rows, :].astype(MXU_DTYPE), dh)
            s = _flash_scores(q_m, kv, ccol_ref[:, rows, :], crow_ref[...], row0, diagonal)
            p = jnp.exp(s - lse_ref[:, rows, :])
            ds = p * (_dot(do_m, vv, _NT) - (dl_ref[:, rows, :] + rs_ref[:, rows, :]))
            dvt_s[...] += _dot(_rows2(do_m), _rows2(p), _TN)
            dkt_s[...] += _dot(_rows2(q_m), _rows2(ds), _TN)
            dcr_ref[...] -= jnp.sum(ds, axis=1, keepdims=True)

        _flash_blocks(i, j, tq, tk, tile, sub=2 * FLASH_SUB)

        @pl.when(i == nq - 1)
        def _():
            dk_ref[...] = dkt_s[...].T
            dv_ref[...] = dvt_s[...].T

    ii, jj = _causal_blocks(S, tq, tk, by_query=False)
    qrow = pl.BlockSpec((2, tq, 1), lambda p, t, ii, jj: (p, ii[t], 0))
    return pl.pallas_call(
        body,
        grid_spec=pltpu.PrefetchScalarGridSpec(
            num_scalar_prefetch=2,
            grid=(npair, len(ii)),
            in_specs=[pl.BlockSpec((tq, LANES), lambda p, t, ii, jj: (ii[t], p)),
                      pl.BlockSpec((tk, LANES), lambda p, t, ii, jj: (jj[t], p)),
                      pl.BlockSpec((tk, LANES), lambda p, t, ii, jj: (jj[t], vblk + p)),
                      qrow,
                      pl.BlockSpec((2, 1, tk), lambda p, t, ii, jj: (p, 0, jj[t])),
                      qrow, qrow, qrow,
                      pl.BlockSpec((tq, LANES), lambda p, t, ii, jj: (ii[t], p))],
            out_specs=[pl.BlockSpec((tk, LANES), lambda p, t, ii, jj: (jj[t], p)),
                       pl.BlockSpec((tk, LANES), lambda p, t, ii, jj: (jj[t], p)),
                       pl.BlockSpec((2, 1, tk), lambda p, t, ii, jj: (p, 0, jj[t]))],
            scratch_shapes=[pltpu.VMEM((LANES, tk), F32), pltpu.VMEM((LANES, tk), F32)]),
        out_shape=[jax.ShapeDtypeStruct((S, W), F32), jax.ShapeDtypeStruct((S, W), F32),
                   jax.ShapeDtypeStruct((H, 1, S), F32)],
        compiler_params=_params(("parallel", "arbitrary")),
        name=name,
    )(ii, jj, q, k, h, ccol, crow, lse, delta, resid, do)


def _flash_bwd_q(q, k, h, ccol, crow, lse, delta, do, H, dh, name):
    S, W = q.shape
    tq, tk = _flash_tiles(S)
    nq, ratio = S // tq, tq // tk
    npair = H // 2
    vblk = 2 * W // LANES

    def body(ii_ref, jj_ref, q_ref, k_ref, v_ref, ccol_ref, crow_ref, lse_ref, dl_ref, do_ref, dq_ref, rs_ref):
        i, j = ii_ref[pl.program_id(1)], jj_ref[pl.program_id(1)]

        @pl.when(j == 0)
        def _():
            dq_ref[...] = jnp.zeros_like(dq_ref)
            rs_ref[...] = jnp.zeros_like(rs_ref)

        def tile(rows, row0, diagonal):
            kv, vv = k_ref[...], _both(v_ref[...].astype(MXU_DTYPE))
            s = _flash_scores(_per_head(q_ref[rows, :], dh), _both(kv), ccol_ref[:, rows, :], crow_ref[...], row0, diagonal)
            p = jnp.exp(s - lse_ref[:, rows, :])
            ds = p * (_dot(_per_head(do_ref[rows, :].astype(MXU_DTYPE), dh), vv, _NT) - dl_ref[:, rows, :])
            dq = _dot(ds, _per_head(kv, dh), _NN)
            dq_ref[rows, :] += dq[0] + dq[1]
            rs_ref[:, rows, :] += jnp.sum(ds, axis=-1, keepdims=True)

        _flash_blocks(i, j, tq, tk, tile, sub=tk)

    ii, jj = _causal_blocks(S, tq, tk, by_query=True)
    qrow = pl.BlockSpec((2, tq, 1), lambda p, t, ii, jj: (p, ii[t], 0))
    return pl.pallas_call(
        body,
        grid_spec=pltpu.PrefetchScalarGridSpec(
            num_scalar_prefetch=2,
            grid=(npair, len(ii)),
            in_specs=[pl.BlockSpec((tq, LANES), lambda p, t, ii, jj: (ii[t], p)),
                      pl.BlockSpec((tk, LANES), lambda p, t, ii, jj: (jj[t], p)),
                      pl.BlockSpec((tk, LANES), lambda p, t, ii, jj: (jj[t], vblk + p)),
                      qrow,
                      pl.BlockSpec((2, 1, tk), lambda p, t, ii, jj: (p, 0, jj[t])),
                      qrow, qrow,
                      pl.BlockSpec((tq, LANES), lambda p, t, ii, jj: (ii[t], p))],
            out_specs=[pl.BlockSpec((tq, LANES), lambda p, t, ii, jj: (ii[t], p)), qrow]),
        out_shape=[jax.ShapeDtypeStruct((S, W), F32), jax.ShapeDtypeStruct((H, S, 1), F32)],
        compiler_params=_params(("parallel", "arbitrary")),
        name=name,
    )(ii, jj, q, k, h, ccol, crow, lse, delta, do)


def _fox_gate_fwd(o, h, W, name):
    def fn(rows, _, __):
        return [rows[0] * _silu(rows[1])], [], []

    return _rowwise(fn, [o, (h, W, 3)], [], [(W, MXU_DTYPE)], [], tile=256, name=name)[0]


def _fox_gate_bwd(o, h, dog, H, dh, name):
    S, W = o.shape
    T = _pick(S, 256, SUBLANES)

    def body(o_ref, z_ref, d_ref, do_ref, dz_ref, dl_ref):
        ov, zv, dv = o_ref[...], z_ref[...], d_ref[...]
        sg = _sigmoid(zv)
        do = dv * zv * sg
        do_ref[...] = do
        dz_ref[...] = dv * ov * sg * (1.0 + zv * (1.0 - sg))
        prod = do * ov
        for p in range(H // 2):
            blk = prod[:, p * LANES:(p + 1) * LANES]
            first = _iota2(blk.shape, 1) < dh
            dl_ref[2 * p] = jnp.sum(jnp.where(first, blk, 0.0), axis=1, keepdims=True)
            dl_ref[2 * p + 1] = jnp.sum(jnp.where(first, 0.0, blk), axis=1, keepdims=True)

    return pl.pallas_call(
        body,
        grid=(S // T,),
        in_specs=[pl.BlockSpec((T, W), lambda i: (i, 0)), pl.BlockSpec((T, W), lambda i: (i, 3)),
                  pl.BlockSpec((T, W), lambda i: (i, 0))],
        out_specs=[pl.BlockSpec((T, W), lambda i: (i, 0)), pl.BlockSpec((T, W), lambda i: (i, 0)),
                   pl.BlockSpec((H, T, 1), lambda i: (0, i, 0))],
        out_shape=[jax.ShapeDtypeStruct((S, W), F32), jax.ShapeDtypeStruct((S, W), F32),
                   jax.ShapeDtypeStruct((H, S, 1), F32)],
        compiler_params=_params(("parallel",)),
        name=name,
    )(o, h, dog)


def _gdn_layer_fwd(x, w, tag):
    H = w["H"]
    qk = H * LANES
    h = _matmul(x, w["w_in"], "nn", name=f"{tag}_in")
    c = _conv_fwd(h, w["conv"], 3 * qk, name=f"{tag}_conv")
    og, states, inverses = _gdn_scan_fwd(c, h, w["hp"], w["ng"], H, name=f"{tag}_scan")
    y = _matmul(og, w["w_out"], "nn", name=f"{tag}_out")
    return y, (x, h, c, states, inverses, og)


def _gdn_layer_bwd(dy, res, w, tag):
    x, h, c, states, inverses, og = res
    H = w["H"]
    qk = H * LANES
    dog = _matmul(dy, w["w_out"], "nt", name=f"{tag}_out_dx")
    dw_out = _matmul(og, dy, "tn", name=f"{tag}_out_dw")
    dq, dk, dv, dz, dba, dhp, dng = _gdn_scan_bwd(c, h, w["hp"], w["ng"], states, inverses, dog, H, name=f"{tag}_scan_bwd")
    dh_parts, dconv = [], []
    for part, d in enumerate((dq, dk, dv)):
        dh_p, dw_p = _conv_bwd(d, h, w["conv"][:, part * qk:(part + 1) * qk], part * qk, name=f"{tag}_conv_bwd{part}")
        dh_parts.append(dh_p)
        dconv.append(dw_p[:GDN_CONV])
    dh = jnp.concatenate(dh_parts + [dz, dba], axis=1)
    dx = _matmul(dh, w["w_in"], "nt", name=f"{tag}_in_dx")
    dw_in = _matmul(x, dh, "tn", name=f"{tag}_in_dw")
    grads = {"w_in": dw_in, "w_out": dw_out, "conv": jnp.concatenate(dconv, axis=1),
             "a_log": dhp[0, :H], "dt_bias": dhp[1, :H], "norm_g": dng[0]}
    return dx, grads


def _fox_layer_fwd(x, w, tag):
    H, dh = w["H"], w["dh"]
    W = H * dh
    h = _matmul(x, w["w_in"], "nn", name=f"{tag}_in")
    q, k, ccol, crow = _fox_prep_fwd(h, w["bf"], w["qg"], w["kg"], H, dh, name=f"{tag}_prep")
    o, lse = _flash_fwd(q, k, h, ccol, crow, H, dh, name=f"{tag}_flash")
    og = _fox_gate_fwd(o, h, W, name=f"{tag}_gate")
    y = _matmul(og, w["w_out"], "nn", name=f"{tag}_out")
    return y, (x, h, q, k, ccol, crow, o, lse, og)


def _fox_layer_bwd(dy, res, w, tag):
    x, h, q, k, ccol, crow, o, lse, og = res
    H, dh = w["H"], w["dh"]
    dog = _matmul(dy, w["w_out"], "nt", name=f"{tag}_out_dx")
    dw_out = _matmul(og, dy, "tn", name=f"{tag}_out_dw")
    do, dz, delta = _fox_gate_bwd(o, h, dog, H, dh, name=f"{tag}_gate_bwd")
    dqq, resid = _flash_bwd_q(q, k, h, ccol, crow, lse, delta, do, H, dh, name=f"{tag}_flash_bwd_q")
    dkk, dvv, dcrow = _flash_bwd_kv(q, k, h, ccol, crow, lse, delta, resid, do, H, dh, name=f"{tag}_flash_bwd_kv")
    dhq, dhk, df, dbf, dqg, dkg = _fox_prep_bwd(h, w["bf"], w["qg"], w["kg"], dqq, dkk, dcrow, H, dh, name=f"{tag}_prep_bwd")
    dhh = jnp.concatenate([dhq, dhk, dvv, dz, df], axis=1)
    dx = _matmul(dhh, w["w_in"], "nt", name=f"{tag}_in_dx")
    dw_in = _matmul(x, dhh, "tn", name=f"{tag}_in_dw")
    grads = {"w_in": dw_in, "w_out": dw_out, "b_f": dbf[0, :H],
             "q_norm_g": dqg[0, :dh] + dqg[0, dh:], "k_norm_g": dkg[0, :dh] + dkg[0, dh:]}
    return dx, grads


def _pad_cols(w, n):
    return jnp.pad(w, ((0, 0), (0, n - w.shape[1])))


def _build_layers(full, small):
    depth = small["ln_g"].shape[0]
    gh = small["gdn_a_log"].shape[1]
    fh, dh = small["fox_b_f"].shape[1], small["fox_q_norm_g"].shape[1]
    layers = []
    for i in range(depth):
        j = i // 2
        w = {"ln_g": small["ln_g"][i][None], "ln_b": small["ln_b"][i][None],
             "w_gate": full["ple_w_gate"][i], "w_proj": full["ple_w_proj"][i]}
        if i % 2 == 0:
            hp = jnp.zeros((SUBLANES, LANES), F32).at[0, :gh].set(small["gdn_a_log"][j]).at[1, :gh].set(small["gdn_dt_bias"][j])
            w.update(kind="gdn", H=gh, w_in=_pad_cols(full["gdn_w_in"][j], 4 * gh * LANES + LANES),
                     conv=full["gdn_conv_w"][j], hp=hp, ng=small["gdn_norm_g"][j][None], w_out=full["gdn_w_out"][j])
        else:
            bf = jnp.zeros((1, LANES), F32).at[0, :fh].set(small["fox_b_f"][j])
            w.update(kind="fox", H=fh, dh=dh, w_in=_pad_cols(full["fox_w_in"][j], 4 * fh * dh + LANES), bf=bf,
                     qg=jnp.tile(small["fox_q_norm_g"][j], 2)[None], kg=jnp.tile(small["fox_k_norm_g"][j], 2)[None],
                     w_out=full["fox_w_out"][j])
        layers.append(w)
    return layers


def _local_step(x, p, target, layers):
    depth = len(layers)
    alpha = (2 * depth) ** 0.25
    saved = []
    for i, w in enumerate(layers):
        tag = f"l{i}"
        if w["kind"] == "gdn":
            y, res = _gdn_layer_fwd(x, w, tag)
        else:
            y, res = _fox_layer_fwd(x, w, tag)
        x1 = _ln_fwd(x, y, w["ln_g"], w["ln_b"], alpha, name=f"{tag}_ln")
        gate_pre = _matmul(x1, w["w_gate"], "nn", name=f"{tag}_gate_mm")
        pp = _matmul(p[i], w["w_proj"], "nn", name=f"{tag}_proj_mm")
        x2 = _ple_fwd(x1, gate_pre, pp, name=f"{tag}_ple")
        saved.append((res, x, y, x1, gate_pre, pp))
        x = x2
    dx, loss_tile = _loss_head(x, target, name="loss_head")
    grads = [None] * depth
    for i in reversed(range(depth)):
        w = layers[i]
        tag = f"l{i}"
        res, xin, y, x1, gate_pre, pp = saved[i]
        dgp, dpp = _ple_bwd(dx, gate_pre, pp, name=f"{tag}_ple_bwd")
        dx1 = _add(dx, _matmul(dgp, w["w_gate"], "nt", name=f"{tag}_gate_dx"), name=f"{tag}_dx1")
        dw_gate = _matmul(x1, dgp, "tn", name=f"{tag}_gate_dw")
        dw_proj = _matmul(p[i], dpp, "tn", name=f"{tag}_proj_dw")
        dy, dg, db = _ln_bwd(xin, y, w["ln_g"], w["ln_b"], dx1, alpha, name=f"{tag}_ln_bwd")
        if w["kind"] == "gdn":
            dxm, g = _gdn_layer_bwd(dy, res, w, tag)
        else:
            dxm, g = _fox_layer_bwd(dy, res, w, tag)
        dx = _axpy(alpha, dy, dxm, name=f"{tag}_dx")
        g.update({"w_gate": dw_gate, "w_proj": dw_proj, "ln_g": dg[0], "ln_b": db[0]})
        grads[i] = g
    return loss_tile, dx, grads


MESH_ID = pl.DeviceIdType.MESH
HBM_SPEC = pl.BlockSpec(memory_space=pl.ANY)
PACK_COLS = 1024
PACK_ROWS = 256


def _all_gather(shards, name):
    nt = len(shards)

    def body(*refs):
        x_refs, out_refs = refs[:nt], refs[nt:2 * nt]
        send_sems, recv_sems, local_sems = refs[2 * nt:]
        x, y, c = lax.axis_index("x"), lax.axis_index("y"), lax.axis_index("c")
        me, sibling = (x, y, c), (x, y, 1 - c)
        chips = [(1 - x, y), (x, 1 - y), (1 - x, 1 - y)]

        def slot(t, px, py, pc):
            return out_refs[t].at[4 * px + 2 * py + pc]

        def copy(k, t, block, to, src=None):
            return pltpu.make_async_remote_copy(
                src_ref=slot(t, *block) if src is None else src, dst_ref=slot(t, *block),
                send_sem=send_sems.at[k, t], recv_sem=recv_sems.at[k, t], device_id=to, device_id_type=MESH_ID)

        every = range(nt)
        mine = [pltpu.make_async_copy(x_refs[t], slot(t, *me), local_sems.at[t]) for t in every]
        for cp in mine:
            cp.start()
        first = [copy(0, t, me, sibling, src=x_refs[t]) for t in every]
        first += [copy(1 + j, t, me, (*chip, c), src=x_refs[t]) for j, chip in enumerate(chips) for t in every]
        for cp in first:
            cp.start()
        passed = []
        for j, chip in enumerate(chips):
            for t in every:
                copy(1 + j, t, (*chip, c), me).wait_recv()
                passed.append(copy(4 + j, t, (*chip, c), sibling))
                passed[-1].start()
        for t in every:
            copy(0, t, sibling, me).wait_recv()
        for j, chip in enumerate(chips):
            for t in every:
                copy(4 + j, t, (*chip, 1 - c), me).wait_recv()
        for cp in first + passed:
            cp.wait_send()
        for cp in mine:
            cp.wait()

    return pl.pallas_call(
        body,
        out_shape=[jax.ShapeDtypeStruct((N_DEV, *s.shape), s.dtype) for s in shards],
        in_specs=[HBM_SPEC] * nt,
        out_specs=[HBM_SPEC] * nt,
        scratch_shapes=[pltpu.SemaphoreType.DMA((7, nt)), pltpu.SemaphoreType.DMA((7, nt)), pltpu.SemaphoreType.DMA((nt,))],
        name=name,
    )(*shards)


def _all_to_all(slabs, name):
    nt = len(slabs)

    def body(*refs):
        g_refs, out_refs = refs[:nt], refs[nt:2 * nt]
        send_sems, recv_sems, local_sems = refs[2 * nt:]
        x, y, c = lax.axis_index("x"), lax.axis_index("y"), lax.axis_index("c")
        me = 4 * x + 2 * y + c
        mine = [pltpu.make_async_copy(g_refs[t].at[me], out_refs[t].at[me], local_sems.at[t]) for t in range(nt)]
        for cp in mine:
            cp.start()
        copies = []
        for k in range(1, N_DEV):
            px = 1 - x if k & 4 else x
            py = 1 - y if k & 2 else y
            pc = 1 - c if k & 1 else c
            peer = 4 * px + 2 * py + pc
            for t in range(nt):
                copies.append(tuple(
                    pltpu.make_async_remote_copy(src_ref=g_refs[t].at[peer], dst_ref=out_refs[t].at[dst],
                                                 send_sem=send_sems.at[k - 1, t], recv_sem=recv_sems.at[k - 1, t],
                                                 device_id=(px, py, pc), device_id_type=MESH_ID)
                    for dst in (me, peer)))
        for send, _ in copies:
            send.start()
        for send, arrive in copies:
            arrive.wait_recv()
            send.wait_send()
        for cp in mine:
            cp.wait()

    return pl.pallas_call(
        body,
        out_shape=[jax.ShapeDtypeStruct(s.shape, s.dtype) for s in slabs],
        in_specs=[HBM_SPEC] * nt,
        out_specs=[HBM_SPEC] * nt,
        scratch_shapes=[pltpu.SemaphoreType.DMA((7, nt)), pltpu.SemaphoreType.DMA((7, nt)), pltpu.SemaphoreType.DMA((nt,))],
        name=name,
    )(*slabs)


def _pack(flats, dtype):
    flat = jnp.concatenate([f.astype(dtype).reshape(-1) for f in flats])
    unit = PACK_ROWS * PACK_COLS
    n = -(-flat.shape[0] // unit) * unit
    return jnp.pad(flat, (0, n - flat.shape[0])).reshape(n // PACK_COLS, PACK_COLS)


def _unpack(buf, shapes):
    lead = buf.shape[:-2]
    flat = buf.reshape(*lead, -1)
    out, off = [], 0
    for s in shapes:
        n = math.prod(s)
        out.append(flat[..., off:off + n].reshape(*lead, *s))
        off += n
    return out


_ROW_SPLIT = ("ple_w_gate", "gdn_w_out", "fox_w_out")
_COL_SPLIT = ("ple_w_proj", "gdn_w_in", "gdn_conv_w", "fox_w_in")
_SHARDED = ("ple_w_gate", "ple_w_proj", "gdn_w_in", "gdn_conv_w", "gdn_w_out", "fox_w_in", "fox_w_out")
_REPLICATED = ("ln_g", "ln_b", "gdn_a_log", "gdn_dt_bias", "gdn_norm_g", "fox_b_f", "fox_q_norm_g", "fox_k_norm_g")
_WEIGHTS = ("ln_g", "ln_b", "ple_w_gate", "ple_w_proj", "gdn_w_in", "gdn_conv_w", "gdn_a_log", "gdn_dt_bias",
            "gdn_norm_g", "gdn_w_out", "fox_w_in", "fox_b_f", "fox_q_norm_g", "fox_k_norm_g", "fox_w_out")


def _join(name, gathered):
    n, l, a, b = gathered.shape
    if name in _ROW_SPLIT:
        return gathered.transpose(1, 0, 2, 3).reshape(l, n * a, b)
    return gathered.transpose(1, 2, 0, 3).reshape(l, a, n * b)


def _split(name, full):
    l, a, b = full.shape
    if name in _ROW_SPLIT:
        return full.reshape(l, N_DEV, a // N_DEV, b).transpose(1, 0, 2, 3)
    return full.reshape(l, a, N_DEV, b // N_DEV).transpose(2, 0, 1, 3)


def _adamw(w, g_parts, m, v, name):
    shape = w.shape
    R, C = math.prod(shape[:-1]), shape[-1]
    tr = _pick(R, 256, SUBLANES)
    c1 = 1.0 - ADAM_B1 ** ADAM_STEP
    c2 = 1.0 - ADAM_B2 ** ADAM_STEP

    def body(w_ref, g_ref, m_ref, v_ref, go_ref, d_ref, mo_ref, vo_ref):
        gv = g_ref[0]
        for s in range(1, N_DEV):
            gv = gv + g_ref[s]
        mn = ADAM_B1 * m_ref[...] + (1.0 - ADAM_B1) * gv
        vn = ADAM_B2 * v_ref[...] + (1.0 - ADAM_B2) * jnp.square(gv)
        go_ref[...] = gv
        d_ref[...] = -ADAM_LR * ((mn / c1) / (jnp.sqrt(vn / c2) + ADAM_EPS) + ADAM_WD * w_ref[...])
        mo_ref[...] = mn
        vo_ref[...] = vn

    row = pl.BlockSpec((tr, C), lambda i: (i, 0))
    outs = pl.pallas_call(
        body,
        grid=(R // tr,),
        in_specs=[row, pl.BlockSpec((N_DEV, tr, C), lambda i: (0, i, 0)), row, row],
        out_specs=[row] * 4,
        out_shape=[jax.ShapeDtypeStruct((R, C), F32)] * 4,
        compiler_params=_params(("parallel",)),
        name=name,
    )(w.reshape(R, C), g_parts.reshape(N_DEV, R, C), m.reshape(R, C), v.reshape(R, C))
    return [o.reshape(shape) for o in outs]


def _train_step(x, p, target, w, m, v):
    shards = [w[n] if n == "gdn_conv_w" else w[n].astype(MXU_DTYPE) for n in _SHARDED]
    gathered = _all_gather(shards, name="gather_weights")
    full = {n: _join(n, part) for n, part in zip(_SHARDED, gathered)}
    layers = _build_layers(full, {n: w[n] for n in _REPLICATED})

    loss_tile, dx, grads = _local_step(x[0], p[:, 0], target[0], layers)
    loss = lax.psum(loss_tile[0, 0], ("x", "y", "c"))

    depth = len(layers)
    gdn_l = [i for i in range(depth) if i % 2 == 0]
    fox_l = [i for i in range(depth) if i % 2 == 1]

    def stack(key, idx):
        return jnp.stack([grads[i][key] for i in idx])

    full_g = {
        "ple_w_gate": stack("w_gate", range(depth)), "ple_w_proj": stack("w_proj", range(depth)),
        "gdn_w_in": stack("w_in", gdn_l)[..., :w["gdn_w_in"].shape[-1] * N_DEV], "gdn_conv_w": stack("conv", gdn_l),
        "gdn_w_out": stack("w_out", gdn_l),
        "fox_w_in": stack("w_in", fox_l)[..., :w["fox_w_in"].shape[-1] * N_DEV], "fox_w_out": stack("w_out", fox_l)}
    small_g = {
        "ln_g": stack("ln_g", range(depth)), "ln_b": stack("ln_b", range(depth)),
        "gdn_a_log": stack("a_log", gdn_l), "gdn_dt_bias": stack("dt_bias", gdn_l), "gdn_norm_g": stack("norm_g", gdn_l),
        "fox_b_f": stack("b_f", fox_l), "fox_q_norm_g": stack("q_norm_g", fox_l), "fox_k_norm_g": stack("k_norm_g", fox_l)}

    g_parts = dict(zip(_SHARDED, _all_to_all([_split(n, full_g[n]) for n in _SHARDED], name="scatter_grads")))
    small_all = _all_gather([_pack([small_g[n] for n in _REPLICATED], F32)], name="gather_small_grads")[0]
    g_parts.update(zip(_REPLICATED, _unpack(small_all, [w[n].shape for n in _REPLICATED])))

    g, delta, new_m, new_v = {}, {}, {}, {}
    for n in _WEIGHTS:
        g[n], delta[n], new_m[n], new_v[n] = _adamw(w[n], g_parts[n], m[n], v[n], name=f"adamw_{n}")
    return (loss, dx[None], *[g[n] for n in _WEIGHTS], *[delta[n] for n in _WEIGHTS],
            *[new_m[n] for n in _WEIGHTS], *[new_v[n] for n in _WEIGHTS])


def kernel(x, p, ln_g, ln_b, ple_w_gate, ple_w_proj, gdn_w_in, gdn_conv_w, gdn_a_log, gdn_dt_bias, gdn_norm_g, gdn_w_out, fox_w_in, fox_b_f, fox_q_norm_g, fox_k_norm_g, fox_w_out, loss_target, m_ln_g, m_ln_b, m_ple_w_gate, m_ple_w_proj, m_gdn_w_in, m_gdn_conv_w, m_gdn_a_log, m_gdn_dt_bias, m_gdn_norm_g, m_gdn_w_out, m_fox_w_in, m_fox_b_f, m_fox_q_norm_g, m_fox_k_norm_g, m_fox_w_out, v_ln_g, v_ln_b, v_ple_w_gate, v_ple_w_proj, v_gdn_w_in, v_gdn_conv_w, v_gdn_a_log, v_gdn_dt_bias, v_gdn_norm_g, v_gdn_w_out, v_fox_w_in, v_fox_b_f, v_fox_q_norm_g, v_fox_k_norm_g, v_fox_w_out):
    given = dict(locals())
    w = {n: given[n] for n in _WEIGHTS}
    m = {n: given["m_" + n] for n in _WEIGHTS}
    v = {n: given["v_" + n] for n in _WEIGHTS}
    return _train_step(x, p, loss_target, w, m, v)
```

```python
import functools
import math

import jax
import jax.numpy as jnp
from jax import lax
from jax.experimental import pallas as pl
from jax.experimental.pallas import tpu as pltpu

F32 = jnp.float32
BF16 = jnp.bfloat16
MXU_DTYPE = BF16
HI = lax.Precision.HIGHEST

N_DEV = 8
LANES = 128
SUBLANES = 8
VMEM_BYTES = 64 * 1024 * 1024

GDN_CHUNK = 64
GDN_CONV = 4
LN_EPS = 1e-5
RMS_EPS = 1e-6
NEG = -1e30

ADAM_LR = 0.001
ADAM_B1 = 0.9
ADAM_B2 = 0.999
ADAM_EPS = 1e-08
ADAM_WD = 0.01
ADAM_STEP = 10


def _params(semantics, vmem_mb=40):
    return pltpu.CompilerParams(dimension_semantics=semantics, vmem_limit_bytes=vmem_mb * 1024 * 1024)


def _pick(dim, cap, unit=LANES):
    if dim <= cap:
        return dim
    best = None
    for t in range(unit, cap + 1, unit):
        if dim % t == 0:
            best = t
    assert best is not None, (dim, cap)
    return best


def _dims(dims, ndim):
    if ndim == 2:
        return (dims, ((), ()))
    return (((dims[0][0] + 1,), (dims[1][0] + 1,)), ((0,), (0,)))


def _dot(a, b, dims):
    return lax.dot_general(a.astype(MXU_DTYPE), b.astype(MXU_DTYPE), _dims(dims, a.ndim), preferred_element_type=F32)


_NN = ((1,), (0,))
_NT = ((1,), (1,))
_TN = ((0,), (0,))


@jax.custom_vjp
def _mm_nn(a, b):
    return _dot(a, b, _NN)


@jax.custom_vjp
def _mm_nt(a, b):
    return _dot(a, b, _NT)


@jax.custom_vjp
def _mm_tn(a, b):
    return _dot(a, b, _TN)


_mm_nn.defvjp(lambda a, b: (_dot(a, b, _NN), (a, b)), lambda r, g: (_mm_nt(g, r[1]), _mm_tn(r[0], g)))
_mm_nt.defvjp(lambda a, b: (_dot(a, b, _NT), (a, b)), lambda r, g: (_mm_nn(g, r[1]), _mm_tn(g, r[0])))
_mm_tn.defvjp(lambda a, b: (_dot(a, b, _TN), (a, b)), lambda r, g: (_mm_nt(r[1], g), _mm_nn(r[0], g)))


def _mm_hi(a, b, precision=HI):
    return lax.dot_general(a, b, _dims(_NN, a.ndim), precision=precision, preferred_element_type=F32)


def _mm_3x(a, b):
    return _mm_hi(a, b, lax.Precision.HIGH)


def _sigmoid(x):
    return 1.0 / (1.0 + jnp.exp(-x))


def _silu(x):
    return x * _sigmoid(x)


def _softplus(x):
    return jnp.maximum(x, 0.0) + jnp.log(1.0 + jnp.exp(-jnp.abs(x)))


def _iota2(shape, dim):
    return lax.broadcasted_iota(jnp.int32, shape, dim)


def _lane_pick(tile, lane):
    return jnp.sum(jnp.where(_iota2(tile.shape, 1) == lane, tile, 0.0), axis=1, keepdims=True)


def _lane_put(col, lane, width=LANES):
    return jnp.where(_iota2((col.shape[0], width), 1) == lane, col, 0.0)


def _matmul(a, b, mode, out_dtype=F32, *, name, tm=1024, tn=1408, tk=1408, a_cols=None, b_cols=None):
    def cols(arr, rng):
        return (0, arr.shape[1]) if rng is None else rng

    a0, an = cols(a, a_cols)
    b0, bn = cols(b, b_cols)
    if mode == "nn":
        M, K, N = a.shape[0], an, bn
        assert b.shape[0] == K
    elif mode == "nt":
        M, K, N = a.shape[0], an, b.shape[0]
        assert bn == K
    else:
        K, M, N = a.shape[0], an, bn
        assert b.shape[0] == K
    tm, tn, tk = _pick(M, tm), _pick(N, tn), _pick(K, tk)
    nk = K // tk
    if mode == "nn":
        assert a0 % tk == 0 and b0 % tn == 0
        a_spec = pl.BlockSpec((tm, tk), lambda i, j, k: (i, a0 // tk + k))
        b_spec = pl.BlockSpec((tk, tn), lambda i, j, k: (k, b0 // tn + j))
        dims = _NN
    elif mode == "nt":
        assert a0 % tk == 0 and b0 % tk == 0
        a_spec = pl.BlockSpec((tm, tk), lambda i, j, k: (i, a0 // tk + k))
        b_spec = pl.BlockSpec((tn, tk), lambda i, j, k: (j, b0 // tk + k))
        dims = _NT
    else:
        assert a0 % tm == 0 and b0 % tn == 0
        a_spec = pl.BlockSpec((tk, tm), lambda i, j, k: (k, a0 // tm + i))
        b_spec = pl.BlockSpec((tk, tn), lambda i, j, k: (k, b0 // tn + j))
        dims = _TN

    def body(a_ref, b_ref, o_ref, acc_ref):
        k = pl.program_id(2)

        @pl.when(k == 0)
        def _():
            acc_ref[...] = jnp.zeros_like(acc_ref)

        acc_ref[...] += _dot(a_ref[...], b_ref[...], dims)

        @pl.when(k == nk - 1)
        def _():
            o_ref[...] = acc_ref[...].astype(o_ref.dtype)

    return pl.pallas_call(
        body,
        grid=(M // tm, N // tn, nk),
        in_specs=[a_spec, b_spec],
        out_specs=pl.BlockSpec((tm, tn), lambda i, j, k: (i, j)),
        out_shape=jax.ShapeDtypeStruct((M, N), out_dtype),
        scratch_shapes=[pltpu.VMEM((tm, tn), F32)],
        compiler_params=_params(("parallel", "parallel", "arbitrary"), 48),
        name=name,
    )(a, b)


def _rowwise(fn, rows, consts, out_rows, out_accs, *, tile, name, reverse=False, carries=(), vmem_mb=40):
    rows = [r if isinstance(r, tuple) else (r, r.shape[1], 0) for r in rows]
    S = rows[0][0].shape[0]
    tile = _pick(S, tile, SUBLANES)
    nt = S // tile
    nr, nc, no, na = len(rows), len(consts), len(out_rows), len(out_accs)

    def ridx(i):
        return nt - 1 - i if reverse else i

    in_specs = [pl.BlockSpec((tile, w), functools.partial(lambda i, cb: (ridx(i), cb), cb=cb)) for _, w, cb in rows]
    in_specs += [pl.BlockSpec(c.shape, functools.partial(lambda i, nd: (0,) * nd, nd=c.ndim)) for c in consts]
    out_specs = [pl.BlockSpec((tile, c), lambda i: (ridx(i), 0)) for c, _ in out_rows]
    out_specs += [pl.BlockSpec(s, functools.partial(lambda i, nd: (0,) * nd, nd=len(s))) for s, _ in out_accs]
    out_shape = [jax.ShapeDtypeStruct((S, c), d) for c, d in out_rows]
    out_shape += [jax.ShapeDtypeStruct(s, d) for s, d in out_accs]

    def body(*refs):
        rin, cin = refs[:nr], refs[nr:nr + nc]
        rout, aout = refs[nr + nc:nr + nc + no], refs[nr + nc + no:nr + nc + no + na]
        carr = refs[nr + nc + no + na:]
        step = pl.program_id(0)

        @pl.when(step == 0)
        def _():
            for r in aout + carr:
                r[...] = jnp.zeros_like(r)

        outs, accs, newc = fn([r[...] for r in rin], [c[...] for c in cin], [c[...] for c in carr])
        for r, o in zip(rout, outs, strict=True):
            r[...] = o.astype(r.dtype)
        for r, v in zip(aout, accs, strict=True):
            r[...] += v
        for r, v in zip(carr, newc, strict=True):
            r[...] = v

    res = pl.pallas_call(
        body,
        grid=(nt,),
        in_specs=in_specs,
        out_specs=out_specs,
        out_shape=out_shape,
        scratch_shapes=[pltpu.VMEM(s, F32) for s in carries],
        compiler_params=_params(("arbitrary",), vmem_mb),
        name=name,
    )(*[r[0] for r in rows], *consts)
    return res


def _ln_fn(x, y, g, b, alpha):
    r = alpha * x + y
    mu = jnp.mean(r, -1, keepdims=True)
    var = jnp.mean(jnp.square(r - mu), -1, keepdims=True)
    return (r - mu) * lax.rsqrt(var + LN_EPS) * g + b


def _ln_fwd(x, y, g, b, alpha, name):
    D = x.shape[1]

    def fn(rows, consts, _):
        return [_ln_fn(rows[0], rows[1], consts[0], consts[1], alpha)], [], []

    return _rowwise(fn, [x, y], [g, b], [(D, F32)], [], tile=256, name=name)[0]


def _ln_bwd(x, y, g, b, dx1, alpha, name):
    D = x.shape[1]

    def fn(rows, consts, _):
        xv, yv, d = rows
        _, vjp = jax.vjp(lambda yy, gg, bb: _ln_fn(xv, yy, gg, bb, alpha), yv, consts[0], consts[1])
        dy, dg, db = vjp(d)
        return [dy], [dg, db], []

    return _rowwise(fn, [x, y, dx1], [g, b], [(D, F32)], [((1, D), F32), ((1, D), F32)], tile=256, name=name)


def _ple_fwd(x1, gate_pre, pp, name):
    D = x1.shape[1]

    def fn(rows, _, __):
        return [rows[0] + _sigmoid(rows[1]) * rows[2]], [], []

    return _rowwise(fn, [x1, gate_pre, pp], [], [(D, F32)], [], tile=256, name=name)[0]


def _ple_bwd(dx2, gate_pre, pp, name):
    D = dx2.shape[1]

    def fn(rows, _, __):
        d, gp, ppv = rows
        s = _sigmoid(gp)
        return [d * ppv * s * (1.0 - s), d * s], [], []

    return _rowwise(fn, [dx2, gate_pre, pp], [], [(D, F32), (D, F32)], [], tile=256, name=name)


def _add(a, b, name):
    def fn(rows, _, __):
        return [rows[0] + rows[1]], [], []

    return _rowwise(fn, [a, b], [], [(a.shape[1], F32)], [], tile=256, name=name)[0]


def _axpy(alpha, a, b, name):
    def fn(rows, _, __):
        return [alpha * rows[0] + rows[1]], [], []

    return _rowwise(fn, [a, b], [], [(a.shape[1], F32)], [], tile=256, name=name)[0]


def _loss_head(y, target, name):
    D = y.shape[1]

    def fn(rows, _, __):
        e = rows[0] - rows[1]
        part = 0.5 * jnp.sum(jnp.sum(e * e, axis=1, keepdims=True), axis=0, keepdims=True) / D
        return [e / D], [jnp.broadcast_to(part, (SUBLANES, LANES))], []

    return _rowwise(fn, [y, target], [], [(D, F32)], [((SUBLANES, LANES), F32)], tile=256, name=name)


def _conv_fwd(h, w, n_cols, name):
    S = h.shape[0]
    T = _pick(S, 512, SUBLANES)
    CB = _pick(n_cols, 512)
    nt = S // T
    K = GDN_CONV

    def body(x_ref, halo_ref, w_ref, o_ref, buf):
        i = pl.program_id(1)
        buf[0:SUBLANES, :] = jnp.where(i > 0, halo_ref[...], 0.0)
        buf[SUBLANES:, :] = x_ref[...]
        acc = jnp.zeros((T, CB), F32)
        for k in range(K):
            acc = acc + w_ref[k:k + 1, :] * buf[pl.ds(SUBLANES - (K - 1) + k, T), :]
        o_ref[...] = acc

    return pl.pallas_call(
        body,
        grid=(n_cols // CB, nt),
        in_specs=[pl.BlockSpec((T, CB), lambda c, i: (i, c)),
                  pl.BlockSpec((SUBLANES, CB), lambda c, i: (jnp.maximum(i * (T // SUBLANES) - 1, 0), c)),
                  pl.BlockSpec((K, CB), lambda c, i: (0, c))],
        out_specs=pl.BlockSpec((T, CB), lambda c, i: (i, c)),
        out_shape=jax.ShapeDtypeStruct((S, n_cols), F32),
        scratch_shapes=[pltpu.VMEM((T + SUBLANES, CB), F32)],
        compiler_params=_params(("parallel", "parallel")),
        name=name,
    )(h, h, w)


def _conv_bwd(dc, h, w, h_col0, name):
    S, n_cols = dc.shape
    T = _pick(S, 512, SUBLANES)
    CB = _pick(n_cols, 512)
    nt = S // T
    K = GDN_CONV
    assert h_col0 % CB == 0
    hb = h_col0 // CB

    def body(d_ref, halo_ref, x_ref, w_ref, dx_ref, dw_ref, buf):
        i = pl.program_id(1)

        @pl.when(i == 0)
        def _():
            dw_ref[...] = jnp.zeros_like(dw_ref)

        buf[0:T, :] = d_ref[...]
        buf[T:, :] = jnp.where(i < nt - 1, halo_ref[...], 0.0)
        x = x_ref[...]
        acc = jnp.zeros((T, CB), F32)
        for k in range(K):
            shifted = buf[pl.ds(K - 1 - k, T), :]
            acc = acc + w_ref[k:k + 1, :] * shifted
            dw_ref[k:k + 1, :] += jnp.sum(shifted * x, axis=0, keepdims=True)
        dx_ref[...] = acc

    last = S // SUBLANES - 1
    return pl.pallas_call(
        body,
        grid=(n_cols // CB, nt),
        in_specs=[pl.BlockSpec((T, CB), lambda c, i: (i, c)),
                  pl.BlockSpec((SUBLANES, CB), lambda c, i: (jnp.minimum((i + 1) * (T // SUBLANES), last), c)),
                  pl.BlockSpec((T, CB), lambda c, i: (i, hb + c)),
                  pl.BlockSpec((K, CB), lambda c, i: (0, c))],
        out_specs=[pl.BlockSpec((T, CB), lambda c, i: (i, c)),
                   pl.BlockSpec((SUBLANES, CB), lambda c, i: (0, c))],
        out_shape=[jax.ShapeDtypeStruct((S, n_cols), F32), jax.ShapeDtypeStruct((SUBLANES, n_cols), F32)],
        scratch_shapes=[pltpu.VMEM((T + SUBLANES, CB), F32)],
        compiler_params=_params(("parallel", "arbitrary")),
        name=name,
    )(dc, dc, h, w)


def _neumann_inverse(L):
    C = L.shape[-1]
    eye = (_iota2((C, C), 0) == _iota2((C, C), 1)).astype(F32)
    X = eye - L
    P = L
    for _ in range(max(0, math.ceil(math.log2(C)) - 1)):
        P = _mm_3x(P, P)
        X = _mm_3x(X, eye + P)
    return X


@jax.custom_vjp
def _unit_lower_inverse(L):
    return _neumann_inverse(L)


def _unit_lower_inverse_bwd(T, dT):
    Tt = jnp.swapaxes(T, -1, -2)
    return (-_mm_3x(_mm_3x(Tt, dT), Tt),)


_unit_lower_inverse.defvjp(lambda L: (_neumann_inverse(L),) * 2, _unit_lower_inverse_bwd)


@jax.custom_vjp
def _known_inverse(L, T):
    return T


_known_inverse.defvjp(lambda L, T: (T, T), lambda T, dT: (*_unit_lower_inverse_bwd(T, dT), jnp.zeros_like(T)))


def _gdn_chunk(cq, ck, cv, zz, bcol, acol, alog, dtb, ng, state, inverse=None):
    G, C, dk = cq.shape
    q = _silu(cq)
    k = _silu(ck)
    v = _silu(cv)
    q = q * lax.rsqrt(jnp.sum(q * q, -1, keepdims=True) + RMS_EPS) * (dk ** -0.5)
    k = k * lax.rsqrt(jnp.sum(k * k, -1, keepdims=True) + RMS_EPS)
    beta = _sigmoid(bcol)
    g = -jnp.exp(alog) * _softplus(acol + dtb)

    row, col = _iota2((C, C), 0), _iota2((C, C), 1)
    causal, strict, eye = row >= col, row > col, row == col
    g_rows = jnp.swapaxes(jnp.broadcast_to(g, (G, C, C)), -1, -2)
    gc = jnp.sum(jnp.where(causal, g_rows, 0.0), axis=-1, keepdims=True)
    gcb = jnp.broadcast_to(gc, (G, C, C))
    g_last = jnp.sum(jnp.sum(jnp.where((row == C - 1) & (col == 0), gcb, 0.0), axis=-1, keepdims=True), axis=-2, keepdims=True)
    gc_rows = jnp.swapaxes(gcb, -1, -2)
    decay = jnp.exp(jnp.where(causal, gcb - gc_rows, NEG))

    kb = k * beta
    L = jnp.where(strict, _mm_nt(kb, k) * decay, 0.0)
    T = _unit_lower_inverse(L) if inverse is None else _known_inverse(L, inverse)
    u = _mm_3x(T, v * beta)
    w = _mm_3x(T, kb * jnp.exp(gc))
    a_qk = jnp.where(causal, _mm_nt(q, k) * decay, 0.0)
    q_dec = q * jnp.exp(gc)
    k_dec = k * jnp.exp(g_last - gc)
    v_new = u - _mm_nn(w, state)
    o = _mm_nn(q_dec, state) + _mm_nn(a_qk, v_new)
    new_state = state * jnp.exp(g_last) + _mm_tn(k_dec, v_new)
    y = o * lax.rsqrt(jnp.mean(o * o, -1, keepdims=True) + RMS_EPS) * ng * _silu(zz)
    return (y, new_state), T


GDN_HEADS_PER_STEP = 8


def _gdn_specs(H, dk, G, chunk_of=lambda n: n):
    C = GDN_CHUNK
    NG = H // G
    cq = pl.BlockSpec((C, G * dk), lambda n, h: (chunk_of(n), h))
    ck = pl.BlockSpec((C, G * dk), lambda n, h: (chunk_of(n), NG + h))
    cv = pl.BlockSpec((C, G * dk), lambda n, h: (chunk_of(n), 2 * NG + h))
    zz = pl.BlockSpec((C, G * dk), lambda n, h: (chunk_of(n), 3 * NG + h))
    ba = pl.BlockSpec((C, LANES), lambda n, h: (chunk_of(n), 4 * H * dk // LANES))
    return cq, ck, cv, zz, ba


def _gdn_step_args(cq_ref, ck_ref, cv_ref, z_ref, ba_ref, hp_ref, hg, G, H, dk):
    ba = ba_ref[...]

    def heads(ref):
        return jnp.stack([ref[:, g * dk:(g + 1) * dk] for g in range(G)])

    def picks(tile, offset):
        return jnp.stack([_lane_pick(tile, offset + hg * G + g) for g in range(G)])

    return (heads(cq_ref), heads(ck_ref), heads(cv_ref), heads(z_ref), picks(ba, 0), picks(ba, H),
            picks(hp_ref[0:1, :], 0), picks(hp_ref[1:2, :], 0))


def _gdn_scan_fwd(c, h, hp, ng, H, name):
    S = c.shape[0]
    dk = c.shape[1] // (3 * H)
    assert dk == LANES
    C = GDN_CHUNK
    NC = S // C

    G = min(GDN_HEADS_PER_STEP, H)
    assert H % G == 0
    NG = H // G

    def body(cq_ref, ck_ref, cv_ref, z_ref, ba_ref, hp_ref, ng_ref, y_ref, s_ref, t_ref, state):
        n, hg = pl.program_id(0), pl.program_id(1)
        heads = pl.ds(hg * G, G)

        @pl.when(n == 0)
        def _():
            state[heads] = jnp.zeros((G, dk, dk), F32)

        st = state[heads]
        s_ref[0] = st
        (y, new_state), inverse = _gdn_chunk(
            *_gdn_step_args(cq_ref, ck_ref, cv_ref, z_ref, ba_ref, hp_ref, hg, G, H, dk), ng_ref[...], st)
        t_ref[0] = inverse
        for g in range(G):
            y_ref[:, g * dk:(g + 1) * dk] = y[g].astype(y_ref.dtype)
        state[heads] = new_state

    cq, ck, cv, zz, ba = _gdn_specs(H, dk, G)
    return pl.pallas_call(
        body,
        grid=(NC, NG),
        in_specs=[cq, ck, cv, zz, ba, pl.BlockSpec((SUBLANES, LANES), lambda n, h: (0, 0)),
                  pl.BlockSpec((1, dk), lambda n, h: (0, 0))],
        out_specs=[pl.BlockSpec((C, G * dk), lambda n, h: (n, h)),
                   pl.BlockSpec((1, G, dk, dk), lambda n, h: (n, h, 0, 0)),
                   pl.BlockSpec((1, G, C, C), lambda n, h: (n, h, 0, 0))],
        out_shape=[jax.ShapeDtypeStruct((S, H * dk), MXU_DTYPE), jax.ShapeDtypeStruct((NC, H, dk, dk), F32),
                   jax.ShapeDtypeStruct((NC, H, C, C), F32)],
        scratch_shapes=[pltpu.VMEM((H, dk, dk), F32)],
        compiler_params=_params(("arbitrary", "arbitrary")),
        name=name,
    )(c, c, c, h, h, hp, ng)


def _gdn_scan_bwd(c, h, hp, ng, states, inverses, dy, H, name):
    S = c.shape[0]
    dk = c.shape[1] // (3 * H)
    C = GDN_CHUNK
    NC = S // C

    G = min(GDN_HEADS_PER_STEP, H)
    NG = H // G

    def body(cq_ref, ck_ref, cv_ref, z_ref, ba_ref, hp_ref, ng_ref, s_ref, t_ref, dy_ref,
             dq_ref, dk_ref, dv_ref, dz_ref, dba_ref, dhp_ref, dng_ref, dstate):
        n, hg = pl.program_id(0), pl.program_id(1)

        @pl.when((n == 0) & (hg == 0))
        def _():
            dhp_ref[...] = jnp.zeros_like(dhp_ref)
            dng_ref[...] = jnp.zeros_like(dng_ref)

        heads = pl.ds(hg * G, G)

        @pl.when(n == 0)
        def _():
            dstate[heads] = jnp.zeros((G, dk, dk), F32)

        args = (*_gdn_step_args(cq_ref, ck_ref, cv_ref, z_ref, ba_ref, hp_ref, hg, G, H, dk), ng_ref[...], s_ref[0])
        inverse = t_ref[0]
        _, vjp, _ = jax.vjp(lambda *a: _gdn_chunk(*a, inverse=inverse), *args, has_aux=True)
        dy = jnp.stack([dy_ref[:, g * dk:(g + 1) * dk] for g in range(G)])
        dcq, dck, dcv, dzz, dbc, dac, dal, ddt, dng, dst = vjp((dy, dstate[heads]))
        dstate[heads] = dst
        dba = jnp.zeros((C, LANES), F32)
        dhp0 = jnp.zeros((1, LANES), F32)
        dhp1 = jnp.zeros((1, LANES), F32)
        for g in range(G):
            hd = hg * G + g
            sl = slice(g * dk, (g + 1) * dk)
            dq_ref[:, sl] = dcq[g]
            dk_ref[:, sl] = dck[g]
            dv_ref[:, sl] = dcv[g]
            dz_ref[:, sl] = dzz[g]
            dba = dba + _lane_put(dbc[g], hd) + _lane_put(dac[g], H + hd)
            dhp0 = dhp0 + _lane_put(dal[g], hd)
            dhp1 = dhp1 + _lane_put(ddt[g], hd)

        @pl.when(hg == 0)
        def _():
            dba_ref[...] = dba

        @pl.when(hg > 0)
        def _():
            dba_ref[...] += dba

        dhp_ref[0:1, :] += dhp0
        dhp_ref[1:2, :] += dhp1
        dng_ref[...] += dng

    rev = lambda n: NC - 1 - n
    blk = pl.BlockSpec
    in_specs = [*_gdn_specs(H, dk, G, rev),
                blk((SUBLANES, LANES), lambda n, h: (0, 0)), blk((1, dk), lambda n, h: (0, 0)),
                blk((1, G, dk, dk), lambda n, h: (rev(n), h, 0, 0)), blk((1, G, C, C), lambda n, h: (rev(n), h, 0, 0)),
                blk((C, G * dk), lambda n, h: (rev(n), h))]
    out_specs = [blk((C, G * dk), lambda n, h: (rev(n), h))] * 4 + [
        blk((C, LANES), lambda n, h: (rev(n), 0)),
        blk((SUBLANES, LANES), lambda n, h: (0, 0)), blk((1, dk), lambda n, h: (0, 0))]
    out_shape = [jax.ShapeDtypeStruct((S, H * dk), F32)] * 4 + [
        jax.ShapeDtypeStruct((S, LANES), F32), jax.ShapeDtypeStruct((SUBLANES, LANES), F32),
        jax.ShapeDtypeStruct((1, dk), F32)]
    return pl.pallas_call(
        body,
        grid=(NC, NG),
        in_specs=in_specs,
        out_specs=out_specs,
        out_shape=out_shape,
        scratch_shapes=[pltpu.VMEM((H, dk, dk), F32)],
        compiler_params=_params(("arbitrary", "arbitrary")),
        name=name,
    )(c, c, c, h, h, hp, ng, states, inverses, dy)


def _pair_rms(x, gain, dh):
    first = _iota2(x.shape, 1) < dh
    sq = x * x
    ss_a = jnp.sum(jnp.where(first, sq, 0.0), axis=1, keepdims=True)
    ss_b = jnp.sum(jnp.where(first, 0.0, sq), axis=1, keepdims=True)
    inv = jnp.where(first, lax.rsqrt(ss_a / dh + RMS_EPS), lax.rsqrt(ss_b / dh + RMS_EPS))
    return x * inv * gain


def _log_sigmoid(x):
    return jnp.minimum(x, 0.0) - jnp.log(1.0 + jnp.exp(-jnp.abs(x)))


def _cum_fn(fr, bf, carry):
    T = fr.shape[0]
    tril = (_iota2((T, T), 0) >= _iota2((T, T), 1)).astype(F32)
    c = _mm_hi(tril, _log_sigmoid(fr + bf)) + carry
    last = jnp.sum(jnp.where(_iota2(c.shape, 0) == T - 1, c, 0.0), axis=0, keepdims=True)
    return c, last


def _fox_prep_fwd(h, bf, qg, kg, H, dh, name):
    S = h.shape[0]
    W = H * dh
    assert 2 * dh == LANES and H <= LANES
    T = _pick(S, 256, LANES)
    nt = S // T
    npair = H // 2

    def body(hq_ref, hk_ref, f_ref, bf_ref, qg_ref, kg_ref, q_ref, k_ref, ccol_ref, crow_ref, carry):
        i = pl.program_id(0)

        @pl.when(i == 0)
        def _():
            carry[...] = jnp.zeros_like(carry)

        for p in range(npair):
            sl = slice(p * LANES, (p + 1) * LANES)
            q_ref[:, sl] = (_pair_rms(hq_ref[:, sl], qg_ref[...], dh) * (dh ** -0.5)).astype(q_ref.dtype)
            k_ref[:, sl] = _pair_rms(hk_ref[:, sl], kg_ref[...], dh).astype(k_ref.dtype)
        c, last = _cum_fn(f_ref[...], bf_ref[...], carry[...])
        carry[...] = last
        ct = c.T
        for hh in range(H):
            ccol_ref[hh] = c[:, hh:hh + 1]
            crow_ref[hh] = ct[hh:hh + 1, :]

    return pl.pallas_call(
        body,
        grid=(nt,),
        in_specs=[pl.BlockSpec((T, W), lambda i: (i, 0)), pl.BlockSpec((T, W), lambda i: (i, 1)),
                  pl.BlockSpec((T, LANES), lambda i: (i, 4 * W // LANES)),
                  pl.BlockSpec((1, LANES), lambda i: (0, 0)), pl.BlockSpec((1, LANES), lambda i: (0, 0)),
                  pl.BlockSpec((1, LANES), lambda i: (0, 0))],
        out_specs=[pl.BlockSpec((T, W), lambda i: (i, 0)), pl.BlockSpec((T, W), lambda i: (i, 0)),
                   pl.BlockSpec((H, T, 1), lambda i: (0, i, 0)), pl.BlockSpec((H, 1, T), lambda i: (0, 0, i))],
        out_shape=[jax.ShapeDtypeStruct((S, W), MXU_DTYPE), jax.ShapeDtypeStruct((S, W), MXU_DTYPE),
                   jax.ShapeDtypeStruct((H, S, 1), F32), jax.ShapeDtypeStruct((H, 1, S), F32)],
        scratch_shapes=[pltpu.VMEM((1, LANES), F32)],
        compiler_params=_params(("arbitrary",)),
        name=name,
    )(h, h, h, bf, qg, kg)


def _fox_prep_bwd(h, bf, qg, kg, dq, dk, dcrow, H, dh, name):
    S = h.shape[0]
    W = H * dh
    T = _pick(S, 256, LANES)
    nt = S // T
    npair = H // 2

    def body(hq_ref, hk_ref, f_ref, bf_ref, qg_ref, kg_ref, dq_ref, dk_ref, dcrow_ref,
             dhq_ref, dhk_ref, df_ref, dbf_ref, dqg_ref, dkg_ref, dcarry, dct):
        i = pl.program_id(0)

        @pl.when(i == 0)
        def _():
            dcarry[...] = jnp.zeros_like(dcarry)
            dbf_ref[...] = jnp.zeros_like(dbf_ref)
            dqg_ref[...] = jnp.zeros_like(dqg_ref)
            dkg_ref[...] = jnp.zeros_like(dkg_ref)

        for p in range(npair):
            sl = slice(p * LANES, (p + 1) * LANES)
            _, vjp = jax.vjp(lambda x, g: _pair_rms(x, g, dh) * (dh ** -0.5), hq_ref[:, sl], qg_ref[...])
            dx, dg = vjp(dq_ref[:, sl])
            dhq_ref[:, sl] = dx
            dqg_ref[...] += dg
            _, vjp = jax.vjp(lambda x, g: _pair_rms(x, g, dh), hk_ref[:, sl], kg_ref[...])
            dx, dg = vjp(dk_ref[:, sl])
            dhk_ref[:, sl] = dx
            dkg_ref[...] += dg
        dct[...] = jnp.zeros_like(dct)
        for hh in range(H):
            dct[hh:hh + 1, :] = dcrow_ref[hh]
        _, vjp = jax.vjp(lambda f, b: _cum_fn(f, b, jnp.zeros((1, LANES), F32)), f_ref[...], bf_ref[...])
        dc = dct[...].T
        df, dbf = vjp((dc, dcarry[...]))
        df_ref[...] = df
        dbf_ref[...] += dbf
        dcarry[...] = dcarry[...] + jnp.sum(dc, axis=0, keepdims=True)

    rv = lambda i: nt - 1 - i
    return pl.pallas_call(
        body,
        grid=(nt,),
        in_specs=[pl.BlockSpec((T, W), lambda i: (rv(i), 0)), pl.BlockSpec((T, W), lambda i: (rv(i), 1)),
                  pl.BlockSpec((T, LANES), lambda i: (rv(i), 4 * W // LANES)),
                  pl.BlockSpec((1, LANES), lambda i: (0, 0)), pl.BlockSpec((1, LANES), lambda i: (0, 0)),
                  pl.BlockSpec((1, LANES), lambda i: (0, 0)),
                  pl.BlockSpec((T, W), lambda i: (rv(i), 0)), pl.BlockSpec((T, W), lambda i: (rv(i), 0)),
                  pl.BlockSpec((H, 1, T), lambda i: (0, 0, rv(i)))],
        out_specs=[pl.BlockSpec((T, W), lambda i: (rv(i), 0)), pl.BlockSpec((T, W), lambda i: (rv(i), 0)),
                   pl.BlockSpec((T, LANES), lambda i: (rv(i), 0)),
                   pl.BlockSpec((1, LANES), lambda i: (0, 0)), pl.BlockSpec((1, LANES), lambda i: (0, 0)),
                   pl.BlockSpec((1, LANES), lambda i: (0, 0))],
        out_shape=[jax.ShapeDtypeStruct((S, W), F32), jax.ShapeDtypeStruct((S, W), F32),
                   jax.ShapeDtypeStruct((S, LANES), F32)] + [jax.ShapeDtypeStruct((1, LANES), F32)] * 3,
        scratch_shapes=[pltpu.VMEM((1, LANES), F32), pltpu.VMEM((LANES, T), F32)],
        compiler_params=_params(("arbitrary",)),
        name=name,
    )(h, h, h, bf, qg, kg, dq, dk, dcrow)


def _head_masks(dh):
    first = _iota2((1, LANES), 1) < dh
    return first, jnp.logical_not(first)


def _per_head(tile, dh):
    return jnp.stack([jnp.where(mask, tile, 0) for mask in _head_masks(dh)])


def _both(tile):
    return jnp.stack([tile, tile])


def _rows2(x):
    return x.reshape(2 * x.shape[1], x.shape[2])


def _stacked(tile, dh):
    return _rows2(_per_head(tile, dh))


FLASH_SUB = 128


def _flash_scores(q_m, k, crow, row0, diagonal):
    s = _dot(q_m, k, _NT) - crow
    if diagonal:
        s = jnp.where(_iota2(s.shape, s.ndim - 1) <= row0 + _iota2(s.shape, s.ndim - 2), s, NEG)
    return s


def _flash_tiles(S, wide=True):
    tk = _pick(S, 512, LANES)
    tq = 2 * tk if wide and S % (2 * tk) == 0 else tk
    return tq, tk


def _causal_blocks(S, tq, tk, by_query):
    ratio = tq // tk
    if by_query:
        pairs = [(i, j) for i in range(S // tq) for j in range(ratio * (i + 1))]
    else:
        pairs = [(i, j) for j in range(S // tk) for i in range(j // ratio, S // tq)]
    return (jnp.asarray([a for a, _ in pairs], jnp.int32), jnp.asarray([b for _, b in pairs], jnp.int32))


def _flash_blocks(i, j, tq, tk, fn, sub=FLASH_SUB):
    sub = min(sub, tq)
    ratio = tq // tk
    offset = j - ratio * i

    @pl.when(offset < 0)
    def _():
        for r in range(tq // sub):
            fn(slice(r * sub, (r + 1) * sub), 0, False)

    for d in range(ratio):
        @pl.when(offset == d)
        def _():
            for r in range(tq // sub):
                first_row, first_key = r * sub, d * tk
                if first_key > first_row + sub - 1:
                    continue
                fn(slice(first_row, first_row + sub), first_row - first_key, first_key + tk - 1 > first_row)


def _flash_fwd(q, k, h, ccol, crow, H, dh, name):
    S, W = q.shape
    tq, tk = _flash_tiles(S)
    nq, ratio = S // tq, tq // tk
    npair = H // 2
    vblk = 2 * W // LANES

    def body(ii_ref, jj_ref, q_ref, k_ref, v_ref, ccol_ref, crow_ref, o_ref, lse_ref, m_s, l_s, acc_s):
        i, j = ii_ref[pl.program_id(1)], jj_ref[pl.program_id(1)]

        @pl.when(j == 0)
        def _():
            m_s[...] = jnp.full_like(m_s, NEG)
            l_s[...] = jnp.zeros_like(l_s)
            acc_s[...] = jnp.zeros_like(acc_s)

        def tile(rows, row0, diagonal):
            s = _flash_scores(_per_head(q_ref[rows, :], dh), _both(k_ref[...]), crow_ref[...], row0, diagonal)
            cc = ccol_ref[:, rows, :]
            m_old = m_s[:, rows, :]
            m_new = jnp.maximum(m_old, jnp.max(s, axis=-1, keepdims=True) + cc)
            alpha = jnp.exp(m_old - m_new)
            p = jnp.exp(s + (cc - m_new))
            l_s[:, rows, :] = alpha * l_s[:, rows, :] + jnp.sum(p, axis=-1, keepdims=True)
            first, _ = _head_masks(dh)
            acc_s[rows, :] = jnp.where(first, alpha[0], alpha[1]) * acc_s[rows, :] + _dot(
                jnp.concatenate([p[0], p[1]], axis=1), _stacked(v_ref[...].astype(MXU_DTYPE), dh), _NN)
            m_s[:, rows, :] = m_new

        _flash_blocks(i, j, tq, tk, tile)

        @pl.when(j == ratio * (i + 1) - 1)
        def _():
            first, _ = _head_masks(dh)
            o_ref[...] = acc_s[...] / jnp.where(first, l_s[0], l_s[1])
            for a in range(2):
                lse_ref[a] = m_s[a] + jnp.log(l_s[a])

    ii, jj = _causal_blocks(S, tq, tk, by_query=True)
    return pl.pallas_call(
        body,
        grid_spec=pltpu.PrefetchScalarGridSpec(
            num_scalar_prefetch=2,
            grid=(npair, len(ii)),
            in_specs=[pl.BlockSpec((tq, LANES), lambda p, t, ii, jj: (ii[t], p)),
                      pl.BlockSpec((tk, LANES), lambda p, t, ii, jj: (jj[t], p)),
                      pl.BlockSpec((tk, LANES), lambda p, t, ii, jj: (jj[t], vblk + p)),
                      pl.BlockSpec((2, tq, 1), lambda p, t, ii, jj: (p, ii[t], 0)),
                      pl.BlockSpec((2, 1, tk), lambda p, t, ii, jj: (p, 0, jj[t]))],
            out_specs=[pl.BlockSpec((tq, LANES), lambda p, t, ii, jj: (ii[t], p)),
                       pl.BlockSpec((2, tq, 1), lambda p, t, ii, jj: (p, ii[t], 0))],
            scratch_shapes=[pltpu.VMEM((2, tq, 1), F32), pltpu.VMEM((2, tq, 1), F32), pltpu.VMEM((tq, LANES), F32)]),
        out_shape=[jax.ShapeDtypeStruct((S, W), F32), jax.ShapeDtypeStruct((H, S, 1), F32)],
        compiler_params=_params(("parallel", "arbitrary")),
        name=name,
    )(ii, jj, q, k, h, ccol, crow)


def _flash_bwd_kv(q, k, h, ccol, crow, lse, delta, resid, do, H, dh, name):
    S, W = q.shape
    tq, tk = _flash_tiles(S)
    nq, ratio = S // tq, tq // tk
    npair = H // 2
    vblk = 2 * W // LANES

    def body(ii_ref, jj_ref, q_ref, k_ref, v_ref, ccol_ref, crow_ref, lse_ref, dl_ref, rs_ref, do_ref, dk_ref, dv_ref, dcr_ref,
             dkt_s, dvt_s):
        i, j = ii_ref[pl.program_id(1)], jj_ref[pl.program_id(1)]

        @pl.when(i == j // ratio)
        def _():
            dkt_s[...] = jnp.zeros_like(dkt_s)
            dvt_s[...] = jnp.zeros_like(dvt_s)
            dcr_ref[...] = jnp.zeros_like(dcr_ref)

        def tile(rows, row0, diagonal):
            kv, vv = _both(k_ref[...]), _both(v_ref[...].astype(MXU_DTYPE))
            q_m = _per_head(q_ref[rows, :], dh)
            do_m = _per_head(do_ref[rows, :].astype(MXU_DTYPE), dh)
            s = _flash_scores(q_m, kv, crow_ref[...], row0, diagonal)
            p = jnp.exp(s + (ccol_ref[:, rows, :] - lse_ref[:, rows, :]))
            ds = p * (_dot(do_m, vv, _NT) - (dl_ref[:, rows, :] + rs_ref[:, rows, :]))
            dvt_s[...] += _dot(_rows2(do_m), _rows2(p), _TN)
            dkt_s[...] += _dot(_rows2(q_m), _rows2(ds), _TN)
            dcr_ref[...] -= jnp.sum(ds, axis=1, keepdims=True)

        _flash_blocks(i, j, tq, tk, tile, sub=2 * FLASH_SUB)

        @pl.when(i == nq - 1)
        def _():
            dk_ref[...] = dkt_s[...].T
            dv_ref[...] = dvt_s[...].T

    ii, jj = _causal_blocks(S, tq, tk, by_query=False)
    qrow = pl.BlockSpec((2, tq, 1), lambda p, t, ii, jj: (p, ii[t], 0))
    return pl.pallas_call(
        body,
        grid_spec=pltpu.PrefetchScalarGridSpec(
            num_scalar_prefetch=2,
            grid=(npair, len(ii)),
            in_specs=[pl.BlockSpec((tq, LANES), lambda p, t, ii, jj: (ii[t], p)),
                      pl.BlockSpec((tk, LANES), lambda p, t, ii, jj: (jj[t], p)),
                      pl.BlockSpec((tk, LANES), lambda p, t, ii, jj: (jj[t], vblk + p)),
                      qrow,
                      pl.BlockSpec((2, 1, tk), lambda p, t, ii, jj: (p, 0, jj[t])),
                      qrow, qrow, qrow,
                      pl.BlockSpec((tq, LANES), lambda p, t, ii, jj: (ii[t], p))],
            out_specs=[pl.BlockSpec((tk, LANES), lambda p, t, ii, jj: (jj[t], p)),
                       pl.BlockSpec((tk, LANES), lambda p, t, ii, jj: (jj[t], p)),
                       pl.BlockSpec((2, 1, tk), lambda p, t, ii, jj: (p, 0, jj[t]))],
            scratch_shapes=[pltpu.VMEM((LANES, tk), F32), pltpu.VMEM((LANES, tk), F32)]),
        out_shape=[jax.ShapeDtypeStruct((S, W), F32), jax.ShapeDtypeStruct((S, W), F32),
                   jax.ShapeDtypeStruct((H, 1, S), F32)],
        compiler_params=_params(("parallel", "arbitrary")),
        name=name,
    )(ii, jj, q, k, h, ccol, crow, lse, delta, resid, do)


def _flash_bwd_q(q, k, h, ccol, crow, lse, delta, do, H, dh, name):
    S, W = q.shape
    tq, tk = _flash_tiles(S, wide=False)
    npair = H // 2
    vblk = 2 * W // LANES

    def body(ii_ref, jj_ref, q_ref, k_ref, v_ref, ccol_ref, crow_ref, lse_ref, dl_ref, do_ref, dq_ref, rs_ref):
        i, j = ii_ref[pl.program_id(1)], jj_ref[pl.program_id(1)]

        @pl.when(j == 0)
        def _():
            dq_ref[...] = jnp.zeros_like(dq_ref)
            rs_ref[...] = jnp.zeros_like(rs_ref)

        def tile(rows, row0, diagonal):
            kv, vv = k_ref[...], _both(v_ref[...].astype(MXU_DTYPE))
            s = _flash_scores(_per_head(q_ref[rows, :], dh), _both(kv), crow_ref[...], row0, diagonal)
            p = jnp.exp(s + (ccol_ref[:, rows, :] - lse_ref[:, rows, :]))
            ds = p * (_dot(_per_head(do_ref[rows, :].astype(MXU_DTYPE), dh), vv, _NT) - dl_ref[:, rows, :])
            dq = _dot(ds, _per_head(kv, dh), _NN)
            dq_ref[rows, :] += dq[0] + dq[1]
            rs_ref[:, rows, :] += jnp.sum(ds, axis=-1, keepdims=True)

        _flash_blocks(i, j, tq, tk, tile, sub=tk)

    ii, jj = _causal_blocks(S, tq, tk, by_query=True)
    qrow = pl.BlockSpec((2, tq, 1), lambda p, t, ii, jj: (p, ii[t], 0))
    return pl.pallas_call(
        body,
        grid_spec=pltpu.PrefetchScalarGridSpec(
            num_scalar_prefetch=2,
            grid=(npair, len(ii)),
            in_specs=[pl.BlockSpec((tq, LANES), lambda p, t, ii, jj: (ii[t], p)),
                      pl.BlockSpec((tk, LANES), lambda p, t, ii, jj: (jj[t], p)),
                      pl.BlockSpec((tk, LANES), lambda p, t, ii, jj: (jj[t], vblk + p)),
                      qrow,
                      pl.BlockSpec((2, 1, tk), lambda p, t, ii, jj: (p, 0, jj[t])),
                      qrow, qrow,
                      pl.BlockSpec((tq, LANES), lambda p, t, ii, jj: (ii[t], p))],
            out_specs=[pl.BlockSpec((tq, LANES), lambda p, t, ii, jj: (ii[t], p)), qrow]),
        out_shape=[jax.ShapeDtypeStruct((S, W), F32), jax.ShapeDtypeStruct((H, S, 1), F32)],
        compiler_params=_params(("parallel", "arbitrary")),
        name=name,
    )(ii, jj, q, k, h, ccol, crow, lse, delta, do)


def _fox_gate_fwd(o, h, W, name):
    def fn(rows, _, __):
        return [rows[0] * _silu(rows[1])], [], []

    return _rowwise(fn, [o, (h, W, 3)], [], [(W, MXU_DTYPE)], [], tile=256, name=name)[0]


def _fox_gate_bwd(o, h, dog, H, dh, name):
    S, W = o.shape
    T = _pick(S, 256, SUBLANES)

    def body(o_ref, z_ref, d_ref, do_ref, dz_ref, dl_ref):
        ov, zv, dv = o_ref[...], z_ref[...], d_ref[...]
        sg = _sigmoid(zv)
        do = dv * zv * sg
        do_ref[...] = do
        dz_ref[...] = dv * ov * sg * (1.0 + zv * (1.0 - sg))
        prod = do * ov
        for p in range(H // 2):
            blk = prod[:, p * LANES:(p + 1) * LANES]
            first = _iota2(blk.shape, 1) < dh
            dl_ref[2 * p] = jnp.sum(jnp.where(first, blk, 0.0), axis=1, keepdims=True)
            dl_ref[2 * p + 1] = jnp.sum(jnp.where(first, 0.0, blk), axis=1, keepdims=True)

    return pl.pallas_call(
        body,
        grid=(S // T,),
        in_specs=[pl.BlockSpec((T, W), lambda i: (i, 0)), pl.BlockSpec((T, W), lambda i: (i, 3)),
                  pl.BlockSpec((T, W), lambda i: (i, 0))],
        out_specs=[pl.BlockSpec((T, W), lambda i: (i, 0)), pl.BlockSpec((T, W), lambda i: (i, 0)),
                   pl.BlockSpec((H, T, 1), lambda i: (0, i, 0))],
        out_shape=[jax.ShapeDtypeStruct((S, W), F32), jax.ShapeDtypeStruct((S, W), F32),
                   jax.ShapeDtypeStruct((H, S, 1), F32)],
        compiler_params=_params(("parallel",)),
        name=name,
    )(o, h, dog)


def _gdn_layer_fwd(x, w, tag):
    H = w["H"]
    qk = H * LANES
    h = _matmul(x, w["w_in"], "nn", name=f"{tag}_in")
    c = _conv_fwd(h, w["conv"], 3 * qk, name=f"{tag}_conv")
    og, states, inverses = _gdn_scan_fwd(c, h, w["hp"], w["ng"], H, name=f"{tag}_scan")
    y = _matmul(og, w["w_out"], "nn", name=f"{tag}_out")
    return y, (x, h, c, states, inverses, og)


def _gdn_layer_bwd(dy, res, w, tag):
    x, h, c, states, inverses, og = res
    H = w["H"]
    qk = H * LANES
    dog = _matmul(dy, w["w_out"], "nt", name=f"{tag}_out_dx")
    dw_out = _matmul(og, dy, "tn", name=f"{tag}_out_dw")
    dq, dk, dv, dz, dba, dhp, dng = _gdn_scan_bwd(c, h, w["hp"], w["ng"], states, inverses, dog, H, name=f"{tag}_scan_bwd")
    dh_parts, dconv = [], []
    for part, d in enumerate((dq, dk, dv)):
        dh_p, dw_p = _conv_bwd(d, h, w["conv"][:, part * qk:(part + 1) * qk], part * qk, name=f"{tag}_conv_bwd{part}")
        dh_parts.append(dh_p)
        dconv.append(dw_p[:GDN_CONV])
    dh = jnp.concatenate(dh_parts + [dz, dba], axis=1)
    dx = _matmul(dh, w["w_in"], "nt", name=f"{tag}_in_dx")
    dw_in = _matmul(x, dh, "tn", name=f"{tag}_in_dw")
    grads = {"w_in": dw_in, "w_out": dw_out, "conv": jnp.concatenate(dconv, axis=1),
             "a_log": dhp[0, :H], "dt_bias": dhp[1, :H], "norm_g": dng[0]}
    return dx, grads


def _fox_layer_fwd(x, w, tag):
    H, dh = w["H"], w["dh"]
    W = H * dh
    h = _matmul(x, w["w_in"], "nn", name=f"{tag}_in")
    q, k, ccol, crow = _fox_prep_fwd(h, w["bf"], w["qg"], w["kg"], H, dh, name=f"{tag}_prep")
    o, lse = _flash_fwd(q, k, h, ccol, crow, H, dh, name=f"{tag}_flash")
    og = _fox_gate_fwd(o, h, W, name=f"{tag}_gate")
    y = _matmul(og, w["w_out"], "nn", name=f"{tag}_out")
    return y, (x, h, q, k, ccol, crow, o, lse, og)


def _fox_layer_bwd(dy, res, w, tag):
    x, h, q, k, ccol, crow, o, lse, og = res
    H, dh = w["H"], w["dh"]
    dog = _matmul(dy, w["w_out"], "nt", name=f"{tag}_out_dx")
    dw_out = _matmul(og, dy, "tn", name=f"{tag}_out_dw")
    do, dz, delta = _fox_gate_bwd(o, h, dog, H, dh, name=f"{tag}_gate_bwd")
    dqq, resid = _flash_bwd_q(q, k, h, ccol, crow, lse, delta, do, H, dh, name=f"{tag}_flash_bwd_q")
    dkk, dvv, dcrow = _flash_bwd_kv(q, k, h, ccol, crow, lse, delta, resid, do, H, dh, name=f"{tag}_flash_bwd_kv")
    dhq, dhk, df, dbf, dqg, dkg = _fox_prep_bwd(h, w["bf"], w["qg"], w["kg"], dqq, dkk, dcrow, H, dh, name=f"{tag}_prep_bwd")
    dhh = jnp.concatenate([dhq, dhk, dvv, dz, df], axis=1)
    dx = _matmul(dhh, w["w_in"], "nt", name=f"{tag}_in_dx")
    dw_in = _matmul(x, dhh, "tn", name=f"{tag}_in_dw")
    grads = {"w_in": dw_in, "w_out": dw_out, "b_f": dbf[0, :H],
             "q_norm_g": dqg[0, :dh] + dqg[0, dh:], "k_norm_g": dkg[0, :dh] + dkg[0, dh:]}
    return dx, grads


def _pad_cols(w, n):
    return jnp.pad(w, ((0, 0), (0, n - w.shape[1])))


def _build_layers(full, small):
    depth = small["ln_g"].shape[0]
    gh = small["gdn_a_log"].shape[1]
    fh, dh = small["fox_b_f"].shape[1], small["fox_q_norm_g"].shape[1]
    layers = []
    for i in range(depth):
        j = i // 2
        w = {"ln_g": small["ln_g"][i][None], "ln_b": small["ln_b"][i][None],
             "w_gate": full["ple_w_gate"][i], "w_proj": full["ple_w_proj"][i]}
        if i % 2 == 0:
            hp = jnp.zeros((SUBLANES, LANES), F32).at[0, :gh].set(small["gdn_a_log"][j]).at[1, :gh].set(small["gdn_dt_bias"][j])
            w.update(kind="gdn", H=gh, w_in=_pad_cols(full["gdn_w_in"][j], 4 * gh * LANES + LANES),
                     conv=full["gdn_conv_w"][j], hp=hp, ng=small["gdn_norm_g"][j][None], w_out=full["gdn_w_out"][j])
        else:
            bf = jnp.zeros((1, LANES), F32).at[0, :fh].set(small["fox_b_f"][j])
            w.update(kind="fox", H=fh, dh=dh, w_in=_pad_cols(full["fox_w_in"][j], 4 * fh * dh + LANES), bf=bf,
                     qg=jnp.tile(small["fox_q_norm_g"][j], 2)[None], kg=jnp.tile(small["fox_k_norm_g"][j], 2)[None],
                     w_out=full["fox_w_out"][j])
        layers.append(w)
    return layers


def _local_step(x, p, target, layers):
    depth = len(layers)
    alpha = (2 * depth) ** 0.25
    saved = []
    for i, w in enumerate(layers):
        tag = f"l{i}"
        if w["kind"] == "gdn":
            y, res = _gdn_layer_fwd(x, w, tag)
        else:
            y, res = _fox_layer_fwd(x, w, tag)
        x1 = _ln_fwd(x, y, w["ln_g"], w["ln_b"], alpha, name=f"{tag}_ln")
        gate_pre = _matmul(x1, w["w_gate"], "nn", name=f"{tag}_gate_mm")
        pp = _matmul(p[i], w["w_proj"], "nn", name=f"{tag}_proj_mm")
        x2 = _ple_fwd(x1, gate_pre, pp, name=f"{tag}_ple")
        saved.append((res, x, y, x1, gate_pre, pp))
        x = x2
    dx, loss_tile = _loss_head(x, target, name="loss_head")
    grads = [None] * depth
    for i in reversed(range(depth)):
        w = layers[i]
        tag = f"l{i}"
        res, xin, y, x1, gate_pre, pp = saved[i]
        dgp, dpp = _ple_bwd(dx, gate_pre, pp, name=f"{tag}_ple_bwd")
        dx1 = _add(dx, _matmul(dgp, w["w_gate"], "nt", name=f"{tag}_gate_dx"), name=f"{tag}_dx1")
        dw_gate = _matmul(x1, dgp, "tn", name=f"{tag}_gate_dw")
        dw_proj = _matmul(p[i], dpp, "tn", name=f"{tag}_proj_dw")
        dy, dg, db = _ln_bwd(xin, y, w["ln_g"], w["ln_b"], dx1, alpha, name=f"{tag}_ln_bwd")
        if w["kind"] == "gdn":
            dxm, g = _gdn_layer_bwd(dy, res, w, tag)
        else:
            dxm, g = _fox_layer_bwd(dy, res, w, tag)
        dx = _axpy(alpha, dy, dxm, name=f"{tag}_dx")
        g.update({"w_gate": dw_gate, "w_proj": dw_proj, "ln_g": dg[0], "ln_b": db[0]})
        grads[i] = g
    return loss_tile, dx, grads


MESH_ID = pl.DeviceIdType.MESH
HBM_SPEC = pl.BlockSpec(memory_space=pl.ANY)
PACK_COLS = 1024
PACK_ROWS = 256


def _all_gather(shards, name):
    nt = len(shards)

    def body(*refs):
        x_refs, out_refs = refs[:nt], refs[nt:2 * nt]
        send_sems, recv_sems, local_sems = refs[2 * nt:]
        x, y, c = lax.axis_index("x"), lax.axis_index("y"), lax.axis_index("c")
        me, sibling = (x, y, c), (x, y, 1 - c)
        chips = [(1 - x, y), (x, 1 - y), (1 - x, 1 - y)]

        def slot(t, px, py, pc):
            return out_refs[t].at[4 * px + 2 * py + pc]

        def copy(k, t, block, to, src=None):
            return pltpu.make_async_remote_copy(
                src_ref=slot(t, *block) if src is None else src, dst_ref=slot(t, *block),
                send_sem=send_sems.at[k, t], recv_sem=recv_sems.at[k, t], device_id=to, device_id_type=MESH_ID)

        every = range(nt)
        mine = [pltpu.make_async_copy(x_refs[t], slot(t, *me), local_sems.at[t]) for t in every]
        for cp in mine:
            cp.start()
        first = [copy(0, t, me, sibling, src=x_refs[t]) for t in every]
        first += [copy(1 + j, t, me, (*chip, c), src=x_refs[t]) for j, chip in enumerate(chips) for t in every]
        for cp in first:
            cp.start()
        passed = []
        for j, chip in enumerate(chips):
            for t in every:
                copy(1 + j, t, (*chip, c), me).wait_recv()
                passed.append(copy(4 + j, t, (*chip, c), sibling))
                passed[-1].start()
        for t in every:
            copy(0, t, sibling, me).wait_recv()
        for j, chip in enumerate(chips):
            for t in every:
                copy(4 + j, t, (*chip, 1 - c), me).wait_recv()
        for cp in first + passed:
            cp.wait_send()
        for cp in mine:
            cp.wait()

    return pl.pallas_call(
        body,
        out_shape=[jax.ShapeDtypeStruct((N_DEV, *s.shape), s.dtype) for s in shards],
        in_specs=[HBM_SPEC] * nt,
        out_specs=[HBM_SPEC] * nt,
        scratch_shapes=[pltpu.SemaphoreType.DMA((7, nt)), pltpu.SemaphoreType.DMA((7, nt)), pltpu.SemaphoreType.DMA((nt,))],
        name=name,
    )(*shards)


def _all_to_all(slabs, name):
    nt = len(slabs)

    def body(*refs):
        g_refs, out_refs = refs[:nt], refs[nt:2 * nt]
        send_sems, recv_sems, local_sems = refs[2 * nt:]
        x, y, c = lax.axis_index("x"), lax.axis_index("y"), lax.axis_index("c")
        me = 4 * x + 2 * y + c
        mine = [pltpu.make_async_copy(g_refs[t].at[me], out_refs[t].at[me], local_sems.at[t]) for t in range(nt)]
        for cp in mine:
            cp.start()
        copies = []
        for k in range(1, N_DEV):
            px = 1 - x if k & 4 else x
            py = 1 - y if k & 2 else y
            pc = 1 - c if k & 1 else c
            peer = 4 * px + 2 * py + pc
            for t in range(nt):
                copies.append(tuple(
                    pltpu.make_async_remote_copy(src_ref=g_refs[t].at[peer], dst_ref=out_refs[t].at[dst],
                                                 send_sem=send_sems.at[k - 1, t], recv_sem=recv_sems.at[k - 1, t],
                                                 device_id=(px, py, pc), device_id_type=MESH_ID)
                    for dst in (me, peer)))
        for send, _ in copies:
            send.start()
        for send, arrive in copies:
            arrive.wait_recv()
            send.wait_send()
        for cp in mine:
            cp.wait()

    return pl.pallas_call(
        body,
        out_shape=[jax.ShapeDtypeStruct(s.shape, s.dtype) for s in slabs],
        in_specs=[HBM_SPEC] * nt,
        out_specs=[HBM_SPEC] * nt,
        scratch_shapes=[pltpu.SemaphoreType.DMA((7, nt)), pltpu.SemaphoreType.DMA((7, nt)), pltpu.SemaphoreType.DMA((nt,))],
        name=name,
    )(*slabs)


def _pack(flats, dtype):
    flat = jnp.concatenate([f.astype(dtype).reshape(-1) for f in flats])
    unit = PACK_ROWS * PACK_COLS
    n = -(-flat.shape[0] // unit) * unit
    return jnp.pad(flat, (0, n - flat.shape[0])).reshape(n // PACK_COLS, PACK_COLS)


def _unpack(buf, shapes):
    lead = buf.shape[:-2]
    flat = buf.reshape(*lead, -1)
    out, off = [], 0
    for s in shapes:
        n = math.prod(s)
        out.append(flat[..., off:off + n].reshape(*lead, *s))
        off += n
    return out


_ROW_SPLIT = ("ple_w_gate", "gdn_w_out", "fox_w_out")
_COL_SPLIT = ("ple_w_proj", "gdn_w_in", "gdn_conv_w", "fox_w_in")
_SHARDED = ("ple_w_gate", "ple_w_proj", "gdn_w_in", "gdn_conv_w", "gdn_w_out", "fox_w_in", "fox_w_out")
_REPLICATED = ("ln_g", "ln_b", "gdn_a_log", "gdn_dt_bias", "gdn_norm_g", "fox_b_f", "fox_q_norm_g", "fox_k_norm_g")
_WEIGHTS = ("ln_g", "ln_b", "ple_w_gate", "ple_w_proj", "gdn_w_in", "gdn_conv_w", "gdn_a_log", "gdn_dt_bias",
            "gdn_norm_g", "gdn_w_out", "fox_w_in", "fox_b_f", "fox_q_norm_g", "fox_k_norm_g", "fox_w_out")


def _join(name, gathered):
    n, l, a, b = gathered.shape
    if name in _ROW_SPLIT:
        return gathered.transpose(1, 0, 2, 3).reshape(l, n * a, b)
    return gathered.transpose(1, 2, 0, 3).reshape(l, a, n * b)


def _split(name, full):
    l, a, b = full.shape
    if name in _ROW_SPLIT:
        return full.reshape(l, N_DEV, a // N_DEV, b).transpose(1, 0, 2, 3)
    return full.reshape(l, a, N_DEV, b // N_DEV).transpose(2, 0, 1, 3)


def _adamw(w, g_parts, m, v, name):
    shape = w.shape
    R, C = math.prod(shape[:-1]), shape[-1]
    tr = _pick(R, 256, SUBLANES)
    c1 = 1.0 - ADAM_B1 ** ADAM_STEP
    c2 = 1.0 - ADAM_B2 ** ADAM_STEP

    def body(w_ref, g_ref, m_ref, v_ref, go_ref, d_ref, mo_ref, vo_ref):
        gv = g_ref[0]
        for s in range(1, N_DEV):
            gv = gv + g_ref[s]
        mn = ADAM_B1 * m_ref[...] + (1.0 - ADAM_B1) * gv
        vn = ADAM_B2 * v_ref[...] + (1.0 - ADAM_B2) * jnp.square(gv)
        go_ref[...] = gv
        d_ref[...] = -ADAM_LR * ((mn / c1) / (jnp.sqrt(vn / c2) + ADAM_EPS) + ADAM_WD * w_ref[...])
        mo_ref[...] = mn
        vo_ref[...] = vn

    row = pl.BlockSpec((tr, C), lambda i: (i, 0))
    outs = pl.pallas_call(
        body,
        grid=(R // tr,),
        in_specs=[row, pl.BlockSpec((N_DEV, tr, C), lambda i: (0, i, 0)), row, row],
        out_specs=[row] * 4,
        out_shape=[jax.ShapeDtypeStruct((R, C), F32)] * 4,
        compiler_params=_params(("parallel",)),
        name=name,
    )(w.reshape(R, C), g_parts.reshape(N_DEV, R, C), m.reshape(R, C), v.reshape(R, C))
    return [o.reshape(shape) for o in outs]


def _train_step(x, p, target, w, m, v):
    shards = [w[n] if n == "gdn_conv_w" else w[n].astype(MXU_DTYPE) for n in _SHARDED]
    gathered = _all_gather(shards, name="gather_weights")
    full = {n: _join(n, part) for n, part in zip(_SHARDED, gathered)}
    layers = _build_layers(full, {n: w[n] for n in _REPLICATED})

    loss_tile, dx, grads = _local_step(x[0], p[:, 0], target[0], layers)
    loss = lax.psum(loss_tile[0, 0], ("x", "y", "c"))

    depth = len(layers)
    gdn_l = [i for i in range(depth) if i % 2 == 0]
    fox_l = [i for i in range(depth) if i % 2 == 1]

    def stack(key, idx):
        return jnp.stack([grads[i][key] for i in idx])

    full_g = {
        "ple_w_gate": stack("w_gate", range(depth)), "ple_w_proj": stack("w_proj", range(depth)),
        "gdn_w_in": stack("w_in", gdn_l)[..., :w["gdn_w_in"].shape[-1] * N_DEV], "gdn_conv_w": stack("conv", gdn_l),
        "gdn_w_out": stack("w_out", gdn_l),
        "fox_w_in": stack("w_in", fox_l)[..., :w["fox_w_in"].shape[-1] * N_DEV], "fox_w_out": stack("w_out", fox_l)}
    small_g = {
        "ln_g": stack("ln_g", range(depth)), "ln_b": stack("ln_b", range(depth)),
        "gdn_a_log": stack("a_log", gdn_l), "gdn_dt_bias": stack("dt_bias", gdn_l), "gdn_norm_g": stack("norm_g", gdn_l),
        "fox_b_f": stack("b_f", fox_l), "fox_q_norm_g": stack("q_norm_g", fox_l), "fox_k_norm_g": stack("k_norm_g", fox_l)}

    g_parts = dict(zip(_SHARDED, _all_to_all([_split(n, full_g[n]) for n in _SHARDED], name="scatter_grads")))
    small_all = _all_gather([_pack([small_g[n] for n in _REPLICATED], F32)], name="gather_small_grads")[0]
    g_parts.update(zip(_REPLICATED, _unpack(small_all, [w[n].shape for n in _REPLICATED])))

    g, delta, new_m, new_v = {}, {}, {}, {}
    for n in _WEIGHTS:
        g[n], delta[n], new_m[n], new_v[n] = _adamw(w[n], g_parts[n], m[n], v[n], name=f"adamw_{n}")
    return (loss, dx[None], *[g[n] for n in _WEIGHTS], *[delta[n] for n in _WEIGHTS],
            *[new_m[n] for n in _WEIGHTS], *[new_v[n] for n in _WEIGHTS])


def kernel(x, p, ln_g, ln_b, ple_w_gate, ple_w_proj, gdn_w_in, gdn_conv_w, gdn_a_log, gdn_dt_bias, gdn_norm_g, gdn_w_out, fox_w_in, fox_b_f, fox_q_norm_g, fox_k_norm_g, fox_w_out, loss_target, m_ln_g, m_ln_b, m_ple_w_gate, m_ple_w_proj, m_gdn_w_in, m_gdn_conv_w, m_gdn_a_log, m_gdn_dt_bias, m_gdn_norm_g, m_gdn_w_out, m_fox_w_in, m_fox_b_f, m_fox_q_norm_g, m_fox_k_norm_g, m_fox_w_out, v_ln_g, v_ln_b, v_ple_w_gate, v_ple_w_proj, v_gdn_w_in, v_gdn_conv_w, v_gdn_a_log, v_gdn_dt_bias, v_gdn_norm_g, v_gdn_w_out, v_fox_w_in, v_fox_b_f, v_fox_q_norm_g, v_fox_k_norm_g, v_fox_w_out):
    given = dict(locals())
    w = {n: given[n] for n in _WEIGHTS}
    m = {n: given["m_" + n] for n in _WEIGHTS}
    v = {n: given["v_" + n] for n in _WEIGHTS}
    return _train_step(x, p, loss_target, w, m, v)
```

```python
import functools
import math

import jax
import jax.numpy as jnp
from jax import lax
from jax.experimental import pallas as pl
from jax.experimental.pallas import tpu as pltpu

F32 = jnp.float32
BF16 = jnp.bfloat16
MXU_DTYPE = BF16
HI = lax.Precision.HIGHEST

N_DEV = 8
LANES = 128
SUBLANES = 8
VMEM_BYTES = 64 * 1024 * 1024

GDN_CHUNK = 64
GDN_CONV = 4
LN_EPS = 1e-5
RMS_EPS = 1e-6
NEG = -1e30

ADAM_LR = 0.001
ADAM_B1 = 0.9
ADAM_B2 = 0.999
ADAM_EPS = 1e-08
ADAM_WD = 0.01
ADAM_STEP = 10


def _params(semantics, vmem_mb=40):
    return pltpu.CompilerParams(dimension_semantics=semantics, vmem_limit_bytes=vmem_mb * 1024 * 1024)


def _pick(dim, cap, unit=LANES):
    if dim <= cap:
        return dim
    best = None
    for t in range(unit, cap + 1, unit):
        if dim % t == 0:
            best = t
    assert best is not None, (dim, cap)
    return best


def _dims(dims, ndim):
    if ndim == 2:
        return (dims, ((), ()))
    return (((dims[0][0] + 1,), (dims[1][0] + 1,)), ((0,), (0,)))


def _dot(a, b, dims):
    return lax.dot_general(a.astype(MXU_DTYPE), b.astype(MXU_DTYPE), _dims(dims, a.ndim), preferred_element_type=F32)


_NN = ((1,), (0,))
_NT = ((1,), (1,))
_TN = ((0,), (0,))


@jax.custom_vjp
def _mm_nn(a, b):
    return _dot(a, b, _NN)


@jax.custom_vjp
def _mm_nt(a, b):
    return _dot(a, b, _NT)


@jax.custom_vjp
def _mm_tn(a, b):
    return _dot(a, b, _TN)


_mm_nn.defvjp(lambda a, b: (_dot(a, b, _NN), (a, b)), lambda r, g: (_mm_nt(g, r[1]), _mm_tn(r[0], g)))
_mm_nt.defvjp(lambda a, b: (_dot(a, b, _NT), (a, b)), lambda r, g: (_mm_nn(g, r[1]), _mm_tn(g, r[0])))
_mm_tn.defvjp(lambda a, b: (_dot(a, b, _TN), (a, b)), lambda r, g: (_mm_nt(r[1], g), _mm_nn(r[0], g)))


def _mm_hi(a, b, precision=HI):
    return lax.dot_general(a, b, _dims(_NN, a.ndim), precision=precision, preferred_element_type=F32)


def _mm_3x(a, b):
    return _mm_hi(a, b, lax.Precision.HIGH)


def _sigmoid(x):
    return 1.0 / (1.0 + jnp.exp(-x))


def _silu(x):
    return x * _sigmoid(x)


def _softplus(x):
    return jnp.maximum(x, 0.0) + jnp.log(1.0 + jnp.exp(-jnp.abs(x)))


def _iota2(shape, dim):
    return lax.broadcasted_iota(jnp.int32, shape, dim)


def _lane_pick(tile, lane):
    return jnp.sum(jnp.where(_iota2(tile.shape, 1) == lane, tile, 0.0), axis=1, keepdims=True)


def _lane_put(col, lane, width=LANES):
    return jnp.where(_iota2((col.shape[0], width), 1) == lane, col, 0.0)


def _matmul(a, b, mode, out_dtype=F32, *, name, tm=1024, tn=1408, tk=1408, a_cols=None, b_cols=None):
    def cols(arr, rng):
        return (0, arr.shape[1]) if rng is None else rng

    a0, an = cols(a, a_cols)
    b0, bn = cols(b, b_cols)
    if mode == "nn":
        M, K, N = a.shape[0], an, bn
        assert b.shape[0] == K
    elif mode == "nt":
        M, K, N = a.shape[0], an, b.shape[0]
        assert bn == K
    else:
        K, M, N = a.shape[0], an, bn
        assert b.shape[0] == K
    tm, tn, tk = _pick(M, tm), _pick(N, tn), _pick(K, tk)
    nk = K // tk
    if mode == "nn":
        assert a0 % tk == 0 and b0 % tn == 0
        a_spec = pl.BlockSpec((tm, tk), lambda i, j, k: (i, a0 // tk + k))
        b_spec = pl.BlockSpec((tk, tn), lambda i, j, k: (k, b0 // tn + j))
        dims = _NN
    elif mode == "nt":
        assert a0 % tk == 0 and b0 % tk == 0
        a_spec = pl.BlockSpec((tm, tk), lambda i, j, k: (i, a0 // tk + k))
        b_spec = pl.BlockSpec((tn, tk), lambda i, j, k: (j, b0 // tk + k))
        dims = _NT
    else:
        assert a0 % tm == 0 and b0 % tn == 0
        a_spec = pl.BlockSpec((tk, tm), lambda i, j, k: (k, a0 // tm + i))
        b_spec = pl.BlockSpec((tk, tn), lambda i, j, k: (k, b0 // tn + j))
        dims = _TN

    def body(a_ref, b_ref, o_ref, acc_ref):
        k = pl.program_id(2)

        @pl.when(k == 0)
        def _():
            acc_ref[...] = jnp.zeros_like(acc_ref)

        acc_ref[...] += _dot(a_ref[...], b_ref[...], dims)

        @pl.when(k == nk - 1)
        def _():
            o_ref[...] = acc_ref[...].astype(o_ref.dtype)

    return pl.pallas_call(
        body,
        grid=(M // tm, N // tn, nk),
        in_specs=[a_spec, b_spec],
        out_specs=pl.BlockSpec((tm, tn), lambda i, j, k: (i, j)),
        out_shape=jax.ShapeDtypeStruct((M, N), out_dtype),
        scratch_shapes=[pltpu.VMEM((tm, tn), F32)],
        compiler_params=_params(("parallel", "parallel", "arbitrary"), 48),
        name=name,
    )(a, b)


def _rowwise(fn, rows, consts, out_rows, out_accs, *, tile, name, reverse=False, carries=(), vmem_mb=40):
    rows = [r if isinstance(r, tuple) else (r, r.shape[1], 0) for r in rows]
    S = rows[0][0].shape[0]
    tile = _pick(S, tile, SUBLANES)
    nt = S // tile
    nr, nc, no, na = len(rows), len(consts), len(out_rows), len(out_accs)

    def ridx(i):
        return nt - 1 - i if reverse else i

    in_specs = [pl.BlockSpec((tile, w), functools.partial(lambda i, cb: (ridx(i), cb), cb=cb)) for _, w, cb in rows]
    in_specs += [pl.BlockSpec(c.shape, functools.partial(lambda i, nd: (0,) * nd, nd=c.ndim)) for c in consts]
    out_specs = [pl.BlockSpec((tile, c), lambda i: (ridx(i), 0)) for c, _ in out_rows]
    out_specs += [pl.BlockSpec(s, functools.partial(lambda i, nd: (0,) * nd, nd=len(s))) for s, _ in out_accs]
    out_shape = [jax.ShapeDtypeStruct((S, c), d) for c, d in out_rows]
    out_shape += [jax.ShapeDtypeStruct(s, d) for s, d in out_accs]

    def body(*refs):
        rin, cin = refs[:nr], refs[nr:nr + nc]
        rout, aout = refs[nr + nc:nr + nc + no], refs[nr + nc + no:nr + nc + no + na]
        carr = refs[nr + nc + no + na:]
        step = pl.program_id(0)

        @pl.when(step == 0)
        def _():
            for r in aout + carr:
                r[...] = jnp.zeros_like(r)

        outs, accs, newc = fn([r[...] for r in rin], [c[...] for c in cin], [c[...] for c in carr])
        for r, o in zip(rout, outs, strict=True):
            r[...] = o.astype(r.dtype)
        for r, v in zip(aout, accs, strict=True):
            r[...] += v
        for r, v in zip(carr, newc, strict=True):
            r[...] = v

    res = pl.pallas_call(
        body,
        grid=(nt,),
        in_specs=in_specs,
        out_specs=out_specs,
        out_shape=out_shape,
        scratch_shapes=[pltpu.VMEM(s, F32) for s in carries],
        compiler_params=_params(("arbitrary",), vmem_mb),
        name=name,
    )(*[r[0] for r in rows], *consts)
    return res


def _ln_fn(x, y, g, b, alpha):
    r = alpha * x + y
    mu = jnp.mean(r, -1, keepdims=True)
    var = jnp.mean(jnp.square(r - mu), -1, keepdims=True)
    return (r - mu) * lax.rsqrt(var + LN_EPS) * g + b


def _ln_fwd(x, y, g, b, alpha, name):
    D = x.shape[1]

    def fn(rows, consts, _):
        return [_ln_fn(rows[0], rows[1], consts[0], consts[1], alpha)], [], []

    return _rowwise(fn, [x, y], [g, b], [(D, F32)], [], tile=256, name=name)[0]


def _ln_bwd(x, y, g, b, dx1, alpha, name):
    D = x.shape[1]

    def fn(rows, consts, _):
        xv, yv, d = rows
        _, vjp = jax.vjp(lambda yy, gg, bb: _ln_fn(xv, yy, gg, bb, alpha), yv, consts[0], consts[1])
        dy, dg, db = vjp(d)
        return [dy], [dg, db], []

    return _rowwise(fn, [x, y, dx1], [g, b], [(D, F32)], [((1, D), F32), ((1, D), F32)], tile=256, name=name)


def _ple_fwd(x1, gate_pre, pp, name):
    D = x1.shape[1]

    def fn(rows, _, __):
        return [rows[0] + _sigmoid(rows[1]) * rows[2]], [], []

    return _rowwise(fn, [x1, gate_pre, pp], [], [(D, F32)], [], tile=256, name=name)[0]


def _ple_bwd(dx2, gate_pre, pp, name):
    D = dx2.shape[1]

    def fn(rows, _, __):
        d, gp, ppv = rows
        s = _sigmoid(gp)
        return [d * ppv * s * (1.0 - s), d * s], [], []

    return _rowwise(fn, [dx2, gate_pre, pp], [], [(D, F32), (D, F32)], [], tile=256, name=name)


def _add(a, b, name):
    def fn(rows, _, __):
        return [rows[0] + rows[1]], [], []

    return _rowwise(fn, [a, b], [], [(a.shape[1], F32)], [], tile=256, name=name)[0]


def _axpy(alpha, a, b, name):
    def fn(rows, _, __):
        return [alpha * rows[0] + rows[1]], [], []

    return _rowwise(fn, [a, b], [], [(a.shape[1], F32)], [], tile=256, name=name)[0]


def _loss_head(y, target, name):
    D = y.shape[1]

    def fn(rows, _, __):
        e = rows[0] - rows[1]
        part = 0.5 * jnp.sum(jnp.sum(e * e, axis=1, keepdims=True), axis=0, keepdims=True) / D
        return [e / D], [jnp.broadcast_to(part, (SUBLANES, LANES))], []

    return _rowwise(fn, [y, target], [], [(D, F32)], [((SUBLANES, LANES), F32)], tile=256, name=name)


def _conv_fwd(h, w, n_cols, name):
    S = h.shape[0]
    T = _pick(S, 512, SUBLANES)
    CB = _pick(n_cols, 512)
    nt = S // T
    K = GDN_CONV

    def body(x_ref, halo_ref, w_ref, o_ref, buf):
        i = pl.program_id(1)
        buf[0:SUBLANES, :] = jnp.where(i > 0, halo_ref[...], 0.0)
        buf[SUBLANES:, :] = x_ref[...]
        acc = jnp.zeros((T, CB), F32)
        for k in range(K):
            acc = acc + w_ref[k:k + 1, :] * buf[pl.ds(SUBLANES - (K - 1) + k, T), :]
        o_ref[...] = acc

    return pl.pallas_call(
        body,
        grid=(n_cols // CB, nt),
        in_specs=[pl.BlockSpec((T, CB), lambda c, i: (i, c)),
                  pl.BlockSpec((SUBLANES, CB), lambda c, i: (jnp.maximum(i * (T // SUBLANES) - 1, 0), c)),
                  pl.BlockSpec((K, CB), lambda c, i: (0, c))],
        out_specs=pl.BlockSpec((T, CB), lambda c, i: (i, c)),
        out_shape=jax.ShapeDtypeStruct((S, n_cols), F32),
        scratch_shapes=[pltpu.VMEM((T + SUBLANES, CB), F32)],
        compiler_params=_params(("parallel", "parallel")),
        name=name,
    )(h, h, w)


def _conv_bwd(dc, h, w, h_col0, name):
    S, n_cols = dc.shape
    T = _pick(S, 512, SUBLANES)
    CB = _pick(n_cols, 512)
    nt = S // T
    K = GDN_CONV
    assert h_col0 % CB == 0
    hb = h_col0 // CB

    def body(d_ref, halo_ref, x_ref, w_ref, dx_ref, dw_ref, buf):
        i = pl.program_id(1)

        @pl.when(i == 0)
        def _():
            dw_ref[...] = jnp.zeros_like(dw_ref)

        buf[0:T, :] = d_ref[...]
        buf[T:, :] = jnp.where(i < nt - 1, halo_ref[...], 0.0)
        x = x_ref[...]
        acc = jnp.zeros((T, CB), F32)
        for k in range(K):
            shifted = buf[pl.ds(K - 1 - k, T), :]
            acc = acc + w_ref[k:k + 1, :] * shifted
            dw_ref[k:k + 1, :] += jnp.sum(shifted * x, axis=0, keepdims=True)
        dx_ref[...] = acc

    last = S // SUBLANES - 1
    return pl.pallas_call(
        body,
        grid=(n_cols // CB, nt),
        in_specs=[pl.BlockSpec((T, CB), lambda c, i: (i, c)),
                  pl.BlockSpec((SUBLANES, CB), lambda c, i: (jnp.minimum((i + 1) * (T // SUBLANES), last), c)),
                  pl.BlockSpec((T, CB), lambda c, i: (i, hb + c)),
                  pl.BlockSpec((K, CB), lambda c, i: (0, c))],
        out_specs=[pl.BlockSpec((T, CB), lambda c, i: (i, c)),
                   pl.BlockSpec((SUBLANES, CB), lambda c, i: (0, c))],
        out_shape=[jax.ShapeDtypeStruct((S, n_cols), F32), jax.ShapeDtypeStruct((SUBLANES, n_cols), F32)],
        scratch_shapes=[pltpu.VMEM((T + SUBLANES, CB), F32)],
        compiler_params=_params(("parallel", "arbitrary")),
        name=name,
    )(dc, dc, h, w)


def _neumann_inverse(L):
    C = L.shape[-1]
    eye = (_iota2((C, C), 0) == _iota2((C, C), 1)).astype(F32)
    X = eye - L
    P = L
    for _ in range(max(0, math.ceil(math.log2(C)) - 1)):
        P = _mm_3x(P, P)
        X = _mm_3x(X, eye + P)
    return X


@jax.custom_vjp
def _unit_lower_inverse(L):
    return _neumann_inverse(L)


def _unit_lower_inverse_bwd(T, dT):
    Tt = jnp.swapaxes(T, -1, -2)
    return (-_mm_3x(_mm_3x(Tt, dT), Tt),)


_unit_lower_inverse.defvjp(lambda L: (_neumann_inverse(L),) * 2, _unit_lower_inverse_bwd)


@jax.custom_vjp
def _known_inverse(L, T):
    return T


_known_inverse.defvjp(lambda L, T: (T, T), lambda T, dT: (*_unit_lower_inverse_bwd(T, dT), jnp.zeros_like(T)))


def _gdn_chunk(cq, ck, cv, zz, bcol, acol, alog, dtb, ng, state, inverse=None):
    G, C, dk = cq.shape
    q = _silu(cq)
    k = _silu(ck)
    v = _silu(cv)
    q = q * lax.rsqrt(jnp.sum(q * q, -1, keepdims=True) + RMS_EPS) * (dk ** -0.5)
    k = k * lax.rsqrt(jnp.sum(k * k, -1, keepdims=True) + RMS_EPS)
    beta = _sigmoid(bcol)
    g = -jnp.exp(alog) * _softplus(acol + dtb)

    row, col = _iota2((C, C), 0), _iota2((C, C), 1)
    causal, strict, eye = row >= col, row > col, row == col
    g_rows = jnp.swapaxes(jnp.broadcast_to(g, (G, C, C)), -1, -2)
    gc = jnp.sum(jnp.where(causal, g_rows, 0.0), axis=-1, keepdims=True)
    gcb = jnp.broadcast_to(gc, (G, C, C))
    g_last = jnp.sum(jnp.sum(jnp.where((row == C - 1) & (col == 0), gcb, 0.0), axis=-1, keepdims=True), axis=-2, keepdims=True)
    gc_rows = jnp.swapaxes(gcb, -1, -2)
    decay = jnp.exp(jnp.where(causal, gcb - gc_rows, NEG))

    kb = k * beta
    L = jnp.where(strict, _mm_nt(kb, k) * decay, 0.0)
    T = _unit_lower_inverse(L) if inverse is None else _known_inverse(L, inverse)
    u = _mm_3x(T, v * beta)
    w = _mm_3x(T, kb * jnp.exp(gc))
    a_qk = jnp.where(causal, _mm_nt(q, k) * decay, 0.0)
    q_dec = q * jnp.exp(gc)
    k_dec = k * jnp.exp(g_last - gc)
    v_new = u - _mm_nn(w, state)
    o = _mm_nn(q_dec, state) + _mm_nn(a_qk, v_new)
    new_state = state * jnp.exp(g_last) + _mm_tn(k_dec, v_new)
    y = o * lax.rsqrt(jnp.mean(o * o, -1, keepdims=True) + RMS_EPS) * ng * _silu(zz)
    return (y, new_state), T


GDN_HEADS_PER_STEP = 8


def _gdn_specs(H, dk, G, chunk_of=lambda n: n):
    C = GDN_CHUNK
    NG = H // G
    cq = pl.BlockSpec((C, G * dk), lambda n, h: (chunk_of(n), h))
    ck = pl.BlockSpec((C, G * dk), lambda n, h: (chunk_of(n), NG + h))
    cv = pl.BlockSpec((C, G * dk), lambda n, h: (chunk_of(n), 2 * NG + h))
    zz = pl.BlockSpec((C, G * dk), lambda n, h: (chunk_of(n), 3 * NG + h))
    ba = pl.BlockSpec((C, LANES), lambda n, h: (chunk_of(n), 4 * H * dk // LANES))
    return cq, ck, cv, zz, ba


def _gdn_step_args(cq_ref, ck_ref, cv_ref, z_ref, ba_ref, hp_ref, hg, G, H, dk):
    ba = ba_ref[...]

    def heads(ref):
        return jnp.stack([ref[:, g * dk:(g + 1) * dk] for g in range(G)])

    def picks(tile, offset):
        return jnp.stack([_lane_pick(tile, offset + hg * G + g) for g in range(G)])

    return (heads(cq_ref), heads(ck_ref), heads(cv_ref), heads(z_ref), picks(ba, 0), picks(ba, H),
            picks(hp_ref[0:1, :], 0), picks(hp_ref[1:2, :], 0))


def _gdn_scan_fwd(c, h, hp, ng, H, name):
    S = c.shape[0]
    dk = c.shape[1] // (3 * H)
    assert dk == LANES
    C = GDN_CHUNK
    NC = S // C

    G = min(GDN_HEADS_PER_STEP, H)
    assert H % G == 0
    NG = H // G

    def body(cq_ref, ck_ref, cv_ref, z_ref, ba_ref, hp_ref, ng_ref, y_ref, s_ref, t_ref, state):
        n, hg = pl.program_id(0), pl.program_id(1)
        heads = pl.ds(hg * G, G)

        @pl.when(n == 0)
        def _():
            state[heads] = jnp.zeros((G, dk, dk), F32)

        st = state[heads]
        s_ref[0] = st
        (y, new_state), inverse = _gdn_chunk(
            *_gdn_step_args(cq_ref, ck_ref, cv_ref, z_ref, ba_ref, hp_ref, hg, G, H, dk), ng_ref[...], st)
        t_ref[0] = inverse
        for g in range(G):
            y_ref[:, g * dk:(g + 1) * dk] = y[g].astype(y_ref.dtype)
        state[heads] = new_state

    cq, ck, cv, zz, ba = _gdn_specs(H, dk, G)
    return pl.pallas_call(
        body,
        grid=(NC, NG),
        in_specs=[cq, ck, cv, zz, ba, pl.BlockSpec((SUBLANES, LANES), lambda n, h: (0, 0)),
                  pl.BlockSpec((1, dk), lambda n, h: (0, 0))],
        out_specs=[pl.BlockSpec((C, G * dk), lambda n, h: (n, h)),
                   pl.BlockSpec((1, G, dk, dk), lambda n, h: (n, h, 0, 0)),
                   pl.BlockSpec((1, G, C, C), lambda n, h: (n, h, 0, 0))],
        out_shape=[jax.ShapeDtypeStruct((S, H * dk), MXU_DTYPE), jax.ShapeDtypeStruct((NC, H, dk, dk), F32),
                   jax.ShapeDtypeStruct((NC, H, C, C), F32)],
        scratch_shapes=[pltpu.VMEM((H, dk, dk), F32)],
        compiler_params=_params(("arbitrary", "arbitrary")),
        name=name,
    )(c, c, c, h, h, hp, ng)


def _gdn_scan_bwd(c, h, hp, ng, states, inverses, dy, H, name):
    S = c.shape[0]
    dk = c.shape[1] // (3 * H)
    C = GDN_CHUNK
    NC = S // C

    G = min(GDN_HEADS_PER_STEP, H)
    NG = H // G

    def body(cq_ref, ck_ref, cv_ref, z_ref, ba_ref, hp_ref, ng_ref, s_ref, t_ref, dy_ref,
             dq_ref, dk_ref, dv_ref, dz_ref, dba_ref, dhp_ref, dng_ref, dstate):
        n, hg = pl.program_id(0), pl.program_id(1)

        @pl.when((n == 0) & (hg == 0))
        def _():
            dhp_ref[...] = jnp.zeros_like(dhp_ref)
            dng_ref[...] = jnp.zeros_like(dng_ref)

        heads = pl.ds(hg * G, G)

        @pl.when(n == 0)
        def _():
            dstate[heads] = jnp.zeros((G, dk, dk), F32)

        args = (*_gdn_step_args(cq_ref, ck_ref, cv_ref, z_ref, ba_ref, hp_ref, hg, G, H, dk), ng_ref[...], s_ref[0])
        inverse = t_ref[0]
        _, vjp, _ = jax.vjp(lambda *a: _gdn_chunk(*a, inverse=inverse), *args, has_aux=True)
        dy = jnp.stack([dy_ref[:, g * dk:(g + 1) * dk] for g in range(G)])
        dcq, dck, dcv, dzz, dbc, dac, dal, ddt, dng, dst = vjp((dy, dstate[heads]))
        dstate[heads] = dst
        dba = jnp.zeros((C, LANES), F32)
        dhp0 = jnp.zeros((1, LANES), F32)
        dhp1 = jnp.zeros((1, LANES), F32)
        for g in range(G):
            hd = hg * G + g
            sl = slice(g * dk, (g + 1) * dk)
            dq_ref[:, sl] = dcq[g]
            dk_ref[:, sl] = dck[g]
            dv_ref[:, sl] = dcv[g]
            dz_ref[:, sl] = dzz[g]
            dba = dba + _lane_put(dbc[g], hd) + _lane_put(dac[g], H + hd)
            dhp0 = dhp0 + _lane_put(dal[g], hd)
            dhp1 = dhp1 + _lane_put(ddt[g], hd)

        @pl.when(hg == 0)
        def _():
            dba_ref[...] = dba

        @pl.when(hg > 0)
        def _():
            dba_ref[...] += dba

        dhp_ref[0:1, :] += dhp0
        dhp_ref[1:2, :] += dhp1
        dng_ref[...] += dng

    rev = lambda n: NC - 1 - n
    blk = pl.BlockSpec
    in_specs = [*_gdn_specs(H, dk, G, rev),
                blk((SUBLANES, LANES), lambda n, h: (0, 0)), blk((1, dk), lambda n, h: (0, 0)),
                blk((1, G, dk, dk), lambda n, h: (rev(n), h, 0, 0)), blk((1, G, C, C), lambda n, h: (rev(n), h, 0, 0)),
                blk((C, G * dk), lambda n, h: (rev(n), h))]
    out_specs = [blk((C, G * dk), lambda n, h: (rev(n), h))] * 4 + [
        blk((C, LANES), lambda n, h: (rev(n), 0)),
        blk((SUBLANES, LANES), lambda n, h: (0, 0)), blk((1, dk), lambda n, h: (0, 0))]
    out_shape = [jax.ShapeDtypeStruct((S, H * dk), F32)] * 4 + [
        jax.ShapeDtypeStruct((S, LANES), F32), jax.ShapeDtypeStruct((SUBLANES, LANES), F32),
        jax.ShapeDtypeStruct((1, dk), F32)]
    return pl.pallas_call(
        body,
        grid=(NC, NG),
        in_specs=in_specs,
        out_specs=out_specs,
        out_shape=out_shape,
        scratch_shapes=[pltpu.VMEM((H, dk, dk), F32)],
        compiler_params=_params(("arbitrary", "arbitrary")),
        name=name,
    )(c, c, c, h, h, hp, ng, states, inverses, dy)


def _pair_rms(x, gain, dh):
    first = _iota2(x.shape, 1) < dh
    sq = x * x
    ss_a = jnp.sum(jnp.where(first, sq, 0.0), axis=1, keepdims=True)
    ss_b = jnp.sum(jnp.where(first, 0.0, sq), axis=1, keepdims=True)
    inv = jnp.where(first, lax.rsqrt(ss_a / dh + RMS_EPS), lax.rsqrt(ss_b / dh + RMS_EPS))
    return x * inv * gain


def _log_sigmoid(x):
    return jnp.minimum(x, 0.0) - jnp.log(1.0 + jnp.exp(-jnp.abs(x)))


def _cum_fn(fr, bf, carry):
    T = fr.shape[0]
    tril = (_iota2((T, T), 0) >= _iota2((T, T), 1)).astype(F32)
    c = _mm_hi(tril, _log_sigmoid(fr + bf)) + carry
    last = jnp.sum(jnp.where(_iota2(c.shape, 0) == T - 1, c, 0.0), axis=0, keepdims=True)
    return c, last


def _fox_prep_fwd(h, bf, qg, kg, H, dh, name):
    S = h.shape[0]
    W = H * dh
    assert 2 * dh == LANES and H <= LANES
    T = _pick(S, 256, LANES)
    nt = S // T
    npair = H // 2

    def body(hq_ref, hk_ref, f_ref, bf_ref, qg_ref, kg_ref, q_ref, k_ref, ccol_ref, crow_ref, carry):
        i = pl.program_id(0)

        @pl.when(i == 0)
        def _():
            carry[...] = jnp.zeros_like(carry)

        for p in range(npair):
            sl = slice(p * LANES, (p + 1) * LANES)
            q_ref[:, sl] = (_pair_rms(hq_ref[:, sl], qg_ref[...], dh) * (dh ** -0.5)).astype(q_ref.dtype)
            k_ref[:, sl] = _pair_rms(hk_ref[:, sl], kg_ref[...], dh).astype(k_ref.dtype)
        c, last = _cum_fn(f_ref[...], bf_ref[...], carry[...])
        carry[...] = last
        ct = c.T
        for hh in range(H):
            ccol_ref[hh] = c[:, hh:hh + 1]
            crow_ref[hh] = ct[hh:hh + 1, :]

    return pl.pallas_call(
        body,
        grid=(nt,),
        in_specs=[pl.BlockSpec((T, W), lambda i: (i, 0)), pl.BlockSpec((T, W), lambda i: (i, 1)),
                  pl.BlockSpec((T, LANES), lambda i: (i, 4 * W // LANES)),
                  pl.BlockSpec((1, LANES), lambda i: (0, 0)), pl.BlockSpec((1, LANES), lambda i: (0, 0)),
                  pl.BlockSpec((1, LANES), lambda i: (0, 0))],
        out_specs=[pl.BlockSpec((T, W), lambda i: (i, 0)), pl.BlockSpec((T, W), lambda i: (i, 0)),
                   pl.BlockSpec((H, T, 1), lambda i: (0, i, 0)), pl.BlockSpec((H, 1, T), lambda i: (0, 0, i))],
        out_shape=[jax.ShapeDtypeStruct((S, W), MXU_DTYPE), jax.ShapeDtypeStruct((S, W), MXU_DTYPE),
                   jax.ShapeDtypeStruct((H, S, 1), F32), jax.ShapeDtypeStruct((H, 1, S), F32)],
        scratch_shapes=[pltpu.VMEM((1, LANES), F32)],
        compiler_params=_params(("arbitrary",)),
        name=name,
    )(h, h, h, bf, qg, kg)


def _fox_prep_bwd(h, bf, qg, kg, dq, dk, dcrow, H, dh, name):
    S = h.shape[0]
    W = H * dh
    T = _pick(S, 256, LANES)
    nt = S // T
    npair = H // 2

    def body(hq_ref, hk_ref, f_ref, bf_ref, qg_ref, kg_ref, dq_ref, dk_ref, dcrow_ref,
             dhq_ref, dhk_ref, df_ref, dbf_ref, dqg_ref, dkg_ref, dcarry, dct):
        i = pl.program_id(0)

        @pl.when(i == 0)
        def _():
            dcarry[...] = jnp.zeros_like(dcarry)
            dbf_ref[...] = jnp.zeros_like(dbf_ref)
            dqg_ref[...] = jnp.zeros_like(dqg_ref)
            dkg_ref[...] = jnp.zeros_like(dkg_ref)

        for p in range(npair):
            sl = slice(p * LANES, (p + 1) * LANES)
            _, vjp = jax.vjp(lambda x, g: _pair_rms(x, g, dh) * (dh ** -0.5), hq_ref[:, sl], qg_ref[...])
            dx, dg = vjp(dq_ref[:, sl])
            dhq_ref[:, sl] = dx
            dqg_ref[...] += dg
            _, vjp = jax.vjp(lambda x, g: _pair_rms(x, g, dh), hk_ref[:, sl], kg_ref[...])
            dx, dg = vjp(dk_ref[:, sl])
            dhk_ref[:, sl] = dx
            dkg_ref[...] += dg
        dct[...] = jnp.zeros_like(dct)
        for hh in range(H):
            dct[hh:hh + 1, :] = dcrow_ref[hh]
        _, vjp = jax.vjp(lambda f, b: _cum_fn(f, b, jnp.zeros((1, LANES), F32)), f_ref[...], bf_ref[...])
        dc = dct[...].T
        df, dbf = vjp((dc, dcarry[...]))
        df_ref[...] = df
        dbf_ref[...] += dbf
        dcarry[...] = dcarry[...] + jnp.sum(dc, axis=0, keepdims=True)

    rv = lambda i: nt - 1 - i
    return pl.pallas_call(
        body,
        grid=(nt,),
        in_specs=[pl.BlockSpec((T, W), lambda i: (rv(i), 0)), pl.BlockSpec((T, W), lambda i: (rv(i), 1)),
                  pl.BlockSpec((T, LANES), lambda i: (rv(i), 4 * W // LANES)),
                  pl.BlockSpec((1, LANES), lambda i: (0, 0)), pl.BlockSpec((1, LANES), lambda i: (0, 0)),
                  pl.BlockSpec((1, LANES), lambda i: (0, 0)),
                  pl.BlockSpec((T, W), lambda i: (rv(i), 0)), pl.BlockSpec((T, W), lambda i: (rv(i), 0)),
                  pl.BlockSpec((H, 1, T), lambda i: (0, 0, rv(i)))],
        out_specs=[pl.BlockSpec((T, W), lambda i: (rv(i), 0)), pl.BlockSpec((T, W), lambda i: (rv(i), 0)),
                   pl.BlockSpec((T, LANES), lambda i: (rv(i), 0)),
                   pl.BlockSpec((1, LANES), lambda i: (0, 0)), pl.BlockSpec((1, LANES), lambda i: (0, 0)),
                   pl.BlockSpec((1, LANES), lambda i: (0, 0))],
        out_shape=[jax.ShapeDtypeStruct((S, W), F32), jax.ShapeDtypeStruct((S, W), F32),
                   jax.ShapeDtypeStruct((S, LANES), F32)] + [jax.ShapeDtypeStruct((1, LANES), F32)] * 3,
        scratch_shapes=[pltpu.VMEM((1, LANES), F32), pltpu.VMEM((LANES, T), F32)],
        compiler_params=_params(("arbitrary",)),
        name=name,
    )(h, h, h, bf, qg, kg, dq, dk, dcrow)


def _head_masks(dh):
    first = _iota2((1, LANES), 1) < dh
    return first, jnp.logical_not(first)


def _per_head(tile, dh):
    return jnp.stack([jnp.where(mask, tile, 0) for mask in _head_masks(dh)])


def _both(tile):
    return jnp.stack([tile, tile])


def _rows2(x):
    return x.reshape(2 * x.shape[1], x.shape[2])


def _stacked(tile, dh):
    return _rows2(_per_head(tile, dh))


FLASH_SUB = 128


def _flash_scores(q_m, k, crow, row0, diagonal):
    s = _dot(q_m, k, _NT) - crow
    if diagonal:
        s = jnp.where(_iota2(s.shape, s.ndim - 1) <= row0 + _iota2(s.shape, s.ndim - 2), s, NEG)
    return s


def _flash_tiles(S, wide=True):
    tk = _pick(S, 512, LANES)
    tq = 2 * tk if wide and S % (2 * tk) == 0 else tk
    return tq, tk


def _causal_blocks(S, tq, tk, by_query):
    ratio = tq // tk
    if by_query:
        pairs = [(i, j) for i in range(S // tq) for j in range(ratio * (i + 1))]
    else:
        pairs = [(i, j) for j in range(S // tk) for i in range(j // ratio, S // tq)]
    return (jnp.asarray([a for a, _ in pairs], jnp.int32), jnp.asarray([b for _, b in pairs], jnp.int32))


def _flash_blocks(i, j, tq, tk, fn, sub=FLASH_SUB):
    sub = min(sub, tq)
    ratio = tq // tk
    offset = j - ratio * i

    @pl.when(offset < 0)
    def _():
        for r in range(tq // sub):
            fn(slice(r * sub, (r + 1) * sub), 0, False)

    for d in range(ratio):
        @pl.when(offset == d)
        def _():
            for r in range(tq // sub):
                first_row, first_key = r * sub, d * tk
                if first_key > first_row + sub - 1:
                    continue
                fn(slice(first_row, first_row + sub), first_row - first_key, first_key + tk - 1 > first_row)


def _flash_fwd(q, k, h, ccol, crow, H, dh, name):
    S, W = q.shape
    tq, tk = _flash_tiles(S)
    nq, ratio = S // tq, tq // tk
    npair = H // 2
    vblk = 2 * W // LANES

    def body(ii_ref, jj_ref, q_ref, k_ref, v_ref, ccol_ref, crow_ref, o_ref, lse_ref, m_s, l_s, acc_s):
        i, j = ii_ref[pl.program_id(1)], jj_ref[pl.program_id(1)]

        @pl.when(j == 0)
        def _():
            m_s[...] = jnp.full_like(m_s, NEG)
            l_s[...] = jnp.zeros_like(l_s)
            acc_s[...] = jnp.zeros_like(acc_s)

        def tile(rows, row0, diagonal):
            s = _flash_scores(_per_head(q_ref[rows, :], dh), _both(k_ref[...]), crow_ref[...], row0, diagonal)
            s = s + ccol_ref[:, rows, :]
            m_old = m_s[:, rows, :]
            m_new = jnp.maximum(m_old, jnp.max(s, axis=-1, keepdims=True))
            alpha = jnp.exp(m_old - m_new)
            p = jnp.exp(s - m_new)
            l_s[:, rows, :] = alpha * l_s[:, rows, :] + jnp.sum(p, axis=-1, keepdims=True)
            first, _ = _head_masks(dh)
            acc_s[rows, :] = jnp.where(first, alpha[0], alpha[1]) * acc_s[rows, :] + _dot(
                jnp.concatenate([p[0], p[1]], axis=1), _stacked(v_ref[...].astype(MXU_DTYPE), dh), _NN)
            m_s[:, rows, :] = m_new

        _flash_blocks(i, j, tq, tk, tile)

        @pl.when(j == ratio * (i + 1) - 1)
        def _():
            first, _ = _head_masks(dh)
            o_ref[...] = acc_s[...] / jnp.where(first, l_s[0], l_s[1])
            for a in range(2):
                lse_ref[a] = m_s[a] + jnp.log(l_s[a])

    ii, jj = _causal_blocks(S, tq, tk, by_query=True)
    return pl.pallas_call(
        body,
        grid_spec=pltpu.PrefetchScalarGridSpec(
            num_scalar_prefetch=2,
            grid=(npair, len(ii)),
            in_specs=[pl.BlockSpec((tq, LANES), lambda p, t, ii, jj: (ii[t], p)),
                      pl.BlockSpec((tk, LANES), lambda p, t, ii, jj: (jj[t], p)),
                      pl.BlockSpec((tk, LANES), lambda p, t, ii, jj: (jj[t], vblk + p)),
                      pl.BlockSpec((2, tq, 1), lambda p, t, ii, jj: (p, ii[t], 0)),
                      pl.BlockSpec((2, 1, tk), lambda p, t, ii, jj: (p, 0, jj[t]))],
            out_specs=[pl.BlockSpec((tq, LANES), lambda p, t, ii, jj: (ii[t], p)),
                       pl.BlockSpec((2, tq, 1), lambda p, t, ii, jj: (p, ii[t], 0))],
            scratch_shapes=[pltpu.VMEM((2, tq, 1), F32), pltpu.VMEM((2, tq, 1), F32), pltpu.VMEM((tq, LANES), F32)]),
        out_shape=[jax.ShapeDtypeStruct((S, W), F32), jax.ShapeDtypeStruct((H, S, 1), F32)],
        compiler_params=_params(("parallel", "arbitrary")),
        name=name,
    )(ii, jj, q, k, h, ccol, crow)


def _flash_bwd_kv(q, k, h, ccol, crow, lse, delta, resid, do, H, dh, name):
    S, W = q.shape
    tq, tk = _flash_tiles(S)
    nq, ratio = S // tq, tq // tk
    npair = H // 2
    vblk = 2 * W // LANES

    def body(ii_ref, jj_ref, q_ref, k_ref, v_ref, ccol_ref, crow_ref, lse_ref, dl_ref, rs_ref, do_ref, dk_ref, dv_ref, dcr_ref,
             dkt_s, dvt_s):
        i, j = ii_ref[pl.program_id(1)], jj_ref[pl.program_id(1)]

        @pl.when(i == j // ratio)
        def _():
            dkt_s[...] = jnp.zeros_like(dkt_s)
            dvt_s[...] = jnp.zeros_like(dvt_s)
            dcr_ref[...] = jnp.zeros_like(dcr_ref)

        def tile(rows, row0, diagonal):
            kv, vv = _both(k_ref[...]), _both(v_ref[...].astype(MXU_DTYPE))
            q_m = _per_head(q_ref[rows, :], dh)
            do_m = _per_head(do_ref[rows, :].astype(MXU_DTYPE), dh)
            s = _flash_scores(q_m, kv, crow_ref[...], row0, diagonal)
            p = jnp.exp(s + (ccol_ref[:, rows, :] - lse_ref[:, rows, :]))
            ds = p * (_dot(do_m, vv, _NT) - (dl_ref[:, rows, :] + rs_ref[:, rows, :]))
            dvt_s[...] += _dot(_rows2(do_m), _rows2(p), _TN)
            dkt_s[...] += _dot(_rows2(q_m), _rows2(ds), _TN)
            dcr_ref[...] -= jnp.sum(ds, axis=1, keepdims=True)

        _flash_blocks(i, j, tq, tk, tile, sub=2 * FLASH_SUB)

        @pl.when(i == nq - 1)
        def _():
            dk_ref[...] = dkt_s[...].T
            dv_ref[...] = dvt_s[...].T

    ii, jj = _causal_blocks(S, tq, tk, by_query=False)
    qrow = pl.BlockSpec((2, tq, 1), lambda p, t, ii, jj: (p, ii[t], 0))
    return pl.pallas_call(
        body,
        grid_spec=pltpu.PrefetchScalarGridSpec(
            num_scalar_prefetch=2,
            grid=(npair, len(ii)),
            in_specs=[pl.BlockSpec((tq, LANES), lambda p, t, ii, jj: (ii[t], p)),
                      pl.BlockSpec((tk, LANES), lambda p, t, ii, jj: (jj[t], p)),
                      pl.BlockSpec((tk, LANES), lambda p, t, ii, jj: (jj[t], vblk + p)),
                      qrow,
                      pl.BlockSpec((2, 1, tk), lambda p, t, ii, jj: (p, 0, jj[t])),
                      qrow, qrow, qrow,
                      pl.BlockSpec((tq, LANES), lambda p, t, ii, jj: (ii[t], p))],
            out_specs=[pl.BlockSpec((tk, LANES), lambda p, t, ii, jj: (jj[t], p)),
                       pl.BlockSpec((tk, LANES), lambda p, t, ii, jj: (jj[t], p)),
                       pl.BlockSpec((2, 1, tk), lambda p, t, ii, jj: (p, 0, jj[t]))],
            scratch_shapes=[pltpu.VMEM((LANES, tk), F32), pltpu.VMEM((LANES, tk), F32)]),
        out_shape=[jax.ShapeDtypeStruct((S, W), F32), jax.ShapeDtypeStruct((S, W), F32),
                   jax.ShapeDtypeStruct((H, 1, S), F32)],
        compiler_params=_params(("parallel", "arbitrary")),
        name=name,
    )(ii, jj, q, k, h, ccol, crow, lse, delta, resid, do)


def _flash_bwd_q(q, k, h, ccol, crow, lse, delta, do, H, dh, name):
    S, W = q.shape
    tq, tk = _flash_tiles(S, wide=False)
    npair = H // 2
    vblk = 2 * W // LANES

    def body(ii_ref, jj_ref, q_ref, k_ref, v_ref, ccol_ref, crow_ref, lse_ref, dl_ref, do_ref, dq_ref, rs_ref):
        i, j = ii_ref[pl.program_id(1)], jj_ref[pl.program_id(1)]

        @pl.when(j == 0)
        def _():
            dq_ref[...] = jnp.zeros_like(dq_ref)
            rs_ref[...] = jnp.zeros_like(rs_ref)

        def tile(rows, row0, diagonal):
            kv, vv = k_ref[...], _both(v_ref[...].astype(MXU_DTYPE))
            s = _flash_scores(_per_head(q_ref[rows, :], dh), _both(kv), crow_ref[...], row0, diagonal)
            p = jnp.exp(s + (ccol_ref[:, rows, :] - lse_ref[:, rows, :]))
            ds = p * (_dot(_per_head(do_ref[rows, :].astype(MXU_DTYPE), dh), vv, _NT) - dl_ref[:, rows, :])
            dq = _dot(ds, _per_head(kv, dh), _NN)
            dq_ref[rows, :] += dq[0] + dq[1]
            rs_ref[:, rows, :] += jnp.sum(ds, axis=-1, keepdims=True)

        _flash_blocks(i, j, tq, tk, tile, sub=tk)

    ii, jj = _causal_blocks(S, tq, tk, by_query=True)
    qrow = pl.BlockSpec((2, tq, 1), lambda p, t, ii, jj: (p, ii[t], 0))
    return pl.pallas_call(
        body,
        grid_spec=pltpu.PrefetchScalarGridSpec(
            num_scalar_prefetch=2,
            grid=(npair, len(ii)),
            in_specs=[pl.BlockSpec((tq, LANES), lambda p, t, ii, jj: (ii[t], p)),
                      pl.BlockSpec((tk, LANES), lambda p, t, ii, jj: (jj[t], p)),
                      pl.BlockSpec((tk, LANES), lambda p, t, ii, jj: (jj[t], vblk + p)),
                      qrow,
                      pl.BlockSpec((2, 1, tk), lambda p, t, ii, jj: (p, 0, jj[t])),
                      qrow, qrow,
                      pl.BlockSpec((tq, LANES), lambda p, t, ii, jj: (ii[t], p))],
            out_specs=[pl.BlockSpec((tq, LANES), lambda p, t, ii, jj: (ii[t], p)), qrow]),
        out_shape=[jax.ShapeDtypeStruct((S, W), F32), jax.ShapeDtypeStruct((H, S, 1), F32)],
        compiler_params=_params(("parallel", "arbitrary")),
        name=name,
    )(ii, jj, q, k, h, ccol, crow, lse, delta, do)


def _fox_gate_fwd(o, h, W, name):
    def fn(rows, _, __):
        return [rows[0] * _silu(rows[1])], [], []

    return _rowwise(fn, [o, (h, W, 3)], [], [(W, MXU_DTYPE)], [], tile=256, name=name)[0]


def _fox_gate_bwd(o, h, dog, H, dh, name):
    S, W = o.shape
    T = _pick(S, 256, SUBLANES)

    def body(o_ref, z_ref, d_ref, do_ref, dz_ref, dl_ref):
        ov, zv, dv = o_ref[...], z_ref[...], d_ref[...]
        sg = _sigmoid(zv)
        do = dv * zv * sg
        do_ref[...] = do
        dz_ref[...] = dv * ov * sg * (1.0 + zv * (1.0 - sg))
        prod = do * ov
        for p in range(H // 2):
            blk = prod[:, p * LANES:(p + 1) * LANES]
            first = _iota2(blk.shape, 1) < dh
            dl_ref[2 * p] = jnp.sum(jnp.where(first, blk, 0.0), axis=1, keepdims=True)
            dl_ref[2 * p + 1] = jnp.sum(jnp.where(first, 0.0, blk), axis=1, keepdims=True)

    return pl.pallas_call(
        body,
        grid=(S // T,),
        in_specs=[pl.BlockSpec((T, W), lambda i: (i, 0)), pl.BlockSpec((T, W), lambda i: (i, 3)),
                  pl.BlockSpec((T, W), lambda i: (i, 0))],
        out_specs=[pl.BlockSpec((T, W), lambda i: (i, 0)), pl.BlockSpec((T, W), lambda i: (i, 0)),
                   pl.BlockSpec((H, T, 1), lambda i: (0, i, 0))],
        out_shape=[jax.ShapeDtypeStruct((S, W), F32), jax.ShapeDtypeStruct((S, W), F32),
                   jax.ShapeDtypeStruct((H, S, 1), F32)],
        compiler_params=_params(("parallel",)),
        name=name,
    )(o, h, dog)


def _gdn_layer_fwd(x, w, tag):
    H = w["H"]
    qk = H * LANES
    h = _matmul(x, w["w_in"], "nn", name=f"{tag}_in")
    c = _conv_fwd(h, w["conv"], 3 * qk, name=f"{tag}_conv")
    og, states, inverses = _gdn_scan_fwd(c, h, w["hp"], w["ng"], H, name=f"{tag}_scan")
    y = _matmul(og, w["w_out"], "nn", name=f"{tag}_out")
    return y, (x, h, c, states, inverses, og)


def _gdn_layer_bwd(dy, res, w, tag):
    x, h, c, states, inverses, og = res
    H = w["H"]
    qk = H * LANES
    dog = _matmul(dy, w["w_out"], "nt", name=f"{tag}_out_dx")
    dw_out = _matmul(og, dy, "tn", name=f"{tag}_out_dw")
    dq, dk, dv, dz, dba, dhp, dng = _gdn_scan_bwd(c, h, w["hp"], w["ng"], states, inverses, dog, H, name=f"{tag}_scan_bwd")
    dh_parts, dconv = [], []
    for part, d in enumerate((dq, dk, dv)):
        dh_p, dw_p = _conv_bwd(d, h, w["conv"][:, part * qk:(part + 1) * qk], part * qk, name=f"{tag}_conv_bwd{part}")
        dh_parts.append(dh_p)
        dconv.append(dw_p[:GDN_CONV])
    dh = jnp.concatenate(dh_parts + [dz, dba], axis=1)
    dx = _matmul(dh, w["w_in"], "nt", name=f"{tag}_in_dx")
    dw_in = _matmul(x, dh, "tn", name=f"{tag}_in_dw")
    grads = {"w_in": dw_in, "w_out": dw_out, "conv": jnp.concatenate(dconv, axis=1),
             "a_log": dhp[0, :H], "dt_bias": dhp[1, :H], "norm_g": dng[0]}
    return dx, grads


def _fox_layer_fwd(x, w, tag):
    H, dh = w["H"], w["dh"]
    W = H * dh
    h = _matmul(x, w["w_in"], "nn", name=f"{tag}_in")
    q, k, ccol, crow = _fox_prep_fwd(h, w["bf"], w["qg"], w["kg"], H, dh, name=f"{tag}_prep")
    o, lse = _flash_fwd(q, k, h, ccol, crow, H, dh, name=f"{tag}_flash")
    og = _fox_gate_fwd(o, h, W, name=f"{tag}_gate")
    y = _matmul(og, w["w_out"], "nn", name=f"{tag}_out")
    return y, (x, h, q, k, ccol, crow, o, lse, og)


def _fox_layer_bwd(dy, res, w, tag):
    x, h, q, k, ccol, crow, o, lse, og = res
    H, dh = w["H"], w["dh"]
    dog = _matmul(dy, w["w_out"], "nt", name=f"{tag}_out_dx")
    dw_out = _matmul(og, dy, "tn", name=f"{tag}_out_dw")
    do, dz, delta = _fox_gate_bwd(o, h, dog, H, dh, name=f"{tag}_gate_bwd")
    dqq, resid = _flash_bwd_q(q, k, h, ccol, crow, lse, delta, do, H, dh, name=f"{tag}_flash_bwd_q")
    dkk, dvv, dcrow = _flash_bwd_kv(q, k, h, ccol, crow, lse, delta, resid, do, H, dh, name=f"{tag}_flash_bwd_kv")
    dhq, dhk, df, dbf, dqg, dkg = _fox_prep_bwd(h, w["bf"], w["qg"], w["kg"], dqq, dkk, dcrow, H, dh, name=f"{tag}_prep_bwd")
    dhh = jnp.concatenate([dhq, dhk, dvv, dz, df], axis=1)
    dx = _matmul(dhh, w["w_in"], "nt", name=f"{tag}_in_dx")
    dw_in = _matmul(x, dhh, "tn", name=f"{tag}_in_dw")
    grads = {"w_in": dw_in, "w_out": dw_out, "b_f": dbf[0, :H],
             "q_norm_g": dqg[0, :dh] + dqg[0, dh:], "k_norm_g": dkg[0, :dh] + dkg[0, dh:]}
    return dx, grads


def _pad_cols(w, n):
    return jnp.pad(w, ((0, 0), (0, n - w.shape[1])))


def _build_layers(full, small):
    depth = small["ln_g"].shape[0]
    gh = small["gdn_a_log"].shape[1]
    fh, dh = small["fox_b_f"].shape[1], small["fox_q_norm_g"].shape[1]
    layers = []
    for i in range(depth):
        j = i // 2
        w = {"ln_g": small["ln_g"][i][None], "ln_b": small["ln_b"][i][None],
             "w_gate": full["ple_w_gate"][i], "w_proj": full["ple_w_proj"][i]}
        if i % 2 == 0:
            hp = jnp.zeros((SUBLANES, LANES), F32).at[0, :gh].set(small["gdn_a_log"][j]).at[1, :gh].set(small["gdn_dt_bias"][j])
            w.update(kind="gdn", H=gh, w_in=_pad_cols(full["gdn_w_in"][j], 4 * gh * LANES + LANES),
                     conv=full["gdn_conv_w"][j], hp=hp, ng=small["gdn_norm_g"][j][None], w_out=full["gdn_w_out"][j])
        else:
            bf = jnp.zeros((1, LANES), F32).at[0, :fh].set(small["fox_b_f"][j])
            w.update(kind="fox", H=fh, dh=dh, w_in=_pad_cols(full["fox_w_in"][j], 4 * fh * dh + LANES), bf=bf,
                     qg=jnp.tile(small["fox_q_norm_g"][j], 2)[None], kg=jnp.tile(small["fox_k_norm_g"][j], 2)[None],
                     w_out=full["fox_w_out"][j])
        layers.append(w)
    return layers


def _local_step(x, p, target, layers):
    depth = len(layers)
    alpha = (2 * depth) ** 0.25
    saved = []
    for i, w in enumerate(layers):
        tag = f"l{i}"
        if w["kind"] == "gdn":
            y, res = _gdn_layer_fwd(x, w, tag)
        else:
            y, res = _fox_layer_fwd(x, w, tag)
        x1 = _ln_fwd(x, y, w["ln_g"], w["ln_b"], alpha, name=f"{tag}_ln")
        gate_pre = _matmul(x1, w["w_gate"], "nn", name=f"{tag}_gate_mm")
        pp = _matmul(p[i], w["w_proj"], "nn", name=f"{tag}_proj_mm")
        x2 = _ple_fwd(x1, gate_pre, pp, name=f"{tag}_ple")
        saved.append((res, x, y, x1, gate_pre, pp))
        x = x2
    dx, loss_tile = _loss_head(x, target, name="loss_head")
    grads = [None] * depth
    for i in reversed(range(depth)):
        w = layers[i]
        tag = f"l{i}"
        res, xin, y, x1, gate_pre, pp = saved[i]
        dgp, dpp = _ple_bwd(dx, gate_pre, pp, name=f"{tag}_ple_bwd")
        dx1 = _add(dx, _matmul(dgp, w["w_gate"], "nt", name=f"{tag}_gate_dx"), name=f"{tag}_dx1")
        dw_gate = _matmul(x1, dgp, "tn", name=f"{tag}_gate_dw")
        dw_proj = _matmul(p[i], dpp, "tn", name=f"{tag}_proj_dw")
        dy, dg, db = _ln_bwd(xin, y, w["ln_g"], w["ln_b"], dx1, alpha, name=f"{tag}_ln_bwd")
        if w["kind"] == "gdn":
            dxm, g = _gdn_layer_bwd(dy, res, w, tag)
        else:
            dxm, g = _fox_layer_bwd(dy, res, w, tag)
        dx = _axpy(alpha, dy, dxm, name=f"{tag}_dx")
        g.update({"w_gate": dw_gate, "w_proj": dw_proj, "ln_g": dg[0], "ln_b": db[0]})
        grads[i] = g
    return loss_tile, dx, grads


MESH_ID = pl.DeviceIdType.MESH
HBM_SPEC = pl.BlockSpec(memory_space=pl.ANY)
PACK_COLS = 1024
PACK_ROWS = 256


def _all_gather(shards, name):
    nt = len(shards)

    def body(*refs):
        x_refs, out_refs = refs[:nt], refs[nt:2 * nt]
        send_sems, recv_sems, local_sems = refs[2 * nt:]
        x, y, c = lax.axis_index("x"), lax.axis_index("y"), lax.axis_index("c")
        me, sibling = (x, y, c), (x, y, 1 - c)
        chips = [(1 - x, y), (x, 1 - y), (1 - x, 1 - y)]

        def slot(t, px, py, pc):
            return out_refs[t].at[4 * px + 2 * py + pc]

        def copy(k, t, block, to, src=None):
            return pltpu.make_async_remote_copy(
                src_ref=slot(t, *block) if src is None else src, dst_ref=slot(t, *block),
                send_sem=send_sems.at[k, t], recv_sem=recv_sems.at[k, t], device_id=to, device_id_type=MESH_ID)

        every = range(nt)
        mine = [pltpu.make_async_copy(x_refs[t], slot(t, *me), local_sems.at[t]) for t in every]
        for cp in mine:
            cp.start()
        first = [copy(0, t, me, sibling, src=x_refs[t]) for t in every]
        first += [copy(1 + j, t, me, (*chip, c), src=x_refs[t]) for j, chip in enumerate(chips) for t in every]
        for cp in first:
            cp.start()
        passed = []
        for j, chip in enumerate(chips):
            for t in every:
                copy(1 + j, t, (*chip, c), me).wait_recv()
                passed.append(copy(4 + j, t, (*chip, c), sibling))
                passed[-1].start()
        for t in every:
            copy(0, t, sibling, me).wait_recv()
        for j, chip in enumerate(chips):
            for t in every:
                copy(4 + j, t, (*chip, 1 - c), me).wait_recv()
        for cp in first + passed:
            cp.wait_send()
        for cp in mine:
            cp.wait()

    return pl.pallas_call(
        body,
        out_shape=[jax.ShapeDtypeStruct((N_DEV, *s.shape), s.dtype) for s in shards],
        in_specs=[HBM_SPEC] * nt,
        out_specs=[HBM_SPEC] * nt,
        scratch_shapes=[pltpu.SemaphoreType.DMA((7, nt)), pltpu.SemaphoreType.DMA((7, nt)), pltpu.SemaphoreType.DMA((nt,))],
        name=name,
    )(*shards)


N_CHIPS = 4


def _sibling_swap(slabs, name):
    nt = len(slabs)

    def body(*refs):
        g_refs, out_refs = refs[:nt], refs[nt:2 * nt]
        send_sems, recv_sems = refs[2 * nt:]
        x, y, c = lax.axis_index("x"), lax.axis_index("y"), lax.axis_index("c")
        copies = [pltpu.make_async_remote_copy(src_ref=g_refs[t].at[1 - c], dst_ref=out_refs[t], send_sem=send_sems.at[t],
                                               recv_sem=recv_sems.at[t], device_id=(x, y, 1 - c), device_id_type=MESH_ID)
                  for t in range(nt)]
        for cp in copies:
            cp.start()
        for cp in copies:
            cp.wait()

    return pl.pallas_call(
        body,
        out_shape=[jax.ShapeDtypeStruct(s.shape[1:], s.dtype) for s in slabs],
        in_specs=[HBM_SPEC] * nt,
        out_specs=[HBM_SPEC] * nt,
        scratch_shapes=[pltpu.SemaphoreType.DMA((nt,)), pltpu.SemaphoreType.DMA((nt,))],
        name=name,
    )(*slabs)


def _add_own_half(core, slabs, received, name):
    shape = received.shape
    R, C = math.prod(shape[:-1]), shape[-1]
    tr = _pick(R, 512, SUBLANES)

    def body(core_ref, mine_ref, got_ref, o_ref):
        o_ref[...] = mine_ref[0] + got_ref[...]

    return pl.pallas_call(
        body,
        grid_spec=pltpu.PrefetchScalarGridSpec(
            num_scalar_prefetch=1,
            grid=(R // tr,),
            in_specs=[pl.BlockSpec((1, tr, C), lambda i, core: (core[0], i, 0)), pl.BlockSpec((tr, C), lambda i, core: (i, 0))],
            out_specs=pl.BlockSpec((tr, C), lambda i, core: (i, 0))),
        out_shape=jax.ShapeDtypeStruct((R, C), F32),
        compiler_params=_params(("parallel",)),
        name=name,
    )(core, slabs.reshape(2, R, C), received.reshape(R, C)).reshape(shape)


def _chip_all_to_all(slabs, name):
    nt = len(slabs)

    def body(*refs):
        g_refs, out_refs = refs[:nt], refs[nt:2 * nt]
        send_sems, recv_sems, local_sems = refs[2 * nt:]
        x, y, c = lax.axis_index("x"), lax.axis_index("y"), lax.axis_index("c")
        me = 2 * x + y
        mine = [pltpu.make_async_copy(g_refs[t].at[me], out_refs[t].at[me], local_sems.at[t]) for t in range(nt)]
        for cp in mine:
            cp.start()
        copies = []
        for k in range(1, N_CHIPS):
            px = 1 - x if k & 2 else x
            py = 1 - y if k & 1 else y
            peer = 2 * px + py
            for t in range(nt):
                copies.append(tuple(
                    pltpu.make_async_remote_copy(src_ref=g_refs[t].at[peer], dst_ref=out_refs[t].at[dst],
                                                 send_sem=send_sems.at[k - 1, t], recv_sem=recv_sems.at[k - 1, t],
                                                 device_id=(px, py, c), device_id_type=MESH_ID)
                    for dst in (me, peer)))
        for send, _ in copies:
            send.start()
        for send, arrive in copies:
            arrive.wait_recv()
            send.wait_send()
        for cp in mine:
            cp.wait()

    return pl.pallas_call(
        body,
        out_shape=[jax.ShapeDtypeStruct(s.shape, s.dtype) for s in slabs],
        in_specs=[HBM_SPEC] * nt,
        out_specs=[HBM_SPEC] * nt,
        scratch_shapes=[pltpu.SemaphoreType.DMA((N_CHIPS - 1, nt)), pltpu.SemaphoreType.DMA((N_CHIPS - 1, nt)),
                        pltpu.SemaphoreType.DMA((nt,))],
        name=name,
    )(*slabs)


def _pack(flats, dtype):
    flat = jnp.concatenate([f.astype(dtype).reshape(-1) for f in flats])
    unit = PACK_ROWS * PACK_COLS
    n = -(-flat.shape[0] // unit) * unit
    return jnp.pad(flat, (0, n - flat.shape[0])).reshape(n // PACK_COLS, PACK_COLS)


def _unpack(buf, shapes):
    lead = buf.shape[:-2]
    flat = buf.reshape(*lead, -1)
    out, off = [], 0
    for s in shapes:
        n = math.prod(s)
        out.append(flat[..., off:off + n].reshape(*lead, *s))
        off += n
    return out


_ROW_SPLIT = ("ple_w_gate", "gdn_w_out", "fox_w_out")
_COL_SPLIT = ("ple_w_proj", "gdn_w_in", "gdn_conv_w", "fox_w_in")
_SHARDED = ("ple_w_gate", "ple_w_proj", "gdn_w_in", "gdn_conv_w", "gdn_w_out", "fox_w_in", "fox_w_out")
_REPLICATED = ("ln_g", "ln_b", "gdn_a_log", "gdn_dt_bias", "gdn_norm_g", "fox_b_f", "fox_q_norm_g", "fox_k_norm_g")
_WEIGHTS = ("ln_g", "ln_b", "ple_w_gate", "ple_w_proj", "gdn_w_in", "gdn_conv_w", "gdn_a_log", "gdn_dt_bias",
            "gdn_norm_g", "gdn_w_out", "fox_w_in", "fox_b_f", "fox_q_norm_g", "fox_k_norm_g", "fox_w_out")


def _join(name, gathered):
    n, l, a, b = gathered.shape
    if name in _ROW_SPLIT:
        return gathered.transpose(1, 0, 2, 3).reshape(l, n * a, b)
    return gathered.transpose(1, 2, 0, 3).reshape(l, a, n * b)


def _split(name, full):
    l, a, b = full.shape
    if name in _ROW_SPLIT:
        return full.reshape(l, N_DEV, a // N_DEV, b).transpose(1, 0, 2, 3)
    return full.reshape(l, a, N_DEV, b // N_DEV).transpose(2, 0, 1, 3)


def _adamw(w, g_parts, m, v, name):
    shape = w.shape
    n_parts = g_parts.shape[0]
    R, C = math.prod(shape[:-1]), shape[-1]
    tr = _pick(R, 256, SUBLANES)
    c1 = 1.0 - ADAM_B1 ** ADAM_STEP
    c2 = 1.0 - ADAM_B2 ** ADAM_STEP

    def body(w_ref, g_ref, m_ref, v_ref, go_ref, d_ref, mo_ref, vo_ref):
        gv = g_ref[0]
        for s in range(1, n_parts):
            gv = gv + g_ref[s]
        mn = ADAM_B1 * m_ref[...] + (1.0 - ADAM_B1) * gv
        vn = ADAM_B2 * v_ref[...] + (1.0 - ADAM_B2) * jnp.square(gv)
        go_ref[...] = gv
        d_ref[...] = -ADAM_LR * ((mn / c1) / (jnp.sqrt(vn / c2) + ADAM_EPS) + ADAM_WD * w_ref[...])
        mo_ref[...] = mn
        vo_ref[...] = vn

    row = pl.BlockSpec((tr, C), lambda i: (i, 0))
    outs = pl.pallas_call(
        body,
        grid=(R // tr,),
        in_specs=[row, pl.BlockSpec((n_parts, tr, C), lambda i: (0, i, 0)), row, row],
        out_specs=[row] * 4,
        out_shape=[jax.ShapeDtypeStruct((R, C), F32)] * 4,
        compiler_params=_params(("parallel",)),
        name=name,
    )(w.reshape(R, C), g_parts.reshape(n_parts, R, C), m.reshape(R, C), v.reshape(R, C))
    return [o.reshape(shape) for o in outs]


def _train_step(x, p, target, w, m, v):
    shards = [w[n] if n == "gdn_conv_w" else w[n].astype(MXU_DTYPE) for n in _SHARDED]
    gathered = _all_gather(shards, name="gather_weights")
    full = {n: _join(n, part) for n, part in zip(_SHARDED, gathered)}
    layers = _build_layers(full, {n: w[n] for n in _REPLICATED})

    loss_tile, dx, grads = _local_step(x[0], p[:, 0], target[0], layers)
    loss = lax.psum(loss_tile[0, 0], ("x", "y", "c"))

    depth = len(layers)
    gdn_l = [i for i in range(depth) if i % 2 == 0]
    fox_l = [i for i in range(depth) if i % 2 == 1]

    def stack(key, idx):
        return jnp.stack([grads[i][key] for i in idx])

    full_g = {
        "ple_w_gate": stack("w_gate", range(depth)), "ple_w_proj": stack("w_proj", range(depth)),
        "gdn_w_in": stack("w_in", gdn_l)[..., :w["gdn_w_in"].shape[-1] * N_DEV], "gdn_conv_w": stack("conv", gdn_l),
        "gdn_w_out": stack("w_out", gdn_l),
        "fox_w_in": stack("w_in", fox_l)[..., :w["fox_w_in"].shape[-1] * N_DEV], "fox_w_out": stack("w_out", fox_l)}
    small_g = {
        "ln_g": stack("ln_g", range(depth)), "ln_b": stack("ln_b", range(depth)),
        "gdn_a_log": stack("a_log", gdn_l), "gdn_dt_bias": stack("dt_bias", gdn_l), "gdn_norm_g": stack("norm_g", gdn_l),
        "fox_b_f": stack("b_f", fox_l), "fox_q_norm_g": stack("q_norm_g", fox_l), "fox_k_norm_g": stack("k_norm_g", fox_l)}

    by_core = []
    for n in _SHARDED:
        per_device = _split(n, full_g[n])
        by_core.append(jnp.swapaxes(per_device.reshape(N_CHIPS, 2, *per_device.shape[1:]), 0, 1))
    from_sibling = _sibling_swap(by_core, name="swap_grads")
    core = lax.axis_index("c").astype(jnp.int32).reshape(1)
    chip_sums = [_add_own_half(core, mine, got, name=f"chip_sum_{n}")
                 for n, mine, got in zip(_SHARDED, by_core, from_sibling)]
    g_parts = dict(zip(_SHARDED, _chip_all_to_all(chip_sums, name="scatter_grads")))
    small_all = _all_gather([_pack([small_g[n] for n in _REPLICATED], F32)], name="gather_small_grads")[0]
    g_parts.update(zip(_REPLICATED, _unpack(small_all, [w[n].shape for n in _REPLICATED])))

    g, delta, new_m, new_v = {}, {}, {}, {}
    for n in _WEIGHTS:
        g[n], delta[n], new_m[n], new_v[n] = _adamw(w[n], g_parts[n], m[n], v[n], name=f"adamw_{n}")
    return (loss, dx[None], *[g[n] for n in _WEIGHTS], *[delta[n] for n in _WEIGHTS],
            *[new_m[n] for n in _WEIGHTS], *[new_v[n] for n in _WEIGHTS])


def kernel(x, p, ln_g, ln_b, ple_w_gate, ple_w_proj, gdn_w_in, gdn_conv_w, gdn_a_log, gdn_dt_bias, gdn_norm_g, gdn_w_out, fox_w_in, fox_b_f, fox_q_norm_g, fox_k_norm_g, fox_w_out, loss_target, m_ln_g, m_ln_b, m_ple_w_gate, m_ple_w_proj, m_gdn_w_in, m_gdn_conv_w, m_gdn_a_log, m_gdn_dt_bias, m_gdn_norm_g, m_gdn_w_out, m_fox_w_in, m_fox_b_f, m_fox_q_norm_g, m_fox_k_norm_g, m_fox_w_out, v_ln_g, v_ln_b, v_ple_w_gate, v_ple_w_proj, v_gdn_w_in, v_gdn_conv_w, v_gdn_a_log, v_gdn_dt_bias, v_gdn_norm_g, v_gdn_w_out, v_fox_w_in, v_fox_b_f, v_fox_q_norm_g, v_fox_k_norm_g, v_fox_w_out):
    given = dict(locals())
    w = {n: given[n] for n in _WEIGHTS}
    m = {n: given["m_" + n] for n in _WEIGHTS}
    v = {n: given["v_" + n] for n in _WEIGHTS}
    return _train_step(x, p, loss_target, w, m, v)
```

```python
import functools
import math

import jax
import jax.numpy as jnp
from jax import lax
from jax.experimental import pallas as pl
from jax.experimental.pallas import tpu as pltpu

F32 = jnp.float32
BF16 = jnp.bfloat16
MXU_DTYPE = BF16
HI = lax.Precision.HIGHEST

N_DEV = 8
LANES = 128
SUBLANES = 8
VMEM_BYTES = 64 * 1024 * 1024

GDN_CHUNK = 64
GDN_CONV = 4
LN_EPS = 1e-5
RMS_EPS = 1e-6
NEG = -1e30

ADAM_LR = 0.001
ADAM_B1 = 0.9
ADAM_B2 = 0.999
ADAM_EPS = 1e-08
ADAM_WD = 0.01
ADAM_STEP = 10


def _params(semantics, vmem_mb=40):
    return pltpu.CompilerParams(dimension_semantics=semantics, vmem_limit_bytes=vmem_mb * 1024 * 1024)


def _pick(dim, cap, unit=LANES):
    if dim <= cap:
        return dim
    best = None
    for t in range(unit, cap + 1, unit):
        if dim % t == 0:
            best = t
    assert best is not None, (dim, cap)
    return best


def _dims(dims, ndim):
    if ndim == 2:
        return (dims, ((), ()))
    return (((dims[0][0] + 1,), (dims[1][0] + 1,)), ((0,), (0,)))


def _dot(a, b, dims):
    return lax.dot_general(a.astype(MXU_DTYPE), b.astype(MXU_DTYPE), _dims(dims, a.ndim), preferred_element_type=F32)


_NN = ((1,), (0,))
_NT = ((1,), (1,))
_TN = ((0,), (0,))


@jax.custom_vjp
def _mm_nn(a, b):
    return _dot(a, b, _NN)


@jax.custom_vjp
def _mm_nt(a, b):
    return _dot(a, b, _NT)


@jax.custom_vjp
def _mm_tn(a, b):
    return _dot(a, b, _TN)


_mm_nn.defvjp(lambda a, b: (_dot(a, b, _NN), (a, b)), lambda r, g: (_mm_nt(g, r[1]), _mm_tn(r[0], g)))
_mm_nt.defvjp(lambda a, b: (_dot(a, b, _NT), (a, b)), lambda r, g: (_mm_nn(g, r[1]), _mm_tn(g, r[0])))
_mm_tn.defvjp(lambda a, b: (_dot(a, b, _TN), (a, b)), lambda r, g: (_mm_nt(r[1], g), _mm_nn(r[0], g)))


def _mm_hi(a, b, precision=HI):
    return lax.dot_general(a, b, _dims(_NN, a.ndim), precision=precision, preferred_element_type=F32)


def _mm_3x(a, b):
    return _mm_hi(a, b, lax.Precision.HIGH)


def _sigmoid(x):
    return 1.0 / (1.0 + jnp.exp(-x))


def _silu(x):
    return x * _sigmoid(x)


def _softplus(x):
    return jnp.maximum(x, 0.0) + jnp.log(1.0 + jnp.exp(-jnp.abs(x)))


def _iota2(shape, dim):
    return lax.broadcasted_iota(jnp.int32, shape, dim)


def _lane_pick(tile, lane):
    return jnp.sum(jnp.where(_iota2(tile.shape, 1) == lane, tile, 0.0), axis=1, keepdims=True)


def _lane_put(col, lane, width=LANES):
    return jnp.where(_iota2((col.shape[0], width), 1) == lane, col, 0.0)


def _matmul(a, b, mode, out_dtype=F32, *, name, tm=1024, tn=1408, tk=1408, a_cols=None, b_cols=None):
    def cols(arr, rng):
        return (0, arr.shape[1]) if rng is None else rng

    a0, an = cols(a, a_cols)
    b0, bn = cols(b, b_cols)
    if mode == "nn":
        M, K, N = a.shape[0], an, bn
        assert b.shape[0] == K
    elif mode == "nt":
        M, K, N = a.shape[0], an, b.shape[0]
        assert bn == K
    else:
        K, M, N = a.shape[0], an, bn
        assert b.shape[0] == K
    tm, tn, tk = _pick(M, tm), _pick(N, tn), _pick(K, tk)
    nk = K // tk
    if mode == "nn":
        assert a0 % tk == 0 and b0 % tn == 0
        a_spec = pl.BlockSpec((tm, tk), lambda i, j, k: (i, a0 // tk + k))
        b_spec = pl.BlockSpec((tk, tn), lambda i, j, k: (k, b0 // tn + j))
        dims = _NN
    elif mode == "nt":
        assert a0 % tk == 0 and b0 % tk == 0
        a_spec = pl.BlockSpec((tm, tk), lambda i, j, k: (i, a0 // tk + k))
        b_spec = pl.BlockSpec((tn, tk), lambda i, j, k: (j, b0 // tk + k))
        dims = _NT
    else:
        assert a0 % tm == 0 and b0 % tn == 0
        a_spec = pl.BlockSpec((tk, tm), lambda i, j, k: (k, a0 // tm + i))
        b_spec = pl.BlockSpec((tk, tn), lambda i, j, k: (k, b0 // tn + j))
        dims = _TN

    def body(a_ref, b_ref, o_ref, acc_ref):
        k = pl.program_id(2)

        @pl.when(k == 0)
        def _():
            acc_ref[...] = jnp.zeros_like(acc_ref)

        acc_ref[...] += _dot(a_ref[...], b_ref[...], dims)

        @pl.when(k == nk - 1)
        def _():
            o_ref[...] = acc_ref[...].astype(o_ref.dtype)

    return pl.pallas_call(
        body,
        grid=(M // tm, N // tn, nk),
        in_specs=[a_spec, b_spec],
        out_specs=pl.BlockSpec((tm, tn), lambda i, j, k: (i, j)),
        out_shape=jax.ShapeDtypeStruct((M, N), out_dtype),
        scratch_shapes=[pltpu.VMEM((tm, tn), F32)],
        compiler_params=_params(("parallel", "parallel", "arbitrary"), 48),
        name=name,
    )(a, b)


def _rowwise(fn, rows, consts, out_rows, out_accs, *, tile, name, reverse=False, carries=(), vmem_mb=40):
    rows = [r if isinstance(r, tuple) else (r, r.shape[1], 0) for r in rows]
    S = rows[0][0].shape[0]
    tile = _pick(S, tile, SUBLANES)
    nt = S // tile
    nr, nc, no, na = len(rows), len(consts), len(out_rows), len(out_accs)

    def ridx(i):
        return nt - 1 - i if reverse else i

    in_specs = [pl.BlockSpec((tile, w), functools.partial(lambda i, cb: (ridx(i), cb), cb=cb)) for _, w, cb in rows]
    in_specs += [pl.BlockSpec(c.shape, functools.partial(lambda i, nd: (0,) * nd, nd=c.ndim)) for c in consts]
    out_specs = [pl.BlockSpec((tile, c), lambda i: (ridx(i), 0)) for c, _ in out_rows]
    out_specs += [pl.BlockSpec(s, functools.partial(lambda i, nd: (0,) * nd, nd=len(s))) for s, _ in out_accs]
    out_shape = [jax.ShapeDtypeStruct((S, c), d) for c, d in out_rows]
    out_shape += [jax.ShapeDtypeStruct(s, d) for s, d in out_accs]

    def body(*refs):
        rin, cin = refs[:nr], refs[nr:nr + nc]
        rout, aout = refs[nr + nc:nr + nc + no], refs[nr + nc + no:nr + nc + no + na]
        carr = refs[nr + nc + no + na:]
        step = pl.program_id(0)

        @pl.when(step == 0)
        def _():
            for r in aout + carr:
                r[...] = jnp.zeros_like(r)

        outs, accs, newc = fn([r[...] for r in rin], [c[...] for c in cin], [c[...] for c in carr])
        for r, o in zip(rout, outs, strict=True):
            r[...] = o.astype(r.dtype)
        for r, v in zip(aout, accs, strict=True):
            r[...] += v
        for r, v in zip(carr, newc, strict=True):
            r[...] = v

    res = pl.pallas_call(
        body,
        grid=(nt,),
        in_specs=in_specs,
        out_specs=out_specs,
        out_shape=out_shape,
        scratch_shapes=[pltpu.VMEM(s, F32) for s in carries],
        compiler_params=_params(("arbitrary",), vmem_mb),
        name=name,
    )(*[r[0] for r in rows], *consts)
    return res


def _ln_fn(x, y, g, b, alpha):
    r = alpha * x + y
    mu = jnp.mean(r, -1, keepdims=True)
    var = jnp.mean(jnp.square(r - mu), -1, keepdims=True)
    return (r - mu) * lax.rsqrt(var + LN_EPS) * g + b


def _ln_fwd(x, y, g, b, alpha, name):
    D = x.shape[1]

    def fn(rows, consts, _):
        return [_ln_fn(rows[0], rows[1], consts[0], consts[1], alpha)], [], []

    return _rowwise(fn, [x, y], [g, b], [(D, F32)], [], tile=512, name=name)[0]


def _ln_bwd(x, y, g, b, dx1, alpha, name):
    D = x.shape[1]

    def fn(rows, consts, _):
        xv, yv, d = rows
        _, vjp = jax.vjp(lambda yy, gg, bb: _ln_fn(xv, yy, gg, bb, alpha), yv, consts[0], consts[1])
        dy, dg, db = vjp(d)
        return [dy], [dg, db], []

    return _rowwise(fn, [x, y, dx1], [g, b], [(D, F32)], [((1, D), F32), ((1, D), F32)], tile=512, name=name)


def _ple_fwd(x1, gate_pre, pp, name):
    D = x1.shape[1]

    def fn(rows, _, __):
        return [rows[0] + _sigmoid(rows[1]) * rows[2]], [], []

    return _rowwise(fn, [x1, gate_pre, pp], [], [(D, F32)], [], tile=512, name=name)[0]


def _ple_bwd(dx2, gate_pre, pp, name):
    D = dx2.shape[1]

    def fn(rows, _, __):
        d, gp, ppv = rows
        s = _sigmoid(gp)
        return [d * ppv * s * (1.0 - s), d * s], [], []

    return _rowwise(fn, [dx2, gate_pre, pp], [], [(D, F32), (D, F32)], [], tile=512, name=name)


def _add(a, b, name):
    def fn(rows, _, __):
        return [rows[0] + rows[1]], [], []

    return _rowwise(fn, [a, b], [], [(a.shape[1], F32)], [], tile=512, name=name)[0]


def _axpy(alpha, a, b, name):
    def fn(rows, _, __):
        return [alpha * rows[0] + rows[1]], [], []

    return _rowwise(fn, [a, b], [], [(a.shape[1], F32)], [], tile=512, name=name)[0]


def _loss_head(y, target, name):
    D = y.shape[1]

    def fn(rows, _, __):
        e = rows[0] - rows[1]
        part = 0.5 * jnp.sum(jnp.sum(e * e, axis=1, keepdims=True), axis=0, keepdims=True) / D
        return [e / D], [jnp.broadcast_to(part, (SUBLANES, LANES))], []

    return _rowwise(fn, [y, target], [], [(D, F32)], [((SUBLANES, LANES), F32)], tile=512, name=name)


def _conv_fwd(h, w, n_cols, name):
    S = h.shape[0]
    T = _pick(S, 512, SUBLANES)
    CB = _pick(n_cols, 512)
    nt = S // T
    K = GDN_CONV

    def body(x_ref, halo_ref, w_ref, o_ref, buf):
        i = pl.program_id(1)
        buf[0:SUBLANES, :] = jnp.where(i > 0, halo_ref[...], 0.0)
        buf[SUBLANES:, :] = x_ref[...]
        acc = jnp.zeros((T, CB), F32)
        for k in range(K):
            acc = acc + w_ref[k:k + 1, :] * buf[pl.ds(SUBLANES - (K - 1) + k, T), :]
        o_ref[...] = acc

    return pl.pallas_call(
        body,
        grid=(n_cols // CB, nt),
        in_specs=[pl.BlockSpec((T, CB), lambda c, i: (i, c)),
                  pl.BlockSpec((SUBLANES, CB), lambda c, i: (jnp.maximum(i * (T // SUBLANES) - 1, 0), c)),
                  pl.BlockSpec((K, CB), lambda c, i: (0, c))],
        out_specs=pl.BlockSpec((T, CB), lambda c, i: (i, c)),
        out_shape=jax.ShapeDtypeStruct((S, n_cols), F32),
        scratch_shapes=[pltpu.VMEM((T + SUBLANES, CB), F32)],
        compiler_params=_params(("parallel", "parallel")),
        name=name,
    )(h, h, w)


def _conv_bwd(dc, h, w, h_col0, name):
    S, n_cols = dc.shape
    T = _pick(S, 512, SUBLANES)
    CB = _pick(n_cols, 512)
    nt = S // T
    K = GDN_CONV
    assert h_col0 % CB == 0
    hb = h_col0 // CB

    def body(d_ref, halo_ref, x_ref, w_ref, dx_ref, dw_ref, buf):
        i = pl.program_id(1)

        @pl.when(i == 0)
        def _():
            dw_ref[...] = jnp.zeros_like(dw_ref)

        buf[0:T, :] = d_ref[...]
        buf[T:, :] = jnp.where(i < nt - 1, halo_ref[...], 0.0)
        x = x_ref[...]
        acc = jnp.zeros((T, CB), F32)
        for k in range(K):
            shifted = buf[pl.ds(K - 1 - k, T), :]
            acc = acc + w_ref[k:k + 1, :] * shifted
            dw_ref[k:k + 1, :] += jnp.sum(shifted * x, axis=0, keepdims=True)
        dx_ref[...] = acc

    last = S // SUBLANES - 1
    return pl.pallas_call(
        body,
        grid=(n_cols // CB, nt),
        in_specs=[pl.BlockSpec((T, CB), lambda c, i: (i, c)),
                  pl.BlockSpec((SUBLANES, CB), lambda c, i: (jnp.minimum((i + 1) * (T // SUBLANES), last), c)),
                  pl.BlockSpec((T, CB), lambda c, i: (i, hb + c)),
                  pl.BlockSpec((K, CB), lambda c, i: (0, c))],
        out_specs=[pl.BlockSpec((T, CB), lambda c, i: (i, c)),
                   pl.BlockSpec((SUBLANES, CB), lambda c, i: (0, c))],
        out_shape=[jax.ShapeDtypeStruct((S, n_cols), F32), jax.ShapeDtypeStruct((SUBLANES, n_cols), F32)],
        scratch_shapes=[pltpu.VMEM((T + SUBLANES, CB), F32)],
        compiler_params=_params(("parallel", "arbitrary")),
        name=name,
    )(dc, dc, h, w)


def _neumann_inverse(L):
    C = L.shape[-1]
    eye = (_iota2((C, C), 0) == _iota2((C, C), 1)).astype(F32)
    X = eye - L
    P = L
    for _ in range(max(0, math.ceil(math.log2(C)) - 1)):
        P = _mm_3x(P, P)
        X = _mm_3x(X, eye + P)
    return X


@jax.custom_vjp
def _unit_lower_inverse(L):
    return _neumann_inverse(L)


def _unit_lower_inverse_bwd(T, dT):
    Tt = jnp.swapaxes(T, -1, -2)
    return (-_mm_3x(_mm_3x(Tt, dT), Tt),)


_unit_lower_inverse.defvjp(lambda L: (_neumann_inverse(L),) * 2, _unit_lower_inverse_bwd)


@jax.custom_vjp
def _known_inverse(L, T):
    return T


_known_inverse.defvjp(lambda L, T: (T, T), lambda T, dT: (*_unit_lower_inverse_bwd(T, dT), jnp.zeros_like(T)))


def _gdn_chunk(cq, ck, cv, zz, bcol, acol, alog, dtb, ng, state, inverse=None):
    G, C, dk = cq.shape
    q = _silu(cq)
    k = _silu(ck)
    v = _silu(cv)
    q = q * lax.rsqrt(jnp.sum(q * q, -1, keepdims=True) + RMS_EPS) * (dk ** -0.5)
    k = k * lax.rsqrt(jnp.sum(k * k, -1, keepdims=True) + RMS_EPS)
    beta = _sigmoid(bcol)
    g = -jnp.exp(alog) * _softplus(acol + dtb)

    row, col = _iota2((C, C), 0), _iota2((C, C), 1)
    causal, strict, eye = row >= col, row > col, row == col
    g_rows = jnp.swapaxes(jnp.broadcast_to(g, (G, C, C)), -1, -2)
    gc = jnp.sum(jnp.where(causal, g_rows, 0.0), axis=-1, keepdims=True)
    gcb = jnp.broadcast_to(gc, (G, C, C))
    g_last = jnp.sum(jnp.sum(jnp.where((row == C - 1) & (col == 0), gcb, 0.0), axis=-1, keepdims=True), axis=-2, keepdims=True)
    gc_rows = jnp.swapaxes(gcb, -1, -2)
    decay = jnp.exp(jnp.where(causal, gcb - gc_rows, NEG))

    kb = k * beta
    L = jnp.where(strict, _mm_nt(kb, k) * decay, 0.0)
    T = _unit_lower_inverse(L) if inverse is None else _known_inverse(L, inverse)
    u = _mm_3x(T, v * beta)
    w = _mm_3x(T, kb * jnp.exp(gc))
    a_qk = jnp.where(causal, _mm_nt(q, k) * decay, 0.0)
    q_dec = q * jnp.exp(gc)
    k_dec = k * jnp.exp(g_last - gc)
    v_new = u - _mm_nn(w, state)
    o = _mm_nn(q_dec, state) + _mm_nn(a_qk, v_new)
    new_state = state * jnp.exp(g_last) + _mm_tn(k_dec, v_new)
    y = o * lax.rsqrt(jnp.mean(o * o, -1, keepdims=True) + RMS_EPS) * ng * _silu(zz)
    return (y, new_state), T


GDN_HEADS_PER_STEP = 8


def _gdn_specs(H, dk, G, chunk_of=lambda n: n):
    C = GDN_CHUNK
    NG = H // G
    cq = pl.BlockSpec((C, G * dk), lambda n, h: (chunk_of(n), h))
    ck = pl.BlockSpec((C, G * dk), lambda n, h: (chunk_of(n), NG + h))
    cv = pl.BlockSpec((C, G * dk), lambda n, h: (chunk_of(n), 2 * NG + h))
    zz = pl.BlockSpec((C, G * dk), lambda n, h: (chunk_of(n), 3 * NG + h))
    ba = pl.BlockSpec((C, LANES), lambda n, h: (chunk_of(n), 4 * H * dk // LANES))
    return cq, ck, cv, zz, ba


def _gdn_step_args(cq_ref, ck_ref, cv_ref, z_ref, ba_ref, hp_ref, hg, G, H, dk):
    ba = ba_ref[...]

    def heads(ref):
        return jnp.stack([ref[:, g * dk:(g + 1) * dk] for g in range(G)])

    def picks(tile, offset):
        return jnp.stack([_lane_pick(tile, offset + hg * G + g) for g in range(G)])

    return (heads(cq_ref), heads(ck_ref), heads(cv_ref), heads(z_ref), picks(ba, 0), picks(ba, H),
            picks(hp_ref[0:1, :], 0), picks(hp_ref[1:2, :], 0))


def _gdn_scan_fwd(c, h, hp, ng, H, name):
    S = c.shape[0]
    dk = c.shape[1] // (3 * H)
    assert dk == LANES
    C = GDN_CHUNK
    NC = S // C

    G = min(GDN_HEADS_PER_STEP, H)
    assert H % G == 0
    NG = H // G

    def body(cq_ref, ck_ref, cv_ref, z_ref, ba_ref, hp_ref, ng_ref, y_ref, s_ref, t_ref, state):
        n, hg = pl.program_id(0), pl.program_id(1)
        heads = pl.ds(hg * G, G)

        @pl.when(n == 0)
        def _():
            state[heads] = jnp.zeros((G, dk, dk), F32)

        st = state[heads]
        s_ref[0] = st
        (y, new_state), inverse = _gdn_chunk(
            *_gdn_step_args(cq_ref, ck_ref, cv_ref, z_ref, ba_ref, hp_ref, hg, G, H, dk), ng_ref[...], st)
        t_ref[0] = inverse
        for g in range(G):
            y_ref[:, g * dk:(g + 1) * dk] = y[g].astype(y_ref.dtype)
        state[heads] = new_state

    cq, ck, cv, zz, ba = _gdn_specs(H, dk, G)
    return pl.pallas_call(
        body,
        grid=(NC, NG),
        in_specs=[cq, ck, cv, zz, ba, pl.BlockSpec((SUBLANES, LANES), lambda n, h: (0, 0)),
                  pl.BlockSpec((1, dk), lambda n, h: (0, 0))],
        out_specs=[pl.BlockSpec((C, G * dk), lambda n, h: (n, h)),
                   pl.BlockSpec((1, G, dk, dk), lambda n, h: (n, h, 0, 0)),
                   pl.BlockSpec((1, G, C, C), lambda n, h: (n, h, 0, 0))],
        out_shape=[jax.ShapeDtypeStruct((S, H * dk), MXU_DTYPE), jax.ShapeDtypeStruct((NC, H, dk, dk), F32),
                   jax.ShapeDtypeStruct((NC, H, C, C), F32)],
        scratch_shapes=[pltpu.VMEM((H, dk, dk), F32)],
        compiler_params=_params(("arbitrary", "arbitrary")),
        name=name,
    )(c, c, c, h, h, hp, ng)


def _gdn_scan_bwd(c, h, hp, ng, states, inverses, dy, H, name):
    S = c.shape[0]
    dk = c.shape[1] // (3 * H)
    C = GDN_CHUNK
    NC = S // C

    G = min(GDN_HEADS_PER_STEP, H)
    NG = H // G

    def body(cq_ref, ck_ref, cv_ref, z_ref, ba_ref, hp_ref, ng_ref, s_ref, t_ref, dy_ref,
             dq_ref, dk_ref, dv_ref, dz_ref, dba_ref, dhp_ref, dng_ref, dstate):
        n, hg = pl.program_id(0), pl.program_id(1)

        @pl.when((n == 0) & (hg == 0))
        def _():
            dhp_ref[...] = jnp.zeros_like(dhp_ref)
            dng_ref[...] = jnp.zeros_like(dng_ref)

        heads = pl.ds(hg * G, G)

        @pl.when(n == 0)
        def _():
            dstate[heads] = jnp.zeros((G, dk, dk), F32)

        args = (*_gdn_step_args(cq_ref, ck_ref, cv_ref, z_ref, ba_ref, hp_ref, hg, G, H, dk), ng_ref[...], s_ref[0])
        inverse = t_ref[0]
        _, vjp, _ = jax.vjp(lambda *a: _gdn_chunk(*a, inverse=inverse), *args, has_aux=True)
        dy = jnp.stack([dy_ref[:, g * dk:(g + 1) * dk] for g in range(G)])
        dcq, dck, dcv, dzz, dbc, dac, dal, ddt, dng, dst = vjp((dy, dstate[heads]))
        dstate[heads] = dst
        dba = jnp.zeros((C, LANES), F32)
        dhp0 = jnp.zeros((1, LANES), F32)
        dhp1 = jnp.zeros((1, LANES), F32)
        for g in range(G):
            hd = hg * G + g
            sl = slice(g * dk, (g + 1) * dk)
            dq_ref[:, sl] = dcq[g]
            dk_ref[:, sl] = dck[g]
            dv_ref[:, sl] = dcv[g]
            dz_ref[:, sl] = dzz[g]
            dba = dba + _lane_put(dbc[g], hd) + _lane_put(dac[g], H + hd)
            dhp0 = dhp0 + _lane_put(dal[g], hd)
            dhp1 = dhp1 + _lane_put(ddt[g], hd)

        @pl.when(hg == 0)
        def _():
            dba_ref[...] = dba

        @pl.when(hg > 0)
        def _():
            dba_ref[...] += dba

        dhp_ref[0:1, :] += dhp0
        dhp_ref[1:2, :] += dhp1
        dng_ref[...] += dng

    rev = lambda n: NC - 1 - n
    blk = pl.BlockSpec
    in_specs = [*_gdn_specs(H, dk, G, rev),
                blk((SUBLANES, LANES), lambda n, h: (0, 0)), blk((1, dk), lambda n, h: (0, 0)),
                blk((1, G, dk, dk), lambda n, h: (rev(n), h, 0, 0)), blk((1, G, C, C), lambda n, h: (rev(n), h, 0, 0)),
                blk((C, G * dk), lambda n, h: (rev(n), h))]
    out_specs = [blk((C, G * dk), lambda n, h: (rev(n), h))] * 4 + [
        blk((C, LANES), lambda n, h: (rev(n), 0)),
        blk((SUBLANES, LANES), lambda n, h: (0, 0)), blk((1, dk), lambda n, h: (0, 0))]
    out_shape = [jax.ShapeDtypeStruct((S, H * dk), F32)] * 4 + [
        jax.ShapeDtypeStruct((S, LANES), F32), jax.ShapeDtypeStruct((SUBLANES, LANES), F32),
        jax.ShapeDtypeStruct((1, dk), F32)]
    return pl.pallas_call(
        body,
        grid=(NC, NG),
        in_specs=in_specs,
        out_specs=out_specs,
        out_shape=out_shape,
        scratch_shapes=[pltpu.VMEM((H, dk, dk), F32)],
        compiler_params=_params(("arbitrary", "arbitrary")),
        name=name,
    )(c, c, c, h, h, hp, ng, states, inverses, dy)


def _pair_rms(x, gain, dh):
    first = _iota2(x.shape, 1) < dh
    sq = x * x
    ss_a = jnp.sum(jnp.where(first, sq, 0.0), axis=1, keepdims=True)
    ss_b = jnp.sum(jnp.where(first, 0.0, sq), axis=1, keepdims=True)
    inv = jnp.where(first, lax.rsqrt(ss_a / dh + RMS_EPS), lax.rsqrt(ss_b / dh + RMS_EPS))
    return x * inv * gain


def _log_sigmoid(x):
    return jnp.minimum(x, 0.0) - jnp.log(1.0 + jnp.exp(-jnp.abs(x)))


def _cum_fn(fr, bf, carry):
    T = fr.shape[0]
    tril = (_iota2((T, T), 0) >= _iota2((T, T), 1)).astype(F32)
    c = _mm_hi(tril, _log_sigmoid(fr + bf)) + carry
    last = jnp.sum(jnp.where(_iota2(c.shape, 0) == T - 1, c, 0.0), axis=0, keepdims=True)
    return c, last


def _fox_prep_fwd(h, bf, qg, kg, H, dh, name):
    S = h.shape[0]
    W = H * dh
    assert 2 * dh == LANES and H <= LANES
    T = _pick(S, 256, LANES)
    nt = S // T
    npair = H // 2

    def body(hq_ref, hk_ref, f_ref, bf_ref, qg_ref, kg_ref, q_ref, k_ref, ccol_ref, crow_ref, carry):
        i = pl.program_id(0)

        @pl.when(i == 0)
        def _():
            carry[...] = jnp.zeros_like(carry)

        for p in range(npair):
            sl = slice(p * LANES, (p + 1) * LANES)
            q_ref[:, sl] = (_pair_rms(hq_ref[:, sl], qg_ref[...], dh) * (dh ** -0.5)).astype(q_ref.dtype)
            k_ref[:, sl] = _pair_rms(hk_ref[:, sl], kg_ref[...], dh).astype(k_ref.dtype)
        c, last = _cum_fn(f_ref[...], bf_ref[...], carry[...])
        carry[...] = last
        ct = c.T
        for hh in range(H):
            ccol_ref[hh] = c[:, hh:hh + 1]
            crow_ref[hh] = ct[hh:hh + 1, :]

    return pl.pallas_call(
        body,
        grid=(nt,),
        in_specs=[pl.BlockSpec((T, W), lambda i: (i, 0)), pl.BlockSpec((T, W), lambda i: (i, 1)),
                  pl.BlockSpec((T, LANES), lambda i: (i, 4 * W // LANES)),
                  pl.BlockSpec((1, LANES), lambda i: (0, 0)), pl.BlockSpec((1, LANES), lambda i: (0, 0)),
                  pl.BlockSpec((1, LANES), lambda i: (0, 0))],
        out_specs=[pl.BlockSpec((T, W), lambda i: (i, 0)), pl.BlockSpec((T, W), lambda i: (i, 0)),
                   pl.BlockSpec((H, T, 1), lambda i: (0, i, 0)), pl.BlockSpec((H, 1, T), lambda i: (0, 0, i))],
        out_shape=[jax.ShapeDtypeStruct((S, W), MXU_DTYPE), jax.ShapeDtypeStruct((S, W), MXU_DTYPE),
                   jax.ShapeDtypeStruct((H, S, 1), F32), jax.ShapeDtypeStruct((H, 1, S), F32)],
        scratch_shapes=[pltpu.VMEM((1, LANES), F32)],
        compiler_params=_params(("arbitrary",)),
        name=name,
    )(h, h, h, bf, qg, kg)


def _fox_prep_bwd(h, bf, qg, kg, dq, dk, dcrow, H, dh, name):
    S = h.shape[0]
    W = H * dh
    T = _pick(S, 256, LANES)
    nt = S // T
    npair = H // 2

    def body(hq_ref, hk_ref, f_ref, bf_ref, qg_ref, kg_ref, dq_ref, dk_ref, dcrow_ref,
             dhq_ref, dhk_ref, df_ref, dbf_ref, dqg_ref, dkg_ref, dcarry, dct):
        i = pl.program_id(0)

        @pl.when(i == 0)
        def _():
            dcarry[...] = jnp.zeros_like(dcarry)
            dbf_ref[...] = jnp.zeros_like(dbf_ref)
            dqg_ref[...] = jnp.zeros_like(dqg_ref)
            dkg_ref[...] = jnp.zeros_like(dkg_ref)

        for p in range(npair):
            sl = slice(p * LANES, (p + 1) * LANES)
            _, vjp = jax.vjp(lambda x, g: _pair_rms(x, g, dh) * (dh ** -0.5), hq_ref[:, sl], qg_ref[...])
            dx, dg = vjp(dq_ref[:, sl])
            dhq_ref[:, sl] = dx
            dqg_ref[...] += dg
            _, vjp = jax.vjp(lambda x, g: _pair_rms(x, g, dh), hk_ref[:, sl], kg_ref[...])
            dx, dg = vjp(dk_ref[:, sl])
            dhk_ref[:, sl] = dx
            dkg_ref[...] += dg
        dct[...] = jnp.zeros_like(dct)
        for hh in range(H):
            dct[hh:hh + 1, :] = dcrow_ref[hh]
        _, vjp = jax.vjp(lambda f, b: _cum_fn(f, b, jnp.zeros((1, LANES), F32)), f_ref[...], bf_ref[...])
        dc = dct[...].T
        df, dbf = vjp((dc, dcarry[...]))
        df_ref[...] = df
        dbf_ref[...] += dbf
        dcarry[...] = dcarry[...] + jnp.sum(dc, axis=0, keepdims=True)

    rv = lambda i: nt - 1 - i
    return pl.pallas_call(
        body,
        grid=(nt,),
        in_specs=[pl.BlockSpec((T, W), lambda i: (rv(i), 0)), pl.BlockSpec((T, W), lambda i: (rv(i), 1)),
                  pl.BlockSpec((T, LANES), lambda i: (rv(i), 4 * W // LANES)),
                  pl.BlockSpec((1, LANES), lambda i: (0, 0)), pl.BlockSpec((1, LANES), lambda i: (0, 0)),
                  pl.BlockSpec((1, LANES), lambda i: (0, 0)),
                  pl.BlockSpec((T, W), lambda i: (rv(i), 0)), pl.BlockSpec((T, W), lambda i: (rv(i), 0)),
                  pl.BlockSpec((H, 1, T), lambda i: (0, 0, rv(i)))],
        out_specs=[pl.BlockSpec((T, W), lambda i: (rv(i), 0)), pl.BlockSpec((T, W), lambda i: (rv(i), 0)),
                   pl.BlockSpec((T, LANES), lambda i: (rv(i), 0)),
                   pl.BlockSpec((1, LANES), lambda i: (0, 0)), pl.BlockSpec((1, LANES), lambda i: (0, 0)),
                   pl.BlockSpec((1, LANES), lambda i: (0, 0))],
        out_shape=[jax.ShapeDtypeStruct((S, W), F32), jax.ShapeDtypeStruct((S, W), F32),
                   jax.ShapeDtypeStruct((S, LANES), F32)] + [jax.ShapeDtypeStruct((1, LANES), F32)] * 3,
        scratch_shapes=[pltpu.VMEM((1, LANES), F32), pltpu.VMEM((LANES, T), F32)],
        compiler_params=_params(("arbitrary",)),
        name=name,
    )(h, h, h, bf, qg, kg, dq, dk, dcrow)


def _head_masks(dh):
    first = _iota2((1, LANES), 1) < dh
    return first, jnp.logical_not(first)


def _per_head(tile, dh):
    return jnp.stack([jnp.where(mask, tile, 0) for mask in _head_masks(dh)])


def _both(tile):
    return jnp.stack([tile, tile])


def _rows2(x):
    return x.reshape(2 * x.shape[1], x.shape[2])


def _stacked(tile, dh):
    return _rows2(_per_head(tile, dh))


FLASH_SUB = 128


def _flash_scores(q_m, k, crow, row0, diagonal):
    s = _dot(q_m, k, _NT) - crow
    if diagonal:
        s = jnp.where(_iota2(s.shape, s.ndim - 1) <= row0 + _iota2(s.shape, s.ndim - 2), s, NEG)
    return s


def _flash_tiles(S, wide=True):
    tk = _pick(S, 512, LANES)
    tq = 2 * tk if wide and S % (2 * tk) == 0 else tk
    return tq, tk


def _causal_blocks(S, tq, tk, by_query):
    ratio = tq // tk
    if by_query:
        pairs = [(i, j) for i in range(S // tq) for j in range(ratio * (i + 1))]
    else:
        pairs = [(i, j) for j in range(S // tk) for i in range(j // ratio, S // tq)]
    return (jnp.asarray([a for a, _ in pairs], jnp.int32), jnp.asarray([b for _, b in pairs], jnp.int32))


def _flash_blocks(i, j, tq, tk, fn, sub=FLASH_SUB):
    sub = min(sub, tq)
    ratio = tq // tk
    offset = j - ratio * i

    @pl.when(offset < 0)
    def _():
        for r in range(tq // sub):
            fn(slice(r * sub, (r + 1) * sub), 0, False)

    for d in range(ratio):
        @pl.when(offset == d)
        def _():
            for r in range(tq // sub):
                first_row, first_key = r * sub, d * tk
                if first_key > first_row + sub - 1:
                    continue
                fn(slice(first_row, first_row + sub), first_row - first_key, first_key + tk - 1 > first_row)


def _flash_fwd(q, k, h, ccol, crow, H, dh, name):
    S, W = q.shape
    tq, tk = _flash_tiles(S)
    nq, ratio = S // tq, tq // tk
    npair = H // 2
    vblk = 2 * W // LANES

    def body(ii_ref, jj_ref, q_ref, k_ref, v_ref, ccol_ref, crow_ref, o_ref, lse_ref, m_s, l_s, acc_s):
        i, j = ii_ref[pl.program_id(1)], jj_ref[pl.program_id(1)]

        @pl.when(j == 0)
        def _():
            m_s[...] = jnp.full_like(m_s, NEG)
            l_s[...] = jnp.zeros_like(l_s)
            acc_s[...] = jnp.zeros_like(acc_s)

        def tile(rows, row0, diagonal):
            s = _flash_scores(_per_head(q_ref[rows, :], dh), _both(k_ref[...]), crow_ref[...], row0, diagonal)
            s = s + ccol_ref[:, rows, :]
            m_old = m_s[:, rows, :]
            m_new = jnp.maximum(m_old, jnp.max(s, axis=-1, keepdims=True))
            alpha = jnp.exp(m_old - m_new)
            p = jnp.exp(s - m_new)
            l_s[:, rows, :] = alpha * l_s[:, rows, :] + jnp.sum(p, axis=-1, keepdims=True)
            first, _ = _head_masks(dh)
            acc_s[rows, :] = jnp.where(first, alpha[0], alpha[1]) * acc_s[rows, :] + _dot(
                jnp.concatenate([p[0], p[1]], axis=1), _stacked(v_ref[...].astype(MXU_DTYPE), dh), _NN)
            m_s[:, rows, :] = m_new

        _flash_blocks(i, j, tq, tk, tile)

        @pl.when(j == ratio * (i + 1) - 1)
        def _():
            first, _ = _head_masks(dh)
            o_ref[...] = acc_s[...] / jnp.where(first, l_s[0], l_s[1])
            for a in range(2):
                lse_ref[a] = m_s[a] + jnp.log(l_s[a])

    ii, jj = _causal_blocks(S, tq, tk, by_query=True)
    return pl.pallas_call(
        body,
        grid_spec=pltpu.PrefetchScalarGridSpec(
            num_scalar_prefetch=2,
            grid=(npair, len(ii)),
            in_specs=[pl.BlockSpec((tq, LANES), lambda p, t, ii, jj: (ii[t], p)),
                      pl.BlockSpec((tk, LANES), lambda p, t, ii, jj: (jj[t], p)),
                      pl.BlockSpec((tk, LANES), lambda p, t, ii, jj: (jj[t], vblk + p)),
                      pl.BlockSpec((2, tq, 1), lambda p, t, ii, jj: (p, ii[t], 0)),
                      pl.BlockSpec((2, 1, tk), lambda p, t, ii, jj: (p, 0, jj[t]))],
            out_specs=[pl.BlockSpec((tq, LANES), lambda p, t, ii, jj: (ii[t], p)),
                       pl.BlockSpec((2, tq, 1), lambda p, t, ii, jj: (p, ii[t], 0))],
            scratch_shapes=[pltpu.VMEM((2, tq, 1), F32), pltpu.VMEM((2, tq, 1), F32), pltpu.VMEM((tq, LANES), F32)]),
        out_shape=[jax.ShapeDtypeStruct((S, W), F32), jax.ShapeDtypeStruct((H, S, 1), F32)],
        compiler_params=_params(("parallel", "arbitrary")),
        name=name,
    )(ii, jj, q, k, h, ccol, crow)


def _flash_bwd_kv(q, k, h, ccol, crow, lse, delta, resid, do, H, dh, name):
    S, W = q.shape
    tq, tk = _flash_tiles(S)
    nq, ratio = S // tq, tq // tk
    npair = H // 2
    vblk = 2 * W // LANES

    def body(ii_ref, jj_ref, q_ref, k_ref, v_ref, ccol_ref, crow_ref, lse_ref, dl_ref, rs_ref, do_ref, dk_ref, dv_ref, dcr_ref,
             dkt_s, dvt_s):
        i, j = ii_ref[pl.program_id(1)], jj_ref[pl.program_id(1)]

        @pl.when(i == j // ratio)
        def _():
            dkt_s[...] = jnp.zeros_like(dkt_s)
            dvt_s[...] = jnp.zeros_like(dvt_s)
            dcr_ref[...] = jnp.zeros_like(dcr_ref)

        def tile(rows, row0, diagonal):
            kv, vv = _both(k_ref[...]), _both(v_ref[...].astype(MXU_DTYPE))
            q_m = _per_head(q_ref[rows, :], dh)
            do_m = _per_head(do_ref[rows, :].astype(MXU_DTYPE), dh)
            s = _flash_scores(q_m, kv, crow_ref[...], row0, diagonal)
            p = jnp.exp(s + (ccol_ref[:, rows, :] - lse_ref[:, rows, :]))
            ds = p * (_dot(do_m, vv, _NT) - (dl_ref[:, rows, :] + rs_ref[:, rows, :]))
            dvt_s[...] += _dot(_rows2(do_m), _rows2(p), _TN)
            dkt_s[...] += _dot(_rows2(q_m), _rows2(ds), _TN)
            dcr_ref[...] -= jnp.sum(ds, axis=1, keepdims=True)

        _flash_blocks(i, j, tq, tk, tile, sub=2 * FLASH_SUB)

        @pl.when(i == nq - 1)
        def _():
            dk_ref[...] = dkt_s[...].T
            dv_ref[...] = dvt_s[...].T

    ii, jj = _causal_blocks(S, tq, tk, by_query=False)
    qrow = pl.BlockSpec((2, tq, 1), lambda p, t, ii, jj: (p, ii[t], 0))
    return pl.pallas_call(
        body,
        grid_spec=pltpu.PrefetchScalarGridSpec(
            num_scalar_prefetch=2,
            grid=(npair, len(ii)),
            in_specs=[pl.BlockSpec((tq, LANES), lambda p, t, ii, jj: (ii[t], p)),
                      pl.BlockSpec((tk, LANES), lambda p, t, ii, jj: (jj[t], p)),
                      pl.BlockSpec((tk, LANES), lambda p, t, ii, jj: (jj[t], vblk + p)),
                      qrow,
                      pl.BlockSpec((2, 1, tk), lambda p, t, ii, jj: (p, 0, jj[t])),
                      qrow, qrow, qrow,
                      pl.BlockSpec((tq, LANES), lambda p, t, ii, jj: (ii[t], p))],
            out_specs=[pl.BlockSpec((tk, LANES), lambda p, t, ii, jj: (jj[t], p)),
                       pl.BlockSpec((tk, LANES), lambda p, t, ii, jj: (jj[t], p)),
                       pl.BlockSpec((2, 1, tk), lambda p, t, ii, jj: (p, 0, jj[t]))],
            scratch_shapes=[pltpu.VMEM((LANES, tk), F32), pltpu.VMEM((LANES, tk), F32)]),
        out_shape=[jax.ShapeDtypeStruct((S, W), F32), jax.ShapeDtypeStruct((S, W), F32),
                   jax.ShapeDtypeStruct((H, 1, S), F32)],
        compiler_params=_params(("parallel", "arbitrary")),
        name=name,
    )(ii, jj, q, k, h, ccol, crow, lse, delta, resid, do)


def _flash_bwd_q(q, k, h, ccol, crow, lse, delta, do, H, dh, name):
    S, W = q.shape
    tq, tk = _flash_tiles(S, wide=False)
    npair = H // 2
    vblk = 2 * W // LANES

    def body(ii_ref, jj_ref, q_ref, k_ref, v_ref, ccol_ref, crow_ref, lse_ref, dl_ref, do_ref, dq_ref, rs_ref):
        i, j = ii_ref[pl.program_id(1)], jj_ref[pl.program_id(1)]

        @pl.when(j == 0)
        def _():
            dq_ref[...] = jnp.zeros_like(dq_ref)
            rs_ref[...] = jnp.zeros_like(rs_ref)

        def tile(rows, row0, diagonal):
            kv, vv = k_ref[...], _both(v_ref[...].astype(MXU_DTYPE))
            s = _flash_scores(_per_head(q_ref[rows, :], dh), _both(kv), crow_ref[...], row0, diagonal)
            p = jnp.exp(s + (ccol_ref[:, rows, :] - lse_ref[:, rows, :]))
            ds = p * (_dot(_per_head(do_ref[rows, :].astype(MXU_DTYPE), dh), vv, _NT) - dl_ref[:, rows, :])
            dq = _dot(ds, _per_head(kv, dh), _NN)
            dq_ref[rows, :] += dq[0] + dq[1]
            rs_ref[:, rows, :] += jnp.sum(ds, axis=-1, keepdims=True)

        _flash_blocks(i, j, tq, tk, tile, sub=tk)

    ii, jj = _causal_blocks(S, tq, tk, by_query=True)
    qrow = pl.BlockSpec((2, tq, 1), lambda p, t, ii, jj: (p, ii[t], 0))
    return pl.pallas_call(
        body,
        grid_spec=pltpu.PrefetchScalarGridSpec(
            num_scalar_prefetch=2,
            grid=(npair, len(ii)),
            in_specs=[pl.BlockSpec((tq, LANES), lambda p, t, ii, jj: (ii[t], p)),
                      pl.BlockSpec((tk, LANES), lambda p, t, ii, jj: (jj[t], p)),
                      pl.BlockSpec((tk, LANES), lambda p, t, ii, jj: (jj[t], vblk + p)),
                      qrow,
                      pl.BlockSpec((2, 1, tk), lambda p, t, ii, jj: (p, 0, jj[t])),
                      qrow, qrow,
                      pl.BlockSpec((tq, LANES), lambda p, t, ii, jj: (ii[t], p))],
            out_specs=[pl.BlockSpec((tq, LANES), lambda p, t, ii, jj: (ii[t], p)), qrow]),
        out_shape=[jax.ShapeDtypeStruct((S, W), F32), jax.ShapeDtypeStruct((H, S, 1), F32)],
        compiler_params=_params(("parallel", "arbitrary")),
        name=name,
    )(ii, jj, q, k, h, ccol, crow, lse, delta, do)


def _fox_gate_fwd(o, h, W, name):
    def fn(rows, _, __):
        return [rows[0] * _silu(rows[1])], [], []

    return _rowwise(fn, [o, (h, W, 3)], [], [(W, MXU_DTYPE)], [], tile=512, name=name)[0]


def _fox_gate_bwd(o, h, dog, H, dh, name):
    S, W = o.shape
    T = _pick(S, 512, SUBLANES)

    def body(o_ref, z_ref, d_ref, do_ref, dz_ref, dl_ref):
        ov, zv, dv = o_ref[...], z_ref[...], d_ref[...]
        sg = _sigmoid(zv)
        do = dv * zv * sg
        do_ref[...] = do
        dz_ref[...] = dv * ov * sg * (1.0 + zv * (1.0 - sg))
        prod = do * ov
        for p in range(H // 2):
            blk = prod[:, p * LANES:(p + 1) * LANES]
            first = _iota2(blk.shape, 1) < dh
            dl_ref[2 * p] = jnp.sum(jnp.where(first, blk, 0.0), axis=1, keepdims=True)
            dl_ref[2 * p + 1] = jnp.sum(jnp.where(first, 0.0, blk), axis=1, keepdims=True)

    return pl.pallas_call(
        body,
        grid=(S // T,),
        in_specs=[pl.BlockSpec((T, W), lambda i: (i, 0)), pl.BlockSpec((T, W), lambda i: (i, 3)),
                  pl.BlockSpec((T, W), lambda i: (i, 0))],
        out_specs=[pl.BlockSpec((T, W), lambda i: (i, 0)), pl.BlockSpec((T, W), lambda i: (i, 0)),
                   pl.BlockSpec((H, T, 1), lambda i: (0, i, 0))],
        out_shape=[jax.ShapeDtypeStruct((S, W), F32), jax.ShapeDtypeStruct((S, W), F32),
                   jax.ShapeDtypeStruct((H, S, 1), F32)],
        compiler_params=_params(("parallel",)),
        name=name,
    )(o, h, dog)


def _gdn_layer_fwd(x, w, tag):
    H = w["H"]
    qk = H * LANES
    h = _matmul(x, w["w_in"], "nn", name=f"{tag}_in")
    c = _conv_fwd(h, w["conv"], 3 * qk, name=f"{tag}_conv")
    og, states, inverses = _gdn_scan_fwd(c, h, w["hp"], w["ng"], H, name=f"{tag}_scan")
    y = _matmul(og, w["w_out"], "nn", name=f"{tag}_out")
    return y, (x, h, c, states, inverses, og)


def _gdn_layer_bwd(dy, res, w, tag):
    x, h, c, states, inverses, og = res
    H = w["H"]
    qk = H * LANES
    dog = _matmul(dy, w["w_out"], "nt", name=f"{tag}_out_dx")
    dw_out = _matmul(og, dy, "tn", name=f"{tag}_out_dw")
    dq, dk, dv, dz, dba, dhp, dng = _gdn_scan_bwd(c, h, w["hp"], w["ng"], states, inverses, dog, H, name=f"{tag}_scan_bwd")
    dh_parts, dconv = [], []
    for part, d in enumerate((dq, dk, dv)):
        dh_p, dw_p = _conv_bwd(d, h, w["conv"][:, part * qk:(part + 1) * qk], part * qk, name=f"{tag}_conv_bwd{part}")
        dh_parts.append(dh_p)
        dconv.append(dw_p[:GDN_CONV])
    dh = jnp.concatenate(dh_parts + [dz, dba], axis=1)
    dx = _matmul(dh, w["w_in"], "nt", name=f"{tag}_in_dx")
    dw_in = _matmul(x, dh, "tn", name=f"{tag}_in_dw")
    grads = {"w_in": dw_in, "w_out": dw_out, "conv": jnp.concatenate(dconv, axis=1),
             "a_log": dhp[0, :H], "dt_bias": dhp[1, :H], "norm_g": dng[0]}
    return dx, grads


def _fox_layer_fwd(x, w, tag):
    H, dh = w["H"], w["dh"]
    W = H * dh
    h = _matmul(x, w["w_in"], "nn", name=f"{tag}_in")
    q, k, ccol, crow = _fox_prep_fwd(h, w["bf"], w["qg"], w["kg"], H, dh, name=f"{tag}_prep")
    o, lse = _flash_fwd(q, k, h, ccol, crow, H, dh, name=f"{tag}_flash")
    og = _fox_gate_fwd(o, h, W, name=f"{tag}_gate")
    y = _matmul(og, w["w_out"], "nn", name=f"{tag}_out")
    return y, (x, h, q, k, ccol, crow, o, lse, og)


def _fox_layer_bwd(dy, res, w, tag):
    x, h, q, k, ccol, crow, o, lse, og = res
    H, dh = w["H"], w["dh"]
    dog = _matmul(dy, w["w_out"], "nt", name=f"{tag}_out_dx")
    dw_out = _matmul(og, dy, "tn", name=f"{tag}_out_dw")
    do, dz, delta = _fox_gate_bwd(o, h, dog, H, dh, name=f"{tag}_gate_bwd")
    dqq, resid = _flash_bwd_q(q, k, h, ccol, crow, lse, delta, do, H, dh, name=f"{tag}_flash_bwd_q")
    dkk, dvv, dcrow = _flash_bwd_kv(q, k, h, ccol, crow, lse, delta, resid, do, H, dh, name=f"{tag}_flash_bwd_kv")
    dhq, dhk, df, dbf, dqg, dkg = _fox_prep_bwd(h, w["bf"], w["qg"], w["kg"], dqq, dkk, dcrow, H, dh, name=f"{tag}_prep_bwd")
    dhh = jnp.concatenate([dhq, dhk, dvv, dz, df], axis=1)
    dx = _matmul(dhh, w["w_in"], "nt", name=f"{tag}_in_dx")
    dw_in = _matmul(x, dhh, "tn", name=f"{tag}_in_dw")
    grads = {"w_in": dw_in, "w_out": dw_out, "b_f": dbf[0, :H],
             "q_norm_g": dqg[0, :dh] + dqg[0, dh:], "k_norm_g": dkg[0, :dh] + dkg[0, dh:]}
    return dx, grads


def _pad_cols(w, n):
    return jnp.pad(w, ((0, 0), (0, n - w.shape[1])))


def _build_layers(full, small):
    depth = small["ln_g"].shape[0]
    gh = small["gdn_a_log"].shape[1]
    fh, dh = small["fox_b_f"].shape[1], small["fox_q_norm_g"].shape[1]
    layers = []
    for i in range(depth):
        j = i // 2
        w = {"ln_g": small["ln_g"][i][None], "ln_b": small["ln_b"][i][None],
             "w_gate": full["ple_w_gate"][i], "w_proj": full["ple_w_proj"][i]}
        if i % 2 == 0:
            hp = jnp.zeros((SUBLANES, LANES), F32).at[0, :gh].set(small["gdn_a_log"][j]).at[1, :gh].set(small["gdn_dt_bias"][j])
            w.update(kind="gdn", H=gh, w_in=_pad_cols(full["gdn_w_in"][j], 4 * gh * LANES + LANES),
                     conv=full["gdn_conv_w"][j], hp=hp, ng=small["gdn_norm_g"][j][None], w_out=full["gdn_w_out"][j])
        else:
            bf = jnp.zeros((1, LANES), F32).at[0, :fh].set(small["fox_b_f"][j])
            w.update(kind="fox", H=fh, dh=dh, w_in=_pad_cols(full["fox_w_in"][j], 4 * fh * dh + LANES), bf=bf,
                     qg=jnp.tile(small["fox_q_norm_g"][j], 2)[None], kg=jnp.tile(small["fox_k_norm_g"][j], 2)[None],
                     w_out=full["fox_w_out"][j])
        layers.append(w)
    return layers


def _local_step(x, p, target, layers):
    depth = len(layers)
    alpha = (2 * depth) ** 0.25
    saved = []
    for i, w in enumerate(layers):
        tag = f"l{i}"
        if w["kind"] == "gdn":
            y, res = _gdn_layer_fwd(x, w, tag)
        else:
            y, res = _fox_layer_fwd(x, w, tag)
        x1 = _ln_fwd(x, y, w["ln_g"], w["ln_b"], alpha, name=f"{tag}_ln")
        gate_pre = _matmul(x1, w["w_gate"], "nn", name=f"{tag}_gate_mm")
        pp = _matmul(p[i], w["w_proj"], "nn", name=f"{tag}_proj_mm")
        x2 = _ple_fwd(x1, gate_pre, pp, name=f"{tag}_ple")
        saved.append((res, x, y, x1, gate_pre, pp))
        x = x2
    dx, loss_tile = _loss_head(x, target, name="loss_head")
    grads = [None] * depth
    for i in reversed(range(depth)):
        w = layers[i]
        tag = f"l{i}"
        res, xin, y, x1, gate_pre, pp = saved[i]
        dgp, dpp = _ple_bwd(dx, gate_pre, pp, name=f"{tag}_ple_bwd")
        dx1 = _add(dx, _matmul(dgp, w["w_gate"], "nt", name=f"{tag}_gate_dx"), name=f"{tag}_dx1")
        dw_gate = _matmul(x1, dgp, "tn", name=f"{tag}_gate_dw")
        dw_proj = _matmul(p[i], dpp, "tn", name=f"{tag}_proj_dw")
        dy, dg, db = _ln_bwd(xin, y, w["ln_g"], w["ln_b"], dx1, alpha, name=f"{tag}_ln_bwd")
        if w["kind"] == "gdn":
            dxm, g = _gdn_layer_bwd(dy, res, w, tag)
        else:
            dxm, g = _fox_layer_bwd(dy, res, w, tag)
        dx = _axpy(alpha, dy, dxm, name=f"{tag}_dx")
        g.update({"w_gate": dw_gate, "w_proj": dw_proj, "ln_g": dg[0], "ln_b": db[0]})
        grads[i] = g
    return loss_tile, dx, grads


MESH_ID = pl.DeviceIdType.MESH
HBM_SPEC = pl.BlockSpec(memory_space=pl.ANY)
PACK_COLS = 1024
PACK_ROWS = 256


def _all_gather(shards, name):
    nt = len(shards)

    def body(*refs):
        x_refs, out_refs = refs[:nt], refs[nt:2 * nt]
        send_sems, recv_sems, local_sems = refs[2 * nt:]
        x, y, c = lax.axis_index("x"), lax.axis_index("y"), lax.axis_index("c")
        me, sibling = (x, y, c), (x, y, 1 - c)
        chips = [(1 - x, y), (x, 1 - y), (1 - x, 1 - y)]

        def slot(t, px, py, pc):
            return out_refs[t].at[4 * px + 2 * py + pc]

        def copy(k, t, block, to, src=None):
            return pltpu.make_async_remote_copy(
                src_ref=slot(t, *block) if src is None else src, dst_ref=slot(t, *block),
                send_sem=send_sems.at[k, t], recv_sem=recv_sems.at[k, t], device_id=to, device_id_type=MESH_ID)

        every = range(nt)
        mine = [pltpu.make_async_copy(x_refs[t], slot(t, *me), local_sems.at[t]) for t in every]
        for cp in mine:
            cp.start()
        first = [copy(0, t, me, sibling, src=x_refs[t]) for t in every]
        first += [copy(1 + j, t, me, (*chip, c), src=x_refs[t]) for j, chip in enumerate(chips) for t in every]
        for cp in first:
            cp.start()
        passed = []
        for j, chip in enumerate(chips):
            for t in every:
                copy(1 + j, t, (*chip, c), me).wait_recv()
                passed.append(copy(4 + j, t, (*chip, c), sibling))
                passed[-1].start()
        for t in every:
            copy(0, t, sibling, me).wait_recv()
        for j, chip in enumerate(chips):
            for t in every:
                copy(4 + j, t, (*chip, 1 - c), me).wait_recv()
        for cp in first + passed:
            cp.wait_send()
        for cp in mine:
            cp.wait()

    return pl.pallas_call(
        body,
        out_shape=[jax.ShapeDtypeStruct((N_DEV, *s.shape), s.dtype) for s in shards],
        in_specs=[HBM_SPEC] * nt,
        out_specs=[HBM_SPEC] * nt,
        scratch_shapes=[pltpu.SemaphoreType.DMA((7, nt)), pltpu.SemaphoreType.DMA((7, nt)), pltpu.SemaphoreType.DMA((nt,))],
        name=name,
    )(*shards)


N_CHIPS = 4


def _sibling_swap(slabs, name):
    nt = len(slabs)

    def body(*refs):
        g_refs, out_refs = refs[:nt], refs[nt:2 * nt]
        send_sems, recv_sems = refs[2 * nt:]
        x, y, c = lax.axis_index("x"), lax.axis_index("y"), lax.axis_index("c")
        copies = [pltpu.make_async_remote_copy(src_ref=g_refs[t].at[1 - c], dst_ref=out_refs[t], send_sem=send_sems.at[t],
                                               recv_sem=recv_sems.at[t], device_id=(x, y, 1 - c), device_id_type=MESH_ID)
                  for t in range(nt)]
        for cp in copies:
            cp.start()
        for cp in copies:
            cp.wait()

    return pl.pallas_call(
        body,
        out_shape=[jax.ShapeDtypeStruct(s.shape[1:], s.dtype) for s in slabs],
        in_specs=[HBM_SPEC] * nt,
        out_specs=[HBM_SPEC] * nt,
        scratch_shapes=[pltpu.SemaphoreType.DMA((nt,)), pltpu.SemaphoreType.DMA((nt,))],
        name=name,
    )(*slabs)


def _add_own_half(core, slabs, received, name):
    shape = received.shape
    R, C = math.prod(shape[:-1]), shape[-1]
    tr = _pick(R, 512, SUBLANES)

    def body(core_ref, mine_ref, got_ref, o_ref):
        o_ref[...] = mine_ref[0] + got_ref[...]

    return pl.pallas_call(
        body,
        grid_spec=pltpu.PrefetchScalarGridSpec(
            num_scalar_prefetch=1,
            grid=(R // tr,),
            in_specs=[pl.BlockSpec((1, tr, C), lambda i, core: (core[0], i, 0)), pl.BlockSpec((tr, C), lambda i, core: (i, 0))],
            out_specs=pl.BlockSpec((tr, C), lambda i, core: (i, 0))),
        out_shape=jax.ShapeDtypeStruct((R, C), F32),
        compiler_params=_params(("parallel",)),
        name=name,
    )(core, slabs.reshape(2, R, C), received.reshape(R, C)).reshape(shape)


def _chip_all_to_all(slabs, name):
    nt = len(slabs)

    def body(*refs):
        g_refs, out_refs = refs[:nt], refs[nt:2 * nt]
        send_sems, recv_sems, local_sems = refs[2 * nt:]
        x, y, c = lax.axis_index("x"), lax.axis_index("y"), lax.axis_index("c")
        me = 2 * x + y
        mine = [pltpu.make_async_copy(g_refs[t].at[me], out_refs[t].at[me], local_sems.at[t]) for t in range(nt)]
        for cp in mine:
            cp.start()
        copies = []
        for k in range(1, N_CHIPS):
            px = 1 - x if k & 2 else x
            py = 1 - y if k & 1 else y
            peer = 2 * px + py
            for t in range(nt):
                copies.append(tuple(
                    pltpu.make_async_remote_copy(src_ref=g_refs[t].at[peer], dst_ref=out_refs[t].at[dst],
                                                 send_sem=send_sems.at[k - 1, t], recv_sem=recv_sems.at[k - 1, t],
                                                 device_id=(px, py, c), device_id_type=MESH_ID)
                    for dst in (me, peer)))
        for send, _ in copies:
            send.start()
        for send, arrive in copies:
            arrive.wait_recv()
            send.wait_send()
        for cp in mine:
            cp.wait()

    return pl.pallas_call(
        body,
        out_shape=[jax.ShapeDtypeStruct(s.shape, s.dtype) for s in slabs],
        in_specs=[HBM_SPEC] * nt,
        out_specs=[HBM_SPEC] * nt,
        scratch_shapes=[pltpu.SemaphoreType.DMA((N_CHIPS - 1, nt)), pltpu.SemaphoreType.DMA((N_CHIPS - 1, nt)),
                        pltpu.SemaphoreType.DMA((nt,))],
        name=name,
    )(*slabs)


def _pack(flats, dtype):
    flat = jnp.concatenate([f.astype(dtype).reshape(-1) for f in flats])
    unit = PACK_ROWS * PACK_COLS
    n = -(-flat.shape[0] // unit) * unit
    return jnp.pad(flat, (0, n - flat.shape[0])).reshape(n // PACK_COLS, PACK_COLS)


def _unpack(buf, shapes):
    lead = buf.shape[:-2]
    flat = buf.reshape(*lead, -1)
    out, off = [], 0
    for s in shapes:
        n = math.prod(s)
        out.append(flat[..., off:off + n].reshape(*lead, *s))
        off += n
    return out


_ROW_SPLIT = ("ple_w_gate", "gdn_w_out", "fox_w_out")
_COL_SPLIT = ("ple_w_proj", "gdn_w_in", "gdn_conv_w", "fox_w_in")
_SHARDED = ("ple_w_gate", "ple_w_proj", "gdn_w_in", "gdn_conv_w", "gdn_w_out", "fox_w_in", "fox_w_out")
_REPLICATED = ("ln_g", "ln_b", "gdn_a_log", "gdn_dt_bias", "gdn_norm_g", "fox_b_f", "fox_q_norm_g", "fox_k_norm_g")
_WEIGHTS = ("ln_g", "ln_b", "ple_w_gate", "ple_w_proj", "gdn_w_in", "gdn_conv_w", "gdn_a_log", "gdn_dt_bias",
            "gdn_norm_g", "gdn_w_out", "fox_w_in", "fox_b_f", "fox_q_norm_g", "fox_k_norm_g", "fox_w_out")


def _join(name, gathered):
    n, l, a, b = gathered.shape
    if name in _ROW_SPLIT:
        return gathered.transpose(1, 0, 2, 3).reshape(l, n * a, b)
    return gathered.transpose(1, 2, 0, 3).reshape(l, a, n * b)


def _split(name, full):
    l, a, b = full.shape
    if name in _ROW_SPLIT:
        return full.reshape(l, N_DEV, a // N_DEV, b).transpose(1, 0, 2, 3)
    return full.reshape(l, a, N_DEV, b // N_DEV).transpose(2, 0, 1, 3)


def _adamw(w, g_parts, m, v, name):
    shape = w.shape
    n_parts = g_parts.shape[0]
    R, C = math.prod(shape[:-1]), shape[-1]
    tr = _pick(R, 256, SUBLANES)
    c1 = 1.0 - ADAM_B1 ** ADAM_STEP
    c2 = 1.0 - ADAM_B2 ** ADAM_STEP

    def body(w_ref, g_ref, m_ref, v_ref, go_ref, d_ref, mo_ref, vo_ref):
        gv = g_ref[0]
        for s in range(1, n_parts):
            gv = gv + g_ref[s]
        mn = ADAM_B1 * m_ref[...] + (1.0 - ADAM_B1) * gv
        vn = ADAM_B2 * v_ref[...] + (1.0 - ADAM_B2) * jnp.square(gv)
        go_ref[...] = gv
        d_ref[...] = -ADAM_LR * ((mn / c1) / (jnp.sqrt(vn / c2) + ADAM_EPS) + ADAM_WD * w_ref[...])
        mo_ref[...] = mn
        vo_ref[...] = vn

    row = pl.BlockSpec((tr, C), lambda i: (i, 0))
    outs = pl.pallas_call(
        body,
        grid=(R // tr,),
        in_specs=[row, pl.BlockSpec((n_parts, tr, C), lambda i: (0, i, 0)), row, row],
        out_specs=[row] * 4,
        out_shape=[jax.ShapeDtypeStruct((R, C), F32)] * 4,
        compiler_params=_params(("parallel",)),
        name=name,
    )(w.reshape(R, C), g_parts.reshape(n_parts, R, C), m.reshape(R, C), v.reshape(R, C))
    return [o.reshape(shape) for o in outs]


def _train_step(x, p, target, w, m, v):
    shards = [w[n] if n == "gdn_conv_w" else w[n].astype(MXU_DTYPE) for n in _SHARDED]
    gathered = _all_gather(shards, name="gather_weights")
    full = {n: _join(n, part) for n, part in zip(_SHARDED, gathered)}
    layers = _build_layers(full, {n: w[n] for n in _REPLICATED})

    loss_tile, dx, grads = _local_step(x[0], p[:, 0], target[0], layers)
    loss = lax.psum(loss_tile[0, 0], ("x", "y", "c"))

    depth = len(layers)
    gdn_l = [i for i in range(depth) if i % 2 == 0]
    fox_l = [i for i in range(depth) if i % 2 == 1]

    def stack(key, idx):
        return jnp.stack([grads[i][key] for i in idx])

    full_g = {
        "ple_w_gate": stack("w_gate", range(depth)), "ple_w_proj": stack("w_proj", range(depth)),
        "gdn_w_in": stack("w_in", gdn_l)[..., :w["gdn_w_in"].shape[-1] * N_DEV], "gdn_conv_w": stack("conv", gdn_l),
        "gdn_w_out": stack("w_out", gdn_l),
        "fox_w_in": stack("w_in", fox_l)[..., :w["fox_w_in"].shape[-1] * N_DEV], "fox_w_out": stack("w_out", fox_l)}
    small_g = {
        "ln_g": stack("ln_g", range(depth)), "ln_b": stack("ln_b", range(depth)),
        "gdn_a_log": stack("a_log", gdn_l), "gdn_dt_bias": stack("dt_bias", gdn_l), "gdn_norm_g": stack("norm_g", gdn_l),
        "fox_b_f": stack("b_f", fox_l), "fox_q_norm_g": stack("q_norm_g", fox_l), "fox_k_norm_g": stack("k_norm_g", fox_l)}

    by_core = []
    for n in _SHARDED:
        per_device = _split(n, full_g[n])
        by_core.append(jnp.swapaxes(per_device.reshape(N_CHIPS, 2, *per_device.shape[1:]), 0, 1))
    from_sibling = _sibling_swap(by_core, name="swap_grads")
    core = lax.axis_index("c").astype(jnp.int32).reshape(1)
    chip_sums = [_add_own_half(core, mine, got, name=f"chip_sum_{n}")
                 for n, mine, got in zip(_SHARDED, by_core, from_sibling)]
    g_parts = dict(zip(_SHARDED, _chip_all_to_all(chip_sums, name="scatter_grads")))
    small_all = _all_gather([_pack([small_g[n] for n in _REPLICATED], F32)], name="gather_small_grads")[0]
    g_parts.update(zip(_REPLICATED, _unpack(small_all, [w[n].shape for n in _REPLICATED])))

    g, delta, new_m, new_v = {}, {}, {}, {}
    for n in _WEIGHTS:
        g[n], delta[n], new_m[n], new_v[n] = _adamw(w[n], g_parts[n], m[n], v[n], name=f"adamw_{n}")
    return (loss, dx[None], *[g[n] for n in _WEIGHTS], *[delta[n] for n in _WEIGHTS],
            *[new_m[n] for n in _WEIGHTS], *[new_v[n] for n in _WEIGHTS])


def kernel(x, p, ln_g, ln_b, ple_w_gate, ple_w_proj, gdn_w_in, gdn_conv_w, gdn_a_log, gdn_dt_bias, gdn_norm_g, gdn_w_out, fox_w_in, fox_b_f, fox_q_norm_g, fox_k_norm_g, fox_w_out, loss_target, m_ln_g, m_ln_b, m_ple_w_gate, m_ple_w_proj, m_gdn_w_in, m_gdn_conv_w, m_gdn_a_log, m_gdn_dt_bias, m_gdn_norm_g, m_gdn_w_out, m_fox_w_in, m_fox_b_f, m_fox_q_norm_g, m_fox_k_norm_g, m_fox_w_out, v_ln_g, v_ln_b, v_ple_w_gate, v_ple_w_proj, v_gdn_w_in, v_gdn_conv_w, v_gdn_a_log, v_gdn_dt_bias, v_gdn_norm_g, v_gdn_w_out, v_fox_w_in, v_fox_b_f, v_fox_q_norm_g, v_fox_k_norm_g, v_fox_w_out):
    given = dict(locals())
    w = {n: given[n] for n in _WEIGHTS}
    m = {n: given["m_" + n] for n in _WEIGHTS}
    v = {n: given["v_" + n] for n in _WEIGHTS}
    return _train_step(x, p, loss_target, w, m, v)
```

```python
import functools
import math

import jax
import jax.numpy as jnp
from jax import lax
from jax.experimental import pallas as pl
from jax.experimental.pallas import tpu as pltpu

F32 = jnp.float32
BF16 = jnp.bfloat16
MXU_DTYPE = BF16
HI = lax.Precision.HIGHEST

N_DEV = 8
LANES = 128
SUBLANES = 8
VMEM_BYTES = 64 * 1024 * 1024

GDN_CHUNK = 64
GDN_CONV = 4
LN_EPS = 1e-5
RMS_EPS = 1e-6
NEG = -1e30

ADAM_LR = 0.001
ADAM_B1 = 0.9
ADAM_B2 = 0.999
ADAM_EPS = 1e-08
ADAM_WD = 0.01
ADAM_STEP = 10


def _params(semantics, vmem_mb=40):
    return pltpu.CompilerParams(dimension_semantics=semantics, vmem_limit_bytes=vmem_mb * 1024 * 1024)


def _pick(dim, cap, unit=LANES):
    if dim <= cap:
        return dim
    best = None
    for t in range(unit, cap + 1, unit):
        if dim % t == 0:
            best = t
    assert best is not None, (dim, cap)
    return best


def _dims(dims, ndim):
    if ndim == 2:
        return (dims, ((), ()))
    return (((dims[0][0] + 1,), (dims[1][0] + 1,)), ((0,), (0,)))


def _dot(a, b, dims):
    return lax.dot_general(a.astype(MXU_DTYPE), b.astype(MXU_DTYPE), _dims(dims, a.ndim), preferred_element_type=F32)


_NN = ((1,), (0,))
_NT = ((1,), (1,))
_TN = ((0,), (0,))


@jax.custom_vjp
def _mm_nn(a, b):
    return _dot(a, b, _NN)


@jax.custom_vjp
def _mm_nt(a, b):
    return _dot(a, b, _NT)


@jax.custom_vjp
def _mm_tn(a, b):
    return _dot(a, b, _TN)


_mm_nn.defvjp(lambda a, b: (_dot(a, b, _NN), (a, b)), lambda r, g: (_mm_nt(g, r[1]), _mm_tn(r[0], g)))
_mm_nt.defvjp(lambda a, b: (_dot(a, b, _NT), (a, b)), lambda r, g: (_mm_nn(g, r[1]), _mm_tn(g, r[0])))
_mm_tn.defvjp(lambda a, b: (_dot(a, b, _TN), (a, b)), lambda r, g: (_mm_nt(r[1], g), _mm_nn(r[0], g)))


def _mm_hi(a, b, precision=HI):
    return lax.dot_general(a, b, _dims(_NN, a.ndim), precision=precision, preferred_element_type=F32)


def _mm_3x(a, b):
    return _mm_hi(a, b, lax.Precision.HIGH)


def _sigmoid(x):
    return 1.0 / (1.0 + jnp.exp(-x))


def _silu(x):
    return x * _sigmoid(x)


def _softplus(x):
    return jnp.maximum(x, 0.0) + jnp.log(1.0 + jnp.exp(-jnp.abs(x)))


def _iota2(shape, dim):
    return lax.broadcasted_iota(jnp.int32, shape, dim)


def _lane_pick(tile, lane):
    return jnp.sum(jnp.where(_iota2(tile.shape, 1) == lane, tile, 0.0), axis=1, keepdims=True)


def _lane_put(col, lane, width=LANES):
    return jnp.where(_iota2((col.shape[0], width), 1) == lane, col, 0.0)


def _matmul(a, b, mode, out_dtype=F32, *, name, tm=1024, tn=1408, tk=1408, a_cols=None, b_cols=None):
    def cols(arr, rng):
        return (0, arr.shape[1]) if rng is None else rng

    a0, an = cols(a, a_cols)
    b0, bn = cols(b, b_cols)
    if mode == "nn":
        M, K, N = a.shape[0], an, bn
        assert b.shape[0] == K
    elif mode == "nt":
        M, K, N = a.shape[0], an, b.shape[0]
        assert bn == K
    else:
        K, M, N = a.shape[0], an, bn
        assert b.shape[0] == K
    tm, tn, tk = _pick(M, tm), _pick(N, tn), _pick(K, tk)
    nk = K // tk
    if mode == "nn":
        assert a0 % tk == 0 and b0 % tn == 0
        a_spec = pl.BlockSpec((tm, tk), lambda i, j, k: (i, a0 // tk + k))
        b_spec = pl.BlockSpec((tk, tn), lambda i, j, k: (k, b0 // tn + j))
        dims = _NN
    elif mode == "nt":
        assert a0 % tk == 0 and b0 % tk == 0
        a_spec = pl.BlockSpec((tm, tk), lambda i, j, k: (i, a0 // tk + k))
        b_spec = pl.BlockSpec((tn, tk), lambda i, j, k: (j, b0 // tk + k))
        dims = _NT
    else:
        assert a0 % tm == 0 and b0 % tn == 0
        a_spec = pl.BlockSpec((tk, tm), lambda i, j, k: (k, a0 // tm + i))
        b_spec = pl.BlockSpec((tk, tn), lambda i, j, k: (k, b0 // tn + j))
        dims = _TN

    def body(a_ref, b_ref, o_ref, acc_ref):
        k = pl.program_id(2)

        @pl.when(k == 0)
        def _():
            acc_ref[...] = jnp.zeros_like(acc_ref)

        acc_ref[...] += _dot(a_ref[...], b_ref[...], dims)

        @pl.when(k == nk - 1)
        def _():
            o_ref[...] = acc_ref[...].astype(o_ref.dtype)

    return pl.pallas_call(
        body,
        grid=(M // tm, N // tn, nk),
        in_specs=[a_spec, b_spec],
        out_specs=pl.BlockSpec((tm, tn), lambda i, j, k: (i, j)),
        out_shape=jax.ShapeDtypeStruct((M, N), out_dtype),
        scratch_shapes=[pltpu.VMEM((tm, tn), F32)],
        compiler_params=_params(("parallel", "parallel", "arbitrary"), 48),
        name=name,
    )(a, b)


def _rowwise(fn, rows, consts, out_rows, out_accs, *, tile, name, reverse=False, carries=(), vmem_mb=40):
    rows = [r if isinstance(r, tuple) else (r, r.shape[1], 0) for r in rows]
    S = rows[0][0].shape[0]
    tile = _pick(S, tile, SUBLANES)
    nt = S // tile
    nr, nc, no, na = len(rows), len(consts), len(out_rows), len(out_accs)

    def ridx(i):
        return nt - 1 - i if reverse else i

    in_specs = [pl.BlockSpec((tile, w), functools.partial(lambda i, cb: (ridx(i), cb), cb=cb)) for _, w, cb in rows]
    in_specs += [pl.BlockSpec(c.shape, functools.partial(lambda i, nd: (0,) * nd, nd=c.ndim)) for c in consts]
    out_specs = [pl.BlockSpec((tile, c), lambda i: (ridx(i), 0)) for c, _ in out_rows]
    out_specs += [pl.BlockSpec(s, functools.partial(lambda i, nd: (0,) * nd, nd=len(s))) for s, _ in out_accs]
    out_shape = [jax.ShapeDtypeStruct((S, c), d) for c, d in out_rows]
    out_shape += [jax.ShapeDtypeStruct(s, d) for s, d in out_accs]

    def body(*refs):
        rin, cin = refs[:nr], refs[nr:nr + nc]
        rout, aout = refs[nr + nc:nr + nc + no], refs[nr + nc + no:nr + nc + no + na]
        carr = refs[nr + nc + no + na:]
        step = pl.program_id(0)

        @pl.when(step == 0)
        def _():
            for r in aout + carr:
                r[...] = jnp.zeros_like(r)

        outs, accs, newc = fn([r[...] for r in rin], [c[...] for c in cin], [c[...] for c in carr])
        for r, o in zip(rout, outs, strict=True):
            r[...] = o.astype(r.dtype)
        for r, v in zip(aout, accs, strict=True):
            r[...] += v
        for r, v in zip(carr, newc, strict=True):
            r[...] = v

    res = pl.pallas_call(
        body,
        grid=(nt,),
        in_specs=in_specs,
        out_specs=out_specs,
        out_shape=out_shape,
        scratch_shapes=[pltpu.VMEM(s, F32) for s in carries],
        compiler_params=_params(("arbitrary",), vmem_mb),
        name=name,
    )(*[r[0] for r in rows], *consts)
    return res


def _ln_fn(x, y, g, b, alpha):
    r = alpha * x + y
    mu = jnp.mean(r, -1, keepdims=True)
    var = jnp.mean(jnp.square(r - mu), -1, keepdims=True)
    return (r - mu) * lax.rsqrt(var + LN_EPS) * g + b


def _ln_fwd(x, y, g, b, alpha, name):
    D = x.shape[1]

    def fn(rows, consts, _):
        return [_ln_fn(rows[0], rows[1], consts[0], consts[1], alpha)], [], []

    return _rowwise(fn, [x, y], [g, b], [(D, F32)], [], tile=512, name=name)[0]


def _ln_bwd(x, y, g, b, dx1, alpha, name):
    D = x.shape[1]

    def fn(rows, consts, _):
        xv, yv, d = rows
        _, vjp = jax.vjp(lambda yy, gg, bb: _ln_fn(xv, yy, gg, bb, alpha), yv, consts[0], consts[1])
        dy, dg, db = vjp(d)
        return [dy], [dg, db], []

    return _rowwise(fn, [x, y, dx1], [g, b], [(D, F32)], [((1, D), F32), ((1, D), F32)], tile=512, name=name)


def _ple_fwd(x1, gate_pre, pp, name):
    D = x1.shape[1]

    def fn(rows, _, __):
        return [rows[0] + _sigmoid(rows[1]) * rows[2]], [], []

    return _rowwise(fn, [x1, gate_pre, pp], [], [(D, F32)], [], tile=512, name=name)[0]


def _ple_bwd(dx2, gate_pre, pp, name):
    D = dx2.shape[1]

    def fn(rows, _, __):
        d, gp, ppv = rows
        s = _sigmoid(gp)
        return [d * ppv * s * (1.0 - s), d * s], [], []

    return _rowwise(fn, [dx2, gate_pre, pp], [], [(D, F32), (D, F32)], [], tile=512, name=name)


def _add(a, b, name):
    def fn(rows, _, __):
        return [rows[0] + rows[1]], [], []

    return _rowwise(fn, [a, b], [], [(a.shape[1], F32)], [], tile=512, name=name)[0]


def _axpy(alpha, a, b, name):
    def fn(rows, _, __):
        return [alpha * rows[0] + rows[1]], [], []

    return _rowwise(fn, [a, b], [], [(a.shape[1], F32)], [], tile=512, name=name)[0]


def _loss_head(y, target, name):
    D = y.shape[1]

    def fn(rows, _, __):
        e = rows[0] - rows[1]
        part = 0.5 * jnp.sum(jnp.sum(e * e, axis=1, keepdims=True), axis=0, keepdims=True) / D
        return [e / D], [jnp.broadcast_to(part, (SUBLANES, LANES))], []

    return _rowwise(fn, [y, target], [], [(D, F32)], [((SUBLANES, LANES), F32)], tile=512, name=name)


def _conv_fwd(h, w, n_cols, name):
    S = h.shape[0]
    T = _pick(S, 512, SUBLANES)
    CB = _pick(n_cols, 512)
    nt = S // T
    K = GDN_CONV

    def body(x_ref, halo_ref, w_ref, o_ref, buf):
        i = pl.program_id(1)
        buf[0:SUBLANES, :] = jnp.where(i > 0, halo_ref[...], 0.0)
        buf[SUBLANES:, :] = x_ref[...]
        acc = jnp.zeros((T, CB), F32)
        for k in range(K):
            acc = acc + w_ref[k:k + 1, :] * buf[pl.ds(SUBLANES - (K - 1) + k, T), :]
        o_ref[...] = acc

    return pl.pallas_call(
        body,
        grid=(n_cols // CB, nt),
        in_specs=[pl.BlockSpec((T, CB), lambda c, i: (i, c)),
                  pl.BlockSpec((SUBLANES, CB), lambda c, i: (jnp.maximum(i * (T // SUBLANES) - 1, 0), c)),
                  pl.BlockSpec((K, CB), lambda c, i: (0, c))],
        out_specs=pl.BlockSpec((T, CB), lambda c, i: (i, c)),
        out_shape=jax.ShapeDtypeStruct((S, n_cols), F32),
        scratch_shapes=[pltpu.VMEM((T + SUBLANES, CB), F32)],
        compiler_params=_params(("parallel", "parallel")),
        name=name,
    )(h, h, w)


def _conv_bwd(dc, h, w, h_col0, name):
    S, n_cols = dc.shape
    T = _pick(S, 512, SUBLANES)
    CB = _pick(n_cols, 512)
    nt = S // T
    K = GDN_CONV
    assert h_col0 % CB == 0
    hb = h_col0 // CB

    def body(d_ref, halo_ref, x_ref, w_ref, dx_ref, dw_ref, buf):
        i = pl.program_id(1)

        @pl.when(i == 0)
        def _():
            dw_ref[...] = jnp.zeros_like(dw_ref)

        buf[0:T, :] = d_ref[...]
        buf[T:, :] = jnp.where(i < nt - 1, halo_ref[...], 0.0)
        x = x_ref[...]
        acc = jnp.zeros((T, CB), F32)
        for k in range(K):
            shifted = buf[pl.ds(K - 1 - k, T), :]
            acc = acc + w_ref[k:k + 1, :] * shifted
            dw_ref[k:k + 1, :] += jnp.sum(shifted * x, axis=0, keepdims=True)
        dx_ref[...] = acc

    last = S // SUBLANES - 1
    return pl.pallas_call(
        body,
        grid=(n_cols // CB, nt),
        in_specs=[pl.BlockSpec((T, CB), lambda c, i: (i, c)),
                  pl.BlockSpec((SUBLANES, CB), lambda c, i: (jnp.minimum((i + 1) * (T // SUBLANES), last), c)),
                  pl.BlockSpec((T, CB), lambda c, i: (i, hb + c)),
                  pl.BlockSpec((K, CB), lambda c, i: (0, c))],
        out_specs=[pl.BlockSpec((T, CB), lambda c, i: (i, c)),
                   pl.BlockSpec((SUBLANES, CB), lambda c, i: (0, c))],
        out_shape=[jax.ShapeDtypeStruct((S, n_cols), F32), jax.ShapeDtypeStruct((SUBLANES, n_cols), F32)],
        scratch_shapes=[pltpu.VMEM((T + SUBLANES, CB), F32)],
        compiler_params=_params(("parallel", "arbitrary")),
        name=name,
    )(dc, dc, h, w)


def _neumann_inverse(L):
    C = L.shape[-1]
    eye = (_iota2((C, C), 0) == _iota2((C, C), 1)).astype(F32)
    X = eye - L
    P = L
    for _ in range(max(0, math.ceil(math.log2(C)) - 1)):
        P = _mm_3x(P, P)
        X = _mm_3x(X, eye + P)
    return X


@jax.custom_vjp
def _unit_lower_inverse(L):
    return _neumann_inverse(L)


def _unit_lower_inverse_bwd(T, dT):
    Tt = jnp.swapaxes(T, -1, -2)
    return (-_mm_3x(_mm_3x(Tt, dT), Tt),)


_unit_lower_inverse.defvjp(lambda L: (_neumann_inverse(L),) * 2, _unit_lower_inverse_bwd)


@jax.custom_vjp
def _known_inverse(L, T):
    return T


_known_inverse.defvjp(lambda L, T: (T, T), lambda T, dT: (*_unit_lower_inverse_bwd(T, dT), jnp.zeros_like(T)))


def _gdn_chunk(cq, ck, cv, zz, bcol, acol, alog, dtb, ng, state, inverse=None):
    G, C, dk = cq.shape
    q = _silu(cq)
    k = _silu(ck)
    v = _silu(cv)
    q = q * lax.rsqrt(jnp.sum(q * q, -1, keepdims=True) + RMS_EPS) * (dk ** -0.5)
    k = k * lax.rsqrt(jnp.sum(k * k, -1, keepdims=True) + RMS_EPS)
    beta = _sigmoid(bcol)
    g = -jnp.exp(alog) * _softplus(acol + dtb)

    row, col = _iota2((C, C), 0), _iota2((C, C), 1)
    causal, strict, eye = row >= col, row > col, row == col
    g_rows = jnp.swapaxes(jnp.broadcast_to(g, (G, C, C)), -1, -2)
    gc = jnp.sum(jnp.where(causal, g_rows, 0.0), axis=-1, keepdims=True)
    gcb = jnp.broadcast_to(gc, (G, C, C))
    g_last = jnp.sum(jnp.sum(jnp.where((row == C - 1) & (col == 0), gcb, 0.0), axis=-1, keepdims=True), axis=-2, keepdims=True)
    gc_rows = jnp.swapaxes(gcb, -1, -2)
    decay = jnp.exp(jnp.where(causal, gcb - gc_rows, NEG))

    kb = k * beta
    L = jnp.where(strict, _mm_nt(kb, k) * decay, 0.0)
    T = _unit_lower_inverse(L) if inverse is None else _known_inverse(L, inverse)
    u = _mm_3x(T, v * beta)
    w = _mm_3x(T, kb * jnp.exp(gc))
    a_qk = jnp.where(causal, _mm_nt(q, k) * decay, 0.0)
    q_dec = q * jnp.exp(gc)
    k_dec = k * jnp.exp(g_last - gc)
    v_new = u - _mm_nn(w, state)
    o = _mm_nn(q_dec, state) + _mm_nn(a_qk, v_new)
    new_state = state * jnp.exp(g_last) + _mm_tn(k_dec, v_new)
    y = o * lax.rsqrt(jnp.mean(o * o, -1, keepdims=True) + RMS_EPS) * ng * _silu(zz)
    return (y, new_state), T


GDN_HEADS_PER_STEP = 8


def _gdn_specs(H, dk, G, chunk_of=lambda n: n):
    C = GDN_CHUNK
    NG = H // G
    cq = pl.BlockSpec((C, G * dk), lambda n, h: (chunk_of(n), h))
    ck = pl.BlockSpec((C, G * dk), lambda n, h: (chunk_of(n), NG + h))
    cv = pl.BlockSpec((C, G * dk), lambda n, h: (chunk_of(n), 2 * NG + h))
    zz = pl.BlockSpec((C, G * dk), lambda n, h: (chunk_of(n), 3 * NG + h))
    ba = pl.BlockSpec((C, LANES), lambda n, h: (chunk_of(n), 4 * H * dk // LANES))
    return cq, ck, cv, zz, ba


def _gdn_step_args(cq_ref, ck_ref, cv_ref, z_ref, ba_ref, hp_ref, hg, G, H, dk):
    ba = ba_ref[...]

    def heads(ref):
        return jnp.stack([ref[:, g * dk:(g + 1) * dk] for g in range(G)])

    def picks(tile, offset):
        return jnp.stack([_lane_pick(tile, offset + hg * G + g) for g in range(G)])

    return (heads(cq_ref), heads(ck_ref), heads(cv_ref), heads(z_ref), picks(ba, 0), picks(ba, H),
            picks(hp_ref[0:1, :], 0), picks(hp_ref[1:2, :], 0))


def _gdn_scan_fwd(c, h, hp, ng, H, name):
    S = c.shape[0]
    dk = c.shape[1] // (3 * H)
    assert dk == LANES
    C = GDN_CHUNK
    NC = S // C

    G = min(GDN_HEADS_PER_STEP, H)
    assert H % G == 0
    NG = H // G

    def body(cq_ref, ck_ref, cv_ref, z_ref, ba_ref, hp_ref, ng_ref, y_ref, s_ref, t_ref, state):
        n, hg = pl.program_id(0), pl.program_id(1)
        heads = pl.ds(hg * G, G)

        @pl.when(n == 0)
        def _():
            state[heads] = jnp.zeros((G, dk, dk), F32)

        st = state[heads]
        s_ref[0] = st
        (y, new_state), inverse = _gdn_chunk(
            *_gdn_step_args(cq_ref, ck_ref, cv_ref, z_ref, ba_ref, hp_ref, hg, G, H, dk), ng_ref[...], st)
        t_ref[0] = inverse
        for g in range(G):
            y_ref[:, g * dk:(g + 1) * dk] = y[g].astype(y_ref.dtype)
        state[heads] = new_state

    cq, ck, cv, zz, ba = _gdn_specs(H, dk, G)
    return pl.pallas_call(
        body,
        grid=(NC, NG),
        in_specs=[cq, ck, cv, zz, ba, pl.BlockSpec((SUBLANES, LANES), lambda n, h: (0, 0)),
                  pl.BlockSpec((1, dk), lambda n, h: (0, 0))],
        out_specs=[pl.BlockSpec((C, G * dk), lambda n, h: (n, h)),
                   pl.BlockSpec((1, G, dk, dk), lambda n, h: (n, h, 0, 0)),
                   pl.BlockSpec((1, G, C, C), lambda n, h: (n, h, 0, 0))],
        out_shape=[jax.ShapeDtypeStruct((S, H * dk), MXU_DTYPE), jax.ShapeDtypeStruct((NC, H, dk, dk), F32),
                   jax.ShapeDtypeStruct((NC, H, C, C), F32)],
        scratch_shapes=[pltpu.VMEM((H, dk, dk), F32)],
        compiler_params=_params(("arbitrary", "arbitrary")),
        name=name,
    )(c, c, c, h, h, hp, ng)


def _gdn_scan_bwd(c, h, hp, ng, states, inverses, dy, H, name):
    S = c.shape[0]
    dk = c.shape[1] // (3 * H)
    C = GDN_CHUNK
    NC = S // C

    G = min(GDN_HEADS_PER_STEP, H)
    NG = H // G

    def body(cq_ref, ck_ref, cv_ref, z_ref, ba_ref, hp_ref, ng_ref, s_ref, t_ref, dy_ref,
             dq_ref, dk_ref, dv_ref, dz_ref, dba_ref, dhp_ref, dng_ref, dstate):
        n, hg = pl.program_id(0), pl.program_id(1)

        @pl.when((n == 0) & (hg == 0))
        def _():
            dhp_ref[...] = jnp.zeros_like(dhp_ref)
            dng_ref[...] = jnp.zeros_like(dng_ref)

        heads = pl.ds(hg * G, G)

        @pl.when(n == 0)
        def _():
            dstate[heads] = jnp.zeros((G, dk, dk), F32)

        args = (*_gdn_step_args(cq_ref, ck_ref, cv_ref, z_ref, ba_ref, hp_ref, hg, G, H, dk), ng_ref[...], s_ref[0])
        inverse = t_ref[0]
        _, vjp, _ = jax.vjp(lambda *a: _gdn_chunk(*a, inverse=inverse), *args, has_aux=True)
        dy = jnp.stack([dy_ref[:, g * dk:(g + 1) * dk] for g in range(G)])
        dcq, dck, dcv, dzz, dbc, dac, dal, ddt, dng, dst = vjp((dy, dstate[heads]))
        dstate[heads] = dst
        dba = jnp.zeros((C, LANES), F32)
        dhp0 = jnp.zeros((1, LANES), F32)
        dhp1 = jnp.zeros((1, LANES), F32)
        for g in range(G):
            hd = hg * G + g
            sl = slice(g * dk, (g + 1) * dk)
            dq_ref[:, sl] = dcq[g]
            dk_ref[:, sl] = dck[g]
            dv_ref[:, sl] = dcv[g]
            dz_ref[:, sl] = dzz[g]
            dba = dba + _lane_put(dbc[g], hd) + _lane_put(dac[g], H + hd)
            dhp0 = dhp0 + _lane_put(dal[g], hd)
            dhp1 = dhp1 + _lane_put(ddt[g], hd)

        @pl.when(hg == 0)
        def _():
            dba_ref[...] = dba

        @pl.when(hg > 0)
        def _():
            dba_ref[...] += dba

        dhp_ref[0:1, :] += dhp0
        dhp_ref[1:2, :] += dhp1
        dng_ref[...] += dng

    rev = lambda n: NC - 1 - n
    blk = pl.BlockSpec
    in_specs = [*_gdn_specs(H, dk, G, rev),
                blk((SUBLANES, LANES), lambda n, h: (0, 0)), blk((1, dk), lambda n, h: (0, 0)),
                blk((1, G, dk, dk), lambda n, h: (rev(n), h, 0, 0)), blk((1, G, C, C), lambda n, h: (rev(n), h, 0, 0)),
                blk((C, G * dk), lambda n, h: (rev(n), h))]
    out_specs = [blk((C, G * dk), lambda n, h: (rev(n), h))] * 4 + [
        blk((C, LANES), lambda n, h: (rev(n), 0)),
        blk((SUBLANES, LANES), lambda n, h: (0, 0)), blk((1, dk), lambda n, h: (0, 0))]
    out_shape = [jax.ShapeDtypeStruct((S, H * dk), F32)] * 4 + [
        jax.ShapeDtypeStruct((S, LANES), F32), jax.ShapeDtypeStruct((SUBLANES, LANES), F32),
        jax.ShapeDtypeStruct((1, dk), F32)]
    return pl.pallas_call(
        body,
        grid=(NC, NG),
        in_specs=in_specs,
        out_specs=out_specs,
        out_shape=out_shape,
        scratch_shapes=[pltpu.VMEM((H, dk, dk), F32)],
        compiler_params=_params(("arbitrary", "arbitrary")),
        name=name,
    )(c, c, c, h, h, hp, ng, states, inverses, dy)


def _pair_rms(x, gain, dh):
    first = _iota2(x.shape, 1) < dh
    sq = x * x
    ss_a = jnp.sum(jnp.where(first, sq, 0.0), axis=1, keepdims=True)
    ss_b = jnp.sum(jnp.where(first, 0.0, sq), axis=1, keepdims=True)
    inv = jnp.where(first, lax.rsqrt(ss_a / dh + RMS_EPS), lax.rsqrt(ss_b / dh + RMS_EPS))
    return x * inv * gain


def _log_sigmoid(x):
    return jnp.minimum(x, 0.0) - jnp.log(1.0 + jnp.exp(-jnp.abs(x)))


def _cum_fn(fr, bf, carry):
    T = fr.shape[0]
    tril = (_iota2((T, T), 0) >= _iota2((T, T), 1)).astype(F32)
    c = _mm_hi(tril, _log_sigmoid(fr + bf)) + carry
    last = jnp.sum(jnp.where(_iota2(c.shape, 0) == T - 1, c, 0.0), axis=0, keepdims=True)
    return c, last


def _fox_prep_fwd(h, bf, qg, kg, H, dh, name):
    S = h.shape[0]
    W = H * dh
    assert 2 * dh == LANES and H <= LANES
    T = _pick(S, 256, LANES)
    nt = S // T
    npair = H // 2

    def body(hq_ref, hk_ref, f_ref, bf_ref, qg_ref, kg_ref, q_ref, k_ref, ccol_ref, crow_ref, carry):
        i = pl.program_id(0)

        @pl.when(i == 0)
        def _():
            carry[...] = jnp.zeros_like(carry)

        for p in range(npair):
            sl = slice(p * LANES, (p + 1) * LANES)
            q_ref[:, sl] = (_pair_rms(hq_ref[:, sl], qg_ref[...], dh) * (dh ** -0.5 * LOG2E)).astype(q_ref.dtype)
            k_ref[:, sl] = _pair_rms(hk_ref[:, sl], kg_ref[...], dh).astype(k_ref.dtype)
        c, last = _cum_fn(f_ref[...], bf_ref[...], carry[...])
        carry[...] = last
        ct = c.T
        for hh in range(H):
            ccol_ref[hh] = c[:, hh:hh + 1]
            crow_ref[hh] = ct[hh:hh + 1, :]

    return pl.pallas_call(
        body,
        grid=(nt,),
        in_specs=[pl.BlockSpec((T, W), lambda i: (i, 0)), pl.BlockSpec((T, W), lambda i: (i, 1)),
                  pl.BlockSpec((T, LANES), lambda i: (i, 4 * W // LANES)),
                  pl.BlockSpec((1, LANES), lambda i: (0, 0)), pl.BlockSpec((1, LANES), lambda i: (0, 0)),
                  pl.BlockSpec((1, LANES), lambda i: (0, 0))],
        out_specs=[pl.BlockSpec((T, W), lambda i: (i, 0)), pl.BlockSpec((T, W), lambda i: (i, 0)),
                   pl.BlockSpec((H, T, 1), lambda i: (0, i, 0)), pl.BlockSpec((H, 1, T), lambda i: (0, 0, i))],
        out_shape=[jax.ShapeDtypeStruct((S, W), MXU_DTYPE), jax.ShapeDtypeStruct((S, W), MXU_DTYPE),
                   jax.ShapeDtypeStruct((H, S, 1), F32), jax.ShapeDtypeStruct((H, 1, S), F32)],
        scratch_shapes=[pltpu.VMEM((1, LANES), F32)],
        compiler_params=_params(("arbitrary",)),
        name=name,
    )(h, h, h, bf, qg, kg)


def _fox_prep_bwd(h, bf, qg, kg, dq, dk, dcrow, H, dh, name):
    S = h.shape[0]
    W = H * dh
    T = _pick(S, 256, LANES)
    nt = S // T
    npair = H // 2

    def body(hq_ref, hk_ref, f_ref, bf_ref, qg_ref, kg_ref, dq_ref, dk_ref, dcrow_ref,
             dhq_ref, dhk_ref, df_ref, dbf_ref, dqg_ref, dkg_ref, dcarry, dct):
        i = pl.program_id(0)

        @pl.when(i == 0)
        def _():
            dcarry[...] = jnp.zeros_like(dcarry)
            dbf_ref[...] = jnp.zeros_like(dbf_ref)
            dqg_ref[...] = jnp.zeros_like(dqg_ref)
            dkg_ref[...] = jnp.zeros_like(dkg_ref)

        for p in range(npair):
            sl = slice(p * LANES, (p + 1) * LANES)
            _, vjp = jax.vjp(lambda x, g: _pair_rms(x, g, dh) * (dh ** -0.5 * LOG2E), hq_ref[:, sl], qg_ref[...])
            dx, dg = vjp(dq_ref[:, sl])
            dhq_ref[:, sl] = dx
            dqg_ref[...] += dg
            _, vjp = jax.vjp(lambda x, g: _pair_rms(x, g, dh), hk_ref[:, sl], kg_ref[...])
            dx, dg = vjp(dk_ref[:, sl])
            dhk_ref[:, sl] = dx
            dkg_ref[...] += dg
        dct[...] = jnp.zeros_like(dct)
        for hh in range(H):
            dct[hh:hh + 1, :] = dcrow_ref[hh]
        _, vjp = jax.vjp(lambda f, b: _cum_fn(f, b, jnp.zeros((1, LANES), F32)), f_ref[...], bf_ref[...])
        dc = dct[...].T
        df, dbf = vjp((dc, dcarry[...]))
        df_ref[...] = df
        dbf_ref[...] += dbf
        dcarry[...] = dcarry[...] + jnp.sum(dc, axis=0, keepdims=True)

    rv = lambda i: nt - 1 - i
    return pl.pallas_call(
        body,
        grid=(nt,),
        in_specs=[pl.BlockSpec((T, W), lambda i: (rv(i), 0)), pl.BlockSpec((T, W), lambda i: (rv(i), 1)),
                  pl.BlockSpec((T, LANES), lambda i: (rv(i), 4 * W // LANES)),
                  pl.BlockSpec((1, LANES), lambda i: (0, 0)), pl.BlockSpec((1, LANES), lambda i: (0, 0)),
                  pl.BlockSpec((1, LANES), lambda i: (0, 0)),
                  pl.BlockSpec((T, W), lambda i: (rv(i), 0)), pl.BlockSpec((T, W), lambda i: (rv(i), 0)),
                  pl.BlockSpec((H, 1, T), lambda i: (0, 0, rv(i)))],
        out_specs=[pl.BlockSpec((T, W), lambda i: (rv(i), 0)), pl.BlockSpec((T, W), lambda i: (rv(i), 0)),
                   pl.BlockSpec((T, LANES), lambda i: (rv(i), 0)),
                   pl.BlockSpec((1, LANES), lambda i: (0, 0)), pl.BlockSpec((1, LANES), lambda i: (0, 0)),
                   pl.BlockSpec((1, LANES), lambda i: (0, 0))],
        out_shape=[jax.ShapeDtypeStruct((S, W), F32), jax.ShapeDtypeStruct((S, W), F32),
                   jax.ShapeDtypeStruct((S, LANES), F32)] + [jax.ShapeDtypeStruct((1, LANES), F32)] * 3,
        scratch_shapes=[pltpu.VMEM((1, LANES), F32), pltpu.VMEM((LANES, T), F32)],
        compiler_params=_params(("arbitrary",)),
        name=name,
    )(h, h, h, bf, qg, kg, dq, dk, dcrow)


def _head_masks(dh):
    first = _iota2((1, LANES), 1) < dh
    return first, jnp.logical_not(first)


def _per_head(tile, dh):
    return jnp.stack([jnp.where(mask, tile, 0) for mask in _head_masks(dh)])


def _both(tile):
    return jnp.stack([tile, tile])


def _rows2(x):
    return x.reshape(2 * x.shape[1], x.shape[2])


def _stacked(tile, dh):
    return _rows2(_per_head(tile, dh))


LOG2E = math.log2(math.e)
FLASH_SUB = 128


def _flash_scores(q_m, k, crow, row0, diagonal):
    s = _dot(q_m, k, _NT) - crow
    if diagonal:
        s = jnp.where(_iota2(s.shape, s.ndim - 1) <= row0 + _iota2(s.shape, s.ndim - 2), s, NEG)
    return s


def _flash_tiles(S, wide=True):
    tk = _pick(S, 512, LANES)
    tq = 2 * tk if wide and S % (2 * tk) == 0 else tk
    return tq, tk


def _causal_blocks(S, tq, tk, by_query):
    ratio = tq // tk
    if by_query:
        pairs = [(i, j) for i in range(S // tq) for j in range(ratio * (i + 1))]
    else:
        pairs = [(i, j) for j in range(S // tk) for i in range(j // ratio, S // tq)]
    return (jnp.asarray([a for a, _ in pairs], jnp.int32), jnp.asarray([b for _, b in pairs], jnp.int32))


def _flash_blocks(i, j, tq, tk, fn, sub=FLASH_SUB):
    sub = min(sub, tq)
    ratio = tq // tk
    offset = j - ratio * i

    @pl.when(offset < 0)
    def _():
        for r in range(tq // sub):
            fn(slice(r * sub, (r + 1) * sub), 0, False)

    for d in range(ratio):
        @pl.when(offset == d)
        def _():
            for r in range(tq // sub):
                first_row, first_key = r * sub, d * tk
                if first_key > first_row + sub - 1:
                    continue
                fn(slice(first_row, first_row + sub), first_row - first_key, first_key + tk - 1 > first_row)


def _flash_fwd(q, k, h, ccol, crow, H, dh, name):
    S, W = q.shape
    tq, tk = _flash_tiles(S)
    nq, ratio = S // tq, tq // tk
    npair = H // 2
    vblk = 2 * W // LANES

    def body(ii_ref, jj_ref, q_ref, k_ref, v_ref, ccol_ref, crow_ref, o_ref, lse_ref, m_s, l_s, acc_s):
        i, j = ii_ref[pl.program_id(1)], jj_ref[pl.program_id(1)]

        @pl.when(j == 0)
        def _():
            m_s[...] = jnp.full_like(m_s, NEG)
            l_s[...] = jnp.zeros_like(l_s)
            acc_s[...] = jnp.zeros_like(acc_s)

        def tile(rows, row0, diagonal):
            s = _flash_scores(_per_head(q_ref[rows, :], dh), _both(k_ref[...]), crow_ref[...] * LOG2E, row0, diagonal)
            s = s + ccol_ref[:, rows, :] * LOG2E
            m_old = m_s[:, rows, :]
            m_new = jnp.maximum(m_old, jnp.max(s, axis=-1, keepdims=True))
            alpha = jnp.exp2(m_old - m_new)
            p = jnp.exp2(s - m_new)
            l_s[:, rows, :] = alpha * l_s[:, rows, :] + jnp.sum(p, axis=-1, keepdims=True)
            first, _ = _head_masks(dh)
            acc_s[rows, :] = jnp.where(first, alpha[0], alpha[1]) * acc_s[rows, :] + _dot(
                jnp.concatenate([p[0], p[1]], axis=1), _stacked(v_ref[...].astype(MXU_DTYPE), dh), _NN)
            m_s[:, rows, :] = m_new

        _flash_blocks(i, j, tq, tk, tile)

        @pl.when(j == ratio * (i + 1) - 1)
        def _():
            first, _ = _head_masks(dh)
            o_ref[...] = acc_s[...] / jnp.where(first, l_s[0], l_s[1])
            for a in range(2):
                lse_ref[a] = m_s[a] + jnp.log2(l_s[a])

    ii, jj = _causal_blocks(S, tq, tk, by_query=True)
    return pl.pallas_call(
        body,
        grid_spec=pltpu.PrefetchScalarGridSpec(
            num_scalar_prefetch=2,
            grid=(npair, len(ii)),
            in_specs=[pl.BlockSpec((tq, LANES), lambda p, t, ii, jj: (ii[t], p)),
                      pl.BlockSpec((tk, LANES), lambda p, t, ii, jj: (jj[t], p)),
                      pl.BlockSpec((tk, LANES), lambda p, t, ii, jj: (jj[t], vblk + p)),
                      pl.BlockSpec((2, tq, 1), lambda p, t, ii, jj: (p, ii[t], 0)),
                      pl.BlockSpec((2, 1, tk), lambda p, t, ii, jj: (p, 0, jj[t]))],
            out_specs=[pl.BlockSpec((tq, LANES), lambda p, t, ii, jj: (ii[t], p)),
                       pl.BlockSpec((2, tq, 1), lambda p, t, ii, jj: (p, ii[t], 0))],
            scratch_shapes=[pltpu.VMEM((2, tq, 1), F32), pltpu.VMEM((2, tq, 1), F32), pltpu.VMEM((tq, LANES), F32)]),
        out_shape=[jax.ShapeDtypeStruct((S, W), F32), jax.ShapeDtypeStruct((H, S, 1), F32)],
        compiler_params=_params(("parallel", "arbitrary")),
        name=name,
    )(ii, jj, q, k, h, ccol, crow)


def _flash_bwd_kv(q, k, h, ccol, crow, lse, delta, resid, do, H, dh, name):
    S, W = q.shape
    tq, tk = _flash_tiles(S)
    nq, ratio = S // tq, tq // tk
    npair = H // 2
    vblk = 2 * W // LANES

    def body(ii_ref, jj_ref, q_ref, k_ref, v_ref, ccol_ref, crow_ref, lse_ref, dl_ref, rs_ref, do_ref, dk_ref, dv_ref, dcr_ref,
             dkt_s, dvt_s):
        i, j = ii_ref[pl.program_id(1)], jj_ref[pl.program_id(1)]

        @pl.when(i == j // ratio)
        def _():
            dkt_s[...] = jnp.zeros_like(dkt_s)
            dvt_s[...] = jnp.zeros_like(dvt_s)
            dcr_ref[...] = jnp.zeros_like(dcr_ref)

        def tile(rows, row0, diagonal):
            kv, vv = _both(k_ref[...]), _both(v_ref[...].astype(MXU_DTYPE))
            q_m = _per_head(q_ref[rows, :], dh)
            do_m = _per_head(do_ref[rows, :].astype(MXU_DTYPE), dh)
            s = _flash_scores(q_m, kv, crow_ref[...] * LOG2E, row0, diagonal)
            p = jnp.exp2(s + (ccol_ref[:, rows, :] * LOG2E - lse_ref[:, rows, :]))
            ds = p * (_dot(do_m, vv, _NT) - (dl_ref[:, rows, :] + rs_ref[:, rows, :]))
            dvt_s[...] += _dot(_rows2(do_m), _rows2(p), _TN)
            dkt_s[...] += _dot(_rows2(q_m), _rows2(ds), _TN)
            dcr_ref[...] -= jnp.sum(ds, axis=1, keepdims=True)

        _flash_blocks(i, j, tq, tk, tile, sub=2 * FLASH_SUB)

        @pl.when(i == nq - 1)
        def _():
            dk_ref[...] = dkt_s[...].T * (1.0 / LOG2E)
            dv_ref[...] = dvt_s[...].T

    ii, jj = _causal_blocks(S, tq, tk, by_query=False)
    qrow = pl.BlockSpec((2, tq, 1), lambda p, t, ii, jj: (p, ii[t], 0))
    return pl.pallas_call(
        body,
        grid_spec=pltpu.PrefetchScalarGridSpec(
            num_scalar_prefetch=2,
            grid=(npair, len(ii)),
            in_specs=[pl.BlockSpec((tq, LANES), lambda p, t, ii, jj: (ii[t], p)),
                      pl.BlockSpec((tk, LANES), lambda p, t, ii, jj: (jj[t], p)),
                      pl.BlockSpec((tk, LANES), lambda p, t, ii, jj: (jj[t], vblk + p)),
                      qrow,
                      pl.BlockSpec((2, 1, tk), lambda p, t, ii, jj: (p, 0, jj[t])),
                      qrow, qrow, qrow,
                      pl.BlockSpec((tq, LANES), lambda p, t, ii, jj: (ii[t], p))],
            out_specs=[pl.BlockSpec((tk, LANES), lambda p, t, ii, jj: (jj[t], p)),
                       pl.BlockSpec((tk, LANES), lambda p, t, ii, jj: (jj[t], p)),
                       pl.BlockSpec((2, 1, tk), lambda p, t, ii, jj: (p, 0, jj[t]))],
            scratch_shapes=[pltpu.VMEM((LANES, tk), F32), pltpu.VMEM((LANES, tk), F32)]),
        out_shape=[jax.ShapeDtypeStruct((S, W), F32), jax.ShapeDtypeStruct((S, W), F32),
                   jax.ShapeDtypeStruct((H, 1, S), F32)],
        compiler_params=_params(("parallel", "arbitrary")),
        name=name,
    )(ii, jj, q, k, h, ccol, crow, lse, delta, resid, do)


def _flash_bwd_q(q, k, h, ccol, crow, lse, delta, do, H, dh, name):
    S, W = q.shape
    tq, tk = _flash_tiles(S, wide=False)
    npair = H // 2
    vblk = 2 * W // LANES

    def body(ii_ref, jj_ref, q_ref, k_ref, v_ref, ccol_ref, crow_ref, lse_ref, dl_ref, do_ref, dq_ref, rs_ref):
        i, j = ii_ref[pl.program_id(1)], jj_ref[pl.program_id(1)]

        @pl.when(j == 0)
        def _():
            dq_ref[...] = jnp.zeros_like(dq_ref)
            rs_ref[...] = jnp.zeros_like(rs_ref)

        def tile(rows, row0, diagonal):
            kv, vv = k_ref[...], _both(v_ref[...].astype(MXU_DTYPE))
            s = _flash_scores(_per_head(q_ref[rows, :], dh), _both(kv), crow_ref[...] * LOG2E, row0, diagonal)
            p = jnp.exp2(s + (ccol_ref[:, rows, :] * LOG2E - lse_ref[:, rows, :]))
            ds = p * (_dot(_per_head(do_ref[rows, :].astype(MXU_DTYPE), dh), vv, _NT) - dl_ref[:, rows, :])
            dq = _dot(ds, _per_head(kv, dh), _NN)
            dq_ref[rows, :] += (dq[0] + dq[1]) * (1.0 / LOG2E)
            rs_ref[:, rows, :] += jnp.sum(ds, axis=-1, keepdims=True)

        _flash_blocks(i, j, tq, tk, tile, sub=tk)

    ii, jj = _causal_blocks(S, tq, tk, by_query=True)
    qrow = pl.BlockSpec((2, tq, 1), lambda p, t, ii, jj: (p, ii[t], 0))
    return pl.pallas_call(
        body,
        grid_spec=pltpu.PrefetchScalarGridSpec(
            num_scalar_prefetch=2,
            grid=(npair, len(ii)),
            in_specs=[pl.BlockSpec((tq, LANES), lambda p, t, ii, jj: (ii[t], p)),
                      pl.BlockSpec((tk, LANES), lambda p, t, ii, jj: (jj[t], p)),
                      pl.BlockSpec((tk, LANES), lambda p, t, ii, jj: (jj[t], vblk + p)),
                      qrow,
                      pl.BlockSpec((2, 1, tk), lambda p, t, ii, jj: (p, 0, jj[t])),
                      qrow, qrow,
                      pl.BlockSpec((tq, LANES), lambda p, t, ii, jj: (ii[t], p))],
            out_specs=[pl.BlockSpec((tq, LANES), lambda p, t, ii, jj: (ii[t], p)), qrow]),
        out_shape=[jax.ShapeDtypeStruct((S, W), F32), jax.ShapeDtypeStruct((H, S, 1), F32)],
        compiler_params=_params(("parallel", "arbitrary")),
        name=name,
    )(ii, jj, q, k, h, ccol, crow, lse, delta, do)


def _fox_gate_fwd(o, h, W, name):
    def fn(rows, _, __):
        return [rows[0] * _silu(rows[1])], [], []

    return _rowwise(fn, [o, (h, W, 3)], [], [(W, MXU_DTYPE)], [], tile=512, name=name)[0]


def _fox_gate_bwd(o, h, dog, H, dh, name):
    S, W = o.shape
    T = _pick(S, 512, SUBLANES)

    def body(o_ref, z_ref, d_ref, do_ref, dz_ref, dl_ref):
        ov, zv, dv = o_ref[...], z_ref[...], d_ref[...]
        sg = _sigmoid(zv)
        do = dv * zv * sg
        do_ref[...] = do
        dz_ref[...] = dv * ov * sg * (1.0 + zv * (1.0 - sg))
        prod = do * ov
        for p in range(H // 2):
            blk = prod[:, p * LANES:(p + 1) * LANES]
            first = _iota2(blk.shape, 1) < dh
            dl_ref[2 * p] = jnp.sum(jnp.where(first, blk, 0.0), axis=1, keepdims=True)
            dl_ref[2 * p + 1] = jnp.sum(jnp.where(first, 0.0, blk), axis=1, keepdims=True)

    return pl.pallas_call(
        body,
        grid=(S // T,),
        in_specs=[pl.BlockSpec((T, W), lambda i: (i, 0)), pl.BlockSpec((T, W), lambda i: (i, 3)),
                  pl.BlockSpec((T, W), lambda i: (i, 0))],
        out_specs=[pl.BlockSpec((T, W), lambda i: (i, 0)), pl.BlockSpec((T, W), lambda i: (i, 0)),
                   pl.BlockSpec((H, T, 1), lambda i: (0, i, 0))],
        out_shape=[jax.ShapeDtypeStruct((S, W), F32), jax.ShapeDtypeStruct((S, W), F32),
                   jax.ShapeDtypeStruct((H, S, 1), F32)],
        compiler_params=_params(("parallel",)),
        name=name,
    )(o, h, dog)


def _gdn_layer_fwd(x, w, tag):
    H = w["H"]
    qk = H * LANES
    h = _matmul(x, w["w_in"], "nn", name=f"{tag}_in")
    c = _conv_fwd(h, w["conv"], 3 * qk, name=f"{tag}_conv")
    og, states, inverses = _gdn_scan_fwd(c, h, w["hp"], w["ng"], H, name=f"{tag}_scan")
    y = _matmul(og, w["w_out"], "nn", name=f"{tag}_out")
    return y, (x, h, c, states, inverses, og)


def _gdn_layer_bwd(dy, res, w, tag):
    x, h, c, states, inverses, og = res
    H = w["H"]
    qk = H * LANES
    dog = _matmul(dy, w["w_out"], "nt", name=f"{tag}_out_dx")
    dw_out = _matmul(og, dy, "tn", name=f"{tag}_out_dw")
    dq, dk, dv, dz, dba, dhp, dng = _gdn_scan_bwd(c, h, w["hp"], w["ng"], states, inverses, dog, H, name=f"{tag}_scan_bwd")
    dh_parts, dconv = [], []
    for part, d in enumerate((dq, dk, dv)):
        dh_p, dw_p = _conv_bwd(d, h, w["conv"][:, part * qk:(part + 1) * qk], part * qk, name=f"{tag}_conv_bwd{part}")
        dh_parts.append(dh_p)
        dconv.append(dw_p[:GDN_CONV])
    dh = jnp.concatenate(dh_parts + [dz, dba], axis=1)
    dx = _matmul(dh, w["w_in"], "nt", name=f"{tag}_in_dx")
    dw_in = _matmul(x, dh, "tn", name=f"{tag}_in_dw")
    grads = {"w_in": dw_in, "w_out": dw_out, "conv": jnp.concatenate(dconv, axis=1),
             "a_log": dhp[0, :H], "dt_bias": dhp[1, :H], "norm_g": dng[0]}
    return dx, grads


def _fox_layer_fwd(x, w, tag):
    H, dh = w["H"], w["dh"]
    W = H * dh
    h = _matmul(x, w["w_in"], "nn", name=f"{tag}_in")
    q, k, ccol, crow = _fox_prep_fwd(h, w["bf"], w["qg"], w["kg"], H, dh, name=f"{tag}_prep")
    o, lse = _flash_fwd(q, k, h, ccol, crow, H, dh, name=f"{tag}_flash")
    og = _fox_gate_fwd(o, h, W, name=f"{tag}_gate")
    y = _matmul(og, w["w_out"], "nn", name=f"{tag}_out")
    return y, (x, h, q, k, ccol, crow, o, lse, og)


def _fox_layer_bwd(dy, res, w, tag):
    x, h, q, k, ccol, crow, o, lse, og = res
    H, dh = w["H"], w["dh"]
    dog = _matmul(dy, w["w_out"], "nt", name=f"{tag}_out_dx")
    dw_out = _matmul(og, dy, "tn", name=f"{tag}_out_dw")
    do, dz, delta = _fox_gate_bwd(o, h, dog, H, dh, name=f"{tag}_gate_bwd")
    dqq, resid = _flash_bwd_q(q, k, h, ccol, crow, lse, delta, do, H, dh, name=f"{tag}_flash_bwd_q")
    dkk, dvv, dcrow = _flash_bwd_kv(q, k, h, ccol, crow, lse, delta, resid, do, H, dh, name=f"{tag}_flash_bwd_kv")
    dhq, dhk, df, dbf, dqg, dkg = _fox_prep_bwd(h, w["bf"], w["qg"], w["kg"], dqq, dkk, dcrow, H, dh, name=f"{tag}_prep_bwd")
    dhh = jnp.concatenate([dhq, dhk, dvv, dz, df], axis=1)
    dx = _matmul(dhh, w["w_in"], "nt", name=f"{tag}_in_dx")
    dw_in = _matmul(x, dhh, "tn", name=f"{tag}_in_dw")
    grads = {"w_in": dw_in, "w_out": dw_out, "b_f": dbf[0, :H],
             "q_norm_g": dqg[0, :dh] + dqg[0, dh:], "k_norm_g": dkg[0, :dh] + dkg[0, dh:]}
    return dx, grads


def _pad_cols(w, n):
    return jnp.pad(w, ((0, 0), (0, n - w.shape[1])))


def _build_layers(full, small):
    depth = small["ln_g"].shape[0]
    gh = small["gdn_a_log"].shape[1]
    fh, dh = small["fox_b_f"].shape[1], small["fox_q_norm_g"].shape[1]
    layers = []
    for i in range(depth):
        j = i // 2
        w = {"ln_g": small["ln_g"][i][None], "ln_b": small["ln_b"][i][None],
             "w_gate": full["ple_w_gate"][i], "w_proj": full["ple_w_proj"][i]}
        if i % 2 == 0:
            hp = jnp.zeros((SUBLANES, LANES), F32).at[0, :gh].set(small["gdn_a_log"][j]).at[1, :gh].set(small["gdn_dt_bias"][j])
            w.update(kind="gdn", H=gh, w_in=_pad_cols(full["gdn_w_in"][j], 4 * gh * LANES + LANES),
                     conv=full["gdn_conv_w"][j], hp=hp, ng=small["gdn_norm_g"][j][None], w_out=full["gdn_w_out"][j])
        else:
            bf = jnp.zeros((1, LANES), F32).at[0, :fh].set(small["fox_b_f"][j])
            w.update(kind="fox", H=fh, dh=dh, w_in=_pad_cols(full["fox_w_in"][j], 4 * fh * dh + LANES), bf=bf,
                     qg=jnp.tile(small["fox_q_norm_g"][j], 2)[None], kg=jnp.tile(small["fox_k_norm_g"][j], 2)[None],
                     w_out=full["fox_w_out"][j])
        layers.append(w)
    return layers


def _local_step(x, p, target, layers):
    depth = len(layers)
    alpha = (2 * depth) ** 0.25
    saved = []
    for i, w in enumerate(layers):
        tag = f"l{i}"
        if w["kind"] == "gdn":
            y, res = _gdn_layer_fwd(x, w, tag)
        else:
            y, res = _fox_layer_fwd(x, w, tag)
        x1 = _ln_fwd(x, y, w["ln_g"], w["ln_b"], alpha, name=f"{tag}_ln")
        gate_pre = _matmul(x1, w["w_gate"], "nn", name=f"{tag}_gate_mm")
        pp = _matmul(p[i], w["w_proj"], "nn", name=f"{tag}_proj_mm")
        x2 = _ple_fwd(x1, gate_pre, pp, name=f"{tag}_ple")
        saved.append((res, x, y, x1, gate_pre, pp))
        x = x2
    dx, loss_tile = _loss_head(x, target, name="loss_head")
    grads = [None] * depth
    for i in reversed(range(depth)):
        w = layers[i]
        tag = f"l{i}"
        res, xin, y, x1, gate_pre, pp = saved[i]
        dgp, dpp = _ple_bwd(dx, gate_pre, pp, name=f"{tag}_ple_bwd")
        dx1 = _add(dx, _matmul(dgp, w["w_gate"], "nt", name=f"{tag}_gate_dx"), name=f"{tag}_dx1")
        dw_gate = _matmul(x1, dgp, "tn", name=f"{tag}_gate_dw")
        dw_proj = _matmul(p[i], dpp, "tn", name=f"{tag}_proj_dw")
        dy, dg, db = _ln_bwd(xin, y, w["ln_g"], w["ln_b"], dx1, alpha, name=f"{tag}_ln_bwd")
        if w["kind"] == "gdn":
            dxm, g = _gdn_layer_bwd(dy, res, w, tag)
        else:
            dxm, g = _fox_layer_bwd(dy, res, w, tag)
        dx = _axpy(alpha, dy, dxm, name=f"{tag}_dx")
        g.update({"w_gate": dw_gate, "w_proj": dw_proj, "ln_g": dg[0], "ln_b": db[0]})
        grads[i] = g
    return loss_tile, dx, grads


MESH_ID = pl.DeviceIdType.MESH
HBM_SPEC = pl.BlockSpec(memory_space=pl.ANY)
PACK_COLS = 1024
PACK_ROWS = 256


def _all_gather(shards, name):
    nt = len(shards)

    def body(*refs):
        x_refs, out_refs = refs[:nt], refs[nt:2 * nt]
        send_sems, recv_sems, local_sems = refs[2 * nt:]
        x, y, c = lax.axis_index("x"), lax.axis_index("y"), lax.axis_index("c")
        me, sibling = (x, y, c), (x, y, 1 - c)
        chips = [(1 - x, y), (x, 1 - y), (1 - x, 1 - y)]

        def slot(t, px, py, pc):
            return out_refs[t].at[4 * px + 2 * py + pc]

        def copy(k, t, block, to, src=None):
            return pltpu.make_async_remote_copy(
                src_ref=slot(t, *block) if src is None else src, dst_ref=slot(t, *block),
                send_sem=send_sems.at[k, t], recv_sem=recv_sems.at[k, t], device_id=to, device_id_type=MESH_ID)

        every = range(nt)
        mine = [pltpu.make_async_copy(x_refs[t], slot(t, *me), local_sems.at[t]) for t in every]
        for cp in mine:
            cp.start()
        first = [copy(0, t, me, sibling, src=x_refs[t]) for t in every]
        first += [copy(1 + j, t, me, (*chip, c), src=x_refs[t]) for j, chip in enumerate(chips) for t in every]
        for cp in first:
            cp.start()
        passed = []
        for j, chip in enumerate(chips):
            for t in every:
                copy(1 + j, t, (*chip, c), me).wait_recv()
                passed.append(copy(4 + j, t, (*chip, c), sibling))
                passed[-1].start()
        for t in every:
            copy(0, t, sibling, me).wait_recv()
        for j, chip in enumerate(chips):
            for t in every:
                copy(4 + j, t, (*chip, 1 - c), me).wait_recv()
        for cp in first + passed:
            cp.wait_send()
        for cp in mine:
            cp.wait()

    return pl.pallas_call(
        body,
        out_shape=[jax.ShapeDtypeStruct((N_DEV, *s.shape), s.dtype) for s in shards],
        in_specs=[HBM_SPEC] * nt,
        out_specs=[HBM_SPEC] * nt,
        scratch_shapes=[pltpu.SemaphoreType.DMA((7, nt)), pltpu.SemaphoreType.DMA((7, nt)), pltpu.SemaphoreType.DMA((nt,))],
        name=name,
    )(*shards)


N_CHIPS = 4


def _sibling_swap(slabs, name):
    nt = len(slabs)

    def body(*refs):
        g_refs, out_refs = refs[:nt], refs[nt:2 * nt]
        send_sems, recv_sems = refs[2 * nt:]
        x, y, c = lax.axis_index("x"), lax.axis_index("y"), lax.axis_index("c")
        copies = [pltpu.make_async_remote_copy(src_ref=g_refs[t].at[1 - c], dst_ref=out_refs[t], send_sem=send_sems.at[t],
                                               recv_sem=recv_sems.at[t], device_id=(x, y, 1 - c), device_id_type=MESH_ID)
                  for t in range(nt)]
        for cp in copies:
            cp.start()
        for cp in copies:
            cp.wait()

    return pl.pallas_call(
        body,
        out_shape=[jax.ShapeDtypeStruct(s.shape[1:], s.dtype) for s in slabs],
        in_specs=[HBM_SPEC] * nt,
        out_specs=[HBM_SPEC] * nt,
        scratch_shapes=[pltpu.SemaphoreType.DMA((nt,)), pltpu.SemaphoreType.DMA((nt,))],
        name=name,
    )(*slabs)


def _add_own_half(core, slabs, received, name):
    shape = received.shape
    R, C = math.prod(shape[:-1]), shape[-1]
    tr = _pick(R, 512, SUBLANES)

    def body(core_ref, mine_ref, got_ref, o_ref):
        o_ref[...] = mine_ref[0] + got_ref[...]

    return pl.pallas_call(
        body,
        grid_spec=pltpu.PrefetchScalarGridSpec(
            num_scalar_prefetch=1,
            grid=(R // tr,),
            in_specs=[pl.BlockSpec((1, tr, C), lambda i, core: (core[0], i, 0)), pl.BlockSpec((tr, C), lambda i, core: (i, 0))],
            out_specs=pl.BlockSpec((tr, C), lambda i, core: (i, 0))),
        out_shape=jax.ShapeDtypeStruct((R, C), F32),
        compiler_params=_params(("parallel",)),
        name=name,
    )(core, slabs.reshape(2, R, C), received.reshape(R, C)).reshape(shape)


def _chip_all_to_all(slabs, name):
    nt = len(slabs)

    def body(*refs):
        g_refs, out_refs = refs[:nt], refs[nt:2 * nt]
        send_sems, recv_sems, local_sems = refs[2 * nt:]
        x, y, c = lax.axis_index("x"), lax.axis_index("y"), lax.axis_index("c")
        me = 2 * x + y
        mine = [pltpu.make_async_copy(g_refs[t].at[me], out_refs[t].at[me], local_sems.at[t]) for t in range(nt)]
        for cp in mine:
            cp.start()
        copies = []
        for k in range(1, N_CHIPS):
            px = 1 - x if k & 2 else x
            py = 1 - y if k & 1 else y
            peer = 2 * px + py
            for t in range(nt):
                copies.append(tuple(
                    pltpu.make_async_remote_copy(src_ref=g_refs[t].at[peer], dst_ref=out_refs[t].at[dst],
                                                 send_sem=send_sems.at[k - 1, t], recv_sem=recv_sems.at[k - 1, t],
                                                 device_id=(px, py, c), device_id_type=MESH_ID)
                    for dst in (me, peer)))
        for send, _ in copies:
            send.start()
        for send, arrive in copies:
            arrive.wait_recv()
            send.wait_send()
        for cp in mine:
            cp.wait()

    return pl.pallas_call(
        body,
        out_shape=[jax.ShapeDtypeStruct(s.shape, s.dtype) for s in slabs],
        in_specs=[HBM_SPEC] * nt,
        out_specs=[HBM_SPEC] * nt,
        scratch_shapes=[pltpu.SemaphoreType.DMA((N_CHIPS - 1, nt)), pltpu.SemaphoreType.DMA((N_CHIPS - 1, nt)),
                        pltpu.SemaphoreType.DMA((nt,))],
        name=name,
    )(*slabs)


def _pack(flats, dtype):
    flat = jnp.concatenate([f.astype(dtype).reshape(-1) for f in flats])
    unit = PACK_ROWS * PACK_COLS
    n = -(-flat.shape[0] // unit) * unit
    return jnp.pad(flat, (0, n - flat.shape[0])).reshape(n // PACK_COLS, PACK_COLS)


def _unpack(buf, shapes):
    lead = buf.shape[:-2]
    flat = buf.reshape(*lead, -1)
    out, off = [], 0
    for s in shapes:
        n = math.prod(s)
        out.append(flat[..., off:off + n].reshape(*lead, *s))
        off += n
    return out


_ROW_SPLIT = ("ple_w_gate", "gdn_w_out", "fox_w_out")
_COL_SPLIT = ("ple_w_proj", "gdn_w_in", "gdn_conv_w", "fox_w_in")
_SHARDED = ("ple_w_gate", "ple_w_proj", "gdn_w_in", "gdn_conv_w", "gdn_w_out", "fox_w_in", "fox_w_out")
_REPLICATED = ("ln_g", "ln_b", "gdn_a_log", "gdn_dt_bias", "gdn_norm_g", "fox_b_f", "fox_q_norm_g", "fox_k_norm_g")
_WEIGHTS = ("ln_g", "ln_b", "ple_w_gate", "ple_w_proj", "gdn_w_in", "gdn_conv_w", "gdn_a_log", "gdn_dt_bias",
            "gdn_norm_g", "gdn_w_out", "fox_w_in", "fox_b_f", "fox_q_norm_g", "fox_k_norm_g", "fox_w_out")


def _join(name, gathered):
    n, l, a, b = gathered.shape
    if name in _ROW_SPLIT:
        return gathered.transpose(1, 0, 2, 3).reshape(l, n * a, b)
    return gathered.transpose(1, 2, 0, 3).reshape(l, a, n * b)


def _split(name, full):
    l, a, b = full.shape
    if name in _ROW_SPLIT:
        return full.reshape(l, N_DEV, a // N_DEV, b).transpose(1, 0, 2, 3)
    return full.reshape(l, a, N_DEV, b // N_DEV).transpose(2, 0, 1, 3)


def _adamw(w, g_parts, m, v, name):
    shape = w.shape
    n_parts = g_parts.shape[0]
    R, C = math.prod(shape[:-1]), shape[-1]
    tr = _pick(R, 256, SUBLANES)
    c1 = 1.0 - ADAM_B1 ** ADAM_STEP
    c2 = 1.0 - ADAM_B2 ** ADAM_STEP

    def body(w_ref, g_ref, m_ref, v_ref, go_ref, d_ref, mo_ref, vo_ref):
        gv = g_ref[0]
        for s in range(1, n_parts):
            gv = gv + g_ref[s]
        mn = ADAM_B1 * m_ref[...] + (1.0 - ADAM_B1) * gv
        vn = ADAM_B2 * v_ref[...] + (1.0 - ADAM_B2) * jnp.square(gv)
        go_ref[...] = gv
        d_ref[...] = -ADAM_LR * ((mn / c1) / (jnp.sqrt(vn / c2) + ADAM_EPS) + ADAM_WD * w_ref[...])
        mo_ref[...] = mn
        vo_ref[...] = vn

    row = pl.BlockSpec((tr, C), lambda i: (i, 0))
    outs = pl.pallas_call(
        body,
        grid=(R // tr,),
        in_specs=[row, pl.BlockSpec((n_parts, tr, C), lambda i: (0, i, 0)), row, row],
        out_specs=[row] * 4,
        out_shape=[jax.ShapeDtypeStruct((R, C), F32)] * 4,
        compiler_params=_params(("parallel",)),
        name=name,
    )(w.reshape(R, C), g_parts.reshape(n_parts, R, C), m.reshape(R, C), v.reshape(R, C))
    return [o.reshape(shape) for o in outs]


def _train_step(x, p, target, w, m, v):
    shards = [w[n] if n == "gdn_conv_w" else w[n].astype(MXU_DTYPE) for n in _SHARDED]
    gathered = _all_gather(shards, name="gather_weights")
    full = {n: _join(n, part) for n, part in zip(_SHARDED, gathered)}
    layers = _build_layers(full, {n: w[n] for n in _REPLICATED})

    loss_tile, dx, grads = _local_step(x[0], p[:, 0], target[0], layers)
    loss = lax.psum(loss_tile[0, 0], ("x", "y", "c"))

    depth = len(layers)
    gdn_l = [i for i in range(depth) if i % 2 == 0]
    fox_l = [i for i in range(depth) if i % 2 == 1]

    def stack(key, idx):
        return jnp.stack([grads[i][key] for i in idx])

    full_g = {
        "ple_w_gate": stack("w_gate", range(depth)), "ple_w_proj": stack("w_proj", range(depth)),
        "gdn_w_in": stack("w_in", gdn_l)[..., :w["gdn_w_in"].shape[-1] * N_DEV], "gdn_conv_w": stack("conv", gdn_l),
        "gdn_w_out": stack("w_out", gdn_l),
        "fox_w_in": stack("w_in", fox_l)[..., :w["fox_w_in"].shape[-1] * N_DEV], "fox_w_out": stack("w_out", fox_l)}
    small_g = {
        "ln_g": stack("ln_g", range(depth)), "ln_b": stack("ln_b", range(depth)),
        "gdn_a_log": stack("a_log", gdn_l), "gdn_dt_bias": stack("dt_bias", gdn_l), "gdn_norm_g": stack("norm_g", gdn_l),
        "fox_b_f": stack("b_f", fox_l), "fox_q_norm_g": stack("q_norm_g", fox_l), "fox_k_norm_g": stack("k_norm_g", fox_l)}

    by_core = []
    for n in _SHARDED:
        per_device = _split(n, full_g[n])
        by_core.append(jnp.swapaxes(per_device.reshape(N_CHIPS, 2, *per_device.shape[1:]), 0, 1))
    from_sibling = _sibling_swap(by_core, name="swap_grads")
    core = lax.axis_index("c").astype(jnp.int32).reshape(1)
    chip_sums = [_add_own_half(core, mine, got, name=f"chip_sum_{n}")
                 for n, mine, got in zip(_SHARDED, by_core, from_sibling)]
    g_parts = dict(zip(_SHARDED, _chip_all_to_all(chip_sums, name="scatter_grads")))
    small_all = _all_gather([_pack([small_g[n] for n in _REPLICATED], F32)], name="gather_small_grads")[0]
    g_parts.update(zip(_REPLICATED, _unpack(small_all, [w[n].shape for n in _REPLICATED])))

    g, delta, new_m, new_v = {}, {}, {}, {}
    for n in _WEIGHTS:
        g[n], delta[n], new_m[n], new_v[n] = _adamw(w[n], g_parts[n], m[n], v[n], name=f"adamw_{n}")
    return (loss, dx[None], *[g[n] for n in _WEIGHTS], *[delta[n] for n in _WEIGHTS],
            *[new_m[n] for n in _WEIGHTS], *[new_v[n] for n in _WEIGHTS])


def kernel(x, p, ln_g, ln_b, ple_w_gate, ple_w_proj, gdn_w_in, gdn_conv_w, gdn_a_log, gdn_dt_bias, gdn_norm_g, gdn_w_out, fox_w_in, fox_b_f, fox_q_norm_g, fox_k_norm_g, fox_w_out, loss_target, m_ln_g, m_ln_b, m_ple_w_gate, m_ple_w_proj, m_gdn_w_in, m_gdn_conv_w, m_gdn_a_log, m_gdn_dt_bias, m_gdn_norm_g, m_gdn_w_out, m_fox_w_in, m_fox_b_f, m_fox_q_norm_g, m_fox_k_norm_g, m_fox_w_out, v_ln_g, v_ln_b, v_ple_w_gate, v_ple_w_proj, v_gdn_w_in, v_gdn_conv_w, v_gdn_a_log, v_gdn_dt_bias, v_gdn_norm_g, v_gdn_w_out, v_fox_w_in, v_fox_b_f, v_fox_q_norm_g, v_fox_k_norm_g, v_fox_w_out):
    given = dict(locals())
    w = {n: given[n] for n in _WEIGHTS}
    m = {n: given["m_" + n] for n in _WEIGHTS}
    v = {n: given["v_" + n] for n in _WEIGHTS}
    return _train_step(x, p, loss_target, w, m, v)
```
